```python
import jax, jax.numpy as jnp
from jax import lax
import numpy as np

D_MODEL = 1024
BATCH = 2
SEQ = 16384
DEPTH = 2
DEC_BATCH = 16
DEC_SEQ = 64
PAST_LEN = 4096

CHUNK = 64
EPS = 1e-6
ML_HEADS = 4
ML_DK = 64
ML_DV = 64
HG_HEADS = 4
HG_DK = 64
HG_DV = 64
SW_HEADS = 4
SW_KV_HEADS = 2
SW_DH = 64
WINDOW = 128
ROPE_DIMS = SW_DH // 4
ROPE_THETA = 500000.0
SSD_HEADS = 4
SSD_P = 64
SSD_N = 128
SSD_GROUPS = 2
SSD_CONV = 4
SSD_INNER = SSD_HEADS * SSD_P
SSD_CONV_DIM = SSD_INNER + 2 * SSD_GROUPS * SSD_N
D_FF = 2816
N_EXPERTS = 8
TOP_K = 2
D_FF_EXPERT = 3584
N_DENSE = (DEPTH + 1) // 2
N_MOE = DEPTH // 2
IN_SPLITS = (ML_HEADS * ML_DK, ML_HEADS * ML_DK, ML_HEADS * ML_DV, ML_HEADS, ML_HEADS, ML_HEADS * ML_DV,
             HG_HEADS * HG_DK, HG_HEADS * HG_DK, HG_HEADS * HG_DV, HG_HEADS * HG_DV,
             SW_HEADS * SW_DH, SW_KV_HEADS * SW_DH, SW_KV_HEADS * SW_DH,
             SSD_INNER, SSD_CONV_DIM, SSD_HEADS)
D_IN = sum(IN_SPLITS)
D_MIX = ML_HEADS * ML_DV + HG_HEADS * HG_DV + SW_HEADS * SW_DH + SSD_INNER
MIXER_KEYS = ('w_in', 'ml_ig_b', 'ml_fg_b', 'ml_norm_w', 'hg_lb', 'hg_norm_w', 'sw_sinks',
              'ssd_conv_w', 'ssd_conv_b', 'ssd_dt_bias', 'ssd_A_log', 'ssd_D', 'ssd_norm_w', 'w_out')

kernel_name = 'hybrid_streaming_encoder_step'


def rmsnorm(x, w):
    xf = x.astype(jnp.float32)
    y = xf * lax.rsqrt(jnp.mean(xf * xf, axis=-1, keepdims=True) + EPS)
    return (y * w.astype(jnp.float32)).astype(x.dtype)


def head_rmsnorm(x, w):
    y = x * lax.rsqrt(jnp.mean(x * x, axis=-1, keepdims=True) + EPS)
    return (y * w.astype(jnp.float32).reshape(x.shape[-2:])).reshape(x.shape[:-2] + (-1,))


def to_chunks(a, L):
    B, T = a.shape[:2]
    return jnp.moveaxis(a.reshape((B, T // L, L) + a.shape[2:]), 1, 0)


def from_chunks(a):
    a = jnp.moveaxis(a, 0, 1)
    return a.reshape((a.shape[0], a.shape[1] * a.shape[2]) + a.shape[3:])


def mlstm_scan(q, k, v, ig, lf, C0, n0, m0, L):
    causal = jnp.tril(jnp.ones((L, L), dtype=bool))
    xs = tuple(to_chunks(a, L) for a in (q, k, v, ig, lf))

    def step(carry, inp):
        C, n, m = carry
        qc, kc, vc, ic, fc = inp
        b = jnp.cumsum(fc, axis=1)
        a = ic - b
        m_t = b + jnp.maximum(m[:, None, :], lax.cummax(a, axis=1))
        logw = b[:, :, None, :] + a[:, None, :, :] - m_t[:, :, None, :]
        w = jnp.exp(jnp.where(causal[None, :, :, None], logw, -jnp.inf))
        wqk = w * jnp.einsum('bthd,bshd->btsh', qc, kc)
        inter = jnp.exp(b + m[:, None, :] - m_t)
        num = jnp.einsum('btsh,bshv->bthv', wqk, vc) + inter[..., None] * jnp.einsum('bthd,bhdv->bthv', qc, C)
        den = jnp.sum(wqk, axis=2) + inter * jnp.einsum('bthd,bhd->bth', qc, n)
        h = num / jnp.maximum(jnp.abs(den), jnp.exp(-m_t))[..., None]
        mL = m_t[:, -1]
        bL = b[:, -1]
        ws = jnp.exp(bL[:, None, :] + a - mL[:, None, :])
        decay = jnp.exp(bL + m - mL)
        C_new = decay[..., None, None] * C + jnp.einsum('bsh,bshd,bshv->bhdv', ws, kc, vc)
        n_new = decay[..., None] * n + jnp.einsum('bsh,bshd->bhd', ws, kc)
        return (C_new, n_new, mL), h

    (C, n, m), hs = lax.scan(step, (C0, n0, m0), xs)
    return from_chunks(hs), C, n, m


def hgrn2_scan(q, k, v, lf, S0, L):
    causal = jnp.tril(jnp.ones((L, L), dtype=bool))
    xs = tuple(to_chunks(a, L) for a in (q, k, v, lf))

    def step(S, inp):
        qc, kc, vc, fc = inp
        b = jnp.cumsum(fc, axis=1)
        diff = b[:, :, None] - b[:, None, :]
        dec = jnp.exp(jnp.where(causal[None, :, :, None, None], diff, -jnp.inf))
        att = jnp.einsum('bthd,btshd,bshd->btsh', qc, dec, kc)
        o = jnp.einsum('btsh,bshv->bthv', att, vc) + jnp.einsum('bthd,bhdv->bthv', qc * jnp.exp(b), S)
        bL = b[:, -1]
        kw = kc * jnp.exp(bL[:, None] - b)
        S_new = jnp.exp(bL)[..., None] * S + jnp.einsum('bshd,bshv->bhdv', kw, vc)
        return S_new, o

    S, os_ = lax.scan(step, S0, xs)
    return from_chunks(os_), S


def ssd_scan(x, dt, A, Bh, Ch, h0, L):
    causal = jnp.tril(jnp.ones((L, L), dtype=bool))
    xs = tuple(to_chunks(a, L) for a in (x, dt, Bh, Ch))

    def step(h, inp):
        xc, dtc, Bc, Cc = inp
        cA = jnp.cumsum(dtc * A, axis=1)
        diff = cA[:, :, None] - cA[:, None, :]
        dec = jnp.exp(jnp.where(causal[None, :, :, None], diff, -jnp.inf))
        cb = jnp.einsum('bthn,bshn->btsh', Cc, Bc) * dec
        y = jnp.einsum('btsh,bsh,bshp->bthp', cb, dtc, xc) + jnp.exp(cA)[..., None] * jnp.einsum('bthn,bhpn->bthp', Cc, h)
        cAL = cA[:, -1]
        ws = jnp.exp(cAL[:, None] - cA) * dtc
        h_new = jnp.exp(cAL)[..., None, None] * h + jnp.einsum('bsh,bshp,bshn->bhpn', ws, xc, Bc)
        return h_new, y

    h, ys = lax.scan(step, h0, xs)
    return from_chunks(ys), h


def causal_conv(x, prev, w, b):
    T = x.shape[1]
    xp = jnp.concatenate([prev.astype(x.dtype), x], axis=1)
    y = b.astype(x.dtype)
    for j in range(SSD_CONV):
        y = y + xp[:, j:j + T] * w[j]
    return jax.nn.silu(y), xp[:, T:]


def rope(x, pos):
    half = ROPE_DIMS // 2
    inv_freq = jnp.power(jnp.float32(ROPE_THETA), -jnp.arange(half, dtype=jnp.float32) / half)
    ang = pos.astype(jnp.float32)[:, None] * inv_freq[None, :]
    cos = jnp.cos(ang)[None, :, None, :]
    sin = jnp.sin(ang)[None, :, None, :]
    x1 = x[..., :half]
    x2 = x[..., half:ROPE_DIMS]
    return jnp.concatenate([x1 * cos - x2 * sin, x2 * cos + x1 * sin, x[..., ROPE_DIMS:]], axis=-1)


def chunk_band(a):
    B, T, H, d = a.shape
    nc = T // CHUNK
    nb = WINDOW // CHUNK
    ap = jnp.pad(a, ((0, 0), (WINDOW, 0), (0, 0), (0, 0))).reshape(B, nc + nb, CHUNK, H, d)
    return jnp.concatenate([ap[:, j:j + nc] for j in range(nb + 1)], axis=2)


def swa_core(q, k, v, valid, sinks):
    B, c, Lq, Hq, dh = q.shape
    Hk = k.shape[3]
    G = Hq // Hk
    qg = q.reshape(B, c, Lq, Hk, G, dh)
    s = jnp.einsum('bcqhgd,bcjhd->bchgqj', qg, k) * (dh ** -0.5)
    s = jnp.where(valid[None, :, None, None, None, :], s, -jnp.inf)
    sk = sinks.astype(jnp.float32).reshape(1, 1, Hk, G, 1, 1)
    m = jnp.maximum(jnp.max(s, axis=-1, keepdims=True), sk)
    e = jnp.exp(s - m)
    p = e / (jnp.sum(e, axis=-1, keepdims=True) + jnp.exp(sk - m))
    o = jnp.einsum('bchgqj,bcjhd->bcqhgd', p, v)
    return o.reshape(B, c, Lq, Hq * dh)


def token_mixers(hn, pos, L, st, p):
    B, T, _ = hn.shape
    f32 = jnp.float32
    mC0, mn0, mm0, hS0, kc0, vc0, sh0, conv0 = st
    proj = jnp.einsum('btd,de->bte', hn, p['w_in']).astype(f32)
    (ml_q, ml_k, ml_v, ml_i, ml_f, ml_o, hg_q, hg_f, hg_i, hg_g,
     sw_q, sw_k, sw_v, sd_z, sd_xbc, sd_dt) = jnp.split(proj, np.cumsum(IN_SPLITS)[:-1].tolist(), axis=-1)

    q = ml_q.reshape(B, T, ML_HEADS, ML_DK)
    k = ml_k.reshape(B, T, ML_HEADS, ML_DK) * (ML_DK ** -0.5)
    v = ml_v.reshape(B, T, ML_HEADS, ML_DV)
    ig = ml_i + p['ml_ig_b']
    lf = jax.nn.log_sigmoid(ml_f + p['ml_fg_b'])
    h, mC, mn, mm = mlstm_scan(q, k, v, ig, lf, mC0.astype(f32), mn0.astype(f32), mm0.astype(f32), L)
    out_ml = jax.nn.sigmoid(ml_o) * head_rmsnorm(h, p['ml_norm_w'])

    lb = p['hg_lb'].astype(f32)
    fg = lb + (1.0 - lb) * jax.nn.sigmoid(hg_f)
    q = jax.nn.silu(hg_q).reshape(B, T, HG_HEADS, HG_DK) * (HG_DK ** -0.5)
    kk = (1.0 - fg).reshape(B, T, HG_HEADS, HG_DK)
    lfg = jnp.log(fg).reshape(B, T, HG_HEADS, HG_DK)
    v = hg_i.reshape(B, T, HG_HEADS, HG_DV)
    o, hS = hgrn2_scan(q, kk, v, lfg, hS0.astype(f32), L)
    out_hg = head_rmsnorm(o, p['hg_norm_w']) * jax.nn.silu(hg_g)

    q = rope(sw_q.reshape(B, T, SW_HEADS, SW_DH), pos)
    k = rope(sw_k.reshape(B, T, SW_KV_HEADS, SW_DH), pos)
    v = sw_v.reshape(B, T, SW_KV_HEADS, SW_DH)
    if kc0 is None:
        nc = T // CHUNK
        J = (WINDOW // CHUNK + 1) * CHUNK
        key_pos = jnp.arange(nc)[:, None] * CHUNK - WINDOW + jnp.arange(J)[None, :]
        out_sw = swa_core(q.reshape(B, nc, CHUNK, SW_HEADS, SW_DH), chunk_band(k), chunk_band(v),
                          key_pos >= 0, p['sw_sinks']).reshape(B, T, -1)
        k_new = k[:, -WINDOW:]
        v_new = v[:, -WINDOW:]
    else:
        kcat = jnp.concatenate([kc0.astype(f32), k], axis=1)
        vcat = jnp.concatenate([vc0.astype(f32), v], axis=1)
        valid = jnp.ones((1, kcat.shape[1]), dtype=bool)
        out_sw = swa_core(q[:, None], kcat[:, None], vcat[:, None], valid, p['sw_sinks'])[:, 0]
        k_new = kcat[:, -WINDOW:]
        v_new = vcat[:, -WINDOW:]

    xbc, conv_new = causal_conv(sd_xbc, conv0, p['ssd_conv_w'], p['ssd_conv_b'])
    xs, Bm, Cm = jnp.split(xbc, [SSD_INNER, SSD_INNER + SSD_GROUPS * SSD_N], axis=-1)
    dtv = jax.nn.softplus(sd_dt + p['ssd_dt_bias'])
    A = -jnp.exp(p['ssd_A_log'].astype(f32))
    x4 = xs.reshape(B, T, SSD_HEADS, SSD_P)
    rep = SSD_HEADS // SSD_GROUPS
    Bh = jnp.repeat(Bm.reshape(B, T, SSD_GROUPS, SSD_N), rep, axis=2)
    Ch = jnp.repeat(Cm.reshape(B, T, SSD_GROUPS, SSD_N), rep, axis=2)
    y, sh = ssd_scan(x4, dtv, A, Bh, Ch, sh0.astype(f32), L)
    y = y + p['ssd_D'][:, None] * x4
    gated = (y.reshape(B, T, SSD_INNER) * jax.nn.silu(sd_z)).reshape(B, T, SSD_GROUPS, SSD_INNER // SSD_GROUPS)
    out_sd = head_rmsnorm(gated, p['ssd_norm_w'])

    mixed = jnp.concatenate([out_ml, out_hg, out_sw, out_sd], axis=-1).astype(hn.dtype)
    out = jnp.einsum('bte,ed->btd', mixed, p['w_out'])
    dt = hn.dtype
    new = (mC, mn, mm, hS, k_new, v_new, sh, conv_new)
    return out.astype(dt), tuple(a.astype(dt) for a in new)


def swiglu(x, wg, wu, wd):
    h = jax.nn.silu(jnp.einsum('btd,df->btf', x, wg)) * jnp.einsum('btd,df->btf', x, wu)
    return jnp.einsum('btf,fd->btd', h, wd)


def moe_swiglu(x, router, wg, wu, wd):
    logits = jnp.einsum('btd,de->bte', x, router).astype(jnp.float32)
    top_val, top_idx = lax.top_k(logits, TOP_K)
    gate = jax.nn.softmax(top_val, axis=-1)
    comb = jnp.einsum('btk,btke->bte', gate, jax.nn.one_hot(top_idx, N_EXPERTS, dtype=jnp.float32))
    y = jnp.zeros(x.shape, jnp.float32)
    for e in range(N_EXPERTS):
        y = y + comb[..., e:e + 1] * swiglu(x, wg[e], wu[e], wd[e]).astype(jnp.float32)
    return y.astype(x.dtype)


def trunk(x, pos, L, states, P):
    new = [[] for _ in states]
    for l in range(DEPTH):
        p = {name: P[name][l] for name in MIXER_KEYS}
        st = tuple(None if s is None else s[l] for s in states)
        out, nst = token_mixers(rmsnorm(x, P['norm1_w'][l]), pos, L, st, p)
        x = x + out
        hn = rmsnorm(x, P['norm2_w'][l])
        j = l // 2
        if l % 2 == 0:
            x = x + swiglu(hn, P['ffn_w_gate'][j], P['ffn_w_up'][j], P['ffn_w_down'][j])
        else:
            x = x + moe_swiglu(hn, P['moe_router'][j], P['moe_w_gate'][j], P['moe_w_up'][j], P['moe_w_down'][j])
        for acc, s in zip(new, nst):
            acc.append(s)
    return rmsnorm(x, P['final_norm_w']), [jnp.stack(acc) for acc in new]


def setup_inputs(seed: int = 0) -> dict:
    key = jax.random.key(seed)
    ks = iter(jax.random.split(key, 48))

    def nrm(shape, scale):
        return jax.random.normal(next(ks), shape, jnp.float32) * scale

    def uni(shape, lo, hi):
        return jax.random.uniform(next(ks), shape, jnp.float32, lo, hi)

    dt0 = jnp.exp(uni((DEPTH, SSD_HEADS), float(np.log(1e-3)), float(np.log(1e-1))))
    return {
        'x_prompt': nrm((BATCH, SEQ, D_MODEL), 1.0),
        'x_sample': nrm((DEC_BATCH, DEC_SEQ, D_MODEL), 1.0),
        'state_mlstm_C': nrm((DEPTH, DEC_BATCH, ML_HEADS, ML_DK, ML_DV), 0.5),
        'state_mlstm_n': jnp.abs(nrm((DEPTH, DEC_BATCH, ML_HEADS, ML_DK), 0.5)),
        'state_mlstm_m': nrm((DEPTH, DEC_BATCH, ML_HEADS), 1.0),
        'state_hgrn_S': nrm((DEPTH, DEC_BATCH, HG_HEADS, HG_DK, HG_DV), 0.5),
        'cache_swa_k': nrm((DEPTH, DEC_BATCH, WINDOW, SW_KV_HEADS, SW_DH), 1.0),
        'cache_swa_v': nrm((DEPTH, DEC_BATCH, WINDOW, SW_KV_HEADS, SW_DH), 1.0),
        'state_ssd_h': nrm((DEPTH, DEC_BATCH, SSD_HEADS, SSD_P, SSD_N), 0.5),
        'state_ssd_conv': nrm((DEPTH, DEC_BATCH, SSD_CONV - 1, SSD_CONV_DIM), 1.0),
        'norm1_w': 1.0 + nrm((DEPTH, D_MODEL), 0.02),
        'w_in': nrm((DEPTH, D_MODEL, D_IN), D_MODEL ** -0.5),
        'ml_ig_b': nrm((DEPTH, ML_HEADS), 0.1),
        'ml_fg_b': jnp.linspace(3.0, 6.0, ML_HEADS, dtype=jnp.float32)[None, :] + nrm((DEPTH, ML_HEADS), 0.1),
        'ml_norm_w': 1.0 + nrm((DEPTH, ML_HEADS * ML_DV), 0.02),
        'hg_lb_logits': nrm((DEPTH, HG_HEADS * HG_DK), 0.1),
        'hg_norm_w': 1.0 + nrm((DEPTH, HG_HEADS * HG_DV), 0.02),
        'sw_sinks': nrm((DEPTH, SW_HEADS), 0.5),
        'ssd_conv_w': nrm((DEPTH, SSD_CONV, SSD_CONV_DIM), SSD_CONV ** -0.5),
        'ssd_conv_b': nrm((DEPTH, SSD_CONV_DIM), 0.02),
        'ssd_dt_bias': dt0 + jnp.log(-jnp.expm1(-dt0)),
        'ssd_A_log': jnp.log(uni((DEPTH, SSD_HEADS), 1.0, 16.0)),
        'ssd_D': 1.0 + nrm((DEPTH, SSD_HEADS), 0.02),
        'ssd_norm_w': 1.0 + nrm((DEPTH, SSD_INNER), 0.02),
        'w_out': nrm((DEPTH, D_MIX, D_MODEL), D_MIX ** -0.5),
        'norm2_w': 1.0 + nrm((DEPTH, D_MODEL), 0.02),
        'ffn_w_gate': nrm((N_DENSE, D_MODEL, D_FF), D_MODEL ** -0.5),
        'ffn_w_up': nrm((N_DENSE, D_MODEL, D_FF), D_MODEL ** -0.5),
        'ffn_w_down': nrm((N_DENSE, D_FF, D_MODEL), D_FF ** -0.5),
        'moe_router': nrm((N_MOE, D_MODEL, N_EXPERTS), D_MODEL ** -0.5),
        'moe_w_gate': nrm((N_MOE, N_EXPERTS, D_MODEL, D_FF_EXPERT), D_MODEL ** -0.5),
        'moe_w_up': nrm((N_MOE, N_EXPERTS, D_MODEL, D_FF_EXPERT), D_MODEL ** -0.5),
        'moe_w_down': nrm((N_MOE, N_EXPERTS, D_FF_EXPERT, D_MODEL), D_FF_EXPERT ** -0.5),
        'final_norm_w': 1.0 + nrm((D_MODEL,), 0.02),
    }


def reference(x_prompt, x_sample, state_mlstm_C, state_mlstm_n, state_mlstm_m, state_hgrn_S,
              cache_swa_k, cache_swa_v, state_ssd_h, state_ssd_conv,
              norm1_w, w_in, ml_ig_b, ml_fg_b, ml_norm_w, hg_lb_logits, hg_norm_w, sw_sinks,
              ssd_conv_w, ssd_conv_b, ssd_dt_bias, ssd_A_log, ssd_D, ssd_norm_w, w_out, norm2_w,
              ffn_w_gate, ffn_w_up, ffn_w_down, moe_router, moe_w_gate, moe_w_up, moe_w_down, final_norm_w):
    f32 = jnp.float32
    sm = jax.nn.softmax(hg_lb_logits.astype(f32), axis=0)
    hg_lb = jnp.cumsum(sm, axis=0) - sm[0]
    P = {'norm1_w': norm1_w, 'w_in': w_in, 'ml_ig_b': ml_ig_b, 'ml_fg_b': ml_fg_b, 'ml_norm_w': ml_norm_w,
         'hg_lb': hg_lb, 'hg_norm_w': hg_norm_w, 'sw_sinks': sw_sinks, 'ssd_conv_w': ssd_conv_w,
         'ssd_conv_b': ssd_conv_b, 'ssd_dt_bias': ssd_dt_bias, 'ssd_A_log': ssd_A_log, 'ssd_D': ssd_D,
         'ssd_norm_w': ssd_norm_w, 'w_out': w_out, 'norm2_w': norm2_w, 'ffn_w_gate': ffn_w_gate,
         'ffn_w_up': ffn_w_up, 'ffn_w_down': ffn_w_down, 'moe_router': moe_router, 'moe_w_gate': moe_w_gate,
         'moe_w_up': moe_w_up, 'moe_w_down': moe_w_down, 'final_norm_w': final_norm_w}

    B, T = x_prompt.shape[:2]
    zero_states = (jnp.zeros((DEPTH, B, ML_HEADS, ML_DK, ML_DV), f32),
                   jnp.zeros((DEPTH, B, ML_HEADS, ML_DK), f32),
                   jnp.zeros((DEPTH, B, ML_HEADS), f32),
                   jnp.zeros((DEPTH, B, HG_HEADS, HG_DK, HG_DV), f32),
                   None, None,
                   jnp.zeros((DEPTH, B, SSD_HEADS, SSD_P, SSD_N), f32),
                   jnp.zeros((DEPTH, B, SSD_CONV - 1, SSD_CONV_DIM), f32))
    y_prompt, ps = trunk(x_prompt, jnp.arange(T), CHUNK, zero_states, P)
    p_mC, p_mn, p_mm, p_hS, p_k, p_v, p_sh, p_conv = ps

    Ts = x_sample.shape[1]
    cache_states = (state_mlstm_C, state_mlstm_n, state_mlstm_m, state_hgrn_S,
                    cache_swa_k, cache_swa_v, state_ssd_h, state_ssd_conv)
    y_sample, ss = trunk(x_sample, PAST_LEN + jnp.arange(Ts), Ts, cache_states, P)
    s_mC, s_mn, s_mm, s_hS, s_k, s_v, s_sh, s_conv = ss

    return (y_prompt, y_sample,
            p_mC, p_mn, p_mm, p_hS, p_k, p_v, p_sh, p_conv,
            s_mC, s_mn, s_mm, s_hS, s_k, s_v, s_sh, s_conv)
```

```python
import functools

import numpy as np
import jax
import jax.numpy as jnp
from jax import lax
from jax.experimental import pallas as pl
from jax.experimental.pallas import tpu as pltpu

F32 = jnp.float32
BF16 = jnp.bfloat16

D_MODEL = 1024
CHUNK = 64
EPS = 1e-6
N_HEADS = 4
HEAD = 64
MIX_W = N_HEADS * HEAD
SW_KV = 2
WINDOW = 128
ROPE_DIMS = 16
ROPE_THETA = 500000.0
SSD_N = 128
SSD_CONV = 4
SSD_CONV_DIM = 768
PAST_LEN = 4096
N_EXPERTS = 8
SUB = 16

C_MLQ, C_MLK, C_MLV, C_MLO = 0, 256, 512, 768
C_HGQ, C_HGF, C_HGI, C_HGG = 1024, 1280, 1536, 1792
C_SWQ, C_SWK, C_SWV = 2048, 2304, 2432
C_SDZ, C_SDX, C_GATE = 2560, 2816, 3584
D_PROJ = 3712
NEG = -1e30
VMEM_LIMIT = 56 * 1024 * 1024


def _mm(a, b):
    return jnp.dot(a.astype(BF16), b.astype(BF16), preferred_element_type=F32)


def _mm_nt(a, b):
    return lax.dot_general(a.astype(BF16), b.astype(BF16), (((1,), (1,)), ((), ())),
                           preferred_element_type=F32)


def _mm_tn(a, b):
    return jnp.dot(a.T.astype(BF16), b.astype(BF16), preferred_element_type=F32)


def _mm_sel(sel, x):
    hi = x.astype(BF16)
    r1 = x - hi.astype(F32)
    mid = r1.astype(BF16)
    lo = (r1 - mid.astype(F32)).astype(BF16)
    return (jnp.dot(sel, hi, preferred_element_type=F32)
            + jnp.dot(sel, mid, preferred_element_type=F32)
            + jnp.dot(sel, lo, preferred_element_type=F32))


def _sigmoid(x):
    return 1.0 / (1.0 + jnp.exp(-x))


def _silu(x):
    return x * _sigmoid(x)


def _softplus(x):
    return jnp.maximum(x, 0.0) + jnp.log(1.0 + jnp.exp(-jnp.abs(x)))


def _rms(x, w):
    return x * lax.rsqrt(jnp.mean(x * x, axis=-1, keepdims=True) + EPS) * w


def _pick_tile(n, candidates):
    for c in candidates:
        if n % c == 0:
            return c
    raise ValueError(f"no tile for {n}")


def _inproj_kernel(x_ref, nw_ref, w_ref, o_ref):
    xn = _rms(x_ref[...], nw_ref[...]).astype(BF16)
    o_ref[...] = jnp.dot(xn, w_ref[...], preferred_element_type=F32)


def _inproj(x, nw, w):
    n = x.shape[0]
    tm = _pick_tile(n, (512, 256, 128, 64))
    return pl.pallas_call(
        _inproj_kernel,
        grid=(n // tm,),
        in_specs=[pl.BlockSpec((tm, D_MODEL), lambda i: (i, 0)),
                  pl.BlockSpec((1, D_MODEL), lambda i: (0, 0)),
                  pl.BlockSpec((D_MODEL, D_PROJ), lambda i: (0, 0))],
        out_specs=pl.BlockSpec((tm, D_PROJ), lambda i: (i, 0)),
        out_shape=jax.ShapeDtypeStruct((n, D_PROJ), F32),
        compiler_params=pltpu.CompilerParams(dimension_semantics=("arbitrary",),
                                             vmem_limit_bytes=VMEM_LIMIT),
        name="inproj",
    )(x, nw, w)


def _head_norm(x, width):
    parts = []
    for g in range(x.shape[1] // width):
        xg = x[:, g * width:(g + 1) * width]
        parts.append(xg * lax.rsqrt(jnp.mean(xg * xg, axis=-1, keepdims=True) + EPS))
    return jnp.concatenate(parts, axis=1)


def _mixer_kernel(n_prompt_steps, chunks_per_prompt,
                  sinks_ref,
                  proj_ref, rope_ref, c0_ref, m0_ref, s0_ref, k0_ref, v0_ref, h0_ref, cv0_ref,
                  gbias_ref, alog_ref, mlnw_ref, hglb_ref, hgnw_ref,
                  convw_ref, convb_ref, ssdd_ref, ssdnw_ref,
                  tri_ref, triblk_ref, blkones_ref, ind_ref,
                  mixed_ref, c_ref, m_ref, s_ref, k_ref, v_ref, h_ref, cv_ref,
                  convbuf, bloc_s, kk_s, hv_s):
    i = pl.program_id(0)
    is_prompt = i < n_prompt_steps
    chunk = lax.rem(i, chunks_per_prompt)
    is_first = jnp.logical_or(jnp.logical_not(is_prompt), chunk == 0)
    n_valid = jnp.where(is_prompt, jnp.minimum(chunk * CHUNK, WINDOW), WINDOW)

    @pl.when(is_first)
    def _():
        c_ref[...] = c0_ref[...]
        m_ref[...] = m0_ref[...]
        s_ref[...] = s0_ref[...]
        k_ref[...] = k0_ref[...]
        v_ref[...] = v0_ref[...]
        h_ref[...] = h0_ref[...]
        cv_ref[...] = cv0_ref[...]

    L = CHUNK
    row = lax.broadcasted_iota(jnp.int32, (L, L), 0)
    col = lax.broadcasted_iota(jnp.int32, (L, L), 1)
    causal = row >= col
    tri = tri_ref[...]

    gate = proj_ref[:, C_GATE:C_GATE + 128] + gbias_ref[...]
    ig = gate
    lf = pltpu.roll(-_softplus(-gate), 124, 1)
    dtv = pltpu.roll(_softplus(gate), 120, 1)

    b = _mm_sel(tri, lf)
    a = ig - b
    cm = a
    rowg = lax.broadcasted_iota(jnp.int32, (L, 128), 0)
    for sh in (1, 2, 4, 8, 16, 32):
        cm = jnp.where(rowg >= sh, jnp.maximum(cm, pltpu.roll(cm, sh, 0)), cm)
    m_prev = m_ref[0, 0:1, :]
    m_t = b + jnp.maximum(m_prev, cm)
    inter = jnp.exp(b + m_prev - m_t)
    bm = b - m_t
    a_t = a.T
    m_last = m_t[L - 1:L, :]
    b_last = b[L - 1:L, :]
    ws = jnp.exp(b_last + a - m_last)
    decay = jnp.exp(b_last + m_prev - m_last)
    emt = jnp.exp(-m_t)
    m_ref[0, 0:1, :] = m_last

    ones_h = jnp.ones((L, HEAD), F32)
    ml_out = []
    for h in range(N_HEADS):
        hs = slice(h * HEAD, (h + 1) * HEAD)
        qh = proj_ref[:, C_MLQ + h * HEAD:C_MLQ + (h + 1) * HEAD]
        kh = proj_ref[:, C_MLK + h * HEAD:C_MLK + (h + 1) * HEAD] * (HEAD ** -0.5)
        vh = proj_ref[:, C_MLV + h * HEAD:C_MLV + (h + 1) * HEAD]
        qk = _mm_nt(qh, kh)
        w = jnp.where(causal, jnp.exp(bm[:, h:h + 1] + a_t[h:h + 1, :]), 0.0)
        wqk = w * qk
        v_aug = jnp.concatenate([vh, ones_h], axis=1)
        c_aug = c_ref[0, h]
        nd = _mm(wqk, v_aug) + inter[:, h:h + 1] * _mm(qh, c_aug)
        den = jnp.maximum(jnp.abs(nd[:, HEAD:HEAD + 1]), emt[:, h:h + 1])
        ml_out.append(nd[:, :HEAD] / den)
        c_ref[0, h] = decay[:, h:h + 1] * c_aug + _mm_tn(kh * ws[:, h:h + 1], v_aug)
    ml = _head_norm(jnp.concatenate(ml_out, axis=1), HEAD)
    mixed_ref[:, 0:MIX_W] = (_sigmoid(proj_ref[:, C_MLO:C_MLO + MIX_W]) * ml
                             * mlnw_ref[...]).astype(mixed_ref.dtype)

    lb = hglb_ref[...]
    fg = lb + (1.0 - lb) * _sigmoid(proj_ref[:, C_HGF:C_HGF + MIX_W])
    hq = _silu(proj_ref[:, C_HGQ:C_HGQ + MIX_W]) * (HEAD ** -0.5)
    kk = 1.0 - fg
    lfg = jnp.log(fg)
    hv = proj_ref[:, C_HGI:C_HGI + MIX_W]
    bloc = _mm_sel(triblk_ref[...], lfg)
    blast = _mm_sel(blkones_ref[...], lfg)
    qe = hq * jnp.exp(bloc)
    kw = kk * jnp.exp(blast - bloc)
    bloc_s[...] = bloc
    kk_s[...] = kk
    hv_s[...] = hv
    ind = ind_ref[...]
    rows = lax.broadcasted_iota(jnp.int32, (SUB, MIX_W), 0)
    hg_rows = []
    for blk in range(L // SUB):
        r0 = blk * SUB
        st = s_ref[0]
        o_int = jnp.concatenate(
            [_mm_nt(qe[r0:r0 + SUB, h * HEAD:(h + 1) * HEAD], st[:, h * HEAD:(h + 1) * HEAD])
             for h in range(N_HEADS)], axis=1)
        q_blk = hq[r0:r0 + SUB]
        b_blk = bloc[r0:r0 + SUB]
        xs = []
        for s in range(SUB):
            r = r0 + s
            e = jnp.exp(jnp.minimum(b_blk - bloc_s[r:r + 1, :], 0.0))
            xs.append((q_blk * (kk_s[r:r + 1, :] * e)).astype(BF16))
        att = jnp.dot(jnp.concatenate(xs, axis=0), ind, preferred_element_type=F32)
        o_blk = o_int
        for s in range(SUB):
            r = r0 + s
            o_blk = o_blk + jnp.where(rows >= s, att[s * SUB:(s + 1) * SUB], 0.0) * hv_s[r:r + 1, :]
        hg_rows.append(o_blk)
        upd = jnp.concatenate(
            [_mm_tn(hv[r0:r0 + SUB, h * HEAD:(h + 1) * HEAD], kw[r0:r0 + SUB, h * HEAD:(h + 1) * HEAD])
             for h in range(N_HEADS)], axis=1)
        s_ref[0] = st * jnp.exp(blast[r0:r0 + 1, :]) + upd
    hg = _head_norm(jnp.concatenate(hg_rows, axis=0), HEAD)
    mixed_ref[:, MIX_W:2 * MIX_W] = (hg * hgnw_ref[...]
                                     * _silu(proj_ref[:, C_HGG:C_HGG + MIX_W])).astype(mixed_ref.dtype)

    cos = rope_ref[:, 0:128]
    sin_a = rope_ref[:, 128:256]
    sin_b = rope_ref[:, 256:384]

    def rope(x):
        return x * cos + pltpu.roll(x, 120, 1) * sin_a + pltpu.roll(x, 8, 1) * sin_b

    q_rot = [rope(proj_ref[:, C_SWQ:C_SWQ + 128]), rope(proj_ref[:, C_SWQ + 128:C_SWQ + 256])]
    k_rot = rope(proj_ref[:, C_SWK:C_SWK + 128])
    v_cur = proj_ref[:, C_SWV:C_SWV + 128]
    k_prev = k_ref[0]
    v_prev = v_ref[0]
    colw = lax.broadcasted_iota(jnp.int32, (2 * L, WINDOW), 1)
    row2 = lax.broadcasted_iota(jnp.int32, (2 * L, 1), 0)
    prev_ok = colw >= WINDOW - n_valid
    sw_out = []
    for g in range(SW_KV):
        gs = slice(g * HEAD, (g + 1) * HEAD)
        q2 = jnp.concatenate([q_rot[g][:, 0:HEAD], q_rot[g][:, HEAD:2 * HEAD]], axis=0) * (HEAD ** -0.5)
        s_p = jnp.where(prev_ok, _mm_nt(q2, k_prev[:, gs]), NEG)
        s_c = _mm_nt(q2, k_rot[:, gs])
        sink = jnp.where(row2 < L, sinks_ref[2 * g], sinks_ref[2 * g + 1])
        mx = jnp.maximum(jnp.maximum(jnp.max(s_p, axis=-1, keepdims=True),
                                     jnp.max(s_c, axis=-1, keepdims=True)), sink)
        e_p = jnp.exp(s_p - mx)
        e_c = jnp.exp(s_c - mx)
        den = (jnp.sum(e_p, axis=-1, keepdims=True) + jnp.sum(e_c, axis=-1, keepdims=True)
               + jnp.exp(sink - mx))
        o = (_mm(e_p, v_prev[:, gs]) + _mm(e_c, v_cur[:, gs])) / den
        sw_out.append(o[0:L])
        sw_out.append(o[L:2 * L])
    mixed_ref[:, 2 * MIX_W:3 * MIX_W] = jnp.concatenate(sw_out, axis=1).astype(mixed_ref.dtype)
    k_ref[0, 0:WINDOW - L, :] = k_prev[L:WINDOW]
    k_ref[0, WINDOW - L:WINDOW, :] = k_rot
    v_ref[0, 0:WINDOW - L, :] = v_prev[L:WINDOW]
    v_ref[0, WINDOW - L:WINDOW, :] = v_cur

    convbuf[0:8, :] = cv_ref[0]
    convbuf[8:8 + L, :] = proj_ref[:, C_SDX:C_SDX + SSD_CONV_DIM]
    acc = convb_ref[...] + convbuf[5:5 + L, :] * convw_ref[0:1, :]
    for j in range(1, SSD_CONV):
        acc = acc + convbuf[5 + j:5 + j + L, :] * convw_ref[j:j + 1, :]
    xbc = _silu(acc)
    cv_ref[0] = convbuf[L:L + 8, :]
    xs_all = xbc[:, 0:MIX_W]
    a_neg = -jnp.exp(alog_ref[...])
    ca = _mm_sel(tri, dtv * a_neg)
    ca_t = ca.T
    dt_t = dtv.T
    eca = jnp.exp(ca)
    ca_last = ca[L - 1:L, :]
    wss = jnp.exp(ca_last - ca) * dtv
    sdec = jnp.exp(ca_last)
    sd_out = []
    for g in range(2):
        bm_g = xbc[:, MIX_W + g * SSD_N:MIX_W + (g + 1) * SSD_N]
        cm_g = xbc[:, MIX_W + 2 * SSD_N + g * SSD_N:MIX_W + 2 * SSD_N + (g + 1) * SSD_N]
        cb = _mm_nt(cm_g, bm_g)
        for h in (2 * g, 2 * g + 1):
            xh = xs_all[:, h * HEAD:(h + 1) * HEAD]
            dec = jnp.where(causal, jnp.exp(ca[:, h:h + 1] - ca_t[h:h + 1, :]), 0.0)
            mmat = cb * dec * dt_t[h:h + 1, :]
            hst = h_ref[0, h]
            sd_out.append(_mm(mmat, xh) + eca[:, h:h + 1] * _mm_nt(cm_g, hst))
            h_ref[0, h] = sdec[:, h:h + 1] * hst + _mm_tn(xh * wss[:, h:h + 1], bm_g)
    y = jnp.concatenate(sd_out, axis=1) + ssdd_ref[...] * xs_all
    gated = y * _silu(proj_ref[:, C_SDZ:C_SDZ + MIX_W])
    mixed_ref[:, 3 * MIX_W:4 * MIX_W] = (_head_norm(gated, 2 * HEAD)
                                         * ssdnw_ref[...]).astype(mixed_ref.dtype)


def _mixer(proj, rope_tab, init, lp, consts, n_prompt_seq, chunks_per_prompt, n_seq):
    n = proj.shape[0]
    n_steps = n // CHUNK
    n_prompt_steps = n_prompt_seq * chunks_per_prompt

    def seq_of(i):
        return jnp.where(i < n_prompt_steps, i // chunks_per_prompt, i - n_prompt_steps + n_prompt_seq)

    def rope_blk(i):
        return jnp.where(i < n_prompt_steps, lax.rem(i, chunks_per_prompt), chunks_per_prompt)

    def per_seq(shape):
        nd = len(shape)
        return pl.BlockSpec((1,) + shape, lambda i, s: (seq_of(i),) + (0,) * nd)

    def const(shape):
        nd = len(shape)
        return pl.BlockSpec(shape, lambda i, s: (0,) * nd)

    state_shapes = [(N_HEADS, HEAD, 128), (8, 128), (HEAD, MIX_W), (WINDOW, 128), (WINDOW, 128),
                    (N_HEADS, HEAD, SSD_N), (8, SSD_CONV_DIM)]
    in_specs = ([pl.BlockSpec((CHUNK, D_PROJ), lambda i, s: (i, 0)),
                 pl.BlockSpec((CHUNK, 384), lambda i, s: (rope_blk(i), 0))]
                + [per_seq(s) for s in state_shapes]
                + [const((1, 128)), const((1, 128)), const((1, MIX_W)), const((1, MIX_W)), const((1, MIX_W)),
                   const((SSD_CONV, SSD_CONV_DIM)), const((1, SSD_CONV_DIM)), const((1, MIX_W)),
                   const((1, MIX_W)),
                   const((CHUNK, CHUNK)), const((CHUNK, CHUNK)), const((CHUNK, CHUNK)), const((MIX_W, MIX_W))])
    out_specs = ([pl.BlockSpec((CHUNK, D_MODEL), lambda i, s: (i, 0))]
                 + [per_seq(s) for s in state_shapes])
    out_shape = ([jax.ShapeDtypeStruct((n, D_MODEL), BF16)]
                 + [jax.ShapeDtypeStruct((n_seq,) + s, F32) for s in state_shapes])
    grid_spec = pltpu.PrefetchScalarGridSpec(
        num_scalar_prefetch=1, grid=(n_steps,), in_specs=in_specs, out_specs=out_specs,
        scratch_shapes=[pltpu.VMEM((CHUNK + 8, SSD_CONV_DIM), F32),
                        pltpu.VMEM((CHUNK, MIX_W), F32), pltpu.VMEM((CHUNK, MIX_W), F32),
                        pltpu.VMEM((CHUNK, MIX_W), F32)])
    return pl.pallas_call(
        functools.partial(_mixer_kernel, n_prompt_steps, chunks_per_prompt),
        grid_spec=grid_spec, out_shape=out_shape,
        compiler_params=pltpu.CompilerParams(dimension_semantics=("arbitrary",),
                                             vmem_limit_bytes=VMEM_LIMIT),
        name="mixer",
    )(lp["sinks"], proj, rope_tab, *init,
      lp["gbias"], lp["alog"], lp["mlnw"], lp["hglb"], lp["hgnw"],
      lp["convw"], lp["convb"], lp["ssdd"], lp["ssdnw"], *consts)


def _dense_ffn_kernel(f_chunk, x_ref, mix_ref, wo_ref, nw_ref, wg_ref, wu_ref, wd_ref, o_ref):
    x1 = x_ref[...] + jnp.dot(mix_ref[...], wo_ref[...], preferred_element_type=F32)
    hn = _rms(x1, nw_ref[...]).astype(BF16)
    o_ref[...] = x1
    for f0 in range(0, wg_ref.shape[1], f_chunk):
        g = jnp.dot(hn, wg_ref[:, f0:f0 + f_chunk], preferred_element_type=F32)
        u = jnp.dot(hn, wu_ref[:, f0:f0 + f_chunk], preferred_element_type=F32)
        act = (_silu(g) * u).astype(BF16)
        o_ref[...] += jnp.dot(act, wd_ref[f0:f0 + f_chunk, :], preferred_element_type=F32)


def _dense_ffn(x, mixed, wo, nw, wg, wu, wd):
    n = x.shape[0]
    d_ff = wg.shape[1]
    tm = _pick_tile(n, (512, 256, 128, 64))
    full = lambda shape: pl.BlockSpec(shape, lambda i: (0, 0))
    return pl.pallas_call(
        functools.partial(_dense_ffn_kernel, 256),
        grid=(n // tm,),
        in_specs=[pl.BlockSpec((tm, D_MODEL), lambda i: (i, 0)),
                  pl.BlockSpec((tm, D_MODEL), lambda i: (i, 0)),
                  full((D_MODEL, D_MODEL)), full((1, D_MODEL)),
                  full((D_MODEL, d_ff)), full((D_MODEL, d_ff)), full((d_ff, D_MODEL))],
        out_specs=pl.BlockSpec((tm, D_MODEL), lambda i: (i, 0)),
        out_shape=jax.ShapeDtypeStruct((n, D_MODEL), F32),
        compiler_params=pltpu.CompilerParams(dimension_semantics=("arbitrary",),
                                             vmem_limit_bytes=VMEM_LIMIT),
        name="dense_ffn",
    )(x, mixed, wo, nw, wg, wu, wd)


def _router_kernel(x_ref, mix_ref, wo_ref, nw_ref, wr_ref, x1_ref, hn_ref, comb_ref):
    x1 = x_ref[...] + jnp.dot(mix_ref[...], wo_ref[...], preferred_element_type=F32)
    x1_ref[...] = x1
    hn = _rms(x1, nw_ref[...]).astype(BF16)
    hn_ref[...] = hn
    logits = jnp.dot(hn, wr_ref[...], preferred_element_type=F32)
    lane = lax.broadcasted_iota(jnp.int32, logits.shape, 1)
    l1 = jnp.where(lane < N_EXPERTS, logits, NEG)
    m1 = jnp.max(l1, axis=-1, keepdims=True)
    i1 = jnp.min(jnp.where(l1 == m1, lane, 128), axis=-1, keepdims=True)
    l2 = jnp.where(lane == i1, NEG, l1)
    m2 = jnp.max(l2, axis=-1, keepdims=True)
    i2 = jnp.min(jnp.where(l2 == m2, lane, 128), axis=-1, keepdims=True)
    e2 = jnp.exp(m2 - m1)
    g1 = 1.0 / (1.0 + e2)
    g2 = e2 / (1.0 + e2)
    comb_ref[...] = jnp.where(lane == i1, g1, 0.0) + jnp.where(lane == i2, g2, 0.0)


def _router(x, mixed, wo, nw, wr):
    n = x.shape[0]
    tm = _pick_tile(n, (512, 256, 128, 64))
    full = lambda shape: pl.BlockSpec(shape, lambda i: (0, 0))
    tok = lambda w: pl.BlockSpec((tm, w), lambda i: (i, 0))
    return pl.pallas_call(
        _router_kernel,
        grid=(n // tm,),
        in_specs=[tok(D_MODEL), tok(D_MODEL), full((D_MODEL, D_MODEL)), full((1, D_MODEL)),
                  full((D_MODEL, 128))],
        out_specs=[tok(D_MODEL), tok(D_MODEL), tok(128)],
        out_shape=[jax.ShapeDtypeStruct((n, D_MODEL), F32), jax.ShapeDtypeStruct((n, D_MODEL), BF16),
                   jax.ShapeDtypeStruct((n, 128), F32)],
        compiler_params=pltpu.CompilerParams(dimension_semantics=("arbitrary",),
                                             vmem_limit_bytes=VMEM_LIMIT),
        name="router",
    )(x, mixed, wo, nw, wr)


def _moe_kernel(x1_ref, hn_ref, comb_ref, wg_ref, wu_ref, wd_ref, fnw_ref, o_ref, acc_ref):
    e = pl.program_id(1)
    f = pl.program_id(2)

    @pl.when(jnp.logical_and(e == 0, f == 0))
    def _():
        acc_ref[...] = jnp.zeros_like(acc_ref)

    hn = hn_ref[...]
    comb = comb_ref[...]
    lane = lax.broadcasted_iota(jnp.int32, comb.shape, 1)
    gate = jnp.sum(jnp.where(lane == e, comb, 0.0), axis=-1, keepdims=True)
    g = jnp.dot(hn, wg_ref[0], preferred_element_type=F32)
    u = jnp.dot(hn, wu_ref[0], preferred_element_type=F32)
    act = (_silu(g) * u * gate).astype(BF16)
    acc_ref[...] += jnp.dot(act, wd_ref[0], preferred_element_type=F32)

    @pl.when(jnp.logical_and(e == pl.num_programs(1) - 1, f == pl.num_programs(2) - 1))
    def _():
        o_ref[...] = _rms(x1_ref[...] + acc_ref[...], fnw_ref[...])


def _moe(x1, hn, comb, wg, wu, wd, fnw):
    n = x1.shape[0]
    d_ff = wg.shape[2]
    tm = _pick_tile(n, (1024, 512, 256, 128, 64))
    tf = _pick_tile(d_ff, (512, 256, 128))
    tok = lambda w: pl.BlockSpec((tm, w), lambda i, e, f: (i, 0))
    return pl.pallas_call(
        _moe_kernel,
        grid=(n // tm, N_EXPERTS, d_ff // tf),
        in_specs=[tok(D_MODEL), tok(D_MODEL), tok(128),
                  pl.BlockSpec((1, D_MODEL, tf), lambda i, e, f: (e, 0, f)),
                  pl.BlockSpec((1, D_MODEL, tf), lambda i, e, f: (e, 0, f)),
                  pl.BlockSpec((1, tf, D_MODEL), lambda i, e, f: (e, f, 0)),
                  pl.BlockSpec((1, D_MODEL), lambda i, e, f: (0, 0))],
        out_specs=tok(D_MODEL),
        out_shape=jax.ShapeDtypeStruct((n, D_MODEL), F32),
        scratch_shapes=[pltpu.VMEM((tm, D_MODEL), F32)],
        compiler_params=pltpu.CompilerParams(dimension_semantics=("arbitrary", "arbitrary", "arbitrary"),
                                             vmem_limit_bytes=VMEM_LIMIT),
        name="moe",
    )(x1, hn, comb, wg, wu, wd, fnw)


def _rope_table(seq_len, dec_len):
    half = ROPE_DIMS // 2
    pos = jnp.concatenate([jnp.arange(seq_len), PAST_LEN + jnp.arange(dec_len)]).astype(F32)
    inv_freq = jnp.power(jnp.float32(ROPE_THETA), -jnp.arange(half, dtype=F32) / half)
    ang = pos[:, None] * inv_freq[None, :]
    cos, sin = jnp.cos(ang), jnp.sin(ang)
    p = pos.shape[0]
    ones = jnp.ones((p, HEAD - ROPE_DIMS), F32)
    zeros = jnp.zeros((p, HEAD - ROPE_DIMS), F32)
    zh = jnp.zeros((p, half), F32)
    cos_h = jnp.concatenate([cos, cos, ones], axis=1)
    sin_a = jnp.concatenate([-sin, zh, zeros], axis=1)
    sin_b = jnp.concatenate([zh, sin, zeros], axis=1)
    return jnp.concatenate([cos_h, cos_h, sin_a, sin_a, sin_b, sin_b], axis=1)


def _mixer_consts():
    t = np.arange(CHUNK)
    tri = (t[:, None] >= t[None, :])
    same = (t[:, None] // SUB) == (t[None, :] // SUB)
    c = np.arange(MIX_W)
    ind = (c[:, None] // HEAD) == (c[None, :] // HEAD)
    as_bf = lambda m: jnp.asarray(m.astype(np.float32), dtype=BF16)
    return [as_bf(tri), as_bf(tri & same), as_bf(same), as_bf(ind)]


def _pad_lanes(v, width):
    return jnp.pad(v, ((0, 0), (0, width - v.shape[1])))


def kernel(x_prompt, x_sample, state_mlstm_C, state_mlstm_n, state_mlstm_m, state_hgrn_S, cache_swa_k, cache_swa_v, state_ssd_h, state_ssd_conv, norm1_w, w_in, ml_ig_b, ml_fg_b, ml_norm_w, hg_lb_logits, hg_norm_w, sw_sinks, ssd_conv_w, ssd_conv_b, ssd_dt_bias, ssd_A_log, ssd_D, ssd_norm_w, w_out, norm2_w, ffn_w_gate, ffn_w_up, ffn_w_down, moe_router, moe_w_gate, moe_w_up, moe_w_down, final_norm_w):
    depth = w_in.shape[0]
    bp, seq_len, _ = x_prompt.shape
    bs, dec_len, _ = x_sample.shape
    assert seq_len % CHUNK == 0 and dec_len == CHUNK and depth % 2 == 0
    n_seq = bp + bs
    chunks_per_prompt = seq_len // CHUNK

    x = jnp.concatenate([x_prompt.reshape(bp * seq_len, D_MODEL), x_sample.reshape(bs * dec_len, D_MODEL)], axis=0)
    rope_tab = _rope_table(seq_len, dec_len)
    consts = _mixer_consts()

    sm = jax.nn.softmax(hg_lb_logits.astype(F32), axis=0)
    hg_lb = jnp.cumsum(sm, axis=0) - sm[0]

    def with_prompt_zeros(a):
        return jnp.concatenate([jnp.zeros((bp,) + a.shape[1:], F32), a.astype(F32)], axis=0)

    state_outs = []
    for l in range(depth):
        w = w_in[l]
        w_l = jnp.concatenate([w[:, 0:768], w[:, 776:3592], w[:, 768:776], w[:, 3592:3596],
                               jnp.zeros((D_MODEL, D_PROJ - 3596), w.dtype)], axis=1).astype(BF16)
        lp = {
            "sinks": sw_sinks[l].astype(F32),
            "gbias": _pad_lanes(jnp.concatenate([ml_ig_b[l], ml_fg_b[l], ssd_dt_bias[l]])[None, :].astype(F32), 128),
            "alog": _pad_lanes(ssd_A_log[l][None, :].astype(F32), 128),
            "mlnw": ml_norm_w[l][None, :].astype(F32),
            "hglb": hg_lb[l][None, :],
            "hgnw": hg_norm_w[l][None, :].astype(F32),
            "convw": ssd_conv_w[l].astype(F32),
            "convb": ssd_conv_b[l][None, :].astype(F32),
            "ssdd": jnp.repeat(ssd_D[l].astype(F32), HEAD)[None, :],
            "ssdnw": ssd_norm_w[l][None, :].astype(F32),
        }
        c_aug = jnp.concatenate([state_mlstm_C[l], jnp.broadcast_to(state_mlstm_n[l][..., None], state_mlstm_C[l].shape)], axis=-1)
        m_pad = jnp.pad(state_mlstm_m[l][:, None, :], ((0, 0), (0, 7), (0, 128 - N_HEADS)))
        s_t = jnp.swapaxes(state_hgrn_S[l], -1, -2)
        s_t = jnp.moveaxis(s_t, 1, 2).reshape(bs, HEAD, MIX_W)
        cv_pad = jnp.pad(state_ssd_conv[l], ((0, 0), (8 - (SSD_CONV - 1), 0), (0, 0)))
        init = [with_prompt_zeros(a) for a in (
            c_aug, m_pad, s_t, cache_swa_k[l].reshape(bs, WINDOW, 128), cache_swa_v[l].reshape(bs, WINDOW, 128),
            state_ssd_h[l], cv_pad)]

        proj = _inproj(x, norm1_w[l][None, :], w_l)
        mixed, c_o, m_o, s_o, k_o, v_o, h_o, cv_o = _mixer(
            proj, rope_tab, init, lp, consts, bp, chunks_per_prompt, n_seq)
        wo = w_out[l].astype(BF16)
        j = l // 2
        if l % 2 == 0:
            x = _dense_ffn(x, mixed, wo, norm2_w[l][None, :], ffn_w_gate[j].astype(BF16),
                           ffn_w_up[j].astype(BF16), ffn_w_down[j].astype(BF16))
        else:
            x1, hn, comb = _router(x, mixed, wo, norm2_w[l][None, :],
                                   _pad_lanes(moe_router[j], 128).astype(BF16))
            fnw = final_norm_w[None, :] if l == depth - 1 else None
            assert fnw is not None
            x = _moe(x1, hn, comb, moe_w_gate[j].astype(BF16), moe_w_up[j].astype(BF16),
                     moe_w_down[j].astype(BF16), fnw)
        s_back = jnp.moveaxis(s_o.reshape(n_seq, HEAD, N_HEADS, HEAD), 2, 1)
        state_outs.append((c_o[..., :HEAD], c_o[:, :, :, HEAD], m_o[:, 0, :N_HEADS],
                           jnp.swapaxes(s_back, -1, -2),
                           k_o.reshape(n_seq, WINDOW, SW_KV, HEAD), v_o.reshape(n_seq, WINDOW, SW_KV, HEAD),
                           h_o, cv_o[:, 8 - (SSD_CONV - 1):, :]))

    n_p = bp * seq_len
    y_prompt = x[:n_p].reshape(bp, seq_len, D_MODEL)
    y_sample = x[n_p:].reshape(bs, dec_len, D_MODEL)
    stacked = [jnp.stack([so[k] for so in state_outs]) for k in range(8)]
    return (y_prompt, y_sample) + tuple(s[:, :bp] for s in stacked) + tuple(s[:, bp:] for s in stacked)
```

```python
import functools

import numpy as np
import jax
import jax.numpy as jnp
from jax import lax
from jax.experimental import pallas as pl
from jax.experimental.pallas import tpu as pltpu

F32 = jnp.float32
BF16 = jnp.bfloat16

D_MODEL = 1024
CHUNK = 64
EPS = 1e-6
N_HEADS = 4
HEAD = 64
MIX_W = N_HEADS * HEAD
SW_KV = 2
WINDOW = 128
ROPE_DIMS = 16
ROPE_THETA = 500000.0
SSD_N = 128
SSD_CONV = 4
SSD_CONV_DIM = 768
PAST_LEN = 4096
N_EXPERTS = 8
SUB = 16

C_MLQ, C_MLK, C_MLV, C_MLO = 0, 256, 512, 768
C_HGQ, C_HGF, C_HGI, C_HGG = 1024, 1280, 1536, 1792
C_SWQ, C_SWK, C_SWV = 2048, 2304, 2432
C_SDZ, C_SDX, C_GATE = 2560, 2816, 3584
D_PROJ = 3712
NEG = -1e30
VMEM_LIMIT = 56 * 1024 * 1024


def _mm(a, b):
    return jnp.dot(a.astype(BF16), b.astype(BF16), preferred_element_type=F32)


def _mm_nt(a, b):
    return lax.dot_general(a.astype(BF16), b.astype(BF16), (((1,), (1,)), ((), ())),
                           preferred_element_type=F32)


def _mm_tn(a, b):
    return jnp.dot(a.T.astype(BF16), b.astype(BF16), preferred_element_type=F32)


def _mm_sel(sel, x):
    hi = x.astype(BF16)
    r1 = x - hi.astype(F32)
    mid = r1.astype(BF16)
    lo = (r1 - mid.astype(F32)).astype(BF16)
    return (jnp.dot(sel, hi, preferred_element_type=F32)
            + jnp.dot(sel, mid, preferred_element_type=F32)
            + jnp.dot(sel, lo, preferred_element_type=F32))


def _sigmoid(x):
    return 1.0 / (1.0 + jnp.exp(-x))


def _silu(x):
    return x * _sigmoid(x)


def _softplus(x):
    return jnp.maximum(x, 0.0) + jnp.log(1.0 + jnp.exp(-jnp.abs(x)))


def _rms(x, w):
    return x * lax.rsqrt(jnp.mean(x * x, axis=-1, keepdims=True) + EPS) * w


def _pick_tile(n, candidates):
    for c in candidates:
        if n % c == 0:
            return c
    raise ValueError(f"no tile for {n}")


def _inproj_kernel(x_ref, nw_ref, w_ref, o_ref):
    xn = _rms(x_ref[...], nw_ref[...]).astype(BF16)
    o_ref[...] = jnp.dot(xn, w_ref[...], preferred_element_type=F32)


def _inproj(x, nw, w):
    n = x.shape[0]
    tm = _pick_tile(n, (512, 256, 128, 64))
    return pl.pallas_call(
        _inproj_kernel,
        grid=(n // tm,),
        in_specs=[pl.BlockSpec((tm, D_MODEL), lambda i: (i, 0)),
                  pl.BlockSpec((1, D_MODEL), lambda i: (0, 0)),
                  pl.BlockSpec((D_MODEL, D_PROJ), lambda i: (0, 0))],
        out_specs=pl.BlockSpec((tm, D_PROJ), lambda i: (i, 0)),
        out_shape=jax.ShapeDtypeStruct((n, D_PROJ), F32),
        compiler_params=pltpu.CompilerParams(dimension_semantics=("arbitrary",),
                                             vmem_limit_bytes=VMEM_LIMIT),
        name="inproj",
    )(x, nw, w)


def _head_norm(x, width):
    parts = []
    for g in range(x.shape[1] // width):
        xg = x[:, g * width:(g + 1) * width]
        parts.append(xg * lax.rsqrt(jnp.mean(xg * xg, axis=-1, keepdims=True) + EPS))
    return jnp.concatenate(parts, axis=1)


def _mixer_kernel(n_prompt_steps, chunks_per_prompt,
                  sinks_ref,
                  proj_ref, rope_ref, c0_ref, m0_ref, s0_ref, k0_ref, v0_ref, h0_ref, cv0_ref,
                  gbias_ref, alog_ref, mlnw_ref, hglb_ref, hgnw_ref,
                  convw_ref, convb_ref, ssdd_ref, ssdnw_ref,
                  tri_ref, triblk_ref, blkones_ref, ind_ref,
                  mixed_ref, c_ref, m_ref, s_ref, k_ref, v_ref, h_ref, cv_ref,
                  convbuf, bloc_s, kk_s, hv_s):
    i = pl.program_id(0)
    is_prompt = i < n_prompt_steps
    chunk = lax.rem(i, chunks_per_prompt)
    is_first = jnp.logical_or(jnp.logical_not(is_prompt), chunk == 0)
    n_valid = jnp.where(is_prompt, jnp.minimum(chunk * CHUNK, WINDOW), WINDOW)

    @pl.when(is_first)
    def _():
        c_ref[...] = c0_ref[...]
        m_ref[...] = m0_ref[...]
        s_ref[...] = s0_ref[...]
        k_ref[...] = k0_ref[...]
        v_ref[...] = v0_ref[...]
        h_ref[...] = h0_ref[...]
        cv_ref[...] = cv0_ref[...]

    L = CHUNK
    row = lax.broadcasted_iota(jnp.int32, (L, L), 0)
    col = lax.broadcasted_iota(jnp.int32, (L, L), 1)
    causal = row >= col
    tri = tri_ref[...]

    gate = proj_ref[:, C_GATE:C_GATE + 128] + gbias_ref[...]
    ig = gate
    lf = pltpu.roll(-_softplus(-gate), 124, 1)
    dtv = pltpu.roll(_softplus(gate), 120, 1)

    b = _mm_sel(tri, lf)
    a = ig - b
    cm = a
    rowg = lax.broadcasted_iota(jnp.int32, (L, 128), 0)
    for sh in (1, 2, 4, 8, 16, 32):
        cm = jnp.where(rowg >= sh, jnp.maximum(cm, pltpu.roll(cm, sh, 0)), cm)
    m_prev = m_ref[0, 0:1, :]
    m_t = b + jnp.maximum(m_prev, cm)
    inter = jnp.exp(b + m_prev - m_t)
    bm = b - m_t
    a_t = a.T
    m_last = m_t[L - 1:L, :]
    b_last = b[L - 1:L, :]
    ws = jnp.exp(b_last + a - m_last)
    decay = jnp.exp(b_last + m_prev - m_last)
    emt = jnp.exp(-m_t)
    m_ref[0, 0:1, :] = m_last

    ones_h = jnp.ones((L, HEAD), F32)
    ml_out = []
    for h in range(N_HEADS):
        hs = slice(h * HEAD, (h + 1) * HEAD)
        qh = proj_ref[:, C_MLQ + h * HEAD:C_MLQ + (h + 1) * HEAD]
        kh = proj_ref[:, C_MLK + h * HEAD:C_MLK + (h + 1) * HEAD] * (HEAD ** -0.5)
        vh = proj_ref[:, C_MLV + h * HEAD:C_MLV + (h + 1) * HEAD]
        qk = _mm_nt(qh, kh)
        w = jnp.where(causal, jnp.exp(bm[:, h:h + 1] + a_t[h:h + 1, :]), 0.0)
        wqk = w * qk
        v_aug = jnp.concatenate([vh, ones_h], axis=1)
        c_aug = c_ref[0, h]
        nd = _mm(wqk, v_aug) + inter[:, h:h + 1] * _mm(qh, c_aug)
        den = jnp.maximum(jnp.abs(nd[:, HEAD:HEAD + 1]), emt[:, h:h + 1])
        ml_out.append(nd[:, :HEAD] / den)
        c_ref[0, h] = decay[:, h:h + 1] * c_aug + _mm_tn(kh * ws[:, h:h + 1], v_aug)
    ml = _head_norm(jnp.concatenate(ml_out, axis=1), HEAD)
    mixed_ref[:, 0:MIX_W] = (_sigmoid(proj_ref[:, C_MLO:C_MLO + MIX_W]) * ml
                             * mlnw_ref[...]).astype(mixed_ref.dtype)

    lb = hglb_ref[...]
    fg = lb + (1.0 - lb) * _sigmoid(proj_ref[:, C_HGF:C_HGF + MIX_W])
    hq = _silu(proj_ref[:, C_HGQ:C_HGQ + MIX_W]) * (HEAD ** -0.5)
    kk = 1.0 - fg
    lfg = jnp.log(fg)
    hv = proj_ref[:, C_HGI:C_HGI + MIX_W]
    bloc = _mm_sel(triblk_ref[...], lfg)
    blast = _mm_sel(blkones_ref[...], lfg)
    qe = hq * jnp.exp(bloc)
    kw = kk * jnp.exp(blast - bloc)
    bloc_s[...] = bloc
    kk_s[...] = kk
    hv_s[...] = hv
    ind = ind_ref[...]
    rows = lax.broadcasted_iota(jnp.int32, (SUB, MIX_W), 0)
    hg_rows = []
    for blk in range(L // SUB):
        r0 = blk * SUB
        st = s_ref[0]
        o_int = jnp.concatenate(
            [_mm_nt(qe[r0:r0 + SUB, h * HEAD:(h + 1) * HEAD], st[:, h * HEAD:(h + 1) * HEAD])
             for h in range(N_HEADS)], axis=1)
        q_blk = hq[r0:r0 + SUB]
        b_blk = bloc[r0:r0 + SUB]
        xs = []
        for s in range(SUB):
            r = r0 + s
            e = jnp.exp(jnp.minimum(b_blk - bloc_s[r:r + 1, :], 0.0))
            xs.append((q_blk * (kk_s[r:r + 1, :] * e)).astype(BF16))
        att = jnp.dot(jnp.concatenate(xs, axis=0), ind, preferred_element_type=F32)
        o_blk = o_int
        for s in range(SUB):
            r = r0 + s
            o_blk = o_blk + jnp.where(rows >= s, att[s * SUB:(s + 1) * SUB], 0.0) * hv_s[r:r + 1, :]
        hg_rows.append(o_blk)
        upd = jnp.concatenate(
            [_mm_tn(hv[r0:r0 + SUB, h * HEAD:(h + 1) * HEAD], kw[r0:r0 + SUB, h * HEAD:(h + 1) * HEAD])
             for h in range(N_HEADS)], axis=1)
        s_ref[0] = st * jnp.exp(blast[r0:r0 + 1, :]) + upd
    hg = _head_norm(jnp.concatenate(hg_rows, axis=0), HEAD)
    mixed_ref[:, MIX_W:2 * MIX_W] = (hg * hgnw_ref[...]
                                     * _silu(proj_ref[:, C_HGG:C_HGG + MIX_W])).astype(mixed_ref.dtype)

    cos = rope_ref[:, 0:128]
    sin_a = rope_ref[:, 128:256]
    sin_b = rope_ref[:, 256:384]

    def rope(x):
        return x * cos + pltpu.roll(x, 120, 1) * sin_a + pltpu.roll(x, 8, 1) * sin_b

    q_rot = [rope(proj_ref[:, C_SWQ:C_SWQ + 128]), rope(proj_ref[:, C_SWQ + 128:C_SWQ + 256])]
    k_rot = rope(proj_ref[:, C_SWK:C_SWK + 128])
    v_cur = proj_ref[:, C_SWV:C_SWV + 128]
    k_prev = k_ref[0]
    v_prev = v_ref[0]
    colw = lax.broadcasted_iota(jnp.int32, (2 * L, WINDOW), 1)
    row2 = lax.broadcasted_iota(jnp.int32, (2 * L, 1), 0)
    prev_ok = colw >= WINDOW - n_valid
    sw_out = []
    for g in range(SW_KV):
        gs = slice(g * HEAD, (g + 1) * HEAD)
        q2 = jnp.concatenate([q_rot[g][:, 0:HEAD], q_rot[g][:, HEAD:2 * HEAD]], axis=0) * (HEAD ** -0.5)
        s_p = jnp.where(prev_ok, _mm_nt(q2, k_prev[:, gs]), NEG)
        s_c = _mm_nt(q2, k_rot[:, gs])
        sink = jnp.where(row2 < L, sinks_ref[2 * g], sinks_ref[2 * g + 1])
        mx = jnp.maximum(jnp.maximum(jnp.max(s_p, axis=-1, keepdims=True),
                                     jnp.max(s_c, axis=-1, keepdims=True)), sink)
        e_p = jnp.exp(s_p - mx)
        e_c = jnp.exp(s_c - mx)
        den = (jnp.sum(e_p, axis=-1, keepdims=True) + jnp.sum(e_c, axis=-1, keepdims=True)
               + jnp.exp(sink - mx))
        o = (_mm(e_p, v_prev[:, gs]) + _mm(e_c, v_cur[:, gs])) / den
        sw_out.append(o[0:L])
        sw_out.append(o[L:2 * L])
    mixed_ref[:, 2 * MIX_W:3 * MIX_W] = jnp.concatenate(sw_out, axis=1).astype(mixed_ref.dtype)
    k_ref[0, 0:WINDOW - L, :] = k_prev[L:WINDOW]
    k_ref[0, WINDOW - L:WINDOW, :] = k_rot
    v_ref[0, 0:WINDOW - L, :] = v_prev[L:WINDOW]
    v_ref[0, WINDOW - L:WINDOW, :] = v_cur

    convbuf[0:8, :] = cv_ref[0]
    convbuf[8:8 + L, :] = proj_ref[:, C_SDX:C_SDX + SSD_CONV_DIM]
    acc = convb_ref[...] + convbuf[5:5 + L, :] * convw_ref[0:1, :]
    for j in range(1, SSD_CONV):
        acc = acc + convbuf[5 + j:5 + j + L, :] * convw_ref[j:j + 1, :]
    xbc = _silu(acc)
    cv_ref[0] = convbuf[L:L + 8, :]
    xs_all = xbc[:, 0:MIX_W]
    a_neg = -jnp.exp(alog_ref[...])
    ca = _mm_sel(tri, dtv * a_neg)
    ca_t = ca.T
    dt_t = dtv.T
    eca = jnp.exp(ca)
    ca_last = ca[L - 1:L, :]
    wss = jnp.exp(ca_last - ca) * dtv
    sdec = jnp.exp(ca_last)
    sd_out = []
    for g in range(2):
        bm_g = xbc[:, MIX_W + g * SSD_N:MIX_W + (g + 1) * SSD_N]
        cm_g = xbc[:, MIX_W + 2 * SSD_N + g * SSD_N:MIX_W + 2 * SSD_N + (g + 1) * SSD_N]
        cb = _mm_nt(cm_g, bm_g)
        for h in (2 * g, 2 * g + 1):
            xh = xs_all[:, h * HEAD:(h + 1) * HEAD]
            dec = jnp.where(causal, jnp.exp(ca[:, h:h + 1] - ca_t[h:h + 1, :]), 0.0)
            mmat = cb * dec * dt_t[h:h + 1, :]
            hst = h_ref[0, h]
            sd_out.append(_mm(mmat, xh) + eca[:, h:h + 1] * _mm_nt(cm_g, hst))
            h_ref[0, h] = sdec[:, h:h + 1] * hst + _mm_tn(xh * wss[:, h:h + 1], bm_g)
    y = jnp.concatenate(sd_out, axis=1) + ssdd_ref[...] * xs_all
    gated = y * _silu(proj_ref[:, C_SDZ:C_SDZ + MIX_W])
    mixed_ref[:, 3 * MIX_W:4 * MIX_W] = (_head_norm(gated, 2 * HEAD)
                                         * ssdnw_ref[...]).astype(mixed_ref.dtype)


def _mixer(proj, rope_tab, init, lp, consts, n_prompt_seq, chunks_per_prompt, n_seq):
    n = proj.shape[0]
    n_steps = n // CHUNK
    n_prompt_steps = n_prompt_seq * chunks_per_prompt

    def seq_of(i):
        return jnp.where(i < n_prompt_steps, i // chunks_per_prompt, i - n_prompt_steps + n_prompt_seq)

    def rope_blk(i):
        return jnp.where(i < n_prompt_steps, lax.rem(i, chunks_per_prompt), chunks_per_prompt)

    def per_seq(shape):
        nd = len(shape)
        return pl.BlockSpec((1,) + shape, lambda i, s: (seq_of(i),) + (0,) * nd)

    def const(shape):
        nd = len(shape)
        return pl.BlockSpec(shape, lambda i, s: (0,) * nd)

    state_shapes = [(N_HEADS, HEAD, 128), (8, 128), (HEAD, MIX_W), (WINDOW, 128), (WINDOW, 128),
                    (N_HEADS, HEAD, SSD_N), (8, SSD_CONV_DIM)]
    in_specs = ([pl.BlockSpec((CHUNK, D_PROJ), lambda i, s: (i, 0)),
                 pl.BlockSpec((CHUNK, 384), lambda i, s: (rope_blk(i), 0))]
                + [per_seq(s) for s in state_shapes]
                + [const((1, 128)), const((1, 128)), const((1, MIX_W)), const((1, MIX_W)), const((1, MIX_W)),
                   const((SSD_CONV, SSD_CONV_DIM)), const((1, SSD_CONV_DIM)), const((1, MIX_W)),
                   const((1, MIX_W)),
                   const((CHUNK, CHUNK)), const((CHUNK, CHUNK)), const((CHUNK, CHUNK)), const((MIX_W, MIX_W))])
    out_specs = ([pl.BlockSpec((CHUNK, D_MODEL), lambda i, s: (i, 0))]
                 + [per_seq(s) for s in state_shapes])
    out_shape = ([jax.ShapeDtypeStruct((n, D_MODEL), BF16)]
                 + [jax.ShapeDtypeStruct((n_seq,) + s, F32) for s in state_shapes])
    grid_spec = pltpu.PrefetchScalarGridSpec(
        num_scalar_prefetch=1, grid=(n_steps,), in_specs=in_specs, out_specs=out_specs,
        scratch_shapes=[pltpu.VMEM((CHUNK + 8, SSD_CONV_DIM), F32),
                        pltpu.VMEM((CHUNK, MIX_W), F32), pltpu.VMEM((CHUNK, MIX_W), F32),
                        pltpu.VMEM((CHUNK, MIX_W), F32)])
    return pl.pallas_call(
        functools.partial(_mixer_kernel, n_prompt_steps, chunks_per_prompt),
        grid_spec=grid_spec, out_shape=out_shape,
        compiler_params=pltpu.CompilerParams(dimension_semantics=("arbitrary",),
                                             vmem_limit_bytes=VMEM_LIMIT),
        name="mixer",
    )(lp["sinks"], proj, rope_tab, *init,
      lp["gbias"], lp["alog"], lp["mlnw"], lp["hglb"], lp["hgnw"],
      lp["convw"], lp["convb"], lp["ssdd"], lp["ssdnw"], *consts)


def _dense_ffn_kernel(f_chunk, x_ref, mix_ref, wo_ref, nw_ref, wg_ref, wu_ref, wd_ref, o_ref):
    x1 = x_ref[...] + jnp.dot(mix_ref[...], wo_ref[...], preferred_element_type=F32)
    hn = _rms(x1, nw_ref[...]).astype(BF16)
    o_ref[...] = x1
    for f0 in range(0, wg_ref.shape[1], f_chunk):
        g = jnp.dot(hn, wg_ref[:, f0:f0 + f_chunk], preferred_element_type=F32)
        u = jnp.dot(hn, wu_ref[:, f0:f0 + f_chunk], preferred_element_type=F32)
        act = (_silu(g) * u).astype(BF16)
        o_ref[...] += jnp.dot(act, wd_ref[f0:f0 + f_chunk, :], preferred_element_type=F32)


def _dense_ffn(x, mixed, wo, nw, wg, wu, wd):
    n = x.shape[0]
    d_ff = wg.shape[1]
    tm = _pick_tile(n, (512, 256, 128, 64))
    full = lambda shape: pl.BlockSpec(shape, lambda i: (0, 0))
    return pl.pallas_call(
        functools.partial(_dense_ffn_kernel, 256),
        grid=(n // tm,),
        in_specs=[pl.BlockSpec((tm, D_MODEL), lambda i: (i, 0)),
                  pl.BlockSpec((tm, D_MODEL), lambda i: (i, 0)),
                  full((D_MODEL, D_MODEL)), full((1, D_MODEL)),
                  full((D_MODEL, d_ff)), full((D_MODEL, d_ff)), full((d_ff, D_MODEL))],
        out_specs=pl.BlockSpec((tm, D_MODEL), lambda i: (i, 0)),
        out_shape=jax.ShapeDtypeStruct((n, D_MODEL), F32),
        compiler_params=pltpu.CompilerParams(dimension_semantics=("arbitrary",),
                                             vmem_limit_bytes=VMEM_LIMIT),
        name="dense_ffn",
    )(x, mixed, wo, nw, wg, wu, wd)


TOK_BLK = 256
FFN_TILE = 1024


def _router_kernel(x_ref, mix_ref, wo_ref, nw_ref, wrt_ref, upper_ref,
                   x1_ref, hn_ref, rank_ref, comb_ref, blkend_ref, carry):
    @pl.when(pl.program_id(0) == 0)
    def _():
        carry[...] = jnp.zeros_like(carry)

    x1 = x_ref[...] + jnp.dot(mix_ref[...], wo_ref[...], preferred_element_type=F32)
    x1_ref[...] = x1
    hn = _rms(x1, nw_ref[...]).astype(BF16)
    hn_ref[...] = hn
    logits = lax.dot_general(wrt_ref[...], hn, (((1,), (1,)), ((), ())),
                             preferred_element_type=F32)
    sub = lax.broadcasted_iota(jnp.int32, logits.shape, 0)
    l1 = jnp.where(sub < N_EXPERTS, logits, NEG)
    m1 = jnp.max(l1, axis=0, keepdims=True)
    i1 = jnp.min(jnp.where(l1 == m1, sub, 99), axis=0, keepdims=True)
    l2 = jnp.where(sub == i1, NEG, l1)
    m2 = jnp.max(l2, axis=0, keepdims=True)
    i2 = jnp.min(jnp.where(l2 == m2, sub, 99), axis=0, keepdims=True)
    e2 = jnp.exp(m2 - m1)
    g1 = 1.0 / (1.0 + e2)
    g2 = e2 / (1.0 + e2)
    comb_ref[...] = jnp.where(sub == i1, g1, jnp.where(sub == i2, g2, 0.0))
    sel = jnp.where(sub == i1, 1.0, jnp.where(sub == i2, 1.0, 0.0))
    before = jnp.dot(sel.astype(BF16), upper_ref[...], preferred_element_type=F32) + carry[:, 0:1]
    rank_ref[...] = jnp.where(sel > 0.0, before, -1.0)
    total = carry[...] + jnp.sum(sel, axis=1, keepdims=True)
    carry[...] = total
    blkend_ref[0] = total


def _router(x, mixed, wo, nw, wrt, upper):
    n = x.shape[0]
    tm = TOK_BLK
    full = lambda shape: pl.BlockSpec(shape, lambda i: (0,) * len(shape))
    tok = lambda w: pl.BlockSpec((tm, w), lambda i: (i, 0))
    return pl.pallas_call(
        _router_kernel,
        grid=(n // tm,),
        in_specs=[tok(D_MODEL), tok(D_MODEL), full((D_MODEL, D_MODEL)), full((1, D_MODEL)),
                  full((16, D_MODEL)), full((tm, tm))],
        out_specs=[tok(D_MODEL), tok(D_MODEL),
                   pl.BlockSpec((16, tm), lambda i: (0, i)), pl.BlockSpec((16, tm), lambda i: (0, i)),
                   pl.BlockSpec((1, 16, 128), lambda i: (i, 0, 0))],
        out_shape=[jax.ShapeDtypeStruct((n, D_MODEL), F32), jax.ShapeDtypeStruct((n, D_MODEL), BF16),
                   jax.ShapeDtypeStruct((16, n), F32), jax.ShapeDtypeStruct((16, n), F32),
                   jax.ShapeDtypeStruct((n // tm, 16, 128), F32)],
        scratch_shapes=[pltpu.VMEM((16, 128), F32)],
        compiler_params=pltpu.CompilerParams(dimension_semantics=("arbitrary",),
                                             vmem_limit_bytes=VMEM_LIMIT),
        name="router",
    )(x, mixed, wo, nw, wrt, upper)


def _dispatch_kernel(tile_ref, blk_ref, exp_ref, first_ref, mode_ref, goff_ref,
                     hn_ref, rank_ref, xs_ref):
    w = pl.program_id(0)
    mode = mode_ref[w]

    @pl.when(mode == 1)
    def _():
        e = exp_ref[w]
        r = rank_ref[pl.ds(e, 1), :]
        shift = (goff_ref[e] - tile_ref[w] * TOK_BLK).astype(F32)
        local = jnp.where(r >= 0.0, r + shift, -1.0)
        slot = lax.broadcasted_iota(jnp.int32, (TOK_BLK, TOK_BLK), 0).astype(F32)
        onehot = jnp.where(slot == local, 1.0, 0.0).astype(BF16)
        rows = jnp.dot(onehot, hn_ref[...], preferred_element_type=F32)

        @pl.when(first_ref[w] == 1)
        def _():
            xs_ref[...] = rows.astype(xs_ref.dtype)

        @pl.when(first_ref[w] == 0)
        def _():
            xs_ref[...] = (xs_ref[...].astype(F32) + rows).astype(xs_ref.dtype)

    @pl.when(mode == 2)
    def _():
        xs_ref[...] = jnp.zeros_like(xs_ref)


def _dispatch(meta, hn, rank_t, n_slots):
    n_items = meta["d_tile"].shape[0]
    grid_spec = pltpu.PrefetchScalarGridSpec(
        num_scalar_prefetch=6, grid=(n_items,),
        in_specs=[pl.BlockSpec((TOK_BLK, D_MODEL), lambda w, t, b, e, f, m, g: (b[w], 0)),
                  pl.BlockSpec((16, TOK_BLK), lambda w, t, b, e, f, m, g: (0, b[w]))],
        out_specs=pl.BlockSpec((TOK_BLK, D_MODEL), lambda w, t, b, e, f, m, g: (t[w], 0)))
    return pl.pallas_call(
        _dispatch_kernel, grid_spec=grid_spec,
        out_shape=jax.ShapeDtypeStruct((n_slots, D_MODEL), BF16),
        compiler_params=pltpu.CompilerParams(dimension_semantics=("arbitrary",),
                                             vmem_limit_bytes=VMEM_LIMIT),
        name="dispatch",
    )(meta["d_tile"], meta["d_blk"], meta["d_exp"], meta["d_first"], meta["d_mode"], meta["goff"],
      hn, rank_t)


def _expert_ffn_kernel(texp_ref, tvalid_ref, xs_ref, wg_ref, wu_ref, wd_ref, o_ref, acc_ref):
    j = pl.program_id(0)
    f = pl.program_id(1)
    last = f == pl.num_programs(1) - 1
    valid = tvalid_ref[j] == 1

    @pl.when(valid)
    def _():
        xs = xs_ref[...]
        g = jnp.dot(xs, wg_ref[0], preferred_element_type=F32)
        u = jnp.dot(xs, wu_ref[0], preferred_element_type=F32)
        act = (_silu(g) * u).astype(BF16)
        part = jnp.dot(act, wd_ref[0], preferred_element_type=F32)

        @pl.when(f == 0)
        def _():
            acc_ref[...] = part

        @pl.when(f > 0)
        def _():
            acc_ref[...] += part

        @pl.when(last)
        def _():
            o_ref[...] = acc_ref[...].astype(o_ref.dtype)

    @pl.when(jnp.logical_and(jnp.logical_not(valid), last))
    def _():
        o_ref[...] = jnp.zeros_like(o_ref)


def _expert_ffn(meta, xs, wg, wu, wd):
    n_slots = xs.shape[0]
    d_ff = wg.shape[2]
    tf = _pick_tile(d_ff, (512, 256, 128))
    grid_spec = pltpu.PrefetchScalarGridSpec(
        num_scalar_prefetch=2, grid=(n_slots // FFN_TILE, d_ff // tf),
        in_specs=[pl.BlockSpec((FFN_TILE, D_MODEL), lambda j, f, te, tv: (j, 0)),
                  pl.BlockSpec((1, D_MODEL, tf), lambda j, f, te, tv: (te[j], 0, f)),
                  pl.BlockSpec((1, D_MODEL, tf), lambda j, f, te, tv: (te[j], 0, f)),
                  pl.BlockSpec((1, tf, D_MODEL), lambda j, f, te, tv: (te[j], f, 0))],
        out_specs=pl.BlockSpec((FFN_TILE, D_MODEL), lambda j, f, te, tv: (j, 0)),
        scratch_shapes=[pltpu.VMEM((FFN_TILE, D_MODEL), F32)])
    return pl.pallas_call(
        _expert_ffn_kernel, grid_spec=grid_spec,
        out_shape=jax.ShapeDtypeStruct((n_slots, D_MODEL), BF16),
        compiler_params=pltpu.CompilerParams(dimension_semantics=("arbitrary", "arbitrary"),
                                             vmem_limit_bytes=VMEM_LIMIT),
        name="expert_ffn",
    )(meta["f_exp"], meta["f_valid"], xs, wg, wu, wd)


def _combine_kernel(blk_ref, tile_ref, exp_ref, first_ref, last_ref, valid_ref,
                    x1_ref, dest_ref, gate_ref, out_ref, fnw_ref, y_ref):
    w = pl.program_id(0)

    @pl.when(valid_ref[w] == 1)
    def _():
        e = exp_ref[w]
        lane = lax.broadcasted_iota(jnp.int32, (TOK_BLK, 128), 1)
        dest = jnp.sum(jnp.where(lane == e, dest_ref[...], 0.0), axis=1, keepdims=True)
        gate = jnp.sum(jnp.where(lane == e, gate_ref[...], 0.0), axis=1, keepdims=True)
        local = dest - (tile_ref[w] * TOK_BLK).astype(F32)
        slot = lax.broadcasted_iota(jnp.int32, (TOK_BLK, TOK_BLK), 1).astype(F32)
        onehot = jnp.where(slot == local, 1.0, 0.0).astype(BF16)
        part = gate * jnp.dot(onehot, out_ref[...], preferred_element_type=F32)

        @pl.when(first_ref[w] == 1)
        def _():
            y_ref[...] = x1_ref[...] + part

        @pl.when(first_ref[w] == 0)
        def _():
            y_ref[...] += part

        @pl.when(last_ref[w] == 1)
        def _():
            y_ref[...] = _rms(y_ref[...], fnw_ref[...])


def _combine(meta, x1, dest_n, gate_n, out, fnw):
    n = x1.shape[0]
    n_items = meta["c_blk"].shape[0]
    tokmap = lambda w, b, t, e, f, l, v: (b[w], 0)
    grid_spec = pltpu.PrefetchScalarGridSpec(
        num_scalar_prefetch=6, grid=(n_items,),
        in_specs=[pl.BlockSpec((TOK_BLK, D_MODEL), tokmap),
                  pl.BlockSpec((TOK_BLK, 128), tokmap),
                  pl.BlockSpec((TOK_BLK, 128), tokmap),
                  pl.BlockSpec((TOK_BLK, D_MODEL), lambda w, b, t, e, f, l, v: (t[w], 0)),
                  pl.BlockSpec((1, D_MODEL), lambda w, b, t, e, f, l, v: (0, 0))],
        out_specs=pl.BlockSpec((TOK_BLK, D_MODEL), tokmap))
    return pl.pallas_call(
        _combine_kernel, grid_spec=grid_spec,
        out_shape=jax.ShapeDtypeStruct((n, D_MODEL), F32),
        compiler_params=pltpu.CompilerParams(dimension_semantics=("arbitrary",),
                                             vmem_limit_bytes=VMEM_LIMIT),
        name="combine",
    )(meta["c_blk"], meta["c_tile"], meta["c_exp"], meta["c_first"], meta["c_last"], meta["c_valid"],
      x1, dest_n, gate_n, out, fnw)


def _routing_meta(blkend, n_slots):
    i32 = jnp.int32
    nb = blkend.shape[0]
    blkend = blkend.astype(i32)
    blkstart = jnp.concatenate([jnp.zeros((1, N_EXPERTS), i32), blkend[:-1]], axis=0)
    counts = blkend[-1]
    gsize = (counts + FFN_TILE - 1) // FFN_TILE * FFN_TILE
    gend = jnp.cumsum(gsize)
    goff = gend - gsize
    n_tiles = n_slots // TOK_BLK
    start = goff[None, :] + blkstart
    end = goff[None, :] + blkend
    nonempty = end > start
    j0 = start // TOK_BLK
    j1 = (end - 1) // TOK_BLK
    k = jnp.arange(2, dtype=i32)
    tile = j0[..., None] + k
    valid = nonempty[..., None] & (tile <= j1[..., None])
    blk = jnp.broadcast_to(jnp.arange(nb, dtype=i32)[:, None, None], tile.shape)
    exp = jnp.broadcast_to(jnp.arange(N_EXPERTS, dtype=i32)[None, :, None], tile.shape)
    big = jnp.iinfo(jnp.int32).max

    def compact(order_key, fields, n_out):
        order = jnp.argsort(order_key, stable=True)[:n_out]
        return [f[order] for f in fields]

    def em(a):
        return jnp.swapaxes(a, 0, 1).reshape(-1)
    has_rows = jnp.zeros((n_tiles,), i32).at[jnp.where(valid, tile, n_tiles).reshape(-1)].max(1, mode="drop")
    t_all = jnp.concatenate([em(tile), jnp.arange(n_tiles, dtype=i32)])
    b_all = jnp.concatenate([em(blk), jnp.zeros((n_tiles,), i32)])
    e_all = jnp.concatenate([em(exp), jnp.zeros((n_tiles,), i32)])
    m_all = jnp.concatenate([jnp.where(em(valid), 1, 0), jnp.where(has_rows == 0, 2, 0)]).astype(i32)
    n_disp = N_EXPERTS * nb + n_tiles
    d_tile, d_blk, d_exp, d_mode = compact(jnp.where(m_all > 0, t_all, big), [t_all, b_all, e_all, m_all], n_disp)
    n_real = jnp.sum(m_all > 0)
    idx = jnp.arange(n_disp)
    last_real = jnp.maximum(n_real - 1, 0)
    d_tile = jnp.where(idx < n_real, d_tile, d_tile[last_real])
    d_blk = jnp.where(idx < n_real, d_blk, d_blk[last_real])
    d_mode = jnp.where(idx < n_real, d_mode, 0)
    d_first = jnp.concatenate([jnp.ones((1,), i32), (d_tile[1:] != d_tile[:-1]).astype(i32)])

    n_ftiles = n_slots // FFN_TILE
    fstart = jnp.arange(n_ftiles, dtype=i32) * FFN_TILE
    f_valid = (fstart < gend[-1]).astype(i32)
    f_exp = jnp.minimum(jnp.searchsorted(gend, fstart, side="right").astype(i32), N_EXPERTS - 1)
    last_exp = f_exp[jnp.maximum(jnp.sum(f_valid) - 1, 0)]
    f_exp = jnp.where(f_valid == 1, f_exp, last_exp)

    fl = lambda a: a.reshape(-1)
    n_comb = min(2 * N_EXPERTS * nb, N_EXPERTS * nb + n_tiles)
    key = jnp.where(fl(valid), jnp.arange(fl(valid).shape[0]), big)
    c_blk, c_tile, c_exp = compact(key, [fl(blk), fl(tile), fl(exp)], n_comb)
    n_c = jnp.sum(valid)
    idx = jnp.arange(n_comb)
    last_c = jnp.maximum(n_c - 1, 0)
    c_valid = (idx < n_c).astype(i32)
    c_blk = jnp.where(idx < n_c, c_blk, c_blk[last_c])
    c_tile = jnp.where(idx < n_c, c_tile, c_tile[last_c])
    c_first = jnp.concatenate([jnp.ones((1,), i32), (c_blk[1:] != c_blk[:-1]).astype(i32)])
    c_last = jnp.concatenate([(c_blk[1:] != c_blk[:-1]).astype(i32), jnp.ones((1,), i32)])
    c_last = jnp.where(idx == last_c, 1, c_last) * c_valid
    return {"goff": goff, "d_tile": d_tile, "d_blk": d_blk, "d_exp": d_exp, "d_first": d_first, "d_mode": d_mode,
            "f_exp": f_exp, "f_valid": f_valid,
            "c_blk": c_blk, "c_tile": c_tile, "c_exp": c_exp, "c_first": c_first * c_valid, "c_last": c_last,
            "c_valid": c_valid}


def _moe(x, mixed, wo, nw, router_w, wg, wu, wd, fnw):
    n = x.shape[0]
    assert n % TOK_BLK == 0
    wrt = jnp.pad(router_w.T, ((0, 16 - N_EXPERTS), (0, 0))).astype(BF16)
    t = np.arange(TOK_BLK)
    upper = jnp.asarray((t[:, None] < t[None, :]).astype(np.float32), dtype=BF16)
    x1, hn, rank_t, comb_t, blkend = _router(x, mixed, wo, nw, wrt, upper)
    n_slots = (-(-2 * n // FFN_TILE) + N_EXPERTS) * FFN_TILE
    meta = _routing_meta(blkend[:, :N_EXPERTS, 0], n_slots)
    xs = _dispatch(meta, hn, rank_t, n_slots)
    out = _expert_ffn(meta, xs, wg, wu, wd)
    rank_n = rank_t[:N_EXPERTS].T
    dest_n = jnp.where(rank_n >= 0.0, rank_n + meta["goff"][None, :].astype(F32), -1.0)
    dest_n = jnp.pad(dest_n, ((0, 0), (0, 128 - N_EXPERTS)), constant_values=-1.0)
    gate_n = _pad_lanes(comb_t[:N_EXPERTS].T, 128)
    return _combine(meta, x1, dest_n, gate_n, out, fnw)


def _rope_table(seq_len, dec_len):
    half = ROPE_DIMS // 2
    pos = jnp.concatenate([jnp.arange(seq_len), PAST_LEN + jnp.arange(dec_len)]).astype(F32)
    inv_freq = jnp.power(jnp.float32(ROPE_THETA), -jnp.arange(half, dtype=F32) / half)
    ang = pos[:, None] * inv_freq[None, :]
    cos, sin = jnp.cos(ang), jnp.sin(ang)
    p = pos.shape[0]
    ones = jnp.ones((p, HEAD - ROPE_DIMS), F32)
    zeros = jnp.zeros((p, HEAD - ROPE_DIMS), F32)
    zh = jnp.zeros((p, half), F32)
    cos_h = jnp.concatenate([cos, cos, ones], axis=1)
    sin_a = jnp.concatenate([-sin, zh, zeros], axis=1)
    sin_b = jnp.concatenate([zh, sin, zeros], axis=1)
    return jnp.concatenate([cos_h, cos_h, sin_a, sin_a, sin_b, sin_b], axis=1)


def _mixer_consts():
    t = np.arange(CHUNK)
    tri = (t[:, None] >= t[None, :])
    same = (t[:, None] // SUB) == (t[None, :] // SUB)
    c = np.arange(MIX_W)
    ind = (c[:, None] // HEAD) == (c[None, :] // HEAD)
    as_bf = lambda m: jnp.asarray(m.astype(np.float32), dtype=BF16)
    return [as_bf(tri), as_bf(tri & same), as_bf(same), as_bf(ind)]


def _pad_lanes(v, width):
    return jnp.pad(v, ((0, 0), (0, width - v.shape[1])))


def kernel(x_prompt, x_sample, state_mlstm_C, state_mlstm_n, state_mlstm_m, state_hgrn_S, cache_swa_k, cache_swa_v, state_ssd_h, state_ssd_conv, norm1_w, w_in, ml_ig_b, ml_fg_b, ml_norm_w, hg_lb_logits, hg_norm_w, sw_sinks, ssd_conv_w, ssd_conv_b, ssd_dt_bias, ssd_A_log, ssd_D, ssd_norm_w, w_out, norm2_w, ffn_w_gate, ffn_w_up, ffn_w_down, moe_router, moe_w_gate, moe_w_up, moe_w_down, final_norm_w):
    depth = w_in.shape[0]
    bp, seq_len, _ = x_prompt.shape
    bs, dec_len, _ = x_sample.shape
    assert seq_len % CHUNK == 0 and dec_len == CHUNK and depth % 2 == 0
    n_seq = bp + bs
    chunks_per_prompt = seq_len // CHUNK

    x = jnp.concatenate([x_prompt.reshape(bp * seq_len, D_MODEL), x_sample.reshape(bs * dec_len, D_MODEL)], axis=0)
    rope_tab = _rope_table(seq_len, dec_len)
    consts = _mixer_consts()

    sm = jax.nn.softmax(hg_lb_logits.astype(F32), axis=0)
    hg_lb = jnp.cumsum(sm, axis=0) - sm[0]

    def with_prompt_zeros(a):
        return jnp.concatenate([jnp.zeros((bp,) + a.shape[1:], F32), a.astype(F32)], axis=0)

    state_outs = []
    for l in range(depth):
        w = w_in[l]
        w_l = jnp.concatenate([w[:, 0:768], w[:, 776:3592], w[:, 768:776], w[:, 3592:3596],
                               jnp.zeros((D_MODEL, D_PROJ - 3596), w.dtype)], axis=1).astype(BF16)
        lp = {
            "sinks": sw_sinks[l].astype(F32),
            "gbias": _pad_lanes(jnp.concatenate([ml_ig_b[l], ml_fg_b[l], ssd_dt_bias[l]])[None, :].astype(F32), 128),
            "alog": _pad_lanes(ssd_A_log[l][None, :].astype(F32), 128),
            "mlnw": ml_norm_w[l][None, :].astype(F32),
            "hglb": hg_lb[l][None, :],
            "hgnw": hg_norm_w[l][None, :].astype(F32),
            "convw": ssd_conv_w[l].astype(F32),
            "convb": ssd_conv_b[l][None, :].astype(F32),
            "ssdd": jnp.repeat(ssd_D[l].astype(F32), HEAD)[None, :],
            "ssdnw": ssd_norm_w[l][None, :].astype(F32),
        }
        c_aug = jnp.concatenate([state_mlstm_C[l], jnp.broadcast_to(state_mlstm_n[l][..., None], state_mlstm_C[l].shape)], axis=-1)
        m_pad = jnp.pad(state_mlstm_m[l][:, None, :], ((0, 0), (0, 7), (0, 128 - N_HEADS)))
        s_t = jnp.swapaxes(state_hgrn_S[l], -1, -2)
        s_t = jnp.moveaxis(s_t, 1, 2).reshape(bs, HEAD, MIX_W)
        cv_pad = jnp.pad(state_ssd_conv[l], ((0, 0), (8 - (SSD_CONV - 1), 0), (0, 0)))
        init = [with_prompt_zeros(a) for a in (
            c_aug, m_pad, s_t, cache_swa_k[l].reshape(bs, WINDOW, 128), cache_swa_v[l].reshape(bs, WINDOW, 128),
            state_ssd_h[l], cv_pad)]

        proj = _inproj(x, norm1_w[l][None, :], w_l)
        mixed, c_o, m_o, s_o, k_o, v_o, h_o, cv_o = _mixer(
            proj, rope_tab, init, lp, consts, bp, chunks_per_prompt, n_seq)
        wo = w_out[l].astype(BF16)
        j = l // 2
        if l % 2 == 0:
            x = _dense_ffn(x, mixed, wo, norm2_w[l][None, :], ffn_w_gate[j].astype(BF16),
                           ffn_w_up[j].astype(BF16), ffn_w_down[j].astype(BF16))
        else:
            assert l == depth - 1
            x = _moe(x, mixed, wo, norm2_w[l][None, :], moe_router[j], moe_w_gate[j].astype(BF16),
                     moe_w_up[j].astype(BF16), moe_w_down[j].astype(BF16), final_norm_w[None, :])
        s_back = jnp.moveaxis(s_o.reshape(n_seq, HEAD, N_HEADS, HEAD), 2, 1)
        state_outs.append((c_o[..., :HEAD], c_o[:, :, :, HEAD], m_o[:, 0, :N_HEADS],
                           jnp.swapaxes(s_back, -1, -2),
                           k_o.reshape(n_seq, WINDOW, SW_KV, HEAD), v_o.reshape(n_seq, WINDOW, SW_KV, HEAD),
                           h_o, cv_o[:, 8 - (SSD_CONV - 1):, :]))

    n_p = bp * seq_len
    y_prompt = x[:n_p].reshape(bp, seq_len, D_MODEL)
    y_sample = x[n_p:].reshape(bs, dec_len, D_MODEL)
    stacked = [jnp.stack([so[k] for so in state_outs]) for k in range(8)]
    return (y_prompt, y_sample) + tuple(s[:, :bp] for s in stacked) + tuple(s[:, bp:] for s in stacked)
```

```python
import functools

import numpy as np
import jax
import jax.numpy as jnp
from jax import lax
from jax.experimental import pallas as pl
from jax.experimental.pallas import tpu as pltpu
from jax.experimental.pallas import tpu_sc as plsc

F32 = jnp.float32
BF16 = jnp.bfloat16

D_MODEL = 1024
CHUNK = 64
EPS = 1e-6
N_HEADS = 4
HEAD = 64
MIX_W = N_HEADS * HEAD
SW_KV = 2
WINDOW = 128
ROPE_DIMS = 16
ROPE_THETA = 500000.0
SSD_N = 128
SSD_CONV = 4
SSD_CONV_DIM = 768
PAST_LEN = 4096
N_EXPERTS = 8
SUB = 16

C_MLQ, C_MLK, C_MLV, C_MLO = 0, 256, 512, 768
C_HGQ, C_HGF, C_HGI, C_HGG = 1024, 1280, 1536, 1792
C_SWQ, C_SWK, C_SWV = 2048, 2304, 2432
C_SDZ, C_SDX, C_GATE = 2560, 2816, 3584
D_PROJ = 3712
NEG = -1e30
VMEM_LIMIT = 56 * 1024 * 1024
SC_CORES = 2
SC_SUBCORES = 16
SC_MAX_GATHER_ROWS = 128


def _mm(a, b):
    return jnp.dot(a.astype(BF16), b.astype(BF16), preferred_element_type=F32)


def _mm_nt(a, b):
    return lax.dot_general(a.astype(BF16), b.astype(BF16), (((1,), (1,)), ((), ())),
                           preferred_element_type=F32)


def _mm_tn(a, b):
    return jnp.dot(a.T.astype(BF16), b.astype(BF16), preferred_element_type=F32)


def _mm_sel(sel, x):
    hi = x.astype(BF16)
    r1 = x - hi.astype(F32)
    mid = r1.astype(BF16)
    lo = (r1 - mid.astype(F32)).astype(BF16)
    return (jnp.dot(sel, hi, preferred_element_type=F32)
            + jnp.dot(sel, mid, preferred_element_type=F32)
            + jnp.dot(sel, lo, preferred_element_type=F32))


def _sigmoid(x):
    return 1.0 / (1.0 + jnp.exp(-x))


def _silu(x):
    return x * _sigmoid(x)


def _softplus(x):
    return jnp.maximum(x, 0.0) + jnp.log(1.0 + jnp.exp(-jnp.abs(x)))


def _rms(x, w):
    return x * lax.rsqrt(jnp.mean(x * x, axis=-1, keepdims=True) + EPS) * w


def _pick_tile(n, candidates):
    for c in candidates:
        if n % c == 0:
            return c
    raise ValueError(f"no tile for {n}")


def _token_parts_specs(parts, tm):
    bounds = np.cumsum([0] + [p.shape[0] // tm for p in parts])
    specs = []
    for k, p in enumerate(parts):
        assert p.shape[0] % tm == 0
        lo, hi = int(bounds[k]), int(bounds[k + 1])
        specs.append(pl.BlockSpec((tm, p.shape[1]), lambda i, lo=lo, hi=hi: (jnp.clip(i - lo, 0, hi - lo - 1), 0)))
    return specs, [int(b) for b in bounds]


def _read_token_tile(refs, bounds):
    i = pl.program_id(0)
    x = refs[-1][...]
    for k in range(len(refs) - 2, -1, -1):
        x = jnp.where(i < bounds[k + 1], refs[k][...], x)
    return x


def _inproj_kernel(n_parts, bounds, *refs):
    x_refs = refs[:n_parts]
    nw_ref, w_ref, o_ref, wp_ref = refs[n_parts:]

    @pl.when(pl.program_id(0) == 0)
    def _():
        rows = 128
        for r0 in range(0, D_MODEL, rows):
            w = w_ref[r0:r0 + rows, :]
            wp_ref[r0:r0 + rows, 0:C_MLO] = w[:, 0:768].astype(BF16)
            wp_ref[r0:r0 + rows, C_MLO:C_GATE] = w[:, 776:3592].astype(BF16)
            gates = jnp.concatenate([w[:, 768:776], w[:, 3592:3596],
                                     jnp.zeros((rows, D_PROJ - C_GATE - 12), F32)], axis=1)
            wp_ref[r0:r0 + rows, C_GATE:D_PROJ] = gates.astype(BF16)

    xn = _rms(_read_token_tile(x_refs, bounds), nw_ref[...]).astype(BF16)
    o_ref[...] = jnp.dot(xn, wp_ref[...], preferred_element_type=F32)


def _inproj(x_parts, nw, w):
    n = sum(p.shape[0] for p in x_parts)
    tm = _pick_tile(np.gcd.reduce([p.shape[0] for p in x_parts]), (512, 256, 128, 64))
    x_specs, bounds = _token_parts_specs(x_parts, tm)
    d_in = w.shape[1]
    return pl.pallas_call(
        functools.partial(_inproj_kernel, len(x_parts), bounds),
        grid=(n // tm,),
        in_specs=x_specs + [pl.BlockSpec((1, D_MODEL), lambda i: (0, 0)),
                            pl.BlockSpec((D_MODEL, d_in), lambda i: (0, 0), pipeline_mode=pl.Buffered(1))],
        out_specs=pl.BlockSpec((tm, D_PROJ), lambda i: (i, 0)),
        out_shape=jax.ShapeDtypeStruct((n, D_PROJ), F32),
        scratch_shapes=[pltpu.VMEM((D_MODEL, D_PROJ), BF16)],
        compiler_params=pltpu.CompilerParams(dimension_semantics=("arbitrary",),
                                             vmem_limit_bytes=VMEM_LIMIT),
        name="inproj",
    )(*x_parts, nw, w)


def _head_norm(x, width):
    parts = []
    for g in range(x.shape[1] // width):
        xg = x[:, g * width:(g + 1) * width]
        parts.append(xg * lax.rsqrt(jnp.mean(xg * xg, axis=-1, keepdims=True) + EPS))
    return jnp.concatenate(parts, axis=1)


def _mixer_kernel(n_prompt_steps, chunks_per_prompt,
                  sinks_ref,
                  proj_ref, rope_ref, c0_ref, m0_ref, s0_ref, k0_ref, v0_ref, h0_ref, cv0_ref,
                  gbias_ref, alog_ref, mlnw_ref, hglb_ref, hgnw_ref,
                  convw_ref, convb_ref, ssdd_ref, ssdnw_ref,
                  tri_ref, triblk_ref, blkones_ref, ind_ref,
                  mixed_ref, c_ref, m_ref, s_ref, k_ref, v_ref, h_ref, cv_ref,
                  convbuf, bloc_s, kk_s, hv_s):
    i = pl.program_id(0)
    is_prompt = i < n_prompt_steps
    chunk = lax.rem(i, chunks_per_prompt)
    is_first = jnp.logical_or(jnp.logical_not(is_prompt), chunk == 0)
    n_valid = jnp.where(is_prompt, jnp.minimum(chunk * CHUNK, WINDOW), WINDOW)

    @pl.when(is_first)
    def _():
        c_ref[...] = c0_ref[...]
        m_ref[...] = m0_ref[...]
        s_ref[...] = s0_ref[...]
        k_ref[...] = k0_ref[...]
        v_ref[...] = v0_ref[...]
        h_ref[...] = h0_ref[...]
        cv_ref[...] = cv0_ref[...]

    L = CHUNK
    row = lax.broadcasted_iota(jnp.int32, (L, L), 0)
    col = lax.broadcasted_iota(jnp.int32, (L, L), 1)
    causal = row >= col
    tri = tri_ref[...]

    gate = proj_ref[:, C_GATE:C_GATE + 128] + gbias_ref[...]
    ig = gate
    lf = pltpu.roll(-_softplus(-gate), 124, 1)
    dtv = pltpu.roll(_softplus(gate), 120, 1)

    b = _mm_sel(tri, lf)
    a = ig - b
    cm = a
    rowg = lax.broadcasted_iota(jnp.int32, (L, 128), 0)
    for sh in (1, 2, 4, 8, 16, 32):
        cm = jnp.where(rowg >= sh, jnp.maximum(cm, pltpu.roll(cm, sh, 0)), cm)
    m_prev = m_ref[0, 0:1, :]
    m_t = b + jnp.maximum(m_prev, cm)
    inter = jnp.exp(b + m_prev - m_t)
    bm = b - m_t
    a_t = a.T
    m_last = m_t[L - 1:L, :]
    b_last = b[L - 1:L, :]
    ws = jnp.exp(b_last + a - m_last)
    decay = jnp.exp(b_last + m_prev - m_last)
    emt = jnp.exp(-m_t)
    m_ref[0, 0:1, :] = m_last

    ones_h = jnp.ones((L, HEAD), F32)
    ml_out = []
    for h in range(N_HEADS):
        hs = slice(h * HEAD, (h + 1) * HEAD)
        qh = proj_ref[:, C_MLQ + h * HEAD:C_MLQ + (h + 1) * HEAD]
        kh = proj_ref[:, C_MLK + h * HEAD:C_MLK + (h + 1) * HEAD] * (HEAD ** -0.5)
        vh = proj_ref[:, C_MLV + h * HEAD:C_MLV + (h + 1) * HEAD]
        qk = _mm_nt(qh, kh)
        w = jnp.where(causal, jnp.exp(bm[:, h:h + 1] + a_t[h:h + 1, :]), 0.0)
        wqk = w * qk
        v_aug = jnp.concatenate([vh, ones_h], axis=1)
        c_aug = c_ref[0, h]
        nd = _mm(wqk, v_aug) + inter[:, h:h + 1] * _mm(qh, c_aug)
        den = jnp.maximum(jnp.abs(nd[:, HEAD:HEAD + 1]), emt[:, h:h + 1])
        ml_out.append(nd[:, :HEAD] / den)
        c_ref[0, h] = decay[:, h:h + 1] * c_aug + _mm_tn(kh * ws[:, h:h + 1], v_aug)
    ml = _head_norm(jnp.concatenate(ml_out, axis=1), HEAD)
    mixed_ref[:, 0:MIX_W] = (_sigmoid(proj_ref[:, C_MLO:C_MLO + MIX_W]) * ml
                             * mlnw_ref[...]).astype(mixed_ref.dtype)

    lb = hglb_ref[...]
    fg = lb + (1.0 - lb) * _sigmoid(proj_ref[:, C_HGF:C_HGF + MIX_W])
    hq = _silu(proj_ref[:, C_HGQ:C_HGQ + MIX_W]) * (HEAD ** -0.5)
    kk = 1.0 - fg
    lfg = jnp.log(fg)
    hv = proj_ref[:, C_HGI:C_HGI + MIX_W]
    bloc = _mm_sel(triblk_ref[...], lfg)
    blast = _mm_sel(blkones_ref[...], lfg)
    qe = hq * jnp.exp(bloc)
    kw = kk * jnp.exp(blast - bloc)
    bloc_s[...] = bloc
    kk_s[...] = kk
    hv_s[...] = hv
    ind = ind_ref[...]
    rows = lax.broadcasted_iota(jnp.int32, (SUB, MIX_W), 0)
    hg_rows = []
    for blk in range(L // SUB):
        r0 = blk * SUB
        st = s_ref[0]
        o_int = jnp.concatenate(
            [_mm_nt(qe[r0:r0 + SUB, h * HEAD:(h + 1) * HEAD], st[:, h * HEAD:(h + 1) * HEAD])
             for h in range(N_HEADS)], axis=1)
        q_blk = hq[r0:r0 + SUB]
        b_blk = bloc[r0:r0 + SUB]
        xs = []
        for s in range(SUB):
            r = r0 + s
            e = jnp.exp(jnp.minimum(b_blk - bloc_s[r:r + 1, :], 0.0))
            xs.append((q_blk * (kk_s[r:r + 1, :] * e)).astype(BF16))
        att = jnp.dot(jnp.concatenate(xs, axis=0), ind, preferred_element_type=F32)
        o_blk = o_int
        for s in range(SUB):
            r = r0 + s
            o_blk = o_blk + jnp.where(rows >= s, att[s * SUB:(s + 1) * SUB], 0.0) * hv_s[r:r + 1, :]
        hg_rows.append(o_blk)
        upd = jnp.concatenate(
            [_mm_tn(hv[r0:r0 + SUB, h * HEAD:(h + 1) * HEAD], kw[r0:r0 + SUB, h * HEAD:(h + 1) * HEAD])
             for h in range(N_HEADS)], axis=1)
        s_ref[0] = st * jnp.exp(blast[r0:r0 + 1, :]) + upd
    hg = _head_norm(jnp.concatenate(hg_rows, axis=0), HEAD)
    mixed_ref[:, MIX_W:2 * MIX_W] = (hg * hgnw_ref[...]
                                     * _silu(proj_ref[:, C_HGG:C_HGG + MIX_W])).astype(mixed_ref.dtype)

    cos = rope_ref[:, 0:128]
    sin_a = rope_ref[:, 128:256]
    sin_b = rope_ref[:, 256:384]

    def rope(x):
        return x * cos + pltpu.roll(x, 120, 1) * sin_a + pltpu.roll(x, 8, 1) * sin_b

    q_rot = [rope(proj_ref[:, C_SWQ:C_SWQ + 128]), rope(proj_ref[:, C_SWQ + 128:C_SWQ + 256])]
    k_rot = rope(proj_ref[:, C_SWK:C_SWK + 128])
    v_cur = proj_ref[:, C_SWV:C_SWV + 128]
    k_prev = k_ref[0]
    v_prev = v_ref[0]
    colw = lax.broadcasted_iota(jnp.int32, (2 * L, WINDOW), 1)
    row2 = lax.broadcasted_iota(jnp.int32, (2 * L, 1), 0)
    prev_ok = colw >= WINDOW - n_valid
    sw_out = []
    for g in range(SW_KV):
        gs = slice(g * HEAD, (g + 1) * HEAD)
        q2 = jnp.concatenate([q_rot[g][:, 0:HEAD], q_rot[g][:, HEAD:2 * HEAD]], axis=0) * (HEAD ** -0.5)
        s_p = jnp.where(prev_ok, _mm_nt(q2, k_prev[:, gs]), NEG)
        s_c = _mm_nt(q2, k_rot[:, gs])
        sink = jnp.where(row2 < L, sinks_ref[2 * g], sinks_ref[2 * g + 1])
        mx = jnp.maximum(jnp.maximum(jnp.max(s_p, axis=-1, keepdims=True),
                                     jnp.max(s_c, axis=-1, keepdims=True)), sink)
        e_p = jnp.exp(s_p - mx)
        e_c = jnp.exp(s_c - mx)
        den = (jnp.sum(e_p, axis=-1, keepdims=True) + jnp.sum(e_c, axis=-1, keepdims=True)
               + jnp.exp(sink - mx))
        o = (_mm(e_p, v_prev[:, gs]) + _mm(e_c, v_cur[:, gs])) / den
        sw_out.append(o[0:L])
        sw_out.append(o[L:2 * L])
    mixed_ref[:, 2 * MIX_W:3 * MIX_W] = jnp.concatenate(sw_out, axis=1).astype(mixed_ref.dtype)
    k_ref[0, 0:WINDOW - L, :] = k_prev[L:WINDOW]
    k_ref[0, WINDOW - L:WINDOW, :] = k_rot
    v_ref[0, 0:WINDOW - L, :] = v_prev[L:WINDOW]
    v_ref[0, WINDOW - L:WINDOW, :] = v_cur

    convbuf[0:8, :] = cv_ref[0]
    convbuf[8:8 + L, :] = proj_ref[:, C_SDX:C_SDX + SSD_CONV_DIM]
    acc = convb_ref[...] + convbuf[5:5 + L, :] * convw_ref[0:1, :]
    for j in range(1, SSD_CONV):
        acc = acc + convbuf[5 + j:5 + j + L, :] * convw_ref[j:j + 1, :]
    xbc = _silu(acc)
    cv_ref[0] = convbuf[L:L + 8, :]
    xs_all = xbc[:, 0:MIX_W]
    a_neg = -jnp.exp(alog_ref[...])
    ca = _mm_sel(tri, dtv * a_neg)
    ca_t = ca.T
    dt_t = dtv.T
    eca = jnp.exp(ca)
    ca_last = ca[L - 1:L, :]
    wss = jnp.exp(ca_last - ca) * dtv
    sdec = jnp.exp(ca_last)
    sd_out = []
    for g in range(2):
        bm_g = xbc[:, MIX_W + g * SSD_N:MIX_W + (g + 1) * SSD_N]
        cm_g = xbc[:, MIX_W + 2 * SSD_N + g * SSD_N:MIX_W + 2 * SSD_N + (g + 1) * SSD_N]
        cb = _mm_nt(cm_g, bm_g)
        for h in (2 * g, 2 * g + 1):
            xh = xs_all[:, h * HEAD:(h + 1) * HEAD]
            dec = jnp.where(causal, jnp.exp(ca[:, h:h + 1] - ca_t[h:h + 1, :]), 0.0)
            mmat = cb * dec * dt_t[h:h + 1, :]
            hst = h_ref[0, h]
            sd_out.append(_mm(mmat, xh) + eca[:, h:h + 1] * _mm_nt(cm_g, hst))
            h_ref[0, h] = sdec[:, h:h + 1] * hst + _mm_tn(xh * wss[:, h:h + 1], bm_g)
    y = jnp.concatenate(sd_out, axis=1) + ssdd_ref[...] * xs_all
    gated = y * _silu(proj_ref[:, C_SDZ:C_SDZ + MIX_W])
    mixed_ref[:, 3 * MIX_W:4 * MIX_W] = (_head_norm(gated, 2 * HEAD)
                                         * ssdnw_ref[...]).astype(mixed_ref.dtype)


def _mixer(proj, rope_tab, init, lp, consts, n_prompt_seq, chunks_per_prompt, n_seq):
    n = proj.shape[0]
    n_steps = n // CHUNK
    n_prompt_steps = n_prompt_seq * chunks_per_prompt

    def seq_of(i):
        return jnp.where(i < n_prompt_steps, i // chunks_per_prompt, i - n_prompt_steps + n_prompt_seq)

    def rope_blk(i):
        return jnp.where(i < n_prompt_steps, lax.rem(i, chunks_per_prompt), chunks_per_prompt)

    def per_seq(shape):
        nd = len(shape)
        return pl.BlockSpec((1,) + shape, lambda i, s: (seq_of(i),) + (0,) * nd)

    def const(shape):
        nd = len(shape)
        return pl.BlockSpec(shape, lambda i, s: (0,) * nd)

    state_shapes = [(N_HEADS, HEAD, 128), (8, 128), (HEAD, MIX_W), (WINDOW, 128), (WINDOW, 128),
                    (N_HEADS, HEAD, SSD_N), (8, SSD_CONV_DIM)]
    in_specs = ([pl.BlockSpec((CHUNK, D_PROJ), lambda i, s: (i, 0)),
                 pl.BlockSpec((CHUNK, 384), lambda i, s: (rope_blk(i), 0))]
                + [per_seq(s) for s in state_shapes]
                + [const((1, 128)), const((1, 128)), const((1, MIX_W)), const((1, MIX_W)), const((1, MIX_W)),
                   const((SSD_CONV, SSD_CONV_DIM)), const((1, SSD_CONV_DIM)), const((1, MIX_W)),
                   const((1, MIX_W)),
                   const((CHUNK, CHUNK)), const((CHUNK, CHUNK)), const((CHUNK, CHUNK)), const((MIX_W, MIX_W))])
    out_specs = ([pl.BlockSpec((CHUNK, D_MODEL), lambda i, s: (i, 0))]
                 + [per_seq(s) for s in state_shapes])
    out_shape = ([jax.ShapeDtypeStruct((n, D_MODEL), BF16)]
                 + [jax.ShapeDtypeStruct((n_seq,) + s, F32) for s in state_shapes])
    grid_spec = pltpu.PrefetchScalarGridSpec(
        num_scalar_prefetch=1, grid=(n_steps,), in_specs=in_specs, out_specs=out_specs,
        scratch_shapes=[pltpu.VMEM((CHUNK + 8, SSD_CONV_DIM), F32),
                        pltpu.VMEM((CHUNK, MIX_W), F32), pltpu.VMEM((CHUNK, MIX_W), F32),
                        pltpu.VMEM((CHUNK, MIX_W), F32)])
    return pl.pallas_call(
        functools.partial(_mixer_kernel, n_prompt_steps, chunks_per_prompt),
        grid_spec=grid_spec, out_shape=out_shape,
        compiler_params=pltpu.CompilerParams(dimension_semantics=("arbitrary",),
                                             vmem_limit_bytes=VMEM_LIMIT),
        name="mixer",
    )(lp["sinks"], proj, rope_tab, *init,
      lp["gbias"], lp["alog"], lp["mlnw"], lp["hglb"], lp["hgnw"],
      lp["convw"], lp["convb"], lp["ssdd"], lp["ssdnw"], *consts)


def _dense_ffn_kernel(f_chunk, n_parts, bounds, *refs):
    x_refs = refs[:n_parts]
    mix_ref, wo_ref, nw_ref, wg_ref, wu_ref, wd_ref, o_ref = refs[n_parts:]
    x1 = _read_token_tile(x_refs, bounds) + jnp.dot(mix_ref[...], wo_ref[...], preferred_element_type=F32)
    hn = _rms(x1, nw_ref[...]).astype(BF16)
    o_ref[...] = x1
    for f0 in range(0, wg_ref.shape[1], f_chunk):
        g = jnp.dot(hn, wg_ref[:, f0:f0 + f_chunk], preferred_element_type=F32)
        u = jnp.dot(hn, wu_ref[:, f0:f0 + f_chunk], preferred_element_type=F32)
        act = (_silu(g) * u).astype(BF16)
        o_ref[...] += jnp.dot(act, wd_ref[f0:f0 + f_chunk, :], preferred_element_type=F32)


def _dense_ffn(x_parts, mixed, wo, nw, wg, wu, wd):
    n = mixed.shape[0]
    d_ff = wg.shape[1]
    tm = _pick_tile(np.gcd.reduce([p.shape[0] for p in x_parts]), (512, 256, 128, 64))
    x_specs, bounds = _token_parts_specs(x_parts, tm)
    full = lambda shape: pl.BlockSpec(shape, lambda i: (0, 0))
    return pl.pallas_call(
        functools.partial(_dense_ffn_kernel, 256, len(x_parts), bounds),
        grid=(n // tm,),
        in_specs=x_specs + [
                  pl.BlockSpec((tm, D_MODEL), lambda i: (i, 0)),
                  full((D_MODEL, D_MODEL)), full((1, D_MODEL)),
                  full((D_MODEL, d_ff)), full((D_MODEL, d_ff)), full((d_ff, D_MODEL))],
        out_specs=pl.BlockSpec((tm, D_MODEL), lambda i: (i, 0)),
        out_shape=jax.ShapeDtypeStruct((n, D_MODEL), F32),
        compiler_params=pltpu.CompilerParams(dimension_semantics=("arbitrary",),
                                             vmem_limit_bytes=VMEM_LIMIT),
        name="dense_ffn",
    )(*x_parts, mixed, wo, nw, wg, wu, wd)


TOK_BLK = 256
FFN_TILE = 1024


def _pack_bf16_pairs(x):
    w = x.shape[1] // 2
    bits = pltpu.bitcast(x.astype(BF16).astype(F32), jnp.int32)
    return lax.shift_right_logical(bits[:, :w], 16) | bits[:, w:]


def _unpack_bf16_pairs(p):
    lo = pltpu.bitcast(lax.shift_left(p, 16), F32)
    hi = pltpu.bitcast(p & jnp.int32(-65536), F32)
    return lo, hi


def _router_kernel(x_ref, mix_ref, wo_ref, nw_ref, wrt_ref, upper_ref,
                   x1_ref, hn_ref, meta_ref, count_ref, carry):
    @pl.when(pl.program_id(0) == 0)
    def _():
        carry[...] = jnp.zeros_like(carry)

    x1 = x_ref[...] + jnp.dot(mix_ref[...], wo_ref[...], preferred_element_type=F32)
    x1_ref[...] = x1
    hn_f = _rms(x1, nw_ref[...])
    hn = hn_f.astype(BF16)
    hn_ref[...] = _pack_bf16_pairs(hn_f)
    logits = lax.dot_general(wrt_ref[...], hn, (((1,), (1,)), ((), ())),
                             preferred_element_type=F32)
    sub = lax.broadcasted_iota(jnp.int32, logits.shape, 0)
    l1 = jnp.where(sub < N_EXPERTS, logits, NEG)
    m1 = jnp.max(l1, axis=0, keepdims=True)
    i1 = jnp.min(jnp.where(l1 == m1, sub, 99), axis=0, keepdims=True)
    l2 = jnp.where(sub == i1, NEG, l1)
    m2 = jnp.max(l2, axis=0, keepdims=True)
    i2 = jnp.min(jnp.where(l2 == m2, sub, 99), axis=0, keepdims=True)
    e2 = jnp.exp(m2 - m1)
    g1 = 1.0 / (1.0 + e2)
    g2 = e2 / (1.0 + e2)
    sel = jnp.where(sub == i1, 1.0, jnp.where(sub == i2, 1.0, 0.0))
    before = jnp.dot(sel.astype(BF16), upper_ref[...], preferred_element_type=F32) + carry[:, 0:1]
    r1 = jnp.sum(jnp.where(sub == i1, before, 0.0), axis=0, keepdims=True)
    r2 = jnp.sum(jnp.where(sub == i2, before, 0.0), axis=0, keepdims=True)
    meta_ref[...] = jnp.concatenate([i1.astype(F32), i2.astype(F32), r1, r2, g1, g2,
                                     jnp.zeros((2, r1.shape[1]), F32)], axis=0)
    total = carry[...] + jnp.sum(sel, axis=1, keepdims=True)
    carry[...] = total
    count_ref[...] = total


def _router(x, mixed, wo, nw, wrt, upper):
    n = x.shape[0]
    tm = TOK_BLK
    full = lambda shape: pl.BlockSpec(shape, lambda i: (0,) * len(shape))
    tok = lambda w: pl.BlockSpec((tm, w), lambda i: (i, 0))
    return pl.pallas_call(
        _router_kernel,
        grid=(n // tm,),
        in_specs=[tok(D_MODEL), tok(D_MODEL), full((D_MODEL, D_MODEL)), full((1, D_MODEL)),
                  full((16, D_MODEL)), full((tm, tm))],
        out_specs=[tok(D_MODEL), tok(D_MODEL // 2),
                   pl.BlockSpec((8, tm), lambda i: (0, i)), full((16, 128))],
        out_shape=[jax.ShapeDtypeStruct((n, D_MODEL), F32), jax.ShapeDtypeStruct((n, D_MODEL // 2), jnp.int32),
                   jax.ShapeDtypeStruct((8, n), F32), jax.ShapeDtypeStruct((16, 128), F32)],
        scratch_shapes=[pltpu.VMEM((16, 128), F32)],
        compiler_params=pltpu.CompilerParams(dimension_semantics=("arbitrary",),
                                             vmem_limit_bytes=VMEM_LIMIT),
        name="router",
    )(x, mixed, wo, nw, wrt, upper)


def _sc_row_gather(table, idx):
    n_workers = SC_CORES * SC_SUBCORES
    b = idx.shape[0]
    d = table.shape[1]
    assert b % (8 * n_workers) == 0
    per_worker = b // n_workers
    chunk = max(c for c in range(8, SC_MAX_GATHER_ROWS + 1, 8) if per_worker % c == 0)
    mesh = plsc.VectorSubcoreMesh(core_axis_name="c", subcore_axis_name="s")

    @functools.partial(
        pl.kernel, mesh=mesh, out_type=jax.ShapeDtypeStruct((b, d), table.dtype),
        scratch_types=[pltpu.VMEM((chunk,), jnp.int32), pltpu.VMEM((chunk, d), table.dtype),
                       pltpu.SemaphoreType.DMA])
    def gather(table_hbm, idx_hbm, out_hbm, idx_v, rows_v, sem):
        worker = lax.axis_index("s") * SC_CORES + lax.axis_index("c")
        base = worker * per_worker

        @pl.loop(0, per_worker // chunk)
        def _(c):
            off = pl.multiple_of(base + c * chunk, 8)
            pltpu.sync_copy(idx_hbm.at[pl.ds(off, chunk)], idx_v)
            pltpu.async_copy(table_hbm.at[idx_v], rows_v, sem).wait()
            pltpu.sync_copy(rows_v, out_hbm.at[pl.ds(off, chunk)])

    return gather(table, idx)


def _expert_ffn_kernel(texp_ref, tvalid_ref, xs_ref, wg_ref, wu_ref, wd_ref, o_ref, acc_ref, xb_ref):
    j = pl.program_id(0)
    f = pl.program_id(1)
    last = f == pl.num_programs(1) - 1
    valid = tvalid_ref[j] == 1
    half = D_MODEL // 2

    @pl.when(jnp.logical_and(valid, f == 0))
    def _():
        lo, hi = _unpack_bf16_pairs(xs_ref[...])
        xb_ref[:, :half] = lo.astype(BF16)
        xb_ref[:, half:] = hi.astype(BF16)

    @pl.when(valid)
    def _():
        xs = xb_ref[...]
        g = jnp.dot(xs, wg_ref[0], preferred_element_type=F32)
        u = jnp.dot(xs, wu_ref[0], preferred_element_type=F32)
        act = (_silu(g) * u).astype(BF16)
        part = jnp.dot(act, wd_ref[0], preferred_element_type=F32)

        @pl.when(f == 0)
        def _():
            acc_ref[...] = part

        @pl.when(f > 0)
        def _():
            acc_ref[...] += part

        @pl.when(last)
        def _():
            o_ref[...] = _pack_bf16_pairs(acc_ref[...])

    @pl.when(jnp.logical_and(jnp.logical_not(valid), last))
    def _():
        o_ref[...] = jnp.zeros_like(o_ref)


def _expert_ffn(f_exp, f_valid, xs, wg, wu, wd):
    n_slots = xs.shape[0]
    d_ff = wg.shape[2]
    tf = _pick_tile(d_ff, (512, 256, 128))
    grid_spec = pltpu.PrefetchScalarGridSpec(
        num_scalar_prefetch=2, grid=(n_slots // FFN_TILE, d_ff // tf),
        in_specs=[pl.BlockSpec((FFN_TILE, D_MODEL // 2), lambda j, f, te, tv: (j, 0)),
                  pl.BlockSpec((1, D_MODEL, tf), lambda j, f, te, tv: (te[j], 0, f)),
                  pl.BlockSpec((1, D_MODEL, tf), lambda j, f, te, tv: (te[j], 0, f)),
                  pl.BlockSpec((1, tf, D_MODEL), lambda j, f, te, tv: (te[j], f, 0))],
        out_specs=pl.BlockSpec((FFN_TILE, D_MODEL // 2), lambda j, f, te, tv: (j, 0)),
        scratch_shapes=[pltpu.VMEM((FFN_TILE, D_MODEL), F32), pltpu.VMEM((FFN_TILE, D_MODEL), BF16)])
    return pl.pallas_call(
        _expert_ffn_kernel, grid_spec=grid_spec,
        out_shape=jax.ShapeDtypeStruct((n_slots, D_MODEL // 2), jnp.int32),
        compiler_params=pltpu.CompilerParams(dimension_semantics=("arbitrary", "arbitrary"),
                                             vmem_limit_bytes=VMEM_LIMIT),
        name="expert_ffn",
    )(f_exp, f_valid, xs, wg, wu, wd)


def _combine_kernel(part_tiles, x1_ref, o1_ref, o2_ref, gate_ref, fnw_ref, *y_refs):
    i = pl.program_id(0)
    lo1, hi1 = _unpack_bf16_pairs(o1_ref[...])
    lo2, hi2 = _unpack_bf16_pairs(o2_ref[...])
    g1 = gate_ref[:, 0:1]
    g2 = gate_ref[:, 1:2]
    half = D_MODEL // 2
    x1 = x1_ref[...]
    fnw = fnw_ref[...]
    ya = x1[:, :half] + g1 * lo1 + g2 * lo2
    yb = x1[:, half:] + g1 * hi1 + g2 * hi2
    scale = lax.rsqrt((jnp.sum(ya * ya, axis=-1, keepdims=True) + jnp.sum(yb * yb, axis=-1, keepdims=True))
                      / D_MODEL + EPS)
    lo_tile = 0
    for y_ref, tiles in zip(y_refs, part_tiles):
        @pl.when(jnp.logical_and(i >= lo_tile, i < lo_tile + tiles))
        def _(y_ref=y_ref):
            y_ref[:, :half] = ya * scale * fnw[:, :half]
            y_ref[:, half:] = yb * scale * fnw[:, half:]
        lo_tile += tiles


def _combine(x1, o12, gates, fnw, part_rows):
    n = x1.shape[0]
    tm = _pick_tile(np.gcd.reduce(list(part_rows)), (512, 256, 128, 64))
    n_tiles = n // tm
    part_tiles = [r // tm for r in part_rows]
    starts = np.cumsum([0] + part_tiles)
    out_specs = [pl.BlockSpec((tm, D_MODEL), lambda i, lo=int(starts[k]), t=part_tiles[k]: (jnp.clip(i - lo, 0, t - 1), 0))
                 for k in range(len(part_rows))]
    return pl.pallas_call(
        functools.partial(_combine_kernel, part_tiles),
        grid=(n_tiles,),
        in_specs=[pl.BlockSpec((tm, D_MODEL), lambda i: (i, 0)),
                  pl.BlockSpec((tm, D_MODEL // 2), lambda i: (i, 0)),
                  pl.BlockSpec((tm, D_MODEL // 2), lambda i: (i + n_tiles, 0)),
                  pl.BlockSpec((tm, 128), lambda i: (i, 0)),
                  pl.BlockSpec((1, D_MODEL), lambda i: (0, 0))],
        out_specs=out_specs,
        out_shape=[jax.ShapeDtypeStruct((r, D_MODEL), F32) for r in part_rows],
        compiler_params=pltpu.CompilerParams(dimension_semantics=("arbitrary",),
                                             vmem_limit_bytes=VMEM_LIMIT),
        name="combine",
    )(x1, o12, o12, gates, fnw)


def _moe(x, mixed, wo, nw, router_w, wg, wu, wd, fnw, part_rows):
    i32 = jnp.int32
    n = x.shape[0]
    assert n % TOK_BLK == 0
    wrt = jnp.pad(router_w.T, ((0, 16 - N_EXPERTS), (0, 0))).astype(BF16)
    t = np.arange(TOK_BLK)
    upper = jnp.asarray((t[:, None] < t[None, :]).astype(np.float32), dtype=BF16)
    x1, hn_p, meta, count = _router(x, mixed, wo, nw, wrt, upper)

    n_slots = (-(-2 * n // FFN_TILE) + N_EXPERTS) * FFN_TILE
    counts = count[:N_EXPERTS, 0].astype(i32)
    gsize = (counts + FFN_TILE - 1) // FFN_TILE * FFN_TILE
    gend = jnp.cumsum(gsize)
    goff = gend - gsize
    fstart = jnp.arange(n_slots // FFN_TILE, dtype=i32) * FFN_TILE
    f_valid = (fstart < gend[-1]).astype(i32)
    f_exp = jnp.sum((fstart[:, None] >= gend[None, :]).astype(i32), axis=1)
    f_exp = jnp.minimum(f_exp, jnp.sum((gend[-1] - 1 >= gend).astype(i32)))
    expert_ids = jnp.arange(N_EXPERTS, dtype=i32)[:, None]
    top = meta[0:2].astype(i32)
    rank = meta[2:4].astype(i32)
    slot = jnp.stack([jnp.sum(jnp.where(top[k][None, :] == expert_ids, goff[:, None], 0), axis=0) + rank[k]
                      for k in range(2)]).reshape(-1)
    tok = jnp.tile(jnp.arange(n, dtype=i32), 2)
    src_tok = jnp.zeros((n_slots,), i32).at[slot].set(tok)

    xs_p = _sc_row_gather(hn_p, src_tok)
    out_p = _expert_ffn(f_exp, f_valid, xs_p, wg, wu, wd)
    o12 = _sc_row_gather(out_p, slot)
    gates = _pad_lanes(meta[4:6].T, 128)
    return _combine(x1, o12, gates, fnw, part_rows)


def _rope_table(seq_len, dec_len):
    half = ROPE_DIMS // 2
    pos = jnp.concatenate([jnp.arange(seq_len), PAST_LEN + jnp.arange(dec_len)]).astype(F32)
    inv_freq = jnp.power(jnp.float32(ROPE_THETA), -jnp.arange(half, dtype=F32) / half)
    lane = np.arange(128) % HEAD
    inv_lane = jnp.where(lane < ROPE_DIMS, inv_freq[lane % half], 0.0)
    ang = pos[:, None] * inv_lane[None, :]
    cos, sin = jnp.cos(ang), jnp.sin(ang)
    sin_a = jnp.where(lane < half, -sin, 0.0)
    sin_b = jnp.where((lane >= half) & (lane < ROPE_DIMS), sin, 0.0)
    return jnp.concatenate([cos, sin_a, sin_b], axis=1)


def _mixer_consts():
    t = np.arange(CHUNK)
    tri = (t[:, None] >= t[None, :])
    same = (t[:, None] // SUB) == (t[None, :] // SUB)
    c = np.arange(MIX_W)
    ind = (c[:, None] // HEAD) == (c[None, :] // HEAD)
    as_bf = lambda m: jnp.asarray(m.astype(np.float32), dtype=BF16)
    return [as_bf(tri), as_bf(tri & same), as_bf(same), as_bf(ind)]


def _pad_lanes(v, width):
    return jnp.pad(v, ((0, 0), (0, width - v.shape[1])))


def kernel(x_prompt, x_sample, state_mlstm_C, state_mlstm_n, state_mlstm_m, state_hgrn_S, cache_swa_k, cache_swa_v, state_ssd_h, state_ssd_conv, norm1_w, w_in, ml_ig_b, ml_fg_b, ml_norm_w, hg_lb_logits, hg_norm_w, sw_sinks, ssd_conv_w, ssd_conv_b, ssd_dt_bias, ssd_A_log, ssd_D, ssd_norm_w, w_out, norm2_w, ffn_w_gate, ffn_w_up, ffn_w_down, moe_router, moe_w_gate, moe_w_up, moe_w_down, final_norm_w):
    depth = w_in.shape[0]
    bp, seq_len, _ = x_prompt.shape
    bs, dec_len, _ = x_sample.shape
    assert seq_len % CHUNK == 0 and dec_len == CHUNK and depth % 2 == 0
    n_seq = bp + bs
    chunks_per_prompt = seq_len // CHUNK

    x_parts = (x_prompt.reshape(bp * seq_len, D_MODEL), x_sample.reshape(bs * dec_len, D_MODEL))
    part_rows = tuple(p.shape[0] for p in x_parts)
    rope_tab = _rope_table(seq_len, dec_len)
    consts = _mixer_consts()

    sm = jax.nn.softmax(hg_lb_logits.astype(F32), axis=0)
    hg_lb = jnp.cumsum(sm, axis=0) - sm[0]

    def with_prompt_zeros(a):
        return jnp.concatenate([jnp.zeros((bp,) + a.shape[1:], F32), a.astype(F32)], axis=0)

    state_outs = []
    for l in range(depth):
        lp = {
            "sinks": sw_sinks[l].astype(F32),
            "gbias": _pad_lanes(jnp.concatenate([ml_ig_b[l], ml_fg_b[l], ssd_dt_bias[l]])[None, :].astype(F32), 128),
            "alog": _pad_lanes(ssd_A_log[l][None, :].astype(F32), 128),
            "mlnw": ml_norm_w[l][None, :].astype(F32),
            "hglb": hg_lb[l][None, :],
            "hgnw": hg_norm_w[l][None, :].astype(F32),
            "convw": ssd_conv_w[l].astype(F32),
            "convb": ssd_conv_b[l][None, :].astype(F32),
            "ssdd": jnp.repeat(ssd_D[l].astype(F32), HEAD)[None, :],
            "ssdnw": ssd_norm_w[l][None, :].astype(F32),
        }
        c_aug = jnp.concatenate([state_mlstm_C[l], jnp.broadcast_to(state_mlstm_n[l][..., None], state_mlstm_C[l].shape)], axis=-1)
        m_pad = jnp.pad(state_mlstm_m[l][:, None, :], ((0, 0), (0, 7), (0, 128 - N_HEADS)))
        s_t = jnp.swapaxes(state_hgrn_S[l], -1, -2)
        s_t = jnp.moveaxis(s_t, 1, 2).reshape(bs, HEAD, MIX_W)
        cv_pad = jnp.pad(state_ssd_conv[l], ((0, 0), (8 - (SSD_CONV - 1), 0), (0, 0)))
        init = [with_prompt_zeros(a) for a in (
            c_aug, m_pad, s_t, cache_swa_k[l].reshape(bs, WINDOW, 128), cache_swa_v[l].reshape(bs, WINDOW, 128),
            state_ssd_h[l], cv_pad)]

        proj = _inproj(x_parts, norm1_w[l][None, :], w_in[l])
        mixed, c_o, m_o, s_o, k_o, v_o, h_o, cv_o = _mixer(
            proj, rope_tab, init, lp, consts, bp, chunks_per_prompt, n_seq)
        wo = w_out[l].astype(BF16)
        j = l // 2
        if l % 2 == 0:
            x_parts = (_dense_ffn(x_parts, mixed, wo, norm2_w[l][None, :], ffn_w_gate[j].astype(BF16),
                                  ffn_w_up[j].astype(BF16), ffn_w_down[j].astype(BF16)),)
        else:
            assert l == depth - 1 and len(x_parts) == 1
            y_parts = _moe(x_parts[0], mixed, wo, norm2_w[l][None, :], moe_router[j], moe_w_gate[j].astype(BF16),
                           moe_w_up[j].astype(BF16), moe_w_down[j].astype(BF16), final_norm_w[None, :],
                           part_rows)
        s_back = jnp.moveaxis(s_o.reshape(n_seq, HEAD, N_HEADS, HEAD), 2, 1)
        state_outs.append((c_o[..., :HEAD], c_o[:, :, :, HEAD], m_o[:, 0, :N_HEADS],
                           jnp.swapaxes(s_back, -1, -2),
                           k_o.reshape(n_seq, WINDOW, SW_KV, HEAD), v_o.reshape(n_seq, WINDOW, SW_KV, HEAD),
                           h_o, cv_o[:, 8 - (SSD_CONV - 1):, :]))

    y_prompt = y_parts[0].reshape(bp, seq_len, D_MODEL)
    y_sample = y_parts[1].reshape(bs, dec_len, D_MODEL)
    stacked = [jnp.stack([so[k] for so in state_outs]) for k in range(8)]
    return (y_prompt, y_sample) + tuple(s[:, :bp] for s in stacked) + tuple(s[:, bp:] for s in stacked)
```

```python
import functools

import numpy as np
import jax
import jax.numpy as jnp
from jax import lax
from jax.experimental import pallas as pl
from jax.experimental.pallas import tpu as pltpu
from jax.experimental.pallas import tpu_sc as plsc

F32 = jnp.float32
BF16 = jnp.bfloat16

D_MODEL = 1024
CHUNK = 64
EPS = 1e-6
N_HEADS = 4
HEAD = 64
MIX_W = N_HEADS * HEAD
SW_KV = 2
WINDOW = 128
ROPE_DIMS = 16
ROPE_THETA = 500000.0
SSD_N = 128
SSD_CONV = 4
SSD_CONV_DIM = 768
PAST_LEN = 4096
N_EXPERTS = 8
SUB = 16
SEQ_GROUP = 2
TOKEN_TILES = (512, 256, 128)

C_MLQ, C_MLK, C_MLV, C_MLO = 0, 256, 512, 768
C_HGQ, C_HGF, C_HGI, C_HGG = 1024, 1280, 1536, 1792
C_SWQ, C_SWK, C_SWV = 2048, 2304, 2432
C_SDZ, C_SDX, C_GATE = 2560, 2816, 3584
D_PROJ = 3712
NEG = -1e30
VMEM_LIMIT = 56 * 1024 * 1024
SC_CORES = 2
SC_SUBCORES = 16
SC_MAX_GATHER_ROWS = 128


def _mm(a, b):
    return jnp.dot(a.astype(BF16), b.astype(BF16), preferred_element_type=F32)


def _mm_nt(a, b):
    return lax.dot_general(a.astype(BF16), b.astype(BF16), (((1,), (1,)), ((), ())),
                           preferred_element_type=F32)


def _mm_tn(a, b):
    return jnp.dot(a.T.astype(BF16), b.astype(BF16), preferred_element_type=F32)


def _mm_sel(sel, x):
    hi = x.astype(BF16)
    r1 = x - hi.astype(F32)
    mid = r1.astype(BF16)
    lo = (r1 - mid.astype(F32)).astype(BF16)
    return (jnp.dot(sel, hi, preferred_element_type=F32)
            + jnp.dot(sel, mid, preferred_element_type=F32)
            + jnp.dot(sel, lo, preferred_element_type=F32))


def _sigmoid(x):
    return 1.0 / (1.0 + jnp.exp(-x))


def _silu(x):
    return x * _sigmoid(x)


def _softplus(x):
    return jnp.maximum(x, 0.0) + jnp.log(1.0 + jnp.exp(-jnp.abs(x)))


def _rms(x, w):
    return x * lax.rsqrt(jnp.mean(x * x, axis=-1, keepdims=True) + EPS) * w


def _pick_tile(n, candidates):
    for c in candidates:
        if n % c == 0:
            return c
    raise ValueError(f"no tile for {n}")


def _token_parts_specs(parts, tm):
    bounds = np.cumsum([0] + [_part_rows(p) // tm for p in parts])
    specs = []
    for k, p in enumerate(parts):
        assert _part_rows(p) % tm == 0
        lo, hi = int(bounds[k]), int(bounds[k + 1])
        specs.append(_part_block_spec(p, tm, lo, hi - lo))
    return specs, [int(b) for b in bounds]


def _part_rows(p):
    return int(np.prod(p.shape[:-1]))


def _part_block_spec(p, tm, first_tile, n_tiles):
    idx = lambda i: jnp.clip(i - first_tile, 0, n_tiles - 1)
    if len(p.shape) == 2:
        return pl.BlockSpec((tm, p.shape[1]), lambda i: (idx(i), 0))
    grp, _, chunk, d = p.shape
    assert tm % (grp * chunk) == 0
    return pl.BlockSpec((grp, tm // (grp * chunk), chunk, d), lambda i: (0, idx(i), 0, 0))


def _load_part_tile(ref):
    if len(ref.shape) == 2:
        return ref[...]
    grp, n_chunks = ref.shape[:2]
    return jnp.concatenate([ref[s, c] for c in range(n_chunks) for s in range(grp)], axis=0)


def _store_part_tile(ref, x):
    if len(ref.shape) == 2:
        ref[...] = x
        return
    grp, n_chunks, chunk = ref.shape[:3]
    for c in range(n_chunks):
        for s in range(grp):
            r0 = (c * grp + s) * chunk
            ref[s, c] = x[r0:r0 + chunk]


def _read_token_tile(refs, bounds):
    i = pl.program_id(0)
    x = _load_part_tile(refs[-1])
    for k in range(len(refs) - 2, -1, -1):
        x = jnp.where(i < bounds[k + 1], _load_part_tile(refs[k]), x)
    return x


def _inproj_kernel(n_parts, bounds, *refs):
    x_refs = refs[:n_parts]
    nw_ref, w_ref, o_ref, wp_ref = refs[n_parts:]

    @pl.when(pl.program_id(0) == 0)
    def _():
        rows = 128
        for r0 in range(0, D_MODEL, rows):
            w = w_ref[r0:r0 + rows, :]
            wp_ref[r0:r0 + rows, 0:C_MLO] = w[:, 0:768].astype(BF16)
            wp_ref[r0:r0 + rows, C_MLO:C_GATE] = w[:, 776:3592].astype(BF16)
            gates = jnp.concatenate([w[:, 768:776], w[:, 3592:3596],
                                     jnp.zeros((rows, D_PROJ - C_GATE - 12), F32)], axis=1)
            wp_ref[r0:r0 + rows, C_GATE:D_PROJ] = gates.astype(BF16)

    xn = _rms(_read_token_tile(x_refs, bounds), nw_ref[...]).astype(BF16)
    o_ref[...] = jnp.dot(xn, wp_ref[...], preferred_element_type=F32)


def _inproj(x_parts, nw, w):
    n = sum(_part_rows(p) for p in x_parts)
    tm = _pick_tile(np.gcd.reduce([_part_rows(p) for p in x_parts]), TOKEN_TILES)
    x_specs, bounds = _token_parts_specs(x_parts, tm)
    d_in = w.shape[1]
    return pl.pallas_call(
        functools.partial(_inproj_kernel, len(x_parts), bounds),
        grid=(n // tm,),
        in_specs=x_specs + [pl.BlockSpec((1, D_MODEL), lambda i: (0, 0)),
                            pl.BlockSpec((D_MODEL, d_in), lambda i: (0, 0), pipeline_mode=pl.Buffered(1))],
        out_specs=pl.BlockSpec((tm, D_PROJ), lambda i: (i, 0)),
        out_shape=jax.ShapeDtypeStruct((n, D_PROJ), F32),
        scratch_shapes=[pltpu.VMEM((D_MODEL, D_PROJ), BF16)],
        compiler_params=pltpu.CompilerParams(dimension_semantics=("arbitrary",),
                                             vmem_limit_bytes=VMEM_LIMIT),
        name="inproj",
    )(*x_parts, nw, w)


def _head_norm(x, width):
    parts = []
    for g in range(x.shape[1] // width):
        xg = x[:, g * width:(g + 1) * width]
        parts.append(xg * lax.rsqrt(jnp.mean(xg * xg, axis=-1, keepdims=True) + EPS))
    return jnp.concatenate(parts, axis=1)


def _mixer_kernel(prompt_steps,
                  sinks_ref,
                  proj_ref, rope_ref, c0_ref, m0_ref, s0_ref, k0_ref, v0_ref, h0_ref, cv0_ref,
                  gbias_ref, alog_ref, mlnw_ref, hglb_ref, hgnw_ref,
                  convw_ref, convb_ref, ssdd_ref, ssdnw_ref,
                  tri_ref, triblk_ref, blkones_ref, ind_ref,
                  mixed_ref, c_ref, m_ref, s_ref, k_ref, v_ref, h_ref, cv_ref,
                  convbuf, bloc_s, kk_s, hv_s):
    i = pl.program_id(0)
    is_prompt = i < prompt_steps
    is_first = jnp.logical_or(jnp.logical_not(is_prompt), i == 0)
    n_valid = jnp.where(is_prompt, jnp.minimum(i * CHUNK, WINDOW), WINDOW)

    @pl.when(is_first)
    def _():
        c_ref[...] = c0_ref[...]
        m_ref[...] = m0_ref[...]
        s_ref[...] = s0_ref[...]
        k_ref[...] = k0_ref[...]
        v_ref[...] = v0_ref[...]
        h_ref[...] = h0_ref[...]
        cv_ref[...] = cv0_ref[...]

    members = [_mixer_chunk(g, n_valid, sinks_ref, proj_ref, rope_ref,
                            gbias_ref, alog_ref, mlnw_ref, hglb_ref, hgnw_ref,
                            convw_ref, convb_ref, ssdd_ref, ssdnw_ref,
                            tri_ref, triblk_ref, blkones_ref, ind_ref,
                            mixed_ref, c_ref, m_ref, s_ref, k_ref, v_ref, h_ref, cv_ref,
                            convbuf, bloc_s, kk_s, hv_s) for g in range(SEQ_GROUP)]
    for phase in zip(*[m[0] for m in members]):
        chains = [c for per_member in zip(*phase) for c in per_member]
        while chains:
            alive = []
            for chain in chains:
                if next(chain, _CHAIN_DONE) is not _CHAIN_DONE:
                    alive.append(chain)
            chains = alive
    for _, finish in members:
        finish()


_CHAIN_DONE = object()


def _mixer_chunk(g, n_valid, sinks_ref, proj_ref, rope_ref,
                 gbias_ref, alog_ref, mlnw_ref, hglb_ref, hgnw_ref,
                 convw_ref, convb_ref, ssdd_ref, ssdnw_ref,
                 tri_ref, triblk_ref, blkones_ref, ind_ref,
                 mixed_ref, c_ref, m_ref, s_ref, k_ref, v_ref, h_ref, cv_ref,
                 convbuf, bloc_s, kk_s, hv_s):
    rs = slice(g * CHUNK, (g + 1) * CHUNK)
    L = CHUNK
    row = lax.broadcasted_iota(jnp.int32, (L, L), 0)
    col = lax.broadcasted_iota(jnp.int32, (L, L), 1)
    causal = row >= col
    tri = tri_ref[...]

    gate = proj_ref[rs,C_GATE:C_GATE + 128] + gbias_ref[...]
    ig = gate
    lf = pltpu.roll(-_softplus(-gate), 124, 1)
    dtv = pltpu.roll(_softplus(gate), 120, 1)

    b = _mm_sel(tri, lf)
    a = ig - b
    cm = a
    rowg = lax.broadcasted_iota(jnp.int32, (L, 128), 0)
    for sh in (1, 2, 4, 8, 16, 32):
        cm = jnp.where(rowg >= sh, jnp.maximum(cm, pltpu.roll(cm, sh, 0)), cm)
    m_prev = m_ref[g, 0:1, :]
    m_t = b + jnp.maximum(m_prev, cm)
    inter = jnp.exp(b + m_prev - m_t)
    bm = b - m_t
    a_t = a.T
    m_last = m_t[L - 1:L, :]
    b_last = b[L - 1:L, :]
    ws = jnp.exp(b_last + a - m_last)
    decay = jnp.exp(b_last + m_prev - m_last)
    emt = jnp.exp(-m_t)
    m_ref[g, 0:1, :] = m_last

    ones_h = jnp.ones((L, HEAD), F32)
    ml_out = [None] * N_HEADS

    def ml_head(h):
        qh = proj_ref[rs,C_MLQ + h * HEAD:C_MLQ + (h + 1) * HEAD]
        kh = proj_ref[rs,C_MLK + h * HEAD:C_MLK + (h + 1) * HEAD] * (HEAD ** -0.5)
        vh = proj_ref[rs,C_MLV + h * HEAD:C_MLV + (h + 1) * HEAD]
        qk = _mm_nt(qh, kh)
        yield
        c_aug = c_ref[g, h]
        from_state = _mm(qh, c_aug)
        yield
        w = jnp.where(causal, jnp.exp(bm[:, h:h + 1] + a_t[h:h + 1, :]), 0.0)
        v_aug = jnp.concatenate([vh, ones_h], axis=1)
        nd = _mm(w * qk, v_aug) + inter[:, h:h + 1] * from_state
        yield
        upd = _mm_tn(kh * ws[:, h:h + 1], v_aug)
        yield
        den = jnp.maximum(jnp.abs(nd[:, HEAD:HEAD + 1]), emt[:, h:h + 1])
        ml_out[h] = nd[:, :HEAD] / den
        c_ref[g, h] = decay[:, h:h + 1] * c_aug + upd

    def ml_finish():
        ml = _head_norm(jnp.concatenate(ml_out, axis=1), HEAD)
        mixed_ref[rs,0:MIX_W] = (_sigmoid(proj_ref[rs,C_MLO:C_MLO + MIX_W]) * ml
                                 * mlnw_ref[...]).astype(mixed_ref.dtype)

    lb = hglb_ref[...]
    fg = lb + (1.0 - lb) * _sigmoid(proj_ref[rs,C_HGF:C_HGF + MIX_W])
    hq = _silu(proj_ref[rs,C_HGQ:C_HGQ + MIX_W]) * (HEAD ** -0.5)
    kk = 1.0 - fg
    lfg = jnp.log(fg)
    hv = proj_ref[rs,C_HGI:C_HGI + MIX_W]
    bloc = _mm_sel(triblk_ref[...], lfg)
    blast = _mm_sel(blkones_ref[...], lfg)
    qe = hq * jnp.exp(bloc)
    kw = kk * jnp.exp(blast - bloc)
    bloc_s[g] = bloc
    kk_s[g] = kk
    hv_s[g] = hv
    ind = ind_ref[...]
    rows = lax.broadcasted_iota(jnp.int32, (SUB, MIX_W), 0)
    n_blk = L // SUB
    hg_state_part = [None] * n_blk
    hg_block_part = [None] * n_blk

    def hg_within_block(blk):
        r0 = blk * SUB
        q_blk = hq[r0:r0 + SUB]
        b_blk = bloc[r0:r0 + SUB]
        xs = []
        for s in range(SUB):
            r = r0 + s
            e = jnp.exp(jnp.minimum(b_blk - bloc_s[g, r:r + 1, :], 0.0))
            xs.append((q_blk * (kk_s[g, r:r + 1, :] * e)).astype(BF16))
        att = jnp.dot(jnp.concatenate(xs, axis=0), ind, preferred_element_type=F32)
        yield
        o_blk = att[0:SUB] * hv_s[g, r0:r0 + 1, :]
        for s in range(1, SUB):
            r = r0 + s
            o_blk = o_blk + jnp.where(rows >= s, att[s * SUB:(s + 1) * SUB], 0.0) * hv_s[g, r:r + 1, :]
        hg_block_part[blk] = o_blk

    def hg_state_chain():
        st = s_ref[g]
        for blk in range(n_blk):
            r0 = blk * SUB
            hg_state_part[blk] = jnp.concatenate(
                [_mm_nt(qe[r0:r0 + SUB, h * HEAD:(h + 1) * HEAD], st[:, h * HEAD:(h + 1) * HEAD])
                 for h in range(N_HEADS)], axis=1)
            upd = jnp.concatenate(
                [_mm_tn(hv[r0:r0 + SUB, h * HEAD:(h + 1) * HEAD], kw[r0:r0 + SUB, h * HEAD:(h + 1) * HEAD])
                 for h in range(N_HEADS)], axis=1)
            yield
            st = st * jnp.exp(blast[r0:r0 + 1, :]) + upd
        s_ref[g] = st

    def hg_finish():
        o = jnp.concatenate([hg_state_part[blk] + hg_block_part[blk] for blk in range(n_blk)], axis=0)
        mixed_ref[rs,MIX_W:2 * MIX_W] = (_head_norm(o, HEAD) * hgnw_ref[...]
                                         * _silu(proj_ref[rs,C_HGG:C_HGG + MIX_W])).astype(mixed_ref.dtype)

    cos = rope_ref[:, 0:128]
    sin_a = rope_ref[:, 128:256]
    sin_b = rope_ref[:, 256:384]

    def rope(x):
        return x * cos + pltpu.roll(x, 120, 1) * sin_a + pltpu.roll(x, 8, 1) * sin_b

    q_rot = [rope(proj_ref[rs,C_SWQ:C_SWQ + 128]), rope(proj_ref[rs,C_SWQ + 128:C_SWQ + 256])]
    k_rot = rope(proj_ref[rs,C_SWK:C_SWK + 128])
    v_cur = proj_ref[rs,C_SWV:C_SWV + 128]
    k_prev = k_ref[g]
    v_prev = v_ref[g]
    colw = lax.broadcasted_iota(jnp.int32, (2 * L, WINDOW), 1)
    row2 = lax.broadcasted_iota(jnp.int32, (2 * L, 1), 0)
    prev_ok = colw >= WINDOW - n_valid
    sw_out = [None] * (2 * SW_KV)

    def sw_group(kv):
        gs = slice(kv * HEAD, (kv + 1) * HEAD)
        q2 = jnp.concatenate([q_rot[kv][:, 0:HEAD], q_rot[kv][:, HEAD:2 * HEAD]], axis=0) * (HEAD ** -0.5)
        s_p = jnp.where(prev_ok, _mm_nt(q2, k_prev[:, gs]), NEG)
        yield
        s_c = _mm_nt(q2, k_rot[:, gs])
        yield
        sink = jnp.where(row2 < L, sinks_ref[2 * kv], sinks_ref[2 * kv + 1])
        mx = jnp.maximum(jnp.maximum(jnp.max(s_p, axis=-1, keepdims=True),
                                     jnp.max(s_c, axis=-1, keepdims=True)), sink)
        e_p = jnp.exp(s_p - mx)
        e_c = jnp.exp(s_c - mx)
        den = (jnp.sum(e_p, axis=-1, keepdims=True) + jnp.sum(e_c, axis=-1, keepdims=True)
               + jnp.exp(sink - mx))
        o_p = _mm(e_p, v_prev[:, gs])
        yield
        o = (o_p + _mm(e_c, v_cur[:, gs])) / den
        yield
        sw_out[2 * kv] = o[0:L]
        sw_out[2 * kv + 1] = o[L:2 * L]

    def sw_finish():
        mixed_ref[rs,2 * MIX_W:3 * MIX_W] = jnp.concatenate(sw_out, axis=1).astype(mixed_ref.dtype)
        k_ref[g, 0:WINDOW - L, :] = k_prev[L:WINDOW]
        k_ref[g, WINDOW - L:WINDOW, :] = k_rot
        v_ref[g, 0:WINDOW - L, :] = v_prev[L:WINDOW]
        v_ref[g, WINDOW - L:WINDOW, :] = v_cur

    convbuf[g, 0:8, :] = cv_ref[g]
    convbuf[g, 8:8 + L, :] = proj_ref[rs,C_SDX:C_SDX + SSD_CONV_DIM]
    acc = convb_ref[...] + convbuf[g, 5:5 + L, :] * convw_ref[0:1, :]
    for j in range(1, SSD_CONV):
        acc = acc + convbuf[g, 5 + j:5 + j + L, :] * convw_ref[j:j + 1, :]
    xbc = _silu(acc)
    cv_ref[g] = convbuf[g, L:L + 8, :]
    xs_all = xbc[:, 0:MIX_W]
    a_neg = -jnp.exp(alog_ref[...])
    ca = _mm_sel(tri, dtv * a_neg)
    ca_t = ca.T
    dt_t = dtv.T
    eca = jnp.exp(ca)
    ca_last = ca[L - 1:L, :]
    wss = jnp.exp(ca_last - ca) * dtv
    sdec = jnp.exp(ca_last)
    sd_out = [None] * N_HEADS

    def sd_group(grp):
        bm_g = xbc[:, MIX_W + grp * SSD_N:MIX_W + (grp + 1) * SSD_N]
        cm_g = xbc[:, MIX_W + 2 * SSD_N + grp * SSD_N:MIX_W + 2 * SSD_N + (grp + 1) * SSD_N]
        cb = _mm_nt(cm_g, bm_g)
        yield
        for h in (2 * grp, 2 * grp + 1):
            xh = xs_all[:, h * HEAD:(h + 1) * HEAD]
            hst = h_ref[g, h]
            from_state = _mm_nt(cm_g, hst)
            yield
            dec = jnp.where(causal, jnp.exp(ca[:, h:h + 1] - ca_t[h:h + 1, :]), 0.0)
            mmat = cb * dec * dt_t[h:h + 1, :]
            sd_out[h] = _mm(mmat, xh) + eca[:, h:h + 1] * from_state
            yield
            upd = _mm_tn(xh * wss[:, h:h + 1], bm_g)
            yield
            h_ref[g, h] = sdec[:, h:h + 1] * hst + upd

    def sd_finish():
        y = jnp.concatenate(sd_out, axis=1) + ssdd_ref[...] * xs_all
        gated = y * _silu(proj_ref[rs,C_SDZ:C_SDZ + MIX_W])
        mixed_ref[rs,3 * MIX_W:4 * MIX_W] = (_head_norm(gated, 2 * HEAD)
                                             * ssdnw_ref[...]).astype(mixed_ref.dtype)

    chains = [[hg_state_chain()] + [ml_head(h) for h in range(N_HEADS)],
              [sd_group(grp) for grp in range(2)] + [hg_within_block(blk) for blk in (0, 1)],
              [sw_group(kv) for kv in range(SW_KV)] + [hg_within_block(blk) for blk in (2, 3)]]

    def finish():
        ml_finish()
        hg_finish()
        sw_finish()
        sd_finish()

    return chains, finish


def _mixer(proj, rope_tab, init, lp, consts, chunks_per_prompt, n_seq):
    n = proj.shape[0]
    rows = SEQ_GROUP * CHUNK
    n_steps = n // rows

    def group_of(i):
        return jnp.where(i < chunks_per_prompt, 0, i - chunks_per_prompt + 1)

    def rope_blk(i):
        return jnp.minimum(i, chunks_per_prompt)

    def per_seq(shape):
        nd = len(shape)
        return pl.BlockSpec((SEQ_GROUP,) + shape, lambda i, s: (group_of(i),) + (0,) * nd)

    def const(shape):
        nd = len(shape)
        return pl.BlockSpec(shape, lambda i, s: (0,) * nd)

    state_shapes = [(N_HEADS, HEAD, 128), (8, 128), (HEAD, MIX_W), (WINDOW, 128), (WINDOW, 128),
                    (N_HEADS, HEAD, SSD_N), (8, SSD_CONV_DIM)]
    in_specs = ([pl.BlockSpec((rows, D_PROJ), lambda i, s: (i, 0)),
                 pl.BlockSpec((CHUNK, 384), lambda i, s: (rope_blk(i), 0))]
                + [per_seq(s) for s in state_shapes]
                + [const((1, 128)), const((1, 128)), const((1, MIX_W)), const((1, MIX_W)), const((1, MIX_W)),
                   const((SSD_CONV, SSD_CONV_DIM)), const((1, SSD_CONV_DIM)), const((1, MIX_W)),
                   const((1, MIX_W)),
                   const((CHUNK, CHUNK)), const((CHUNK, CHUNK)), const((CHUNK, CHUNK)), const((MIX_W, MIX_W))])
    out_specs = ([pl.BlockSpec((rows, D_MODEL), lambda i, s: (i, 0))]
                 + [per_seq(s) for s in state_shapes])
    out_shape = ([jax.ShapeDtypeStruct((n, D_MODEL), BF16)]
                 + [jax.ShapeDtypeStruct((n_seq,) + s, F32) for s in state_shapes])
    grid_spec = pltpu.PrefetchScalarGridSpec(
        num_scalar_prefetch=1, grid=(n_steps,), in_specs=in_specs, out_specs=out_specs,
        scratch_shapes=[pltpu.VMEM((SEQ_GROUP, CHUNK + 8, SSD_CONV_DIM), F32),
                        pltpu.VMEM((SEQ_GROUP, CHUNK, MIX_W), F32), pltpu.VMEM((SEQ_GROUP, CHUNK, MIX_W), F32),
                        pltpu.VMEM((SEQ_GROUP, CHUNK, MIX_W), F32)])
    return pl.pallas_call(
        functools.partial(_mixer_kernel, chunks_per_prompt),
        grid_spec=grid_spec, out_shape=out_shape,
        compiler_params=pltpu.CompilerParams(dimension_semantics=("arbitrary",),
                                             vmem_limit_bytes=VMEM_LIMIT),
        name="mixer",
    )(lp["sinks"], proj, rope_tab, *init,
      lp["gbias"], lp["alog"], lp["mlnw"], lp["hglb"], lp["hgnw"],
      lp["convw"], lp["convb"], lp["ssdd"], lp["ssdnw"], *consts)


def _dense_ffn_kernel(f_chunk, n_parts, bounds, *refs):
    x_refs = refs[:n_parts]
    mix_ref, wo_ref, nw_ref, wg_ref, wu_ref, wd_ref, o_ref = refs[n_parts:]
    x1 = _read_token_tile(x_refs, bounds) + jnp.dot(mix_ref[...], wo_ref[...], preferred_element_type=F32)
    hn = _rms(x1, nw_ref[...]).astype(BF16)
    o_ref[...] = x1
    for f0 in range(0, wg_ref.shape[1], f_chunk):
        g = jnp.dot(hn, wg_ref[:, f0:f0 + f_chunk], preferred_element_type=F32)
        u = jnp.dot(hn, wu_ref[:, f0:f0 + f_chunk], preferred_element_type=F32)
        act = (_silu(g) * u).astype(BF16)
        o_ref[...] += jnp.dot(act, wd_ref[f0:f0 + f_chunk, :], preferred_element_type=F32)


def _dense_ffn(x_parts, mixed, wo, nw, wg, wu, wd):
    n = mixed.shape[0]
    d_ff = wg.shape[1]
    tm = _pick_tile(np.gcd.reduce([_part_rows(p) for p in x_parts]), TOKEN_TILES)
    x_specs, bounds = _token_parts_specs(x_parts, tm)
    full = lambda shape: pl.BlockSpec(shape, lambda i: (0, 0))
    return pl.pallas_call(
        functools.partial(_dense_ffn_kernel, 256, len(x_parts), bounds),
        grid=(n // tm,),
        in_specs=x_specs + [
                  pl.BlockSpec((tm, D_MODEL), lambda i: (i, 0)),
                  full((D_MODEL, D_MODEL)), full((1, D_MODEL)),
                  full((D_MODEL, d_ff)), full((D_MODEL, d_ff)), full((d_ff, D_MODEL))],
        out_specs=pl.BlockSpec((tm, D_MODEL), lambda i: (i, 0)),
        out_shape=jax.ShapeDtypeStruct((n, D_MODEL), F32),
        compiler_params=pltpu.CompilerParams(dimension_semantics=("arbitrary",),
                                             vmem_limit_bytes=VMEM_LIMIT),
        name="dense_ffn",
    )(*x_parts, mixed, wo, nw, wg, wu, wd)


TOK_BLK = 256
FFN_TILE = 1024


def _pack_bf16_pairs(x):
    w = x.shape[1] // 2
    bits = pltpu.bitcast(x.astype(BF16).astype(F32), jnp.int32)
    return lax.shift_right_logical(bits[:, :w], 16) | bits[:, w:]


def _unpack_bf16_pairs(p):
    lo = pltpu.bitcast(lax.shift_left(p, 16), F32)
    hi = pltpu.bitcast(p & jnp.int32(-65536), F32)
    return lo, hi


def _router_kernel(x_ref, mix_ref, wo_ref, nw_ref, wrt_ref, upper_ref,
                   x1_ref, hn_ref, meta_ref, count_ref, carry):
    @pl.when(pl.program_id(0) == 0)
    def _():
        carry[...] = jnp.zeros_like(carry)

    x1 = x_ref[...] + jnp.dot(mix_ref[...], wo_ref[...], preferred_element_type=F32)
    x1_ref[...] = x1
    hn_f = _rms(x1, nw_ref[...])
    hn = hn_f.astype(BF16)
    hn_ref[...] = _pack_bf16_pairs(hn_f)
    logits = lax.dot_general(wrt_ref[...], hn, (((1,), (1,)), ((), ())),
                             preferred_element_type=F32)
    sub = lax.broadcasted_iota(jnp.int32, logits.shape, 0)
    l1 = jnp.where(sub < N_EXPERTS, logits, NEG)
    m1 = jnp.max(l1, axis=0, keepdims=True)
    i1 = jnp.min(jnp.where(l1 == m1, sub, 99), axis=0, keepdims=True)
    l2 = jnp.where(sub == i1, NEG, l1)
    m2 = jnp.max(l2, axis=0, keepdims=True)
    i2 = jnp.min(jnp.where(l2 == m2, sub, 99), axis=0, keepdims=True)
    e2 = jnp.exp(m2 - m1)
    g1 = 1.0 / (1.0 + e2)
    g2 = e2 / (1.0 + e2)
    sel = jnp.where(sub == i1, 1.0, jnp.where(sub == i2, 1.0, 0.0))
    before = jnp.dot(sel.astype(BF16), upper_ref[...], preferred_element_type=F32) + carry[:, 0:1]
    r1 = jnp.sum(jnp.where(sub == i1, before, 0.0), axis=0, keepdims=True)
    r2 = jnp.sum(jnp.where(sub == i2, before, 0.0), axis=0, keepdims=True)
    meta_ref[...] = jnp.concatenate([i1.astype(F32), i2.astype(F32), r1, r2, g1, g2,
                                     jnp.zeros((2, r1.shape[1]), F32)], axis=0)
    total = carry[...] + jnp.sum(sel, axis=1, keepdims=True)
    carry[...] = total
    count_ref[...] = total


def _router(x, mixed, wo, nw, wrt, upper):
    n = x.shape[0]
    tm = TOK_BLK
    full = lambda shape: pl.BlockSpec(shape, lambda i: (0,) * len(shape))
    tok = lambda w: pl.BlockSpec((tm, w), lambda i: (i, 0))
    return pl.pallas_call(
        _router_kernel,
        grid=(n // tm,),
        in_specs=[tok(D_MODEL), tok(D_MODEL), full((D_MODEL, D_MODEL)), full((1, D_MODEL)),
                  full((16, D_MODEL)), full((tm, tm))],
        out_specs=[tok(D_MODEL), tok(D_MODEL // 2),
                   pl.BlockSpec((8, tm), lambda i: (0, i)), full((16, 128))],
        out_shape=[jax.ShapeDtypeStruct((n, D_MODEL), F32), jax.ShapeDtypeStruct((n, D_MODEL // 2), jnp.int32),
                   jax.ShapeDtypeStruct((8, n), F32), jax.ShapeDtypeStruct((16, 128), F32)],
        scratch_shapes=[pltpu.VMEM((16, 128), F32)],
        compiler_params=pltpu.CompilerParams(dimension_semantics=("arbitrary",),
                                             vmem_limit_bytes=VMEM_LIMIT),
        name="router",
    )(x, mixed, wo, nw, wrt, upper)


def _sc_row_gather(table, idx):
    n_workers = SC_CORES * SC_SUBCORES
    b = idx.shape[0]
    d = table.shape[1]
    assert b % (8 * n_workers) == 0
    per_worker = b // n_workers
    chunk = max(c for c in range(8, SC_MAX_GATHER_ROWS + 1, 8) if per_worker % c == 0)
    mesh = plsc.VectorSubcoreMesh(core_axis_name="c", subcore_axis_name="s")

    @functools.partial(
        pl.kernel, mesh=mesh, out_type=jax.ShapeDtypeStruct((b, d), table.dtype),
        scratch_types=[pltpu.VMEM((chunk,), jnp.int32), pltpu.VMEM((chunk, d), table.dtype),
                       pltpu.SemaphoreType.DMA])
    def gather(table_hbm, idx_hbm, out_hbm, idx_v, rows_v, sem):
        worker = lax.axis_index("s") * SC_CORES + lax.axis_index("c")
        base = worker * per_worker

        @pl.loop(0, per_worker // chunk)
        def _(c):
            off = pl.multiple_of(base + c * chunk, 8)
            pltpu.sync_copy(idx_hbm.at[pl.ds(off, chunk)], idx_v)
            pltpu.async_copy(table_hbm.at[idx_v], rows_v, sem).wait()
            pltpu.sync_copy(rows_v, out_hbm.at[pl.ds(off, chunk)])

    return gather(table, idx)


def _sc_row_scatter(rows, idx):
    n_workers = SC_CORES * SC_SUBCORES
    b = n_out = idx.shape[0]
    v, d = rows.shape
    assert b % (8 * n_workers) == 0
    per_worker = b // n_workers
    chunk = max(c for c in range(8, SC_MAX_GATHER_ROWS + 1, 8) if per_worker % c == 0 and v % c == 0)
    mesh = plsc.VectorSubcoreMesh(core_axis_name="c", subcore_axis_name="s")

    @functools.partial(
        pl.kernel, mesh=mesh, out_type=jax.ShapeDtypeStruct((n_out, d), rows.dtype),
        scratch_types=[pltpu.VMEM((chunk,), jnp.int32), pltpu.VMEM((chunk, d), rows.dtype)])
    def scatter(rows_hbm, idx_hbm, out_hbm, idx_v, rows_v):
        worker = lax.axis_index("s") * SC_CORES + lax.axis_index("c")
        base = worker * per_worker

        @pl.loop(0, per_worker // chunk)
        def _(c):
            off = pl.multiple_of(base + c * chunk, 8)
            pltpu.sync_copy(idx_hbm.at[pl.ds(off, chunk)], idx_v)
            pltpu.sync_copy(rows_hbm.at[pl.ds(pl.multiple_of(lax.rem(off, v), 8), chunk)], rows_v)
            pltpu.sync_copy(rows_v, out_hbm.at[idx_v])

    return scatter(rows, idx)


def _expert_ffn_kernel(texp_ref, tvalid_ref, xs_ref, wg_ref, wu_ref, wd_ref, o_ref, acc_ref, xb_ref):
    j = pl.program_id(0)
    f = pl.program_id(1)
    last = f == pl.num_programs(1) - 1
    valid = tvalid_ref[j] == 1
    half = D_MODEL // 2

    @pl.when(jnp.logical_and(valid, f == 0))
    def _():
        lo, hi = _unpack_bf16_pairs(xs_ref[...])
        xb_ref[:, :half] = lo.astype(BF16)
        xb_ref[:, half:] = hi.astype(BF16)

    @pl.when(valid)
    def _():
        xs = xb_ref[...]
        g = jnp.dot(xs, wg_ref[0], preferred_element_type=F32)
        u = jnp.dot(xs, wu_ref[0], preferred_element_type=F32)
        act = (_silu(g) * u).astype(BF16)
        part = jnp.dot(act, wd_ref[0], preferred_element_type=F32)

        @pl.when(f == 0)
        def _():
            acc_ref[...] = part

        @pl.when(f > 0)
        def _():
            acc_ref[...] += part

        @pl.when(last)
        def _():
            o_ref[...] = _pack_bf16_pairs(acc_ref[...])

    @pl.when(jnp.logical_and(jnp.logical_not(valid), last))
    def _():
        o_ref[...] = jnp.zeros_like(o_ref)


def _expert_ffn(f_exp, f_valid, xs, wg, wu, wd):
    n_slots = xs.shape[0]
    d_ff = wg.shape[2]
    tf = _pick_tile(d_ff, (512, 256, 128))
    grid_spec = pltpu.PrefetchScalarGridSpec(
        num_scalar_prefetch=2, grid=(n_slots // FFN_TILE, d_ff // tf),
        in_specs=[pl.BlockSpec((FFN_TILE, D_MODEL // 2), lambda j, f, te, tv: (j, 0)),
                  pl.BlockSpec((1, D_MODEL, tf), lambda j, f, te, tv: (te[j], 0, f)),
                  pl.BlockSpec((1, D_MODEL, tf), lambda j, f, te, tv: (te[j], 0, f)),
                  pl.BlockSpec((1, tf, D_MODEL), lambda j, f, te, tv: (te[j], f, 0))],
        out_specs=pl.BlockSpec((FFN_TILE, D_MODEL // 2), lambda j, f, te, tv: (j, 0)),
        scratch_shapes=[pltpu.VMEM((FFN_TILE, D_MODEL), F32), pltpu.VMEM((FFN_TILE, D_MODEL), BF16)])
    return pl.pallas_call(
        _expert_ffn_kernel, grid_spec=grid_spec,
        out_shape=jax.ShapeDtypeStruct((n_slots, D_MODEL // 2), jnp.int32),
        compiler_params=pltpu.CompilerParams(dimension_semantics=("arbitrary", "arbitrary"),
                                             vmem_limit_bytes=VMEM_LIMIT),
        name="expert_ffn",
    )(f_exp, f_valid, xs, wg, wu, wd)


def _combine_kernel(part_tiles, x1_ref, o1_ref, o2_ref, gate_ref, fnw_ref, *y_refs):
    i = pl.program_id(0)
    lo1, hi1 = _unpack_bf16_pairs(o1_ref[...])
    lo2, hi2 = _unpack_bf16_pairs(o2_ref[...])
    g1 = gate_ref[:, 0:1]
    g2 = gate_ref[:, 1:2]
    half = D_MODEL // 2
    x1 = x1_ref[...]
    fnw = fnw_ref[...]
    ya = x1[:, :half] + g1 * lo1 + g2 * lo2
    yb = x1[:, half:] + g1 * hi1 + g2 * hi2
    scale = lax.rsqrt((jnp.sum(ya * ya, axis=-1, keepdims=True) + jnp.sum(yb * yb, axis=-1, keepdims=True))
                      / D_MODEL + EPS)
    y = jnp.concatenate([ya * scale * fnw[:, :half], yb * scale * fnw[:, half:]], axis=1)
    lo_tile = 0
    for y_ref, tiles in zip(y_refs, part_tiles):
        @pl.when(jnp.logical_and(i >= lo_tile, i < lo_tile + tiles))
        def _(y_ref=y_ref):
            _store_part_tile(y_ref, y)
        lo_tile += tiles


def _combine(x1, o12, gates, fnw, part_shapes):
    n = x1.shape[0]
    parts = [jax.ShapeDtypeStruct(s, F32) for s in part_shapes]
    tm = _pick_tile(np.gcd.reduce([_part_rows(p) for p in parts]), TOKEN_TILES)
    n_tiles = n // tm
    out_specs, bounds = _token_parts_specs(parts, tm)
    part_tiles = [bounds[k + 1] - bounds[k] for k in range(len(parts))]
    return pl.pallas_call(
        functools.partial(_combine_kernel, part_tiles),
        grid=(n_tiles,),
        in_specs=[pl.BlockSpec((tm, D_MODEL), lambda i: (i, 0)),
                  pl.BlockSpec((tm, D_MODEL // 2), lambda i: (i, 0)),
                  pl.BlockSpec((tm, D_MODEL // 2), lambda i: (i + n_tiles, 0)),
                  pl.BlockSpec((tm, 128), lambda i: (i, 0)),
                  pl.BlockSpec((1, D_MODEL), lambda i: (0, 0))],
        out_specs=out_specs,
        out_shape=parts,
        compiler_params=pltpu.CompilerParams(dimension_semantics=("arbitrary",),
                                             vmem_limit_bytes=VMEM_LIMIT),
        name="combine",
    )(x1, o12, o12, gates, fnw)


def _moe(x, mixed, wo, nw, router_w, wg, wu, wd, fnw, part_shapes):
    i32 = jnp.int32
    n = x.shape[0]
    assert n % TOK_BLK == 0
    wrt = jnp.pad(router_w.T, ((0, 16 - N_EXPERTS), (0, 0))).astype(BF16)
    t = np.arange(TOK_BLK)
    upper = jnp.asarray((t[:, None] < t[None, :]).astype(np.float32), dtype=BF16)
    x1, hn_p, meta, count = _router(x, mixed, wo, nw, wrt, upper)

    n_slots = (-(-2 * n // FFN_TILE) + N_EXPERTS) * FFN_TILE
    counts = count[:N_EXPERTS, 0].astype(i32)
    gsize = (counts + FFN_TILE - 1) // FFN_TILE * FFN_TILE
    gend = jnp.cumsum(gsize)
    goff = gend - gsize
    fstart = jnp.arange(n_slots // FFN_TILE, dtype=i32) * FFN_TILE
    f_valid = (fstart < gend[-1]).astype(i32)
    f_exp = jnp.sum((fstart[:, None] >= gend[None, :]).astype(i32), axis=1)
    f_exp = jnp.minimum(f_exp, jnp.sum((gend[-1] - 1 >= gend).astype(i32)))
    expert_ids = jnp.arange(N_EXPERTS, dtype=i32)[:, None]
    top = meta[0:2].astype(i32)
    rank = meta[2:4].astype(i32)
    slot = jnp.stack([jnp.sum(jnp.where(top[k][None, :] == expert_ids, goff[:, None], 0), axis=0) + rank[k]
                      for k in range(2)]).reshape(-1)
    pad = gsize - counts
    cpad = jnp.cumsum(pad)
    j = jnp.arange(n_slots - 2 * n, dtype=i32)
    pad_e = jnp.sum((j[:, None] >= cpad[None, :]).astype(i32), axis=1)
    pad_base = jnp.sum(jnp.where(pad_e[:, None] == expert_ids.T, (goff + counts - (cpad - pad))[None, :], 0), axis=1)
    pad_slot = jnp.where(pad_e < N_EXPERTS, pad_base + j, gend[-1] + j - cpad[-1])

    xs_p = _sc_row_scatter(hn_p, jnp.concatenate([slot, pad_slot]))
    out_p = _expert_ffn(f_exp, f_valid, xs_p, wg, wu, wd)
    o12 = _sc_row_gather(out_p, slot)
    gates = _pad_lanes(meta[4:6].T, 128)
    return _combine(x1, o12, gates, fnw, part_shapes)


def _rope_table(seq_len, dec_len):
    half = ROPE_DIMS // 2
    pos = jnp.concatenate([jnp.arange(seq_len), PAST_LEN + jnp.arange(dec_len)]).astype(F32)
    inv_freq = jnp.power(jnp.float32(ROPE_THETA), -jnp.arange(half, dtype=F32) / half)
    lane = np.arange(128) % HEAD
    inv_lane = jnp.where(lane < ROPE_DIMS, inv_freq[lane % half], 0.0)
    ang = pos[:, None] * inv_lane[None, :]
    cos, sin = jnp.cos(ang), jnp.sin(ang)
    sin_a = jnp.where(lane < half, -sin, 0.0)
    sin_b = jnp.where((lane >= half) & (lane < ROPE_DIMS), sin, 0.0)
    return jnp.concatenate([cos, sin_a, sin_b], axis=1)


def _mixer_consts():
    t = np.arange(CHUNK)
    tri = (t[:, None] >= t[None, :])
    same = (t[:, None] // SUB) == (t[None, :] // SUB)
    c = np.arange(MIX_W)
    ind = (c[:, None] // HEAD) == (c[None, :] // HEAD)
    as_bf = lambda m: jnp.asarray(m.astype(np.float32), dtype=BF16)
    return [as_bf(tri), as_bf(tri & same), as_bf(same), as_bf(ind)]


def _pad_lanes(v, width):
    return jnp.pad(v, ((0, 0), (0, width - v.shape[1])))


def kernel(x_prompt, x_sample, state_mlstm_C, state_mlstm_n, state_mlstm_m, state_hgrn_S, cache_swa_k, cache_swa_v, state_ssd_h, state_ssd_conv, norm1_w, w_in, ml_ig_b, ml_fg_b, ml_norm_w, hg_lb_logits, hg_norm_w, sw_sinks, ssd_conv_w, ssd_conv_b, ssd_dt_bias, ssd_A_log, ssd_D, ssd_norm_w, w_out, norm2_w, ffn_w_gate, ffn_w_up, ffn_w_down, moe_router, moe_w_gate, moe_w_up, moe_w_down, final_norm_w):
    depth = w_in.shape[0]
    bp, seq_len, _ = x_prompt.shape
    bs, dec_len, _ = x_sample.shape
    assert seq_len % CHUNK == 0 and dec_len == CHUNK and depth % 2 == 0
    n_seq = bp + bs
    chunks_per_prompt = seq_len // CHUNK

    assert bp == SEQ_GROUP and bs % SEQ_GROUP == 0
    x_parts = (x_prompt.reshape(bp, chunks_per_prompt, CHUNK, D_MODEL), x_sample.reshape(bs * dec_len, D_MODEL))
    part_shapes = tuple(p.shape for p in x_parts)
    rope_tab = _rope_table(seq_len, dec_len)
    consts = _mixer_consts()

    sm = jax.nn.softmax(hg_lb_logits.astype(F32), axis=0)
    hg_lb = jnp.cumsum(sm, axis=0) - sm[0]

    def with_prompt_zeros(a):
        return jnp.concatenate([jnp.zeros((bp,) + a.shape[1:], F32), a.astype(F32)], axis=0)

    state_outs = []
    for l in range(depth):
        lp = {
            "sinks": sw_sinks[l].astype(F32),
            "gbias": _pad_lanes(jnp.concatenate([ml_ig_b[l], ml_fg_b[l], ssd_dt_bias[l]])[None, :].astype(F32), 128),
            "alog": _pad_lanes(ssd_A_log[l][None, :].astype(F32), 128),
            "mlnw": ml_norm_w[l][None, :].astype(F32),
            "hglb": hg_lb[l][None, :],
            "hgnw": hg_norm_w[l][None, :].astype(F32),
            "convw": ssd_conv_w[l].astype(F32),
            "convb": ssd_conv_b[l][None, :].astype(F32),
            "ssdd": jnp.repeat(ssd_D[l].astype(F32), HEAD)[None, :],
            "ssdnw": ssd_norm_w[l][None, :].astype(F32),
        }
        c_aug = jnp.concatenate([state_mlstm_C[l], jnp.broadcast_to(state_mlstm_n[l][..., None], state_mlstm_C[l].shape)], axis=-1)
        m_pad = jnp.pad(state_mlstm_m[l][:, None, :], ((0, 0), (0, 7), (0, 128 - N_HEADS)))
        s_t = jnp.swapaxes(state_hgrn_S[l], -1, -2)
        s_t = jnp.moveaxis(s_t, 1, 2).reshape(bs, HEAD, MIX_W)
        cv_pad = jnp.pad(state_ssd_conv[l], ((0, 0), (8 - (SSD_CONV - 1), 0), (0, 0)))
        init = [with_prompt_zeros(a) for a in (
            c_aug, m_pad, s_t, cache_swa_k[l].reshape(bs, WINDOW, 128), cache_swa_v[l].reshape(bs, WINDOW, 128),
            state_ssd_h[l], cv_pad)]

        proj = _inproj(x_parts, norm1_w[l][None, :], w_in[l])
        mixed, c_o, m_o, s_o, k_o, v_o, h_o, cv_o = _mixer(
            proj, rope_tab, init, lp, consts, chunks_per_prompt, n_seq)
        wo = w_out[l].astype(BF16)
        j = l // 2
        if l % 2 == 0:
            x_parts = (_dense_ffn(x_parts, mixed, wo, norm2_w[l][None, :], ffn_w_gate[j].astype(BF16),
                                  ffn_w_up[j].astype(BF16), ffn_w_down[j].astype(BF16)),)
        else:
            assert l == depth - 1 and len(x_parts) == 1
            y_parts = _moe(x_parts[0], mixed, wo, norm2_w[l][None, :], moe_router[j], moe_w_gate[j].astype(BF16),
                           moe_w_up[j].astype(BF16), moe_w_down[j].astype(BF16), final_norm_w[None, :],
                           part_shapes)
        s_back = jnp.moveaxis(s_o.reshape(n_seq, HEAD, N_HEADS, HEAD), 2, 1)
        state_outs.append((c_o[..., :HEAD], c_o[:, :, :, HEAD], m_o[:, 0, :N_HEADS],
                           jnp.swapaxes(s_back, -1, -2),
                           k_o.reshape(n_seq, WINDOW, SW_KV, HEAD), v_o.reshape(n_seq, WINDOW, SW_KV, HEAD),
                           h_o, cv_o[:, 8 - (SSD_CONV - 1):, :]))

    y_prompt = y_parts[0].reshape(bp, seq_len, D_MODEL)
    y_sample = y_parts[1].reshape(bs, dec_len, D_MODEL)
    stacked = [jnp.stack([so[k] for so in state_outs]) for k in range(8)]
    return (y_prompt, y_sample) + tuple(s[:, :bp] for s in stacked) + tuple(s[:, bp:] for s in stacked)
```

```python
import functools

import numpy as np
import jax
import jax.numpy as jnp
from jax import lax
from jax.experimental import pallas as pl
from jax.experimental.pallas import tpu as pltpu
from jax.experimental.pallas import tpu_sc as plsc

F32 = jnp.float32
BF16 = jnp.bfloat16

D_MODEL = 1024
CHUNK = 64
EPS = 1e-6
N_HEADS = 4
HEAD = 64
MIX_W = N_HEADS * HEAD
SW_KV = 2
WINDOW = 128
ROPE_DIMS = 16
ROPE_THETA = 500000.0
SSD_N = 128
SSD_CONV = 4
SSD_CONV_DIM = 768
PAST_LEN = 4096
N_EXPERTS = 8
SUB = 16
SEQ_GROUP = 2
HG_SLICE = 4
TOKEN_TILES = (512, 256, 128)

C_MLQ, C_MLK, C_MLV, C_MLO = 0, 256, 512, 768
C_HGQ, C_HGF, C_HGI, C_HGG = 1024, 1280, 1536, 1792
C_SWQ, C_SWK, C_SWV = 2048, 2304, 2432
C_SDZ, C_SDX, C_GATE = 2560, 2816, 3584
D_PROJ = 3712
NEG = -1e30
VMEM_LIMIT = 56 * 1024 * 1024
SC_CORES = 2
SC_SUBCORES = 16
SC_MAX_GATHER_ROWS = 128


def _mm(a, b):
    return jnp.dot(a.astype(BF16), b.astype(BF16), preferred_element_type=F32)


def _mm_nt(a, b):
    return lax.dot_general(a.astype(BF16), b.astype(BF16), (((1,), (1,)), ((), ())),
                           preferred_element_type=F32)


def _mm_tn(a, b):
    return jnp.dot(a.T.astype(BF16), b.astype(BF16), preferred_element_type=F32)


def _mm_sel(sel, x):
    hi = x.astype(BF16)
    r1 = x - hi.astype(F32)
    mid = r1.astype(BF16)
    lo = (r1 - mid.astype(F32)).astype(BF16)
    return (jnp.dot(sel, hi, preferred_element_type=F32)
            + jnp.dot(sel, mid, preferred_element_type=F32)
            + jnp.dot(sel, lo, preferred_element_type=F32))


def _sigmoid(x):
    return 1.0 / (1.0 + jnp.exp(-x))


def _silu(x):
    return x * _sigmoid(x)


def _softplus(x):
    return jnp.maximum(x, 0.0) + jnp.log(1.0 + jnp.exp(-jnp.abs(x)))


def _rms(x, w):
    return x * lax.rsqrt(jnp.mean(x * x, axis=-1, keepdims=True) + EPS) * w


def _pick_tile(n, candidates):
    for c in candidates:
        if n % c == 0:
            return c
    raise ValueError(f"no tile for {n}")


def _token_parts_specs(parts, tm):
    bounds = np.cumsum([0] + [_part_rows(p) // tm for p in parts])
    specs = []
    for k, p in enumerate(parts):
        assert _part_rows(p) % tm == 0
        lo, hi = int(bounds[k]), int(bounds[k + 1])
        specs.append(_part_block_spec(p, tm, lo, hi - lo))
    return specs, [int(b) for b in bounds]


def _part_rows(p):
    return int(np.prod(p.shape[:-1]))


def _part_block_spec(p, tm, first_tile, n_tiles):
    idx = lambda i: jnp.clip(i - first_tile, 0, n_tiles - 1)
    if len(p.shape) == 2:
        return pl.BlockSpec((tm, p.shape[1]), lambda i: (idx(i), 0))
    grp, _, chunk, d = p.shape
    assert tm % (grp * chunk) == 0
    return pl.BlockSpec((grp, tm // (grp * chunk), chunk, d), lambda i: (0, idx(i), 0, 0))


def _load_part_tile(ref):
    if len(ref.shape) == 2:
        return ref[...]
    grp, n_chunks = ref.shape[:2]
    return jnp.concatenate([ref[s, c] for c in range(n_chunks) for s in range(grp)], axis=0)


def _store_part_tile(ref, x):
    if len(ref.shape) == 2:
        ref[...] = x
        return
    grp, n_chunks, chunk = ref.shape[:3]
    for c in range(n_chunks):
        for s in range(grp):
            r0 = (c * grp + s) * chunk
            ref[s, c] = x[r0:r0 + chunk]


def _read_token_tile(refs, bounds):
    i = pl.program_id(0)
    x = _load_part_tile(refs[-1])
    for k in range(len(refs) - 2, -1, -1):
        x = jnp.where(i < bounds[k + 1], _load_part_tile(refs[k]), x)
    return x


def _inproj_kernel(n_parts, bounds, *refs):
    x_refs = refs[:n_parts]
    nw_ref, w_ref, o_ref, wp_ref = refs[n_parts:]

    @pl.when(pl.program_id(0) == 0)
    def _():
        rows = 128
        for r0 in range(0, D_MODEL, rows):
            w = w_ref[r0:r0 + rows, :]
            wp_ref[r0:r0 + rows, 0:C_MLO] = w[:, 0:768].astype(BF16)
            wp_ref[r0:r0 + rows, C_MLO:C_GATE] = w[:, 776:3592].astype(BF16)
            gates = jnp.concatenate([w[:, 768:776], w[:, 3592:3596],
                                     jnp.zeros((rows, D_PROJ - C_GATE - 12), F32)], axis=1)
            wp_ref[r0:r0 + rows, C_GATE:D_PROJ] = gates.astype(BF16)

    xn = _rms(_read_token_tile(x_refs, bounds), nw_ref[...]).astype(BF16)
    o_ref[...] = jnp.dot(xn, wp_ref[...], preferred_element_type=F32)


def _inproj(x_parts, nw, w):
    n = sum(_part_rows(p) for p in x_parts)
    tm = _pick_tile(np.gcd.reduce([_part_rows(p) for p in x_parts]), TOKEN_TILES)
    x_specs, bounds = _token_parts_specs(x_parts, tm)
    d_in = w.shape[1]
    return pl.pallas_call(
        functools.partial(_inproj_kernel, len(x_parts), bounds),
        grid=(n // tm,),
        in_specs=x_specs + [pl.BlockSpec((1, D_MODEL), lambda i: (0, 0)),
                            pl.BlockSpec((D_MODEL, d_in), lambda i: (0, 0), pipeline_mode=pl.Buffered(1))],
        out_specs=pl.BlockSpec((tm, D_PROJ), lambda i: (i, 0)),
        out_shape=jax.ShapeDtypeStruct((n, D_PROJ), F32),
        scratch_shapes=[pltpu.VMEM((D_MODEL, D_PROJ), BF16)],
        compiler_params=pltpu.CompilerParams(dimension_semantics=("arbitrary",),
                                             vmem_limit_bytes=VMEM_LIMIT),
        name="inproj",
    )(*x_parts, nw, w)


def _head_norm(x, width):
    parts = []
    for g in range(x.shape[1] // width):
        xg = x[:, g * width:(g + 1) * width]
        parts.append(xg * lax.rsqrt(jnp.mean(xg * xg, axis=-1, keepdims=True) + EPS))
    return jnp.concatenate(parts, axis=1)


def _mixer_kernel(prompt_steps,
                  sinks_ref,
                  proj_ref, rope_ref, c0_ref, m0_ref, s0_ref, k0_ref, v0_ref, h0_ref, cv0_ref,
                  gbias_ref, alog_ref, mlnw_ref, hglb_ref, hgnw_ref,
                  convw_ref, convb_ref, ssdd_ref, ssdnw_ref,
                  tri_ref, triblk_ref, blkones_ref, ind_ref,
                  mixed_ref, c_ref, m_ref, s_ref, k_ref, v_ref, h_ref, cv_ref,
                  convbuf, bloc_s, kk_s, hv_s):
    i = pl.program_id(0)
    is_prompt = i < prompt_steps
    is_first = jnp.logical_or(jnp.logical_not(is_prompt), i == 0)
    n_valid = jnp.where(is_prompt, jnp.minimum(i * CHUNK, WINDOW), WINDOW)

    @pl.when(is_first)
    def _():
        c_ref[...] = c0_ref[...]
        m_ref[...] = m0_ref[...]
        s_ref[...] = s0_ref[...]
        k_ref[...] = k0_ref[...]
        v_ref[...] = v0_ref[...]
        h_ref[...] = h0_ref[...]
        cv_ref[...] = cv0_ref[...]

    members = [_mixer_chunk(g, n_valid, sinks_ref, proj_ref, rope_ref,
                            gbias_ref, alog_ref, mlnw_ref, hglb_ref, hgnw_ref,
                            convw_ref, convb_ref, ssdd_ref, ssdnw_ref,
                            tri_ref, triblk_ref, blkones_ref, ind_ref,
                            mixed_ref, c_ref, m_ref, s_ref, k_ref, v_ref, h_ref, cv_ref,
                            convbuf, bloc_s, kk_s, hv_s) for g in range(SEQ_GROUP)]
    for phase in zip(*[m[0] for m in members]):
        chains = [c for per_member in zip(*phase) for c in per_member]
        while chains:
            alive = []
            for chain in chains:
                if next(chain, _CHAIN_DONE) is not _CHAIN_DONE:
                    alive.append(chain)
            chains = alive
    for _, finish in members:
        finish()


_CHAIN_DONE = object()


def _mixer_chunk(g, n_valid, sinks_ref, proj_ref, rope_ref,
                 gbias_ref, alog_ref, mlnw_ref, hglb_ref, hgnw_ref,
                 convw_ref, convb_ref, ssdd_ref, ssdnw_ref,
                 tri_ref, triblk_ref, blkones_ref, ind_ref,
                 mixed_ref, c_ref, m_ref, s_ref, k_ref, v_ref, h_ref, cv_ref,
                 convbuf, bloc_s, kk_s, hv_s):
    rs = slice(g * CHUNK, (g + 1) * CHUNK)
    L = CHUNK
    row = lax.broadcasted_iota(jnp.int32, (L, L), 0)
    col = lax.broadcasted_iota(jnp.int32, (L, L), 1)
    causal = row >= col
    tri = tri_ref[...]

    gate = proj_ref[rs,C_GATE:C_GATE + 128] + gbias_ref[...]
    ig = gate
    lf = pltpu.roll(-_softplus(-gate), 124, 1)
    dtv = pltpu.roll(_softplus(gate), 120, 1)

    b = _mm_sel(tri, lf)
    a = ig - b
    cm = a
    rowg = lax.broadcasted_iota(jnp.int32, (L, 128), 0)
    for sh in (1, 2, 4, 8, 16, 32):
        cm = jnp.where(rowg >= sh, jnp.maximum(cm, pltpu.roll(cm, sh, 0)), cm)
    m_prev = m_ref[g, 0:1, :]
    m_t = b + jnp.maximum(m_prev, cm)
    inter = jnp.exp(b + m_prev - m_t)
    bm = b - m_t
    a_t = a.T
    m_last = m_t[L - 1:L, :]
    b_last = b[L - 1:L, :]
    ws = jnp.exp(b_last + a - m_last)
    decay = jnp.exp(b_last + m_prev - m_last)
    emt = jnp.exp(-m_t)
    m_ref[g, 0:1, :] = m_last

    ones_h = jnp.ones((L, HEAD), F32)
    ml_out = [None] * N_HEADS

    def ml_head(h):
        qh = proj_ref[rs,C_MLQ + h * HEAD:C_MLQ + (h + 1) * HEAD]
        kh = proj_ref[rs,C_MLK + h * HEAD:C_MLK + (h + 1) * HEAD] * (HEAD ** -0.5)
        vh = proj_ref[rs,C_MLV + h * HEAD:C_MLV + (h + 1) * HEAD]
        qk = _mm_nt(qh, kh)
        yield
        c_aug = c_ref[g, h]
        from_state = _mm(qh, c_aug)
        yield
        w = jnp.where(causal, jnp.exp(bm[:, h:h + 1] + a_t[h:h + 1, :]), 0.0)
        v_aug = jnp.concatenate([vh, ones_h], axis=1)
        nd = _mm(w * qk, v_aug) + inter[:, h:h + 1] * from_state
        yield
        upd = _mm_tn(kh * ws[:, h:h + 1], v_aug)
        yield
        den = jnp.maximum(jnp.abs(nd[:, HEAD:HEAD + 1]), emt[:, h:h + 1])
        ml_out[h] = nd[:, :HEAD] / den
        c_ref[g, h] = decay[:, h:h + 1] * c_aug + upd

    def ml_finish():
        ml = _head_norm(jnp.concatenate(ml_out, axis=1), HEAD)
        mixed_ref[rs,0:MIX_W] = (_sigmoid(proj_ref[rs,C_MLO:C_MLO + MIX_W]) * ml
                                 * mlnw_ref[...]).astype(mixed_ref.dtype)

    lb = hglb_ref[...]
    fg = lb + (1.0 - lb) * _sigmoid(proj_ref[rs,C_HGF:C_HGF + MIX_W])
    hq = _silu(proj_ref[rs,C_HGQ:C_HGQ + MIX_W]) * (HEAD ** -0.5)
    kk = 1.0 - fg
    lfg = jnp.log(fg)
    hv = proj_ref[rs,C_HGI:C_HGI + MIX_W]
    bloc = _mm_sel(triblk_ref[...], lfg)
    blast = _mm_sel(blkones_ref[...], lfg)
    qe = hq * jnp.exp(bloc)
    kw = kk * jnp.exp(blast - bloc)
    bloc_s[g] = bloc
    kk_s[g] = kk
    hv_s[g] = hv
    ind = ind_ref[...]
    rows = lax.broadcasted_iota(jnp.int32, (SUB, MIX_W), 0)
    n_blk = L // SUB
    hg_state_part = [None] * n_blk
    hg_block_part = [None] * n_blk

    def hg_within_block(blk):
        r0 = blk * SUB
        q_blk = hq[r0:r0 + SUB]
        b_blk = bloc[r0:r0 + SUB]
        xs = []
        for s in range(SUB):
            r = r0 + s
            e = jnp.exp(jnp.minimum(b_blk - bloc_s[g, r:r + 1, :], 0.0))
            xs.append((q_blk * (kk_s[g, r:r + 1, :] * e)).astype(BF16))
            if s % HG_SLICE == HG_SLICE - 1:
                yield
        att = jnp.dot(jnp.concatenate(xs, axis=0), ind, preferred_element_type=F32)
        yield
        o_blk = att[0:SUB] * hv_s[g, r0:r0 + 1, :]
        for s in range(1, SUB):
            r = r0 + s
            o_blk = o_blk + jnp.where(rows >= s, att[s * SUB:(s + 1) * SUB], 0.0) * hv_s[g, r:r + 1, :]
            if s % HG_SLICE == HG_SLICE - 1:
                yield
        hg_block_part[blk] = o_blk

    def hg_state_chain():
        st = s_ref[g]
        for blk in range(n_blk):
            r0 = blk * SUB
            hg_state_part[blk] = jnp.concatenate(
                [_mm_nt(qe[r0:r0 + SUB, h * HEAD:(h + 1) * HEAD], st[:, h * HEAD:(h + 1) * HEAD])
                 for h in range(N_HEADS)], axis=1)
            upd = jnp.concatenate(
                [_mm_tn(hv[r0:r0 + SUB, h * HEAD:(h + 1) * HEAD], kw[r0:r0 + SUB, h * HEAD:(h + 1) * HEAD])
                 for h in range(N_HEADS)], axis=1)
            yield
            st = st * jnp.exp(blast[r0:r0 + 1, :]) + upd
        s_ref[g] = st

    def hg_finish():
        o = jnp.concatenate([hg_state_part[blk] + hg_block_part[blk] for blk in range(n_blk)], axis=0)
        mixed_ref[rs,MIX_W:2 * MIX_W] = (_head_norm(o, HEAD) * hgnw_ref[...]
                                         * _silu(proj_ref[rs,C_HGG:C_HGG + MIX_W])).astype(mixed_ref.dtype)

    cos = rope_ref[:, 0:128]
    sin_a = rope_ref[:, 128:256]
    sin_b = rope_ref[:, 256:384]

    def rope(x):
        return x * cos + pltpu.roll(x, 120, 1) * sin_a + pltpu.roll(x, 8, 1) * sin_b

    q_rot = [rope(proj_ref[rs,C_SWQ:C_SWQ + 128]), rope(proj_ref[rs,C_SWQ + 128:C_SWQ + 256])]
    k_rot = rope(proj_ref[rs,C_SWK:C_SWK + 128])
    v_cur = proj_ref[rs,C_SWV:C_SWV + 128]
    k_prev = k_ref[g]
    v_prev = v_ref[g]
    colw = lax.broadcasted_iota(jnp.int32, (2 * L, WINDOW), 1)
    row2 = lax.broadcasted_iota(jnp.int32, (2 * L, 1), 0)
    prev_ok = colw >= WINDOW - n_valid
    sw_out = [None] * (2 * SW_KV)

    def sw_group(kv):
        gs = slice(kv * HEAD, (kv + 1) * HEAD)
        q2 = jnp.concatenate([q_rot[kv][:, 0:HEAD], q_rot[kv][:, HEAD:2 * HEAD]], axis=0) * (HEAD ** -0.5)
        s_p = jnp.where(prev_ok, _mm_nt(q2, k_prev[:, gs]), NEG)
        yield
        s_c = _mm_nt(q2, k_rot[:, gs])
        yield
        sink = jnp.where(row2 < L, sinks_ref[2 * kv], sinks_ref[2 * kv + 1])
        mx = jnp.maximum(jnp.maximum(jnp.max(s_p, axis=-1, keepdims=True),
                                     jnp.max(s_c, axis=-1, keepdims=True)), sink)
        e_p = jnp.exp(s_p - mx)
        e_c = jnp.exp(s_c - mx)
        den = (jnp.sum(e_p, axis=-1, keepdims=True) + jnp.sum(e_c, axis=-1, keepdims=True)
               + jnp.exp(sink - mx))
        o_p = _mm(e_p, v_prev[:, gs])
        yield
        o = (o_p + _mm(e_c, v_cur[:, gs])) / den
        yield
        sw_out[2 * kv] = o[0:L]
        sw_out[2 * kv + 1] = o[L:2 * L]

    def sw_finish():
        mixed_ref[rs,2 * MIX_W:3 * MIX_W] = jnp.concatenate(sw_out, axis=1).astype(mixed_ref.dtype)
        k_ref[g, 0:WINDOW - L, :] = k_prev[L:WINDOW]
        k_ref[g, WINDOW - L:WINDOW, :] = k_rot
        v_ref[g, 0:WINDOW - L, :] = v_prev[L:WINDOW]
        v_ref[g, WINDOW - L:WINDOW, :] = v_cur

    convbuf[g, 0:8, :] = cv_ref[g]
    convbuf[g, 8:8 + L, :] = proj_ref[rs,C_SDX:C_SDX + SSD_CONV_DIM]
    acc = convb_ref[...] + convbuf[g, 5:5 + L, :] * convw_ref[0:1, :]
    for j in range(1, SSD_CONV):
        acc = acc + convbuf[g, 5 + j:5 + j + L, :] * convw_ref[j:j + 1, :]
    xbc = _silu(acc)
    cv_ref[g] = convbuf[g, L:L + 8, :]
    xs_all = xbc[:, 0:MIX_W]
    a_neg = -jnp.exp(alog_ref[...])
    ca = _mm_sel(tri, dtv * a_neg)
    ca_t = ca.T
    dt_t = dtv.T
    eca = jnp.exp(ca)
    ca_last = ca[L - 1:L, :]
    wss = jnp.exp(ca_last - ca) * dtv
    sdec = jnp.exp(ca_last)
    sd_out = [None] * N_HEADS

    def sd_group(grp):
        bm_g = xbc[:, MIX_W + grp * SSD_N:MIX_W + (grp + 1) * SSD_N]
        cm_g = xbc[:, MIX_W + 2 * SSD_N + grp * SSD_N:MIX_W + 2 * SSD_N + (grp + 1) * SSD_N]
        cb = _mm_nt(cm_g, bm_g)
        yield
        for h in (2 * grp, 2 * grp + 1):
            xh = xs_all[:, h * HEAD:(h + 1) * HEAD]
            hst = h_ref[g, h]
            from_state = _mm_nt(cm_g, hst)
            yield
            dec = jnp.where(causal, jnp.exp(ca[:, h:h + 1] - ca_t[h:h + 1, :]), 0.0)
            mmat = cb * dec * dt_t[h:h + 1, :]
            sd_out[h] = _mm(mmat, xh) + eca[:, h:h + 1] * from_state
            yield
            upd = _mm_tn(xh * wss[:, h:h + 1], bm_g)
            yield
            h_ref[g, h] = sdec[:, h:h + 1] * hst + upd

    def sd_finish():
        y = jnp.concatenate(sd_out, axis=1) + ssdd_ref[...] * xs_all
        gated = y * _silu(proj_ref[rs,C_SDZ:C_SDZ + MIX_W])
        mixed_ref[rs,3 * MIX_W:4 * MIX_W] = (_head_norm(gated, 2 * HEAD)
                                             * ssdnw_ref[...]).astype(mixed_ref.dtype)

    chains = ([[hg_within_block(blk) for blk in range(n_blk)]]
              + [[ml_head(h)] for h in range(N_HEADS)] + [[hg_state_chain()]] + [[sd_group(grp)] for grp in range(2)]
              + [[sw_group(kv)] for kv in range(SW_KV)])

    def finish():
        ml_finish()
        hg_finish()
        sw_finish()
        sd_finish()

    return chains, finish


def _mixer(proj, rope_tab, init, lp, consts, chunks_per_prompt, n_seq):
    n = proj.shape[0]
    rows = SEQ_GROUP * CHUNK
    n_steps = n // rows

    def group_of(i):
        return jnp.where(i < chunks_per_prompt, 0, i - chunks_per_prompt + 1)

    def rope_blk(i):
        return jnp.minimum(i, chunks_per_prompt)

    def per_seq(shape):
        nd = len(shape)
        return pl.BlockSpec((SEQ_GROUP,) + shape, lambda i, s: (group_of(i),) + (0,) * nd)

    def const(shape):
        nd = len(shape)
        return pl.BlockSpec(shape, lambda i, s: (0,) * nd)

    state_shapes = [(N_HEADS, HEAD, 128), (8, 128), (HEAD, MIX_W), (WINDOW, 128), (WINDOW, 128),
                    (N_HEADS, HEAD, SSD_N), (8, SSD_CONV_DIM)]
    in_specs = ([pl.BlockSpec((rows, D_PROJ), lambda i, s: (i, 0)),
                 pl.BlockSpec((CHUNK, 384), lambda i, s: (rope_blk(i), 0))]
                + [per_seq(s) for s in state_shapes]
                + [const((1, 128)), const((1, 128)), const((1, MIX_W)), const((1, MIX_W)), const((1, MIX_W)),
                   const((SSD_CONV, SSD_CONV_DIM)), const((1, SSD_CONV_DIM)), const((1, MIX_W)),
                   const((1, MIX_W)),
                   const((CHUNK, CHUNK)), const((CHUNK, CHUNK)), const((CHUNK, CHUNK)), const((MIX_W, MIX_W))])
    out_specs = ([pl.BlockSpec((rows, D_MODEL), lambda i, s: (i, 0))]
                 + [per_seq(s) for s in state_shapes])
    out_shape = ([jax.ShapeDtypeStruct((n, D_MODEL), BF16)]
                 + [jax.ShapeDtypeStruct((n_seq,) + s, F32) for s in state_shapes])
    grid_spec = pltpu.PrefetchScalarGridSpec(
        num_scalar_prefetch=1, grid=(n_steps,), in_specs=in_specs, out_specs=out_specs,
        scratch_shapes=[pltpu.VMEM((SEQ_GROUP, CHUNK + 8, SSD_CONV_DIM), F32),
                        pltpu.VMEM((SEQ_GROUP, CHUNK, MIX_W), F32), pltpu.VMEM((SEQ_GROUP, CHUNK, MIX_W), F32),
                        pltpu.VMEM((SEQ_GROUP, CHUNK, MIX_W), F32)])
    return pl.pallas_call(
        functools.partial(_mixer_kernel, chunks_per_prompt),
        grid_spec=grid_spec, out_shape=out_shape,
        compiler_params=pltpu.CompilerParams(dimension_semantics=("arbitrary",),
                                             vmem_limit_bytes=VMEM_LIMIT),
        name="mixer",
    )(lp["sinks"], proj, rope_tab, *init,
      lp["gbias"], lp["alog"], lp["mlnw"], lp["hglb"], lp["hgnw"],
      lp["convw"], lp["convb"], lp["ssdd"], lp["ssdnw"], *consts)


def _dense_ffn_kernel(f_chunk, n_parts, bounds, *refs):
    x_refs = refs[:n_parts]
    mix_ref, wo_ref, nw_ref, wg_ref, wu_ref, wd_ref, o_ref = refs[n_parts:]
    x1 = _read_token_tile(x_refs, bounds) + jnp.dot(mix_ref[...], wo_ref[...], preferred_element_type=F32)
    hn = _rms(x1, nw_ref[...]).astype(BF16)
    o_ref[...] = x1
    for f0 in range(0, wg_ref.shape[1], f_chunk):
        g = jnp.dot(hn, wg_ref[:, f0:f0 + f_chunk], preferred_element_type=F32)
        u = jnp.dot(hn, wu_ref[:, f0:f0 + f_chunk], preferred_element_type=F32)
        act = (_silu(g) * u).astype(BF16)
        o_ref[...] += jnp.dot(act, wd_ref[f0:f0 + f_chunk, :], preferred_element_type=F32)


def _dense_ffn(x_parts, mixed, wo, nw, wg, wu, wd):
    n = mixed.shape[0]
    d_ff = wg.shape[1]
    tm = _pick_tile(np.gcd.reduce([_part_rows(p) for p in x_parts]), TOKEN_TILES)
    x_specs, bounds = _token_parts_specs(x_parts, tm)
    full = lambda shape: pl.BlockSpec(shape, lambda i: (0, 0))
    return pl.pallas_call(
        functools.partial(_dense_ffn_kernel, 256, len(x_parts), bounds),
        grid=(n // tm,),
        in_specs=x_specs + [
                  pl.BlockSpec((tm, D_MODEL), lambda i: (i, 0)),
                  full((D_MODEL, D_MODEL)), full((1, D_MODEL)),
                  full((D_MODEL, d_ff)), full((D_MODEL, d_ff)), full((d_ff, D_MODEL))],
        out_specs=pl.BlockSpec((tm, D_MODEL), lambda i: (i, 0)),
        out_shape=jax.ShapeDtypeStruct((n, D_MODEL), F32),
        compiler_params=pltpu.CompilerParams(dimension_semantics=("arbitrary",),
                                             vmem_limit_bytes=VMEM_LIMIT),
        name="dense_ffn",
    )(*x_parts, mixed, wo, nw, wg, wu, wd)


TOK_BLK = 256
FFN_TILE = 1024


def _pack_bf16_pairs(x):
    w = x.shape[1] // 2
    bits = pltpu.bitcast(x.astype(BF16).astype(F32), jnp.int32)
    return lax.shift_right_logical(bits[:, :w], 16) | bits[:, w:]


def _unpack_bf16_pairs(p):
    lo = pltpu.bitcast(lax.shift_left(p, 16), F32)
    hi = pltpu.bitcast(p & jnp.int32(-65536), F32)
    return lo, hi


def _router_kernel(x_ref, mix_ref, wo_ref, nw_ref, wrt_ref, upper_ref,
                   x1_ref, hn_ref, meta_ref, count_ref, carry):
    @pl.when(pl.program_id(0) == 0)
    def _():
        carry[...] = jnp.zeros_like(carry)

    x1 = x_ref[...] + jnp.dot(mix_ref[...], wo_ref[...], preferred_element_type=F32)
    x1_ref[...] = x1
    hn_f = _rms(x1, nw_ref[...])
    hn = hn_f.astype(BF16)
    hn_ref[...] = _pack_bf16_pairs(hn_f)
    logits = lax.dot_general(wrt_ref[...], hn, (((1,), (1,)), ((), ())),
                             preferred_element_type=F32)
    sub = lax.broadcasted_iota(jnp.int32, logits.shape, 0)
    l1 = jnp.where(sub < N_EXPERTS, logits, NEG)
    m1 = jnp.max(l1, axis=0, keepdims=True)
    i1 = jnp.min(jnp.where(l1 == m1, sub, 99), axis=0, keepdims=True)
    l2 = jnp.where(sub == i1, NEG, l1)
    m2 = jnp.max(l2, axis=0, keepdims=True)
    i2 = jnp.min(jnp.where(l2 == m2, sub, 99), axis=0, keepdims=True)
    e2 = jnp.exp(m2 - m1)
    g1 = 1.0 / (1.0 + e2)
    g2 = e2 / (1.0 + e2)
    sel = jnp.where(sub == i1, 1.0, jnp.where(sub == i2, 1.0, 0.0))
    before = jnp.dot(sel.astype(BF16), upper_ref[...], preferred_element_type=F32) + carry[:, 0:1]
    r1 = jnp.sum(jnp.where(sub == i1, before, 0.0), axis=0, keepdims=True)
    r2 = jnp.sum(jnp.where(sub == i2, before, 0.0), axis=0, keepdims=True)
    meta_ref[...] = jnp.concatenate([i1.astype(F32), i2.astype(F32), r1, r2, g1, g2,
                                     jnp.zeros((2, r1.shape[1]), F32)], axis=0)
    total = carry[...] + jnp.sum(sel, axis=1, keepdims=True)
    carry[...] = total
    count_ref[...] = total


def _router(x, mixed, wo, nw, wrt, upper):
    n = x.shape[0]
    tm = TOK_BLK
    full = lambda shape: pl.BlockSpec(shape, lambda i: (0,) * len(shape))
    tok = lambda w: pl.BlockSpec((tm, w), lambda i: (i, 0))
    return pl.pallas_call(
        _router_kernel,
        grid=(n // tm,),
        in_specs=[tok(D_MODEL), tok(D_MODEL), full((D_MODEL, D_MODEL)), full((1, D_MODEL)),
                  full((16, D_MODEL)), full((tm, tm))],
        out_specs=[tok(D_MODEL), tok(D_MODEL // 2),
                   pl.BlockSpec((8, tm), lambda i: (0, i)), full((16, 128))],
        out_shape=[jax.ShapeDtypeStruct((n, D_MODEL), F32), jax.ShapeDtypeStruct((n, D_MODEL // 2), jnp.int32),
                   jax.ShapeDtypeStruct((8, n), F32), jax.ShapeDtypeStruct((16, 128), F32)],
        scratch_shapes=[pltpu.VMEM((16, 128), F32)],
        compiler_params=pltpu.CompilerParams(dimension_semantics=("arbitrary",),
                                             vmem_limit_bytes=VMEM_LIMIT),
        name="router",
    )(x, mixed, wo, nw, wrt, upper)


def _sc_row_gather(table, idx):
    n_workers = SC_CORES * SC_SUBCORES
    b = idx.shape[0]
    d = table.shape[1]
    assert b % (8 * n_workers) == 0
    per_worker = b // n_workers
    chunk = max(c for c in range(8, SC_MAX_GATHER_ROWS + 1, 8) if per_worker % c == 0)
    mesh = plsc.VectorSubcoreMesh(core_axis_name="c", subcore_axis_name="s")

    @functools.partial(
        pl.kernel, mesh=mesh, out_type=jax.ShapeDtypeStruct((b, d), table.dtype),
        scratch_types=[pltpu.VMEM((chunk,), jnp.int32), pltpu.VMEM((chunk, d), table.dtype),
                       pltpu.SemaphoreType.DMA])
    def gather(table_hbm, idx_hbm, out_hbm, idx_v, rows_v, sem):
        worker = lax.axis_index("s") * SC_CORES + lax.axis_index("c")
        base = worker * per_worker

        @pl.loop(0, per_worker // chunk)
        def _(c):
            off = pl.multiple_of(base + c * chunk, 8)
            pltpu.sync_copy(idx_hbm.at[pl.ds(off, chunk)], idx_v)
            pltpu.async_copy(table_hbm.at[idx_v], rows_v, sem).wait()
            pltpu.sync_copy(rows_v, out_hbm.at[pl.ds(off, chunk)])

    return gather(table, idx)


def _sc_row_scatter(rows, idx):
    n_workers = SC_CORES * SC_SUBCORES
    b = n_out = idx.shape[0]
    v, d = rows.shape
    assert b % (8 * n_workers) == 0
    per_worker = b // n_workers
    chunk = max(c for c in range(8, SC_MAX_GATHER_ROWS + 1, 8) if per_worker % c == 0 and v % c == 0)
    mesh = plsc.VectorSubcoreMesh(core_axis_name="c", subcore_axis_name="s")

    @functools.partial(
        pl.kernel, mesh=mesh, out_type=jax.ShapeDtypeStruct((n_out, d), rows.dtype),
        scratch_types=[pltpu.VMEM((chunk,), jnp.int32), pltpu.VMEM((chunk, d), rows.dtype)])
    def scatter(rows_hbm, idx_hbm, out_hbm, idx_v, rows_v):
        worker = lax.axis_index("s") * SC_CORES + lax.axis_index("c")
        base = worker * per_worker

        @pl.loop(0, per_worker // chunk)
        def _(c):
            off = pl.multiple_of(base + c * chunk, 8)
            pltpu.sync_copy(idx_hbm.at[pl.ds(off, chunk)], idx_v)
            pltpu.sync_copy(rows_hbm.at[pl.ds(pl.multiple_of(lax.rem(off, v), 8), chunk)], rows_v)
            pltpu.sync_copy(rows_v, out_hbm.at[idx_v])

    return scatter(rows, idx)


def _expert_ffn_kernel(texp_ref, tvalid_ref, xs_ref, wg_ref, wu_ref, wd_ref, o_ref, acc_ref, xb_ref):
    j = pl.program_id(0)
    f = pl.program_id(1)
    last = f == pl.num_programs(1) - 1
    valid = tvalid_ref[j] == 1
    half = D_MODEL // 2

    @pl.when(jnp.logical_and(valid, f == 0))
    def _():
        lo, hi = _unpack_bf16_pairs(xs_ref[...])
        xb_ref[:, :half] = lo.astype(BF16)
        xb_ref[:, half:] = hi.astype(BF16)

    @pl.when(valid)
    def _():
        xs = xb_ref[...]
        g = jnp.dot(xs, wg_ref[0], preferred_element_type=F32)
        u = jnp.dot(xs, wu_ref[0], preferred_element_type=F32)
        act = (_silu(g) * u).astype(BF16)
        part = jnp.dot(act, wd_ref[0], preferred_element_type=F32)

        @pl.when(f == 0)
        def _():
            acc_ref[...] = part

        @pl.when(f > 0)
        def _():
            acc_ref[...] += part

        @pl.when(last)
        def _():
            o_ref[...] = _pack_bf16_pairs(acc_ref[...])

    @pl.when(jnp.logical_and(jnp.logical_not(valid), last))
    def _():
        o_ref[...] = jnp.zeros_like(o_ref)


def _expert_ffn(f_exp, f_valid, xs, wg, wu, wd):
    n_slots = xs.shape[0]
    d_ff = wg.shape[2]
    tf = _pick_tile(d_ff, (512, 256, 128))
    grid_spec = pltpu.PrefetchScalarGridSpec(
        num_scalar_prefetch=2, grid=(n_slots // FFN_TILE, d_ff // tf),
        in_specs=[pl.BlockSpec((FFN_TILE, D_MODEL // 2), lambda j, f, te, tv: (j, 0)),
                  pl.BlockSpec((1, D_MODEL, tf), lambda j, f, te, tv: (te[j], 0, f)),
                  pl.BlockSpec((1, D_MODEL, tf), lambda j, f, te, tv: (te[j], 0, f)),
                  pl.BlockSpec((1, tf, D_MODEL), lambda j, f, te, tv: (te[j], f, 0))],
        out_specs=pl.BlockSpec((FFN_TILE, D_MODEL // 2), lambda j, f, te, tv: (j, 0)),
        scratch_shapes=[pltpu.VMEM((FFN_TILE, D_MODEL), F32), pltpu.VMEM((FFN_TILE, D_MODEL), BF16)])
    return pl.pallas_call(
        _expert_ffn_kernel, grid_spec=grid_spec,
        out_shape=jax.ShapeDtypeStruct((n_slots, D_MODEL // 2), jnp.int32),
        compiler_params=pltpu.CompilerParams(dimension_semantics=("arbitrary", "arbitrary"),
                                             vmem_limit_bytes=VMEM_LIMIT),
        name="expert_ffn",
    )(f_exp, f_valid, xs, wg, wu, wd)


def _combine_kernel(part_tiles, x1_ref, o1_ref, o2_ref, gate_ref, fnw_ref, *y_refs):
    i = pl.program_id(0)
    lo1, hi1 = _unpack_bf16_pairs(o1_ref[...])
    lo2, hi2 = _unpack_bf16_pairs(o2_ref[...])
    g1 = gate_ref[:, 0:1]
    g2 = gate_ref[:, 1:2]
    half = D_MODEL // 2
    x1 = x1_ref[...]
    fnw = fnw_ref[...]
    ya = x1[:, :half] + g1 * lo1 + g2 * lo2
    yb = x1[:, half:] + g1 * hi1 + g2 * hi2
    scale = lax.rsqrt((jnp.sum(ya * ya, axis=-1, keepdims=True) + jnp.sum(yb * yb, axis=-1, keepdims=True))
                      / D_MODEL + EPS)
    y = jnp.concatenate([ya * scale * fnw[:, :half], yb * scale * fnw[:, half:]], axis=1)
    lo_tile = 0
    for y_ref, tiles in zip(y_refs, part_tiles):
        @pl.when(jnp.logical_and(i >= lo_tile, i < lo_tile + tiles))
        def _(y_ref=y_ref):
            _store_part_tile(y_ref, y)
        lo_tile += tiles


def _combine(x1, o12, gates, fnw, part_shapes):
    n = x1.shape[0]
    parts = [jax.ShapeDtypeStruct(s, F32) for s in part_shapes]
    tm = _pick_tile(np.gcd.reduce([_part_rows(p) for p in parts]), TOKEN_TILES)
    n_tiles = n // tm
    out_specs, bounds = _token_parts_specs(parts, tm)
    part_tiles = [bounds[k + 1] - bounds[k] for k in range(len(parts))]
    return pl.pallas_call(
        functools.partial(_combine_kernel, part_tiles),
        grid=(n_tiles,),
        in_specs=[pl.BlockSpec((tm, D_MODEL), lambda i: (i, 0)),
                  pl.BlockSpec((tm, D_MODEL // 2), lambda i: (i, 0)),
                  pl.BlockSpec((tm, D_MODEL // 2), lambda i: (i + n_tiles, 0)),
                  pl.BlockSpec((tm, 128), lambda i: (i, 0)),
                  pl.BlockSpec((1, D_MODEL), lambda i: (0, 0))],
        out_specs=out_specs,
        out_shape=parts,
        compiler_params=pltpu.CompilerParams(dimension_semantics=("arbitrary",),
                                             vmem_limit_bytes=VMEM_LIMIT),
        name="combine",
    )(x1, o12, o12, gates, fnw)


def _moe(x, mixed, wo, nw, router_w, wg, wu, wd, fnw, part_shapes):
    i32 = jnp.int32
    n = x.shape[0]
    assert n % TOK_BLK == 0
    wrt = jnp.pad(router_w.T, ((0, 16 - N_EXPERTS), (0, 0))).astype(BF16)
    t = np.arange(TOK_BLK)
    upper = jnp.asarray((t[:, None] < t[None, :]).astype(np.float32), dtype=BF16)
    x1, hn_p, meta, count = _router(x, mixed, wo, nw, wrt, upper)

    n_slots = (-(-2 * n // FFN_TILE) + N_EXPERTS) * FFN_TILE
    counts = count[:N_EXPERTS, 0].astype(i32)
    gsize = (counts + FFN_TILE - 1) // FFN_TILE * FFN_TILE
    gend = jnp.cumsum(gsize)
    goff = gend - gsize
    fstart = jnp.arange(n_slots // FFN_TILE, dtype=i32) * FFN_TILE
    f_valid = (fstart < gend[-1]).astype(i32)
    f_exp = jnp.sum((fstart[:, None] >= gend[None, :]).astype(i32), axis=1)
    f_exp = jnp.minimum(f_exp, jnp.sum((gend[-1] - 1 >= gend).astype(i32)))
    expert_ids = jnp.arange(N_EXPERTS, dtype=i32)[:, None]
    top = meta[0:2].astype(i32)
    rank = meta[2:4].astype(i32)
    slot = jnp.stack([jnp.sum(jnp.where(top[k][None, :] == expert_ids, goff[:, None], 0), axis=0) + rank[k]
                      for k in range(2)]).reshape(-1)
    pad = gsize - counts
    cpad = jnp.cumsum(pad)
    j = jnp.arange(n_slots - 2 * n, dtype=i32)
    pad_e = jnp.sum((j[:, None] >= cpad[None, :]).astype(i32), axis=1)
    pad_base = jnp.sum(jnp.where(pad_e[:, None] == expert_ids.T, (goff + counts - (cpad - pad))[None, :], 0), axis=1)
    pad_slot = jnp.where(pad_e < N_EXPERTS, pad_base + j, gend[-1] + j - cpad[-1])

    xs_p = _sc_row_scatter(hn_p, jnp.concatenate([slot, pad_slot]))
    out_p = _expert_ffn(f_exp, f_valid, xs_p, wg, wu, wd)
    o12 = _sc_row_gather(out_p, slot)
    gates = _pad_lanes(meta[4:6].T, 128)
    return _combine(x1, o12, gates, fnw, part_shapes)


def _rope_table(seq_len, dec_len):
    half = ROPE_DIMS // 2
    pos = jnp.concatenate([jnp.arange(seq_len), PAST_LEN + jnp.arange(dec_len)]).astype(F32)
    inv_freq = jnp.power(jnp.float32(ROPE_THETA), -jnp.arange(half, dtype=F32) / half)
    lane = np.arange(128) % HEAD
    inv_lane = jnp.where(lane < ROPE_DIMS, inv_freq[lane % half], 0.0)
    ang = pos[:, None] * inv_lane[None, :]
    cos, sin = jnp.cos(ang), jnp.sin(ang)
    sin_a = jnp.where(lane < half, -sin, 0.0)
    sin_b = jnp.where((lane >= half) & (lane < ROPE_DIMS), sin, 0.0)
    return jnp.concatenate([cos, sin_a, sin_b], axis=1)


def _mixer_consts():
    t = np.arange(CHUNK)
    tri = (t[:, None] >= t[None, :])
    same = (t[:, None] // SUB) == (t[None, :] // SUB)
    c = np.arange(MIX_W)
    ind = (c[:, None] // HEAD) == (c[None, :] // HEAD)
    as_bf = lambda m: jnp.asarray(m.astype(np.float32), dtype=BF16)
    return [as_bf(tri), as_bf(tri & same), as_bf(same), as_bf(ind)]


def _pad_lanes(v, width):
    return jnp.pad(v, ((0, 0), (0, width - v.shape[1])))


def kernel(x_prompt, x_sample, state_mlstm_C, state_mlstm_n, state_mlstm_m, state_hgrn_S, cache_swa_k, cache_swa_v, state_ssd_h, state_ssd_conv, norm1_w, w_in, ml_ig_b, ml_fg_b, ml_norm_w, hg_lb_logits, hg_norm_w, sw_sinks, ssd_conv_w, ssd_conv_b, ssd_dt_bias, ssd_A_log, ssd_D, ssd_norm_w, w_out, norm2_w, ffn_w_gate, ffn_w_up, ffn_w_down, moe_router, moe_w_gate, moe_w_up, moe_w_down, final_norm_w):
    depth = w_in.shape[0]
    bp, seq_len, _ = x_prompt.shape
    bs, dec_len, _ = x_sample.shape
    assert seq_len % CHUNK == 0 and dec_len == CHUNK and depth % 2 == 0
    n_seq = bp + bs
    chunks_per_prompt = seq_len // CHUNK

    assert bp == SEQ_GROUP and bs % SEQ_GROUP == 0
    x_parts = (x_prompt.reshape(bp, chunks_per_prompt, CHUNK, D_MODEL), x_sample.reshape(bs * dec_len, D_MODEL))
    part_shapes = tuple(p.shape for p in x_parts)
    rope_tab = _rope_table(seq_len, dec_len)
    consts = _mixer_consts()

    sm = jax.nn.softmax(hg_lb_logits.astype(F32), axis=0)
    hg_lb = jnp.cumsum(sm, axis=0) - sm[0]

    def with_prompt_zeros(a):
        return jnp.concatenate([jnp.zeros((bp,) + a.shape[1:], F32), a.astype(F32)], axis=0)

    state_outs = []
    for l in range(depth):
        lp = {
            "sinks": sw_sinks[l].astype(F32),
            "gbias": _pad_lanes(jnp.concatenate([ml_ig_b[l], ml_fg_b[l], ssd_dt_bias[l]])[None, :].astype(F32), 128),
            "alog": _pad_lanes(ssd_A_log[l][None, :].astype(F32), 128),
            "mlnw": ml_norm_w[l][None, :].astype(F32),
            "hglb": hg_lb[l][None, :],
            "hgnw": hg_norm_w[l][None, :].astype(F32),
            "convw": ssd_conv_w[l].astype(F32),
            "convb": ssd_conv_b[l][None, :].astype(F32),
            "ssdd": jnp.repeat(ssd_D[l].astype(F32), HEAD)[None, :],
            "ssdnw": ssd_norm_w[l][None, :].astype(F32),
        }
        c_aug = jnp.concatenate([state_mlstm_C[l], jnp.broadcast_to(state_mlstm_n[l][..., None], state_mlstm_C[l].shape)], axis=-1)
        m_pad = jnp.pad(state_mlstm_m[l][:, None, :], ((0, 0), (0, 7), (0, 128 - N_HEADS)))
        s_t = jnp.swapaxes(state_hgrn_S[l], -1, -2)
        s_t = jnp.moveaxis(s_t, 1, 2).reshape(bs, HEAD, MIX_W)
        cv_pad = jnp.pad(state_ssd_conv[l], ((0, 0), (8 - (SSD_CONV - 1), 0), (0, 0)))
        init = [with_prompt_zeros(a) for a in (
            c_aug, m_pad, s_t, cache_swa_k[l].reshape(bs, WINDOW, 128), cache_swa_v[l].reshape(bs, WINDOW, 128),
            state_ssd_h[l], cv_pad)]

        proj = _inproj(x_parts, norm1_w[l][None, :], w_in[l])
        mixed, c_o, m_o, s_o, k_o, v_o, h_o, cv_o = _mixer(
            proj, rope_tab, init, lp, consts, chunks_per_prompt, n_seq)
        wo = w_out[l].astype(BF16)
        j = l // 2
        if l % 2 == 0:
            x_parts = (_dense_ffn(x_parts, mixed, wo, norm2_w[l][None, :], ffn_w_gate[j].astype(BF16),
                                  ffn_w_up[j].astype(BF16), ffn_w_down[j].astype(BF16)),)
        else:
            assert l == depth - 1 and len(x_parts) == 1
            y_parts = _moe(x_parts[0], mixed, wo, norm2_w[l][None, :], moe_router[j], moe_w_gate[j].astype(BF16),
                           moe_w_up[j].astype(BF16), moe_w_down[j].astype(BF16), final_norm_w[None, :],
                           part_shapes)
        s_back = jnp.moveaxis(s_o.reshape(n_seq, HEAD, N_HEADS, HEAD), 2, 1)
        state_outs.append((c_o[..., :HEAD], c_o[:, :, :, HEAD], m_o[:, 0, :N_HEADS],
                           jnp.swapaxes(s_back, -1, -2),
                           k_o.reshape(n_seq, WINDOW, SW_KV, HEAD), v_o.reshape(n_seq, WINDOW, SW_KV, HEAD),
                           h_o, cv_o[:, 8 - (SSD_CONV - 1):, :]))

    y_prompt = y_parts[0].reshape(bp, seq_len, D_MODEL)
    y_sample = y_parts[1].reshape(bs, dec_len, D_MODEL)
    stacked = [jnp.stack([so[k] for so in state_outs]) for k in range(8)]
    return (y_prompt, y_sample) + tuple(s[:, :bp] for s in stacked) + tuple(s[:, bp:] for s in stacked)
```

```python
import functools

import numpy as np
import jax
import jax.numpy as jnp
from jax import lax
from jax.experimental import pallas as pl
from jax.experimental.pallas import tpu as pltpu
from jax.experimental.pallas import tpu_sc as plsc

F32 = jnp.float32
BF16 = jnp.bfloat16

D_MODEL = 1024
CHUNK = 64
EPS = 1e-6
N_HEADS = 4
HEAD = 64
MIX_W = N_HEADS * HEAD
SW_KV = 2
WINDOW = 128
ROPE_DIMS = 16
ROPE_THETA = 500000.0
SSD_N = 128
SSD_CONV = 4
SSD_CONV_DIM = 768
PAST_LEN = 4096
N_EXPERTS = 8
SUB = 16
SEQ_GROUP = 2
HG_SLICE = 4
TOKEN_TILES = (512, 256, 128)

C_MLQ, C_MLK, C_MLV, C_MLO = 0, 256, 512, 768
C_HGQ, C_HGF, C_HGI, C_HGG = 1024, 1280, 1536, 1792
C_SWQ, C_SWK, C_SWV = 2048, 2304, 2432
C_SDZ, C_SDX, C_GATE = 2560, 2816, 3584
D_PROJ = 3712
NEG = -1e30
VMEM_LIMIT = 56 * 1024 * 1024
SC_CORES = 2
SC_SUBCORES = 16
SC_MAX_GATHER_ROWS = 128


def _mm(a, b):
    return jnp.dot(a.astype(BF16), b.astype(BF16), preferred_element_type=F32)


def _mm_nt(a, b):
    return lax.dot_general(a.astype(BF16), b.astype(BF16), (((1,), (1,)), ((), ())),
                           preferred_element_type=F32)


def _mm_tn(a, b):
    return jnp.dot(a.T.astype(BF16), b.astype(BF16), preferred_element_type=F32)


def _mm_sel(sel, x):
    hi = x.astype(BF16)
    r1 = x - hi.astype(F32)
    mid = r1.astype(BF16)
    lo = (r1 - mid.astype(F32)).astype(BF16)
    return (jnp.dot(sel, hi, preferred_element_type=F32)
            + jnp.dot(sel, mid, preferred_element_type=F32)
            + jnp.dot(sel, lo, preferred_element_type=F32))


def _sigmoid(x):
    return 1.0 / (1.0 + jnp.exp(-x))


def _silu(x):
    return x * _sigmoid(x)


def _softplus(x):
    return jnp.maximum(x, 0.0) + jnp.log(1.0 + jnp.exp(-jnp.abs(x)))


def _rms(x, w):
    return x * lax.rsqrt(jnp.mean(x * x, axis=-1, keepdims=True) + EPS) * w


def _pick_tile(n, candidates):
    for c in candidates:
        if n % c == 0:
            return c
    raise ValueError(f"no tile for {n}")


def _token_parts_specs(parts, tm):
    bounds = np.cumsum([0] + [_part_rows(p) // tm for p in parts])
    specs = []
    for k, p in enumerate(parts):
        assert _part_rows(p) % tm == 0
        lo, hi = int(bounds[k]), int(bounds[k + 1])
        specs.append(_part_block_spec(p, tm, lo, hi - lo))
    return specs, [int(b) for b in bounds]


def _part_rows(p):
    return int(np.prod(p.shape[:-1]))


def _part_block_spec(p, tm, first_tile, n_tiles):
    idx = lambda i: jnp.clip(i - first_tile, 0, n_tiles - 1)
    if len(p.shape) == 2:
        return pl.BlockSpec((tm, p.shape[1]), lambda i: (idx(i), 0))
    grp, _, chunk, d = p.shape
    assert tm % (grp * chunk) == 0
    return pl.BlockSpec((grp, tm // (grp * chunk), chunk, d), lambda i: (0, idx(i), 0, 0))


def _load_part_tile(ref):
    if len(ref.shape) == 2:
        return ref[...]
    grp, n_chunks = ref.shape[:2]
    return jnp.concatenate([ref[s, c] for c in range(n_chunks) for s in range(grp)], axis=0)


def _store_part_tile(ref, x):
    if len(ref.shape) == 2:
        ref[...] = x
        return
    grp, n_chunks, chunk = ref.shape[:3]
    for c in range(n_chunks):
        for s in range(grp):
            r0 = (c * grp + s) * chunk
            ref[s, c] = x[r0:r0 + chunk]


def _read_token_tile(refs, bounds):
    i = pl.program_id(0)
    x = _load_part_tile(refs[-1])
    for k in range(len(refs) - 2, -1, -1):
        x = jnp.where(i < bounds[k + 1], _load_part_tile(refs[k]), x)
    return x


def _inproj_kernel(n_parts, bounds, *refs):
    x_refs = refs[:n_parts]
    nw_ref, w_ref, o_ref, wp_ref = refs[n_parts:]

    @pl.when(pl.program_id(0) == 0)
    def _():
        rows = 128
        for r0 in range(0, D_MODEL, rows):
            w = w_ref[r0:r0 + rows, :]
            wp_ref[r0:r0 + rows, 0:C_MLO] = w[:, 0:768].astype(BF16)
            wp_ref[r0:r0 + rows, C_MLO:C_GATE] = w[:, 776:3592].astype(BF16)
            gates = jnp.concatenate([w[:, 768:776], w[:, 3592:3596],
                                     jnp.zeros((rows, D_PROJ - C_GATE - 12), F32)], axis=1)
            wp_ref[r0:r0 + rows, C_GATE:D_PROJ] = gates.astype(BF16)

    xn = _rms(_read_token_tile(x_refs, bounds), nw_ref[...]).astype(BF16)
    o_ref[...] = jnp.dot(xn, wp_ref[...], preferred_element_type=F32)


def _inproj(x_parts, nw, w):
    n = sum(_part_rows(p) for p in x_parts)
    tm = _pick_tile(np.gcd.reduce([_part_rows(p) for p in x_parts]), TOKEN_TILES)
    x_specs, bounds = _token_parts_specs(x_parts, tm)
    d_in = w.shape[1]
    return pl.pallas_call(
        functools.partial(_inproj_kernel, len(x_parts), bounds),
        grid=(n // tm,),
        in_specs=x_specs + [pl.BlockSpec((1, D_MODEL), lambda i: (0, 0)),
                            pl.BlockSpec((D_MODEL, d_in), lambda i: (0, 0), pipeline_mode=pl.Buffered(1))],
        out_specs=pl.BlockSpec((tm, D_PROJ), lambda i: (i, 0)),
        out_shape=jax.ShapeDtypeStruct((n, D_PROJ), F32),
        scratch_shapes=[pltpu.VMEM((D_MODEL, D_PROJ), BF16)],
        compiler_params=pltpu.CompilerParams(dimension_semantics=("arbitrary",),
                                             vmem_limit_bytes=VMEM_LIMIT),
        name="inproj",
    )(*x_parts, nw, w)


def _head_norm(x, width):
    parts = []
    for g in range(x.shape[1] // width):
        xg = x[:, g * width:(g + 1) * width]
        parts.append(xg * lax.rsqrt(jnp.mean(xg * xg, axis=-1, keepdims=True) + EPS))
    return jnp.concatenate(parts, axis=1)


def _mixer_kernel(prompt_steps,
                  sinks_ref,
                  proj_ref, rope_ref, c0_ref, m0_ref, s0_ref, k0_ref, v0_ref, h0_ref, cv0_ref,
                  gbias_ref, alog_ref, mlnw_ref, hglb_ref, hgnw_ref,
                  convw_ref, convb_ref, ssdd_ref, ssdnw_ref,
                  tri_ref, triblk_ref, blkones_ref, ind_ref,
                  mixed_ref, c_ref, m_ref, s_ref, k_ref, v_ref, h_ref, cv_ref,
                  convbuf, bloc_s, kk_s, hv_s):
    i = pl.program_id(0)
    is_prompt = i < prompt_steps
    is_first = jnp.logical_or(jnp.logical_not(is_prompt), i == 0)
    n_valid = jnp.where(is_prompt, jnp.minimum(i * CHUNK, WINDOW), WINDOW)

    @pl.when(is_first)
    def _():
        c_ref[...] = c0_ref[...]
        m_ref[...] = m0_ref[...]
        s_ref[...] = s0_ref[...]
        k_ref[...] = k0_ref[...]
        v_ref[...] = v0_ref[...]
        h_ref[...] = h0_ref[...]
        cv_ref[...] = cv0_ref[...]

    members = [_mixer_chunk(g, n_valid, sinks_ref, proj_ref, rope_ref,
                            gbias_ref, alog_ref, mlnw_ref, hglb_ref, hgnw_ref,
                            convw_ref, convb_ref, ssdd_ref, ssdnw_ref,
                            tri_ref, triblk_ref, blkones_ref, ind_ref,
                            mixed_ref, c_ref, m_ref, s_ref, k_ref, v_ref, h_ref, cv_ref,
                            convbuf, bloc_s, kk_s, hv_s) for g in range(SEQ_GROUP)]
    for phase in zip(*[m[0] for m in members]):
        chains = [c for per_member in zip(*phase) for c in per_member]
        while chains:
            alive = []
            for chain in chains:
                if next(chain, _CHAIN_DONE) is not _CHAIN_DONE:
                    alive.append(chain)
            chains = alive
    for _, finish in members:
        finish()


_CHAIN_DONE = object()


def _mixer_chunk(g, n_valid, sinks_ref, proj_ref, rope_ref,
                 gbias_ref, alog_ref, mlnw_ref, hglb_ref, hgnw_ref,
                 convw_ref, convb_ref, ssdd_ref, ssdnw_ref,
                 tri_ref, triblk_ref, blkones_ref, ind_ref,
                 mixed_ref, c_ref, m_ref, s_ref, k_ref, v_ref, h_ref, cv_ref,
                 convbuf, bloc_s, kk_s, hv_s):
    rs = slice(g * CHUNK, (g + 1) * CHUNK)
    L = CHUNK
    row = lax.broadcasted_iota(jnp.int32, (L, L), 0)
    col = lax.broadcasted_iota(jnp.int32, (L, L), 1)
    causal = row >= col
    tri = tri_ref[...]

    gate = proj_ref[rs,C_GATE:C_GATE + 128] + gbias_ref[...]
    ig = gate
    lf = pltpu.roll(-_softplus(-gate), 124, 1)
    dtv = pltpu.roll(_softplus(gate), 120, 1)

    b = _mm_sel(tri, lf)
    a = ig - b
    cm = a
    rowg = lax.broadcasted_iota(jnp.int32, (L, 128), 0)
    for sh in (1, 2, 4, 8, 16, 32):
        cm = jnp.where(rowg >= sh, jnp.maximum(cm, pltpu.roll(cm, sh, 0)), cm)
    m_prev = m_ref[g, 0:1, :]
    m_t = b + jnp.maximum(m_prev, cm)
    inter = jnp.exp(b + m_prev - m_t)
    bm = b - m_t
    a_t = a.T
    m_last = m_t[L - 1:L, :]
    b_last = b[L - 1:L, :]
    ws = jnp.exp(b_last + a - m_last)
    decay = jnp.exp(b_last + m_prev - m_last)
    emt = jnp.exp(-m_t)
    m_ref[g, 0:1, :] = m_last

    ones_h = jnp.ones((L, HEAD), F32)
    ml_out = [None] * N_HEADS

    def ml_head(h):
        qh = proj_ref[rs,C_MLQ + h * HEAD:C_MLQ + (h + 1) * HEAD]
        kh = proj_ref[rs,C_MLK + h * HEAD:C_MLK + (h + 1) * HEAD] * (HEAD ** -0.5)
        vh = proj_ref[rs,C_MLV + h * HEAD:C_MLV + (h + 1) * HEAD]
        qk = _mm_nt(qh, kh)
        yield
        c_aug = c_ref[g, h]
        from_state = _mm(qh, c_aug)
        yield
        w = jnp.where(causal, jnp.exp(bm[:, h:h + 1] + a_t[h:h + 1, :]), 0.0)
        v_aug = jnp.concatenate([vh, ones_h], axis=1)
        nd = _mm(w * qk, v_aug) + inter[:, h:h + 1] * from_state
        yield
        upd = _mm_tn(kh * ws[:, h:h + 1], v_aug)
        yield
        den = jnp.maximum(jnp.abs(nd[:, HEAD:HEAD + 1]), emt[:, h:h + 1])
        ml_out[h] = nd[:, :HEAD] / den
        c_ref[g, h] = decay[:, h:h + 1] * c_aug + upd

    def ml_finish():
        ml = _head_norm(jnp.concatenate(ml_out, axis=1), HEAD)
        mixed_ref[rs,0:MIX_W] = (_sigmoid(proj_ref[rs,C_MLO:C_MLO + MIX_W]) * ml
                                 * mlnw_ref[...]).astype(mixed_ref.dtype)

    lb = hglb_ref[...]
    fg = lb + (1.0 - lb) * _sigmoid(proj_ref[rs,C_HGF:C_HGF + MIX_W])
    hq = _silu(proj_ref[rs,C_HGQ:C_HGQ + MIX_W]) * (HEAD ** -0.5)
    kk = 1.0 - fg
    lfg = jnp.log(fg)
    hv = proj_ref[rs,C_HGI:C_HGI + MIX_W]
    bloc = _mm_sel(triblk_ref[...], lfg)
    blast = _mm_sel(blkones_ref[...], lfg)
    qe = hq * jnp.exp(bloc)
    kw = kk * jnp.exp(blast - bloc)
    bloc_s[g] = bloc
    kk_s[g] = kk
    hv_s[g] = hv
    ind = ind_ref[...]
    rows = lax.broadcasted_iota(jnp.int32, (SUB, MIX_W), 0)
    n_blk = L // SUB
    hg_state_part = [None] * n_blk
    hg_block_part = [None] * n_blk

    def hg_within_block(blk):
        r0 = blk * SUB
        q_blk = hq[r0:r0 + SUB]
        b_blk = bloc[r0:r0 + SUB]
        xs = []
        for s in range(SUB):
            r = r0 + s
            e = jnp.exp(jnp.minimum(b_blk - bloc_s[g, r:r + 1, :], 0.0))
            xs.append((q_blk * (kk_s[g, r:r + 1, :] * e)).astype(BF16))
            if s % HG_SLICE == HG_SLICE - 1:
                yield
        att = jnp.dot(jnp.concatenate(xs, axis=0), ind, preferred_element_type=F32)
        yield
        o_blk = att[0:SUB] * hv_s[g, r0:r0 + 1, :]
        for s in range(1, SUB):
            r = r0 + s
            o_blk = o_blk + jnp.where(rows >= s, att[s * SUB:(s + 1) * SUB], 0.0) * hv_s[g, r:r + 1, :]
            if s % HG_SLICE == HG_SLICE - 1:
                yield
        hg_block_part[blk] = o_blk

    def hg_state_chain():
        st = s_ref[g]
        for blk in range(n_blk):
            r0 = blk * SUB
            hg_state_part[blk] = jnp.concatenate(
                [_mm_nt(qe[r0:r0 + SUB, h * HEAD:(h + 1) * HEAD], st[:, h * HEAD:(h + 1) * HEAD])
                 for h in range(N_HEADS)], axis=1)
            upd = jnp.concatenate(
                [_mm_tn(hv[r0:r0 + SUB, h * HEAD:(h + 1) * HEAD], kw[r0:r0 + SUB, h * HEAD:(h + 1) * HEAD])
                 for h in range(N_HEADS)], axis=1)
            yield
            st = st * jnp.exp(blast[r0:r0 + 1, :]) + upd
        s_ref[g] = st

    def hg_finish():
        o = jnp.concatenate([hg_state_part[blk] + hg_block_part[blk] for blk in range(n_blk)], axis=0)
        mixed_ref[rs,MIX_W:2 * MIX_W] = (_head_norm(o, HEAD) * hgnw_ref[...]
                                         * _silu(proj_ref[rs,C_HGG:C_HGG + MIX_W])).astype(mixed_ref.dtype)

    @functools.lru_cache(maxsize=None)
    def sw_prep():
        cos = rope_ref[:, 0:128]
        sin_a = rope_ref[:, 128:256]
        sin_b = rope_ref[:, 256:384]

        def rope(x):
            return x * cos + pltpu.roll(x, 120, 1) * sin_a + pltpu.roll(x, 8, 1) * sin_b

        q_rot = [rope(proj_ref[rs,C_SWQ:C_SWQ + 128]), rope(proj_ref[rs,C_SWQ + 128:C_SWQ + 256])]
        k_rot = rope(proj_ref[rs,C_SWK:C_SWK + 128])
        v_cur = proj_ref[rs,C_SWV:C_SWV + 128]
        colw = lax.broadcasted_iota(jnp.int32, (2 * L, WINDOW), 1)
        row2 = lax.broadcasted_iota(jnp.int32, (2 * L, 1), 0)
        return q_rot, k_rot, v_cur, k_ref[g], v_ref[g], colw >= WINDOW - n_valid, row2

    sw_prep()
    sw_out = [None] * (2 * SW_KV)

    def sw_group(kv):
        q_rot, k_rot, v_cur, k_prev, v_prev, prev_ok, row2 = sw_prep()
        gs = slice(kv * HEAD, (kv + 1) * HEAD)
        q2 = jnp.concatenate([q_rot[kv][:, 0:HEAD], q_rot[kv][:, HEAD:2 * HEAD]], axis=0) * (HEAD ** -0.5)
        s_p = jnp.where(prev_ok, _mm_nt(q2, k_prev[:, gs]), NEG)
        yield
        s_c = _mm_nt(q2, k_rot[:, gs])
        yield
        sink = jnp.where(row2 < L, sinks_ref[2 * kv], sinks_ref[2 * kv + 1])
        mx = jnp.maximum(jnp.maximum(jnp.max(s_p, axis=-1, keepdims=True),
                                     jnp.max(s_c, axis=-1, keepdims=True)), sink)
        e_p = jnp.exp(s_p - mx)
        e_c = jnp.exp(s_c - mx)
        den = (jnp.sum(e_p, axis=-1, keepdims=True) + jnp.sum(e_c, axis=-1, keepdims=True)
               + jnp.exp(sink - mx))
        o_p = _mm(e_p, v_prev[:, gs])
        yield
        o = (o_p + _mm(e_c, v_cur[:, gs])) / den
        yield
        sw_out[2 * kv] = o[0:L]
        sw_out[2 * kv + 1] = o[L:2 * L]

    def sw_finish():
        _, k_rot, v_cur, k_prev, v_prev, _, _ = sw_prep()
        mixed_ref[rs,2 * MIX_W:3 * MIX_W] = jnp.concatenate(sw_out, axis=1).astype(mixed_ref.dtype)
        k_ref[g, 0:WINDOW - L, :] = k_prev[L:WINDOW]
        k_ref[g, WINDOW - L:WINDOW, :] = k_rot
        v_ref[g, 0:WINDOW - L, :] = v_prev[L:WINDOW]
        v_ref[g, WINDOW - L:WINDOW, :] = v_cur

    @functools.lru_cache(maxsize=None)
    def sd_prep():
        convbuf[g, 0:8, :] = cv_ref[g]
        convbuf[g, 8:8 + L, :] = proj_ref[rs,C_SDX:C_SDX + SSD_CONV_DIM]
        acc = convb_ref[...] + convbuf[g, 5:5 + L, :] * convw_ref[0:1, :]
        for j in range(1, SSD_CONV):
            acc = acc + convbuf[g, 5 + j:5 + j + L, :] * convw_ref[j:j + 1, :]
        xbc = _silu(acc)
        cv_ref[g] = convbuf[g, L:L + 8, :]
        a_neg = -jnp.exp(alog_ref[...])
        ca = _mm_sel(tri, dtv * a_neg)
        ca_last = ca[L - 1:L, :]
        return (xbc, xbc[:, 0:MIX_W], ca, ca.T, dtv.T, jnp.exp(ca), jnp.exp(ca_last - ca) * dtv,
                jnp.exp(ca_last))

    sd_prep()
    sd_out = [None] * N_HEADS

    def sd_group(grp):
        xbc, xs_all, ca, ca_t, dt_t, eca, wss, sdec = sd_prep()
        bm_g = xbc[:, MIX_W + grp * SSD_N:MIX_W + (grp + 1) * SSD_N]
        cm_g = xbc[:, MIX_W + 2 * SSD_N + grp * SSD_N:MIX_W + 2 * SSD_N + (grp + 1) * SSD_N]
        cb = _mm_nt(cm_g, bm_g)
        yield
        for h in (2 * grp, 2 * grp + 1):
            xh = xs_all[:, h * HEAD:(h + 1) * HEAD]
            hst = h_ref[g, h]
            from_state = _mm_nt(cm_g, hst)
            yield
            dec = jnp.where(causal, jnp.exp(ca[:, h:h + 1] - ca_t[h:h + 1, :]), 0.0)
            mmat = cb * dec * dt_t[h:h + 1, :]
            sd_out[h] = _mm(mmat, xh) + eca[:, h:h + 1] * from_state
            yield
            upd = _mm_tn(xh * wss[:, h:h + 1], bm_g)
            yield
            h_ref[g, h] = sdec[:, h:h + 1] * hst + upd

    def sd_finish():
        xs_all = sd_prep()[1]
        y = jnp.concatenate(sd_out, axis=1) + ssdd_ref[...] * xs_all
        gated = y * _silu(proj_ref[rs,C_SDZ:C_SDZ + MIX_W])
        mixed_ref[rs,3 * MIX_W:4 * MIX_W] = (_head_norm(gated, 2 * HEAD)
                                             * ssdnw_ref[...]).astype(mixed_ref.dtype)

    chains = ([[hg_within_block(blk) for blk in range(n_blk)]]
              + [[ml_head(h)] for h in range(N_HEADS)] + [[hg_state_chain()]] + [[sd_group(grp)] for grp in range(2)]
              + [[sw_group(kv)] for kv in range(SW_KV)])

    def finish():
        ml_finish()
        hg_finish()
        sw_finish()
        sd_finish()

    return chains, finish


def _mixer(proj, rope_tab, init, lp, consts, chunks_per_prompt, n_seq):
    n = proj.shape[0]
    rows = SEQ_GROUP * CHUNK
    n_steps = n // rows

    def group_of(i):
        return jnp.where(i < chunks_per_prompt, 0, i - chunks_per_prompt + 1)

    def rope_blk(i):
        return jnp.minimum(i, chunks_per_prompt)

    def per_seq(shape):
        nd = len(shape)
        return pl.BlockSpec((SEQ_GROUP,) + shape, lambda i, s: (group_of(i),) + (0,) * nd)

    def const(shape):
        nd = len(shape)
        return pl.BlockSpec(shape, lambda i, s: (0,) * nd)

    state_shapes = [(N_HEADS, HEAD, 128), (8, 128), (HEAD, MIX_W), (WINDOW, 128), (WINDOW, 128),
                    (N_HEADS, HEAD, SSD_N), (8, SSD_CONV_DIM)]
    in_specs = ([pl.BlockSpec((rows, D_PROJ), lambda i, s: (i, 0)),
                 pl.BlockSpec((CHUNK, 384), lambda i, s: (rope_blk(i), 0))]
                + [per_seq(s) for s in state_shapes]
                + [const((1, 128)), const((1, 128)), const((1, MIX_W)), const((1, MIX_W)), const((1, MIX_W)),
                   const((SSD_CONV, SSD_CONV_DIM)), const((1, SSD_CONV_DIM)), const((1, MIX_W)),
                   const((1, MIX_W)),
                   const((CHUNK, CHUNK)), const((CHUNK, CHUNK)), const((CHUNK, CHUNK)), const((MIX_W, MIX_W))])
    out_specs = ([pl.BlockSpec((rows, D_MODEL), lambda i, s: (i, 0))]
                 + [per_seq(s) for s in state_shapes])
    out_shape = ([jax.ShapeDtypeStruct((n, D_MODEL), BF16)]
                 + [jax.ShapeDtypeStruct((n_seq,) + s, F32) for s in state_shapes])
    grid_spec = pltpu.PrefetchScalarGridSpec(
        num_scalar_prefetch=1, grid=(n_steps,), in_specs=in_specs, out_specs=out_specs,
        scratch_shapes=[pltpu.VMEM((SEQ_GROUP, CHUNK + 8, SSD_CONV_DIM), F32),
                        pltpu.VMEM((SEQ_GROUP, CHUNK, MIX_W), F32), pltpu.VMEM((SEQ_GROUP, CHUNK, MIX_W), F32),
                        pltpu.VMEM((SEQ_GROUP, CHUNK, MIX_W), F32)])
    return pl.pallas_call(
        functools.partial(_mixer_kernel, chunks_per_prompt),
        grid_spec=grid_spec, out_shape=out_shape,
        compiler_params=pltpu.CompilerParams(dimension_semantics=("arbitrary",),
                                             vmem_limit_bytes=VMEM_LIMIT),
        name="mixer",
    )(lp["sinks"], proj, rope_tab, *init,
      lp["gbias"], lp["alog"], lp["mlnw"], lp["hglb"], lp["hgnw"],
      lp["convw"], lp["convb"], lp["ssdd"], lp["ssdnw"], *consts)


def _dense_ffn_kernel(f_chunk, n_parts, bounds, *refs):
    x_refs = refs[:n_parts]
    mix_ref, wo_ref, nw_ref, wg_ref, wu_ref, wd_ref, o_ref = refs[n_parts:]
    x1 = _read_token_tile(x_refs, bounds) + jnp.dot(mix_ref[...], wo_ref[...], preferred_element_type=F32)
    hn = _rms(x1, nw_ref[...]).astype(BF16)
    o_ref[...] = x1
    for f0 in range(0, wg_ref.shape[1], f_chunk):
        g = jnp.dot(hn, wg_ref[:, f0:f0 + f_chunk], preferred_element_type=F32)
        u = jnp.dot(hn, wu_ref[:, f0:f0 + f_chunk], preferred_element_type=F32)
        act = (_silu(g) * u).astype(BF16)
        o_ref[...] += jnp.dot(act, wd_ref[f0:f0 + f_chunk, :], preferred_element_type=F32)


def _dense_ffn(x_parts, mixed, wo, nw, wg, wu, wd):
    n = mixed.shape[0]
    d_ff = wg.shape[1]
    tm = _pick_tile(np.gcd.reduce([_part_rows(p) for p in x_parts]), TOKEN_TILES)
    x_specs, bounds = _token_parts_specs(x_parts, tm)
    full = lambda shape: pl.BlockSpec(shape, lambda i: (0, 0))
    return pl.pallas_call(
        functools.partial(_dense_ffn_kernel, 256, len(x_parts), bounds),
        grid=(n // tm,),
        in_specs=x_specs + [
                  pl.BlockSpec((tm, D_MODEL), lambda i: (i, 0)),
                  full((D_MODEL, D_MODEL)), full((1, D_MODEL)),
                  full((D_MODEL, d_ff)), full((D_MODEL, d_ff)), full((d_ff, D_MODEL))],
        out_specs=pl.BlockSpec((tm, D_MODEL), lambda i: (i, 0)),
        out_shape=jax.ShapeDtypeStruct((n, D_MODEL), F32),
        compiler_params=pltpu.CompilerParams(dimension_semantics=("arbitrary",),
                                             vmem_limit_bytes=VMEM_LIMIT),
        name="dense_ffn",
    )(*x_parts, mixed, wo, nw, wg, wu, wd)


TOK_BLK = 256
FFN_TILE = 1024


def _pack_bf16_pairs(x):
    w = x.shape[1] // 2
    bits = pltpu.bitcast(x.astype(BF16).astype(F32), jnp.int32)
    return lax.shift_right_logical(bits[:, :w], 16) | bits[:, w:]


def _unpack_bf16_pairs(p):
    lo = pltpu.bitcast(lax.shift_left(p, 16), F32)
    hi = pltpu.bitcast(p & jnp.int32(-65536), F32)
    return lo, hi


def _router_kernel(x_ref, mix_ref, wo_ref, nw_ref, wrt_ref, upper_ref,
                   x1_ref, hn_ref, meta_ref, count_ref, carry):
    @pl.when(pl.program_id(0) == 0)
    def _():
        carry[...] = jnp.zeros_like(carry)

    x1 = x_ref[...] + jnp.dot(mix_ref[...], wo_ref[...], preferred_element_type=F32)
    x1_ref[...] = x1
    hn_f = _rms(x1, nw_ref[...])
    hn = hn_f.astype(BF16)
    hn_ref[...] = _pack_bf16_pairs(hn_f)
    logits = lax.dot_general(wrt_ref[...], hn, (((1,), (1,)), ((), ())),
                             preferred_element_type=F32)
    sub = lax.broadcasted_iota(jnp.int32, logits.shape, 0)
    l1 = jnp.where(sub < N_EXPERTS, logits, NEG)
    m1 = jnp.max(l1, axis=0, keepdims=True)
    i1 = jnp.min(jnp.where(l1 == m1, sub, 99), axis=0, keepdims=True)
    l2 = jnp.where(sub == i1, NEG, l1)
    m2 = jnp.max(l2, axis=0, keepdims=True)
    i2 = jnp.min(jnp.where(l2 == m2, sub, 99), axis=0, keepdims=True)
    e2 = jnp.exp(m2 - m1)
    g1 = 1.0 / (1.0 + e2)
    g2 = e2 / (1.0 + e2)
    sel = jnp.where(sub == i1, 1.0, jnp.where(sub == i2, 1.0, 0.0))
    before = jnp.dot(sel.astype(BF16), upper_ref[...], preferred_element_type=F32) + carry[:, 0:1]
    r1 = jnp.sum(jnp.where(sub == i1, before, 0.0), axis=0, keepdims=True)
    r2 = jnp.sum(jnp.where(sub == i2, before, 0.0), axis=0, keepdims=True)
    meta_ref[...] = jnp.concatenate([i1.astype(F32), i2.astype(F32), r1, r2, g1, g2,
                                     jnp.zeros((2, r1.shape[1]), F32)], axis=0)
    total = carry[...] + jnp.sum(sel, axis=1, keepdims=True)
    carry[...] = total
    count_ref[...] = total


def _router(x, mixed, wo, nw, wrt, upper):
    n = x.shape[0]
    tm = TOK_BLK
    full = lambda shape: pl.BlockSpec(shape, lambda i: (0,) * len(shape))
    tok = lambda w: pl.BlockSpec((tm, w), lambda i: (i, 0))
    return pl.pallas_call(
        _router_kernel,
        grid=(n // tm,),
        in_specs=[tok(D_MODEL), tok(D_MODEL), full((D_MODEL, D_MODEL)), full((1, D_MODEL)),
                  full((16, D_MODEL)), full((tm, tm))],
        out_specs=[tok(D_MODEL), tok(D_MODEL // 2),
                   pl.BlockSpec((8, tm), lambda i: (0, i)), full((16, 128))],
        out_shape=[jax.ShapeDtypeStruct((n, D_MODEL), F32), jax.ShapeDtypeStruct((n, D_MODEL // 2), jnp.int32),
                   jax.ShapeDtypeStruct((8, n), F32), jax.ShapeDtypeStruct((16, 128), F32)],
        scratch_shapes=[pltpu.VMEM((16, 128), F32)],
        compiler_params=pltpu.CompilerParams(dimension_semantics=("arbitrary",),
                                             vmem_limit_bytes=VMEM_LIMIT),
        name="router",
    )(x, mixed, wo, nw, wrt, upper)


def _sc_row_gather(table, idx):
    n_workers = SC_CORES * SC_SUBCORES
    b = idx.shape[0]
    d = table.shape[1]
    assert b % (8 * n_workers) == 0
    per_worker = b // n_workers
    chunk = max(c for c in range(8, SC_MAX_GATHER_ROWS + 1, 8) if per_worker % c == 0)
    mesh = plsc.VectorSubcoreMesh(core_axis_name="c", subcore_axis_name="s")

    @functools.partial(
        pl.kernel, mesh=mesh, out_type=jax.ShapeDtypeStruct((b, d), table.dtype),
        scratch_types=[pltpu.VMEM((chunk,), jnp.int32), pltpu.VMEM((chunk, d), table.dtype),
                       pltpu.SemaphoreType.DMA])
    def gather(table_hbm, idx_hbm, out_hbm, idx_v, rows_v, sem):
        worker = lax.axis_index("s") * SC_CORES + lax.axis_index("c")
        base = worker * per_worker

        @pl.loop(0, per_worker // chunk)
        def _(c):
            off = pl.multiple_of(base + c * chunk, 8)
            pltpu.sync_copy(idx_hbm.at[pl.ds(off, chunk)], idx_v)
            pltpu.async_copy(table_hbm.at[idx_v], rows_v, sem).wait()
            pltpu.sync_copy(rows_v, out_hbm.at[pl.ds(off, chunk)])

    return gather(table, idx)


def _sc_row_scatter(rows, idx):
    n_workers = SC_CORES * SC_SUBCORES
    b = n_out = idx.shape[0]
    v, d = rows.shape
    assert b % (8 * n_workers) == 0
    per_worker = b // n_workers
    chunk = max(c for c in range(8, SC_MAX_GATHER_ROWS + 1, 8) if per_worker % c == 0 and v % c == 0)
    mesh = plsc.VectorSubcoreMesh(core_axis_name="c", subcore_axis_name="s")

    @functools.partial(
        pl.kernel, mesh=mesh, out_type=jax.ShapeDtypeStruct((n_out, d), rows.dtype),
        scratch_types=[pltpu.VMEM((chunk,), jnp.int32), pltpu.VMEM((chunk, d), rows.dtype)])
    def scatter(rows_hbm, idx_hbm, out_hbm, idx_v, rows_v):
        worker = lax.axis_index("s") * SC_CORES + lax.axis_index("c")
        base = worker * per_worker

        @pl.loop(0, per_worker // chunk)
        def _(c):
            off = pl.multiple_of(base + c * chunk, 8)
            pltpu.sync_copy(idx_hbm.at[pl.ds(off, chunk)], idx_v)
            pltpu.sync_copy(rows_hbm.at[pl.ds(pl.multiple_of(lax.rem(off, v), 8), chunk)], rows_v)
            pltpu.sync_copy(rows_v, out_hbm.at[idx_v])

    return scatter(rows, idx)


def _expert_ffn_kernel(texp_ref, tvalid_ref, xs_ref, wg_ref, wu_ref, wd_ref, o_ref, acc_ref, xb_ref):
    j = pl.program_id(0)
    f = pl.program_id(1)
    last = f == pl.num_programs(1) - 1
    valid = tvalid_ref[j] == 1
    half = D_MODEL // 2

    @pl.when(jnp.logical_and(valid, f == 0))
    def _():
        lo, hi = _unpack_bf16_pairs(xs_ref[...])
        xb_ref[:, :half] = lo.astype(BF16)
        xb_ref[:, half:] = hi.astype(BF16)

        acc_ref[...] = jnp.zeros_like(acc_ref)

    @pl.when(valid)
    def _():
        xs = xb_ref[...]
        g = jnp.dot(xs, wg_ref[0].astype(BF16), preferred_element_type=F32)
        u = jnp.dot(xs, wu_ref[0].astype(BF16), preferred_element_type=F32)
        act = (_silu(g) * u).astype(BF16)
        acc_ref[...] += jnp.dot(act, wd_ref[0].astype(BF16), preferred_element_type=F32)

        @pl.when(last)
        def _():
            o_ref[...] = _pack_bf16_pairs(acc_ref[...])

    @pl.when(jnp.logical_and(jnp.logical_not(valid), last))
    def _():
        o_ref[...] = jnp.zeros_like(o_ref)


def _expert_ffn(f_exp, f_valid, xs, wg, wu, wd):
    n_slots = xs.shape[0]
    d_ff = wg.shape[2]
    tf = _pick_tile(d_ff, (512, 256, 128))
    grid_spec = pltpu.PrefetchScalarGridSpec(
        num_scalar_prefetch=2, grid=(n_slots // FFN_TILE, d_ff // tf),
        in_specs=[pl.BlockSpec((FFN_TILE, D_MODEL // 2), lambda j, f, te, tv: (j, 0)),
                  pl.BlockSpec((1, D_MODEL, tf), lambda j, f, te, tv: (te[j], 0, f)),
                  pl.BlockSpec((1, D_MODEL, tf), lambda j, f, te, tv: (te[j], 0, f)),
                  pl.BlockSpec((1, tf, D_MODEL), lambda j, f, te, tv: (te[j], f, 0))],
        out_specs=pl.BlockSpec((FFN_TILE, D_MODEL // 2), lambda j, f, te, tv: (j, 0)),
        scratch_shapes=[pltpu.VMEM((FFN_TILE, D_MODEL), F32), pltpu.VMEM((FFN_TILE, D_MODEL), BF16)])
    return pl.pallas_call(
        _expert_ffn_kernel, grid_spec=grid_spec,
        out_shape=jax.ShapeDtypeStruct((n_slots, D_MODEL // 2), jnp.int32),
        compiler_params=pltpu.CompilerParams(dimension_semantics=("arbitrary", "arbitrary"),
                                             vmem_limit_bytes=VMEM_LIMIT),
        name="expert_ffn",
    )(f_exp, f_valid, xs, wg, wu, wd)


def _combine_kernel(part_tiles, x1_ref, o1_ref, o2_ref, gate_ref, fnw_ref, *y_refs):
    i = pl.program_id(0)
    lo1, hi1 = _unpack_bf16_pairs(o1_ref[...])
    lo2, hi2 = _unpack_bf16_pairs(o2_ref[...])
    g1 = gate_ref[:, 0:1]
    g2 = gate_ref[:, 1:2]
    half = D_MODEL // 2
    x1 = x1_ref[...]
    fnw = fnw_ref[...]
    ya = x1[:, :half] + g1 * lo1 + g2 * lo2
    yb = x1[:, half:] + g1 * hi1 + g2 * hi2
    scale = lax.rsqrt((jnp.sum(ya * ya, axis=-1, keepdims=True) + jnp.sum(yb * yb, axis=-1, keepdims=True))
                      / D_MODEL + EPS)
    y = jnp.concatenate([ya * scale * fnw[:, :half], yb * scale * fnw[:, half:]], axis=1)
    lo_tile = 0
    for y_ref, tiles in zip(y_refs, part_tiles):
        @pl.when(jnp.logical_and(i >= lo_tile, i < lo_tile + tiles))
        def _(y_ref=y_ref):
            _store_part_tile(y_ref, y)
        lo_tile += tiles


def _combine(x1, o12, gates, fnw, part_shapes):
    n = x1.shape[0]
    parts = [jax.ShapeDtypeStruct(s, F32) for s in part_shapes]
    tm = _pick_tile(np.gcd.reduce([_part_rows(p) for p in parts]), TOKEN_TILES)
    n_tiles = n // tm
    out_specs, bounds = _token_parts_specs(parts, tm)
    part_tiles = [bounds[k + 1] - bounds[k] for k in range(len(parts))]
    return pl.pallas_call(
        functools.partial(_combine_kernel, part_tiles),
        grid=(n_tiles,),
        in_specs=[pl.BlockSpec((tm, D_MODEL), lambda i: (i, 0)),
                  pl.BlockSpec((tm, D_MODEL // 2), lambda i: (i, 0)),
                  pl.BlockSpec((tm, D_MODEL // 2), lambda i: (i + n_tiles, 0)),
                  pl.BlockSpec((tm, 128), lambda i: (i, 0)),
                  pl.BlockSpec((1, D_MODEL), lambda i: (0, 0))],
        out_specs=out_specs,
        out_shape=parts,
        compiler_params=pltpu.CompilerParams(dimension_semantics=("arbitrary",),
                                             vmem_limit_bytes=VMEM_LIMIT),
        name="combine",
    )(x1, o12, o12, gates, fnw)


def _moe(x, mixed, wo, nw, router_w, wg, wu, wd, fnw, part_shapes):
    i32 = jnp.int32
    n = x.shape[0]
    assert n % TOK_BLK == 0
    wrt = jnp.pad(router_w.T, ((0, 16 - N_EXPERTS), (0, 0))).astype(BF16)
    t = np.arange(TOK_BLK)
    upper = jnp.asarray((t[:, None] < t[None, :]).astype(np.float32), dtype=BF16)
    x1, hn_p, meta, count = _router(x, mixed, wo, nw, wrt, upper)

    n_slots = (-(-2 * n // FFN_TILE) + N_EXPERTS) * FFN_TILE
    counts = count[:N_EXPERTS, 0].astype(i32)
    gsize = (counts + FFN_TILE - 1) // FFN_TILE * FFN_TILE
    gend = jnp.cumsum(gsize)
    goff = gend - gsize
    fstart = jnp.arange(n_slots // FFN_TILE, dtype=i32) * FFN_TILE
    f_valid = (fstart < gend[-1]).astype(i32)
    f_exp = jnp.sum((fstart[:, None] >= gend[None, :]).astype(i32), axis=1)
    f_exp = jnp.minimum(f_exp, jnp.sum((gend[-1] - 1 >= gend).astype(i32)))
    expert_ids = jnp.arange(N_EXPERTS, dtype=i32)[:, None]
    top = meta[0:2].astype(i32)
    rank = meta[2:4].astype(i32)
    slot = jnp.stack([jnp.sum(jnp.where(top[k][None, :] == expert_ids, goff[:, None], 0), axis=0) + rank[k]
                      for k in range(2)]).reshape(-1)
    pad = gsize - counts
    cpad = jnp.cumsum(pad)
    j = jnp.arange(n_slots - 2 * n, dtype=i32)
    pad_e = jnp.sum((j[:, None] >= cpad[None, :]).astype(i32), axis=1)
    pad_base = jnp.sum(jnp.where(pad_e[:, None] == expert_ids.T, (goff + counts - (cpad - pad))[None, :], 0), axis=1)
    pad_slot = jnp.where(pad_e < N_EXPERTS, pad_base + j, gend[-1] + j - cpad[-1])

    xs_p = _sc_row_scatter(hn_p, jnp.concatenate([slot, pad_slot]))
    out_p = _expert_ffn(f_exp, f_valid, xs_p, wg, wu, wd)
    o12 = _sc_row_gather(out_p, slot)
    gates = _pad_lanes(meta[4:6].T, 128)
    return _combine(x1, o12, gates, fnw, part_shapes)


def _rope_table(seq_len, dec_len):
    half = ROPE_DIMS // 2
    pos = jnp.concatenate([jnp.arange(seq_len), PAST_LEN + jnp.arange(dec_len)]).astype(F32)
    inv_freq = jnp.power(jnp.float32(ROPE_THETA), -jnp.arange(half, dtype=F32) / half)
    lane = np.arange(128) % HEAD
    inv_lane = jnp.where(lane < ROPE_DIMS, inv_freq[lane % half], 0.0)
    ang = pos[:, None] * inv_lane[None, :]
    cos, sin = jnp.cos(ang), jnp.sin(ang)
    sin_a = jnp.where(lane < half, -sin, 0.0)
    sin_b = jnp.where((lane >= half) & (lane < ROPE_DIMS), sin, 0.0)
    return jnp.concatenate([cos, sin_a, sin_b], axis=1)


def _mixer_consts():
    t = np.arange(CHUNK)
    tri = (t[:, None] >= t[None, :])
    same = (t[:, None] // SUB) == (t[None, :] // SUB)
    c = np.arange(MIX_W)
    ind = (c[:, None] // HEAD) == (c[None, :] // HEAD)
    as_bf = lambda m: jnp.asarray(m.astype(np.float32), dtype=BF16)
    return [as_bf(tri), as_bf(tri & same), as_bf(same), as_bf(ind)]


def _pad_lanes(v, width):
    return jnp.pad(v, ((0, 0), (0, width - v.shape[1])))


def kernel(x_prompt, x_sample, state_mlstm_C, state_mlstm_n, state_mlstm_m, state_hgrn_S, cache_swa_k, cache_swa_v, state_ssd_h, state_ssd_conv, norm1_w, w_in, ml_ig_b, ml_fg_b, ml_norm_w, hg_lb_logits, hg_norm_w, sw_sinks, ssd_conv_w, ssd_conv_b, ssd_dt_bias, ssd_A_log, ssd_D, ssd_norm_w, w_out, norm2_w, ffn_w_gate, ffn_w_up, ffn_w_down, moe_router, moe_w_gate, moe_w_up, moe_w_down, final_norm_w):
    depth = w_in.shape[0]
    bp, seq_len, _ = x_prompt.shape
    bs, dec_len, _ = x_sample.shape
    assert seq_len % CHUNK == 0 and dec_len == CHUNK and depth % 2 == 0
    n_seq = bp + bs
    chunks_per_prompt = seq_len // CHUNK

    assert bp == SEQ_GROUP and bs % SEQ_GROUP == 0
    x_parts = (x_prompt.reshape(bp, chunks_per_prompt, CHUNK, D_MODEL), x_sample.reshape(bs * dec_len, D_MODEL))
    part_shapes = tuple(p.shape for p in x_parts)
    rope_tab = _rope_table(seq_len, dec_len)
    consts = _mixer_consts()

    sm = jax.nn.softmax(hg_lb_logits.astype(F32), axis=0)
    hg_lb = jnp.cumsum(sm, axis=0) - sm[0]

    def with_prompt_zeros(a):
        return jnp.concatenate([jnp.zeros((bp,) + a.shape[1:], F32), a.astype(F32)], axis=0)

    state_outs = []
    for l in range(depth):
        lp = {
            "sinks": sw_sinks[l].astype(F32),
            "gbias": _pad_lanes(jnp.concatenate([ml_ig_b[l], ml_fg_b[l], ssd_dt_bias[l]])[None, :].astype(F32), 128),
            "alog": _pad_lanes(ssd_A_log[l][None, :].astype(F32), 128),
            "mlnw": ml_norm_w[l][None, :].astype(F32),
            "hglb": hg_lb[l][None, :],
            "hgnw": hg_norm_w[l][None, :].astype(F32),
            "convw": ssd_conv_w[l].astype(F32),
            "convb": ssd_conv_b[l][None, :].astype(F32),
            "ssdd": jnp.repeat(ssd_D[l].astype(F32), HEAD)[None, :],
            "ssdnw": ssd_norm_w[l][None, :].astype(F32),
        }
        c_aug = jnp.concatenate([state_mlstm_C[l], jnp.broadcast_to(state_mlstm_n[l][..., None], state_mlstm_C[l].shape)], axis=-1)
        m_pad = jnp.pad(state_mlstm_m[l][:, None, :], ((0, 0), (0, 7), (0, 128 - N_HEADS)))
        s_t = jnp.swapaxes(state_hgrn_S[l], -1, -2)
        s_t = jnp.moveaxis(s_t, 1, 2).reshape(bs, HEAD, MIX_W)
        cv_pad = jnp.pad(state_ssd_conv[l], ((0, 0), (8 - (SSD_CONV - 1), 0), (0, 0)))
        init = [with_prompt_zeros(a) for a in (
            c_aug, m_pad, s_t, cache_swa_k[l].reshape(bs, WINDOW, 128), cache_swa_v[l].reshape(bs, WINDOW, 128),
            state_ssd_h[l], cv_pad)]

        proj = _inproj(x_parts, norm1_w[l][None, :], w_in[l])
        mixed, c_o, m_o, s_o, k_o, v_o, h_o, cv_o = _mixer(
            proj, rope_tab, init, lp, consts, chunks_per_prompt, n_seq)
        wo = w_out[l].astype(BF16)
        j = l // 2
        if l % 2 == 0:
            x_parts = (_dense_ffn(x_parts, mixed, wo, norm2_w[l][None, :], ffn_w_gate[j].astype(BF16),
                                  ffn_w_up[j].astype(BF16), ffn_w_down[j].astype(BF16)),)
        else:
            assert l == depth - 1 and len(x_parts) == 1
            y_parts = _moe(x_parts[0], mixed, wo, norm2_w[l][None, :], moe_router[j], moe_w_gate[j],
                           moe_w_up[j], moe_w_down[j], final_norm_w[None, :], part_shapes)
        s_back = jnp.moveaxis(s_o.reshape(n_seq, HEAD, N_HEADS, HEAD), 2, 1)
        state_outs.append((c_o[..., :HEAD], c_o[:, :, :, HEAD], m_o[:, 0, :N_HEADS],
                           jnp.swapaxes(s_back, -1, -2),
                           k_o.reshape(n_seq, WINDOW, SW_KV, HEAD), v_o.reshape(n_seq, WINDOW, SW_KV, HEAD),
                           h_o, cv_o[:, 8 - (SSD_CONV - 1):, :]))

    y_prompt = y_parts[0].reshape(bp, seq_len, D_MODEL)
    y_sample = y_parts[1].reshape(bs, dec_len, D_MODEL)
    stacked = [jnp.stack([so[k] for so in state_outs]) for k in range(8)]
    return (y_prompt, y_sample) + tuple(s[:, :bp] for s in stacked) + tuple(s[:, bp:] for s in stacked)
```

```python
import functools

import numpy as np
import jax
import jax.numpy as jnp
from jax import lax
from jax.experimental import pallas as pl
from jax.experimental.pallas import tpu as pltpu
from jax.experimental.pallas import tpu_sc as plsc

F32 = jnp.float32
BF16 = jnp.bfloat16

D_MODEL = 1024
CHUNK = 64
EPS = 1e-6
N_HEADS = 4
HEAD = 64
MIX_W = N_HEADS * HEAD
SW_KV = 2
WINDOW = 128
ROPE_DIMS = 16
ROPE_THETA = 500000.0
SSD_N = 128
SSD_CONV = 4
SSD_CONV_DIM = 768
PAST_LEN = 4096
N_EXPERTS = 8
SUB = 16
SEQ_GROUP = 2
HG_SLICE = 4
TOKEN_TILES = (512, 256, 128)

C_MLQ, C_MLK, C_MLV, C_MLO = 0, 256, 512, 768
C_HGQ, C_HGF, C_HGI, C_HGG = 1024, 1280, 1536, 1792
C_SWQ, C_SWK, C_SWV = 2048, 2304, 2432
C_SDZ, C_SDX, C_GATE = 2560, 2816, 3584
D_PROJ = 3712
NEG = -1e30
VMEM_LIMIT = 56 * 1024 * 1024
SC_CORES = 2
SC_SUBCORES = 16
SC_MAX_GATHER_ROWS = 128


def _mm(a, b):
    return jnp.dot(a.astype(BF16), b.astype(BF16), preferred_element_type=F32)


def _mm_nt(a, b):
    return lax.dot_general(a.astype(BF16), b.astype(BF16), (((1,), (1,)), ((), ())),
                           preferred_element_type=F32)


def _mm_tn(a, b):
    return jnp.dot(a.T.astype(BF16), b.astype(BF16), preferred_element_type=F32)


def _mm_sel(sel, x):
    hi = x.astype(BF16)
    r1 = x - hi.astype(F32)
    mid = r1.astype(BF16)
    lo = (r1 - mid.astype(F32)).astype(BF16)
    return (jnp.dot(sel, hi, preferred_element_type=F32)
            + jnp.dot(sel, mid, preferred_element_type=F32)
            + jnp.dot(sel, lo, preferred_element_type=F32))


def _sigmoid(x):
    return 1.0 / (1.0 + jnp.exp(-x))


def _silu(x):
    return x * _sigmoid(x)


def _softplus(x):
    return jnp.maximum(x, 0.0) + jnp.log(1.0 + jnp.exp(-jnp.abs(x)))


def _rms(x, w):
    return x * lax.rsqrt(jnp.mean(x * x, axis=-1, keepdims=True) + EPS) * w


def _pick_tile(n, candidates):
    for c in candidates:
        if n % c == 0:
            return c
    raise ValueError(f"no tile for {n}")


def _token_parts_specs(parts, tm):
    bounds = np.cumsum([0] + [_part_rows(p) // tm for p in parts])
    specs = []
    for k, p in enumerate(parts):
        assert _part_rows(p) % tm == 0
        lo, hi = int(bounds[k]), int(bounds[k + 1])
        specs.append(_part_block_spec(p, tm, lo, hi - lo))
    return specs, [int(b) for b in bounds]


def _part_rows(p):
    return int(np.prod(p.shape[:-1]))


def _part_block_spec(p, tm, first_tile, n_tiles):
    idx = lambda i: jnp.clip(i - first_tile, 0, n_tiles - 1)
    if len(p.shape) == 2:
        return pl.BlockSpec((tm, p.shape[1]), lambda i: (idx(i), 0))
    grp, _, chunk, d = p.shape
    assert tm % (grp * chunk) == 0
    return pl.BlockSpec((grp, tm // (grp * chunk), chunk, d), lambda i: (0, idx(i), 0, 0))


def _load_part_tile(ref):
    if len(ref.shape) == 2:
        return ref[...]
    grp, n_chunks = ref.shape[:2]
    return jnp.concatenate([ref[s, c] for c in range(n_chunks) for s in range(grp)], axis=0)


def _store_part_tile(ref, x):
    if len(ref.shape) == 2:
        ref[...] = x
        return
    grp, n_chunks, chunk = ref.shape[:3]
    for c in range(n_chunks):
        for s in range(grp):
            r0 = (c * grp + s) * chunk
            ref[s, c] = x[r0:r0 + chunk]


def _read_token_tile(refs, bounds):
    i = pl.program_id(0)
    x = _load_part_tile(refs[-1])
    for k in range(len(refs) - 2, -1, -1):
        x = jnp.where(i < bounds[k + 1], _load_part_tile(refs[k]), x)
    return x


def _inproj_kernel(n_parts, bounds, *refs):
    x_refs = refs[:n_parts]
    nw_ref, w_ref, o_ref, wp_ref = refs[n_parts:]

    @pl.when(pl.program_id(0) == 0)
    def _():
        rows = 128
        for r0 in range(0, D_MODEL, rows):
            w = w_ref[r0:r0 + rows, :]
            wp_ref[r0:r0 + rows, 0:C_MLO] = w[:, 0:768].astype(BF16)
            wp_ref[r0:r0 + rows, C_MLO:C_GATE] = w[:, 776:3592].astype(BF16)
            gates = jnp.concatenate([w[:, 768:776], w[:, 3592:3596],
                                     jnp.zeros((rows, D_PROJ - C_GATE - 12), F32)], axis=1)
            wp_ref[r0:r0 + rows, C_GATE:D_PROJ] = gates.astype(BF16)

    xn = _rms(_read_token_tile(x_refs, bounds), nw_ref[...]).astype(BF16)
    o_ref[...] = jnp.dot(xn, wp_ref[...], preferred_element_type=F32)


def _inproj(x_parts, nw, w):
    n = sum(_part_rows(p) for p in x_parts)
    tm = _pick_tile(np.gcd.reduce([_part_rows(p) for p in x_parts]), TOKEN_TILES)
    x_specs, bounds = _token_parts_specs(x_parts, tm)
    d_in = w.shape[1]
    return pl.pallas_call(
        functools.partial(_inproj_kernel, len(x_parts), bounds),
        grid=(n // tm,),
        in_specs=x_specs + [pl.BlockSpec((1, D_MODEL), lambda i: (0, 0)),
                            pl.BlockSpec((D_MODEL, d_in), lambda i: (0, 0), pipeline_mode=pl.Buffered(1))],
        out_specs=pl.BlockSpec((tm, D_PROJ), lambda i: (i, 0)),
        out_shape=jax.ShapeDtypeStruct((n, D_PROJ), F32),
        scratch_shapes=[pltpu.VMEM((D_MODEL, D_PROJ), BF16)],
        compiler_params=pltpu.CompilerParams(dimension_semantics=("arbitrary",),
                                             vmem_limit_bytes=VMEM_LIMIT),
        name="inproj",
    )(*x_parts, nw, w)


def _head_norm(x, width):
    parts = []
    for g in range(x.shape[1] // width):
        xg = x[:, g * width:(g + 1) * width]
        parts.append(xg * lax.rsqrt(jnp.mean(xg * xg, axis=-1, keepdims=True) + EPS))
    return jnp.concatenate(parts, axis=1)


def _mixer_kernel(prompt_steps,
                  sinks_ref,
                  proj_ref, rope_ref, c0_ref, m0_ref, s0_ref, k0_ref, v0_ref, h0_ref, cv0_ref,
                  gbias_ref, alog_ref, mlnw_ref, hglb_ref, hgnw_ref,
                  convw_ref, convb_ref, ssdd_ref, ssdnw_ref,
                  tri_ref, triblk_ref, blkones_ref, ind_ref,
                  mixed_ref, c_ref, m_ref, s_ref, k_ref, v_ref, h_ref, cv_ref,
                  convbuf, bloc_s, kk_s, hv_s):
    i = pl.program_id(0)
    is_prompt = i < prompt_steps
    is_first = jnp.logical_or(jnp.logical_not(is_prompt), i == 0)
    n_valid = jnp.where(is_prompt, jnp.minimum(i * CHUNK, WINDOW), WINDOW)

    @pl.when(is_first)
    def _():
        c_ref[...] = c0_ref[...]
        m_ref[...] = m0_ref[...]
        s_ref[...] = s0_ref[...]
        k_ref[...] = k0_ref[...]
        v_ref[...] = v0_ref[...]
        h_ref[...] = h0_ref[...]
        cv_ref[...] = cv0_ref[...]

    members = [_mixer_chunk(g, n_valid, sinks_ref, proj_ref, rope_ref,
                            gbias_ref, alog_ref, mlnw_ref, hglb_ref, hgnw_ref,
                            convw_ref, convb_ref, ssdd_ref, ssdnw_ref,
                            tri_ref, triblk_ref, blkones_ref, ind_ref,
                            mixed_ref, c_ref, m_ref, s_ref, k_ref, v_ref, h_ref, cv_ref,
                            convbuf, bloc_s, kk_s, hv_s) for g in range(SEQ_GROUP)]
    for phase in zip(*[m[0] for m in members]):
        chains = [c for per_member in zip(*phase) for c in per_member]
        while chains:
            alive = []
            for chain in chains:
                if next(chain, _CHAIN_DONE) is not _CHAIN_DONE:
                    alive.append(chain)
            chains = alive
    for _, finish in members:
        finish()


_CHAIN_DONE = object()


def _mixer_chunk(g, n_valid, sinks_ref, proj_ref, rope_ref,
                 gbias_ref, alog_ref, mlnw_ref, hglb_ref, hgnw_ref,
                 convw_ref, convb_ref, ssdd_ref, ssdnw_ref,
                 tri_ref, triblk_ref, blkones_ref, ind_ref,
                 mixed_ref, c_ref, m_ref, s_ref, k_ref, v_ref, h_ref, cv_ref,
                 convbuf, bloc_s, kk_s, hv_s):
    rs = slice(g * CHUNK, (g + 1) * CHUNK)
    L = CHUNK
    row = lax.broadcasted_iota(jnp.int32, (L, L), 0)
    col = lax.broadcasted_iota(jnp.int32, (L, L), 1)
    causal = row >= col
    tri = tri_ref[...]

    gate = proj_ref[rs,C_GATE:C_GATE + 128] + gbias_ref[...]
    ig = gate
    lf = pltpu.roll(-_softplus(-gate), 124, 1)
    dtv = pltpu.roll(_softplus(gate), 120, 1)

    b = _mm_sel(tri, lf)
    a = ig - b
    cm = a
    rowg = lax.broadcasted_iota(jnp.int32, (L, 128), 0)
    for sh in (1, 2, 4, 8, 16, 32):
        cm = jnp.where(rowg >= sh, jnp.maximum(cm, pltpu.roll(cm, sh, 0)), cm)
    m_prev = m_ref[g, 0:1, :]
    m_t = b + jnp.maximum(m_prev, cm)
    inter = jnp.exp(b + m_prev - m_t)
    bm = b - m_t
    a_t = a.T
    m_last = m_t[L - 1:L, :]
    b_last = b[L - 1:L, :]
    ws = jnp.exp(b_last + a - m_last)
    decay = jnp.exp(b_last + m_prev - m_last)
    emt = jnp.exp(-m_t)
    m_ref[g, 0:1, :] = m_last

    lane2 = lax.broadcasted_iota(jnp.int32, (L, 2 * HEAD), 1)
    low = lane2 < HEAD
    causal2 = lax.broadcasted_iota(jnp.int32, (L, 2 * HEAD), 0) >= (lane2 & (HEAD - 1))
    ones_blk = ind_ref[0:2 * HEAD, 0:2 * HEAD]
    ones_rows = jnp.ones((L, 2 * HEAD), BF16)
    st_row = lax.broadcasted_iota(jnp.int32, (2 * HEAD, 4 * HEAD), 0) // HEAD
    st_lane = (lax.broadcasted_iota(jnp.int32, (2 * HEAD, 4 * HEAD), 1) // HEAD) % 2
    ml_out = [None] * (N_HEADS // 2)

    def pair_cols(x, p):
        return jnp.where(low, x[:, 2 * p:2 * p + 1], x[:, 2 * p + 1:2 * p + 2])

    def split_rows(x):
        return jnp.concatenate([jnp.where(low, x, 0.0), jnp.where(low, 0.0, x)], axis=0)

    def ml_pair(p):
        cs = slice(2 * p * HEAD, 2 * (p + 1) * HEAD)
        q = proj_ref[rs, C_MLQ + cs.start:C_MLQ + cs.stop]
        k = proj_ref[rs, C_MLK + cs.start:C_MLK + cs.stop] * (HEAD ** -0.5)
        v = proj_ref[rs, C_MLV + cs.start:C_MLV + cs.stop]
        qk = _mm_nt(q, split_rows(k))
        yield
        st = c_ref[g, p]
        from_state = _mm(q, st)
        yield
        a_row = jnp.concatenate([a_t[2 * p:2 * p + 1, :], a_t[2 * p + 1:2 * p + 2, :]], axis=1)
        w = jnp.where(causal2, jnp.exp(pair_cols(bm, p) + a_row), 0.0)
        rhs = jnp.concatenate([split_rows(v).astype(BF16), ones_blk], axis=1)
        gate_in = pair_cols(inter, p)
        nd = (jnp.dot((w * qk).astype(BF16), rhs, preferred_element_type=F32)
              + jnp.concatenate([gate_in, gate_in], axis=1) * from_state)
        yield
        kws = k * pair_cols(ws, p)
        upd = jnp.dot(kws.T.astype(BF16), jnp.concatenate([v.astype(BF16), ones_rows], axis=1),
                      preferred_element_type=F32)
        yield
        den = jnp.maximum(jnp.abs(nd[:, 2 * HEAD:]), pair_cols(emt, p))
        ml_out[p] = nd[:, :2 * HEAD] / den
        dec = jnp.where(st_row == 0, decay[:, 2 * p:2 * p + 1], decay[:, 2 * p + 1:2 * p + 2])
        c_ref[g, p] = dec * st + jnp.where(st_row == st_lane, upd, 0.0)

    def ml_finish():
        ml = _head_norm(jnp.concatenate(ml_out, axis=1), HEAD)
        mixed_ref[rs,0:MIX_W] = (_sigmoid(proj_ref[rs,C_MLO:C_MLO + MIX_W]) * ml
                                 * mlnw_ref[...]).astype(mixed_ref.dtype)

    lb = hglb_ref[...]
    fg = lb + (1.0 - lb) * _sigmoid(proj_ref[rs,C_HGF:C_HGF + MIX_W])
    hq = _silu(proj_ref[rs,C_HGQ:C_HGQ + MIX_W]) * (HEAD ** -0.5)
    kk = 1.0 - fg
    lfg = jnp.log(fg)
    hv = proj_ref[rs,C_HGI:C_HGI + MIX_W]
    bloc = _mm_sel(triblk_ref[...], lfg)
    blast = _mm_sel(blkones_ref[...], lfg)
    qe = hq * jnp.exp(bloc)
    kw = kk * jnp.exp(blast - bloc)
    bloc_s[g] = bloc
    kk_s[g] = kk
    hv_s[g] = hv
    ind = ind_ref[...]
    rows = lax.broadcasted_iota(jnp.int32, (SUB, MIX_W), 0)
    n_blk = L // SUB
    hg_state_part = [None] * n_blk
    hg_block_part = [None] * n_blk

    def hg_within_block(blk):
        r0 = blk * SUB
        q_blk = hq[r0:r0 + SUB]
        b_blk = bloc[r0:r0 + SUB]
        xs = []
        for s in range(SUB):
            r = r0 + s
            e = jnp.exp(jnp.minimum(b_blk - bloc_s[g, r:r + 1, :], 0.0))
            xs.append((q_blk * (kk_s[g, r:r + 1, :] * e)).astype(BF16))
            if s % HG_SLICE == HG_SLICE - 1:
                yield
        att = jnp.dot(jnp.concatenate(xs, axis=0), ind, preferred_element_type=F32)
        yield
        o_blk = att[0:SUB] * hv_s[g, r0:r0 + 1, :]
        for s in range(1, SUB):
            r = r0 + s
            o_blk = o_blk + jnp.where(rows >= s, att[s * SUB:(s + 1) * SUB], 0.0) * hv_s[g, r:r + 1, :]
            if s % HG_SLICE == HG_SLICE - 1:
                yield
        hg_block_part[blk] = o_blk

    def hg_state_chain():
        st = s_ref[g]
        for blk in range(n_blk):
            r0 = blk * SUB
            hg_state_part[blk] = jnp.concatenate(
                [_mm_nt(qe[r0:r0 + SUB, h * HEAD:(h + 1) * HEAD], st[:, h * HEAD:(h + 1) * HEAD])
                 for h in range(N_HEADS)], axis=1)
            upd = jnp.concatenate(
                [_mm_tn(hv[r0:r0 + SUB, h * HEAD:(h + 1) * HEAD], kw[r0:r0 + SUB, h * HEAD:(h + 1) * HEAD])
                 for h in range(N_HEADS)], axis=1)
            yield
            st = st * jnp.exp(blast[r0:r0 + 1, :]) + upd
        s_ref[g] = st

    def hg_finish():
        o = jnp.concatenate([hg_state_part[blk] + hg_block_part[blk] for blk in range(n_blk)], axis=0)
        mixed_ref[rs,MIX_W:2 * MIX_W] = (_head_norm(o, HEAD) * hgnw_ref[...]
                                         * _silu(proj_ref[rs,C_HGG:C_HGG + MIX_W])).astype(mixed_ref.dtype)

    @functools.lru_cache(maxsize=None)
    def sw_prep():
        cos = rope_ref[:, 0:128]
        sin_a = rope_ref[:, 128:256]
        sin_b = rope_ref[:, 256:384]

        def rope(x):
            return x * cos + pltpu.roll(x, 120, 1) * sin_a + pltpu.roll(x, 8, 1) * sin_b

        q_rot = [rope(proj_ref[rs,C_SWQ:C_SWQ + 128]), rope(proj_ref[rs,C_SWQ + 128:C_SWQ + 256])]
        k_rot = rope(proj_ref[rs,C_SWK:C_SWK + 128])
        v_cur = proj_ref[rs,C_SWV:C_SWV + 128]
        colw = lax.broadcasted_iota(jnp.int32, (2 * L, WINDOW), 1)
        row2 = lax.broadcasted_iota(jnp.int32, (2 * L, 1), 0)
        return q_rot, k_rot, v_cur, k_ref[g], v_ref[g], colw >= WINDOW - n_valid, row2

    sw_prep()
    sw_out = [None] * (2 * SW_KV)

    def sw_group(kv):
        q_rot, k_rot, v_cur, k_prev, v_prev, prev_ok, row2 = sw_prep()
        gs = slice(kv * HEAD, (kv + 1) * HEAD)
        q2 = jnp.concatenate([q_rot[kv][:, 0:HEAD], q_rot[kv][:, HEAD:2 * HEAD]], axis=0) * (HEAD ** -0.5)
        s_p = jnp.where(prev_ok, _mm_nt(q2, k_prev[:, gs]), NEG)
        yield
        s_c = _mm_nt(q2, k_rot[:, gs])
        yield
        sink = jnp.where(row2 < L, sinks_ref[2 * kv], sinks_ref[2 * kv + 1])
        mx = jnp.maximum(jnp.maximum(jnp.max(s_p, axis=-1, keepdims=True),
                                     jnp.max(s_c, axis=-1, keepdims=True)), sink)
        e_p = jnp.exp(s_p - mx)
        e_c = jnp.exp(s_c - mx)
        den = (jnp.sum(e_p, axis=-1, keepdims=True) + jnp.sum(e_c, axis=-1, keepdims=True)
               + jnp.exp(sink - mx))
        o_p = _mm(e_p, v_prev[:, gs])
        yield
        o = (o_p + _mm(e_c, v_cur[:, gs])) / den
        yield
        sw_out[2 * kv] = o[0:L]
        sw_out[2 * kv + 1] = o[L:2 * L]

    def sw_finish():
        _, k_rot, v_cur, k_prev, v_prev, _, _ = sw_prep()
        mixed_ref[rs,2 * MIX_W:3 * MIX_W] = jnp.concatenate(sw_out, axis=1).astype(mixed_ref.dtype)
        k_ref[g, 0:WINDOW - L, :] = k_prev[L:WINDOW]
        k_ref[g, WINDOW - L:WINDOW, :] = k_rot
        v_ref[g, 0:WINDOW - L, :] = v_prev[L:WINDOW]
        v_ref[g, WINDOW - L:WINDOW, :] = v_cur

    @functools.lru_cache(maxsize=None)
    def sd_prep():
        convbuf[g, 0:8, :] = cv_ref[g]
        convbuf[g, 8:8 + L, :] = proj_ref[rs,C_SDX:C_SDX + SSD_CONV_DIM]
        acc = convb_ref[...] + convbuf[g, 5:5 + L, :] * convw_ref[0:1, :]
        for j in range(1, SSD_CONV):
            acc = acc + convbuf[g, 5 + j:5 + j + L, :] * convw_ref[j:j + 1, :]
        xbc = _silu(acc)
        cv_ref[g] = convbuf[g, L:L + 8, :]
        a_neg = -jnp.exp(alog_ref[...])
        ca = _mm_sel(tri, dtv * a_neg)
        ca_last = ca[L - 1:L, :]
        return (xbc, xbc[:, 0:MIX_W], ca, ca.T, dtv.T, jnp.exp(ca), jnp.exp(ca_last - ca) * dtv,
                jnp.exp(ca_last))

    sd_prep()
    sd_out = [None] * (N_HEADS // 2)

    def sd_group(grp):
        xbc, xs_all, ca, ca_t, dt_t, eca, wss, sdec = sd_prep()
        bm_g = xbc[:, MIX_W + grp * SSD_N:MIX_W + (grp + 1) * SSD_N]
        cm_g = xbc[:, MIX_W + 2 * SSD_N + grp * SSD_N:MIX_W + 2 * SSD_N + (grp + 1) * SSD_N]
        e, o = 2 * grp, 2 * grp + 1
        x = xs_all[:, e * HEAD:(o + 1) * HEAD]
        cb2 = _mm_nt(cm_g, jnp.concatenate([bm_g, bm_g], axis=0))
        yield
        hst = h_ref[g, e:o + 1].reshape(2 * HEAD, SSD_N)
        from_state = _mm_nt(cm_g, hst)
        yield
        ca_row = jnp.concatenate([ca_t[e:e + 1, :], ca_t[o:o + 1, :]], axis=1)
        dt_row = jnp.concatenate([dt_t[e:e + 1, :], dt_t[o:o + 1, :]], axis=1)
        dec = jnp.where(causal2, jnp.exp(pair_cols(ca, grp) - ca_row), 0.0)
        sd_out[grp] = _mm(cb2 * dec * dt_row, split_rows(x)) + pair_cols(eca, grp) * from_state
        yield
        upd = _mm_tn(x * pair_cols(wss, grp), bm_g)
        yield
        keep = jnp.where(lax.broadcasted_iota(jnp.int32, (2 * HEAD, SSD_N), 0) < HEAD,
                         sdec[:, e:e + 1], sdec[:, o:o + 1])
        h_ref[g, e:o + 1] = (keep * hst + upd).reshape(2, HEAD, SSD_N)

    def sd_finish():
        xs_all = sd_prep()[1]
        y = jnp.concatenate(sd_out, axis=1) + ssdd_ref[...] * xs_all
        gated = y * _silu(proj_ref[rs,C_SDZ:C_SDZ + MIX_W])
        mixed_ref[rs,3 * MIX_W:4 * MIX_W] = (_head_norm(gated, 2 * HEAD)
                                             * ssdnw_ref[...]).astype(mixed_ref.dtype)

    chains = ([[hg_within_block(blk) for blk in range(n_blk)]]
              + [[ml_pair(p)] for p in range(N_HEADS // 2)] + [[hg_state_chain(), sd_group(0)]]
              + [[sd_group(1)]]
              + [[sw_group(kv)] for kv in range(SW_KV)])

    def finish():
        ml_finish()
        hg_finish()
        sw_finish()
        sd_finish()

    return chains, finish


def _mixer(proj, rope_tab, init, lp, consts, chunks_per_prompt, n_seq):
    n = proj.shape[0]
    rows = SEQ_GROUP * CHUNK
    n_steps = n // rows

    def group_of(i):
        return jnp.where(i < chunks_per_prompt, 0, i - chunks_per_prompt + 1)

    def rope_blk(i):
        return jnp.minimum(i, chunks_per_prompt)

    def per_seq(shape):
        nd = len(shape)
        return pl.BlockSpec((SEQ_GROUP,) + shape, lambda i, s: (group_of(i),) + (0,) * nd)

    def const(shape):
        nd = len(shape)
        return pl.BlockSpec(shape, lambda i, s: (0,) * nd)

    state_shapes = [(N_HEADS // 2, 2 * HEAD, 4 * HEAD), (8, 128), (HEAD, MIX_W), (WINDOW, 128), (WINDOW, 128),
                    (N_HEADS, HEAD, SSD_N), (8, SSD_CONV_DIM)]
    in_specs = ([pl.BlockSpec((rows, D_PROJ), lambda i, s: (i, 0)),
                 pl.BlockSpec((CHUNK, 384), lambda i, s: (rope_blk(i), 0))]
                + [per_seq(s) for s in state_shapes]
                + [const((1, 128)), const((1, 128)), const((1, MIX_W)), const((1, MIX_W)), const((1, MIX_W)),
                   const((SSD_CONV, SSD_CONV_DIM)), const((1, SSD_CONV_DIM)), const((1, MIX_W)),
                   const((1, MIX_W)),
                   const((CHUNK, CHUNK)), const((CHUNK, CHUNK)), const((CHUNK, CHUNK)), const((MIX_W, MIX_W))])
    out_specs = ([pl.BlockSpec((rows, D_MODEL), lambda i, s: (i, 0))]
                 + [per_seq(s) for s in state_shapes])
    out_shape = ([jax.ShapeDtypeStruct((n, D_MODEL), BF16)]
                 + [jax.ShapeDtypeStruct((n_seq,) + s, F32) for s in state_shapes])
    grid_spec = pltpu.PrefetchScalarGridSpec(
        num_scalar_prefetch=1, grid=(n_steps,), in_specs=in_specs, out_specs=out_specs,
        scratch_shapes=[pltpu.VMEM((SEQ_GROUP, CHUNK + 8, SSD_CONV_DIM), F32),
                        pltpu.VMEM((SEQ_GROUP, CHUNK, MIX_W), F32), pltpu.VMEM((SEQ_GROUP, CHUNK, MIX_W), F32),
                        pltpu.VMEM((SEQ_GROUP, CHUNK, MIX_W), F32)])
    return pl.pallas_call(
        functools.partial(_mixer_kernel, chunks_per_prompt),
        grid_spec=grid_spec, out_shape=out_shape,
        compiler_params=pltpu.CompilerParams(dimension_semantics=("arbitrary",),
                                             vmem_limit_bytes=VMEM_LIMIT),
        name="mixer",
    )(lp["sinks"], proj, rope_tab, *init,
      lp["gbias"], lp["alog"], lp["mlnw"], lp["hglb"], lp["hgnw"],
      lp["convw"], lp["convb"], lp["ssdd"], lp["ssdnw"], *consts)


def _dense_ffn_kernel(f_chunk, n_parts, bounds, *refs):
    x_refs = refs[:n_parts]
    mix_ref, wo_ref, nw_ref, wg_ref, wu_ref, wd_ref, o_ref = refs[n_parts:]
    x1 = _read_token_tile(x_refs, bounds) + jnp.dot(mix_ref[...], wo_ref[...], preferred_element_type=F32)
    hn = _rms(x1, nw_ref[...]).astype(BF16)
    o_ref[...] = x1
    for f0 in range(0, wg_ref.shape[1], f_chunk):
        g = jnp.dot(hn, wg_ref[:, f0:f0 + f_chunk], preferred_element_type=F32)
        u = jnp.dot(hn, wu_ref[:, f0:f0 + f_chunk], preferred_element_type=F32)
        act = (_silu(g) * u).astype(BF16)
        o_ref[...] += jnp.dot(act, wd_ref[f0:f0 + f_chunk, :], preferred_element_type=F32)


def _dense_ffn(x_parts, mixed, wo, nw, wg, wu, wd):
    n = mixed.shape[0]
    d_ff = wg.shape[1]
    tm = _pick_tile(np.gcd.reduce([_part_rows(p) for p in x_parts]), TOKEN_TILES)
    x_specs, bounds = _token_parts_specs(x_parts, tm)
    full = lambda shape: pl.BlockSpec(shape, lambda i: (0, 0))
    return pl.pallas_call(
        functools.partial(_dense_ffn_kernel, 256, len(x_parts), bounds),
        grid=(n // tm,),
        in_specs=x_specs + [
                  pl.BlockSpec((tm, D_MODEL), lambda i: (i, 0)),
                  full((D_MODEL, D_MODEL)), full((1, D_MODEL)),
                  full((D_MODEL, d_ff)), full((D_MODEL, d_ff)), full((d_ff, D_MODEL))],
        out_specs=pl.BlockSpec((tm, D_MODEL), lambda i: (i, 0)),
        out_shape=jax.ShapeDtypeStruct((n, D_MODEL), F32),
        compiler_params=pltpu.CompilerParams(dimension_semantics=("arbitrary",),
                                             vmem_limit_bytes=VMEM_LIMIT),
        name="dense_ffn",
    )(*x_parts, mixed, wo, nw, wg, wu, wd)


TOK_BLK = 256
FFN_TILE = 1024


def _pack_bf16_pairs(x):
    w = x.shape[1] // 2
    bits = pltpu.bitcast(x.astype(BF16).astype(F32), jnp.int32)
    return lax.shift_right_logical(bits[:, :w], 16) | bits[:, w:]


def _unpack_bf16_pairs(p):
    lo = pltpu.bitcast(lax.shift_left(p, 16), F32)
    hi = pltpu.bitcast(p & jnp.int32(-65536), F32)
    return lo, hi


def _router_kernel(x_ref, mix_ref, wo_ref, nw_ref, wrt_ref, upper_ref,
                   x1_ref, hn_ref, meta_ref, count_ref, carry):
    @pl.when(pl.program_id(0) == 0)
    def _():
        carry[...] = jnp.zeros_like(carry)

    x1 = x_ref[...] + jnp.dot(mix_ref[...], wo_ref[...], preferred_element_type=F32)
    x1_ref[...] = x1
    hn_f = _rms(x1, nw_ref[...])
    hn = hn_f.astype(BF16)
    hn_ref[...] = _pack_bf16_pairs(hn_f)
    logits = lax.dot_general(wrt_ref[...], hn, (((1,), (1,)), ((), ())),
                             preferred_element_type=F32)
    sub = lax.broadcasted_iota(jnp.int32, logits.shape, 0)
    l1 = jnp.where(sub < N_EXPERTS, logits, NEG)
    m1 = jnp.max(l1, axis=0, keepdims=True)
    i1 = jnp.min(jnp.where(l1 == m1, sub, 99), axis=0, keepdims=True)
    l2 = jnp.where(sub == i1, NEG, l1)
    m2 = jnp.max(l2, axis=0, keepdims=True)
    i2 = jnp.min(jnp.where(l2 == m2, sub, 99), axis=0, keepdims=True)
    e2 = jnp.exp(m2 - m1)
    g1 = 1.0 / (1.0 + e2)
    g2 = e2 / (1.0 + e2)
    sel = jnp.where(sub == i1, 1.0, jnp.where(sub == i2, 1.0, 0.0))
    before = jnp.dot(sel.astype(BF16), upper_ref[...], preferred_element_type=F32) + carry[:, 0:1]
    r1 = jnp.sum(jnp.where(sub == i1, before, 0.0), axis=0, keepdims=True)
    r2 = jnp.sum(jnp.where(sub == i2, before, 0.0), axis=0, keepdims=True)
    meta_ref[...] = jnp.concatenate([i1.astype(F32), i2.astype(F32), r1, r2, g1, g2,
                                     jnp.zeros((2, r1.shape[1]), F32)], axis=0)
    total = carry[...] + jnp.sum(sel, axis=1, keepdims=True)
    carry[...] = total
    count_ref[...] = total


def _router(x, mixed, wo, nw, wrt, upper):
    n = x.shape[0]
    tm = TOK_BLK
    full = lambda shape: pl.BlockSpec(shape, lambda i: (0,) * len(shape))
    tok = lambda w: pl.BlockSpec((tm, w), lambda i: (i, 0))
    return pl.pallas_call(
        _router_kernel,
        grid=(n // tm,),
        in_specs=[tok(D_MODEL), tok(D_MODEL), full((D_MODEL, D_MODEL)), full((1, D_MODEL)),
                  full((16, D_MODEL)), full((tm, tm))],
        out_specs=[tok(D_MODEL), tok(D_MODEL // 2),
                   pl.BlockSpec((8, tm), lambda i: (0, i)), full((16, 128))],
        out_shape=[jax.ShapeDtypeStruct((n, D_MODEL), F32), jax.ShapeDtypeStruct((n, D_MODEL // 2), jnp.int32),
                   jax.ShapeDtypeStruct((8, n), F32), jax.ShapeDtypeStruct((16, 128), F32)],
        scratch_shapes=[pltpu.VMEM((16, 128), F32)],
        compiler_params=pltpu.CompilerParams(dimension_semantics=("arbitrary",),
                                             vmem_limit_bytes=VMEM_LIMIT),
        name="router",
    )(x, mixed, wo, nw, wrt, upper)


def _sc_row_gather(table, idx):
    n_workers = SC_CORES * SC_SUBCORES
    b = idx.shape[0]
    d = table.shape[1]
    assert b % (8 * n_workers) == 0
    per_worker = b // n_workers
    chunk = max(c for c in range(8, SC_MAX_GATHER_ROWS + 1, 8) if per_worker % c == 0)
    mesh = plsc.VectorSubcoreMesh(core_axis_name="c", subcore_axis_name="s")

    @functools.partial(
        pl.kernel, mesh=mesh, out_type=jax.ShapeDtypeStruct((b, d), table.dtype),
        scratch_types=[pltpu.VMEM((chunk,), jnp.int32), pltpu.VMEM((chunk, d), table.dtype),
                       pltpu.SemaphoreType.DMA])
    def gather(table_hbm, idx_hbm, out_hbm, idx_v, rows_v, sem):
        worker = lax.axis_index("s") * SC_CORES + lax.axis_index("c")
        base = worker * per_worker

        @pl.loop(0, per_worker // chunk)
        def _(c):
            off = pl.multiple_of(base + c * chunk, 8)
            pltpu.sync_copy(idx_hbm.at[pl.ds(off, chunk)], idx_v)
            pltpu.async_copy(table_hbm.at[idx_v], rows_v, sem).wait()
            pltpu.sync_copy(rows_v, out_hbm.at[pl.ds(off, chunk)])

    return gather(table, idx)


def _sc_row_scatter(rows, idx):
    n_workers = SC_CORES * SC_SUBCORES
    b = n_out = idx.shape[0]
    v, d = rows.shape
    assert b % (8 * n_workers) == 0
    per_worker = b // n_workers
    chunk = max(c for c in range(8, SC_MAX_GATHER_ROWS + 1, 8) if per_worker % c == 0 and v % c == 0)
    mesh = plsc.VectorSubcoreMesh(core_axis_name="c", subcore_axis_name="s")

    @functools.partial(
        pl.kernel, mesh=mesh, out_type=jax.ShapeDtypeStruct((n_out, d), rows.dtype),
        scratch_types=[pltpu.VMEM((chunk,), jnp.int32), pltpu.VMEM((chunk, d), rows.dtype)])
    def scatter(rows_hbm, idx_hbm, out_hbm, idx_v, rows_v):
        worker = lax.axis_index("s") * SC_CORES + lax.axis_index("c")
        base = worker * per_worker

        @pl.loop(0, per_worker // chunk)
        def _(c):
            off = pl.multiple_of(base + c * chunk, 8)
            pltpu.sync_copy(idx_hbm.at[pl.ds(off, chunk)], idx_v)
            pltpu.sync_copy(rows_hbm.at[pl.ds(pl.multiple_of(lax.rem(off, v), 8), chunk)], rows_v)
            pltpu.sync_copy(rows_v, out_hbm.at[idx_v])

    return scatter(rows, idx)


def _expert_ffn_kernel(texp_ref, tvalid_ref, xs_ref, wg_ref, wu_ref, wd_ref, o_ref, acc_ref, xb_ref):
    j = pl.program_id(0)
    f = pl.program_id(1)
    last = f == pl.num_programs(1) - 1
    valid = tvalid_ref[j] == 1
    half = D_MODEL // 2

    @pl.when(jnp.logical_and(valid, f == 0))
    def _():
        lo, hi = _unpack_bf16_pairs(xs_ref[...])
        xb_ref[:, :half] = lo.astype(BF16)
        xb_ref[:, half:] = hi.astype(BF16)

        acc_ref[...] = jnp.zeros_like(acc_ref)

    @pl.when(valid)
    def _():
        xs = xb_ref[...]
        g = jnp.dot(xs, wg_ref[0].astype(BF16), preferred_element_type=F32)
        u = jnp.dot(xs, wu_ref[0].astype(BF16), preferred_element_type=F32)
        act = (_silu(g) * u).astype(BF16)
        acc_ref[...] += jnp.dot(act, wd_ref[0].astype(BF16), preferred_element_type=F32)

        @pl.when(last)
        def _():
            o_ref[...] = _pack_bf16_pairs(acc_ref[...])

    @pl.when(jnp.logical_and(jnp.logical_not(valid), last))
    def _():
        o_ref[...] = jnp.zeros_like(o_ref)


def _expert_ffn(f_exp, f_valid, xs, wg, wu, wd):
    n_slots = xs.shape[0]
    d_ff = wg.shape[2]
    tf = _pick_tile(d_ff, (512, 256, 128))
    grid_spec = pltpu.PrefetchScalarGridSpec(
        num_scalar_prefetch=2, grid=(n_slots // FFN_TILE, d_ff // tf),
        in_specs=[pl.BlockSpec((FFN_TILE, D_MODEL // 2), lambda j, f, te, tv: (j, 0)),
                  pl.BlockSpec((1, D_MODEL, tf), lambda j, f, te, tv: (te[j], 0, f)),
                  pl.BlockSpec((1, D_MODEL, tf), lambda j, f, te, tv: (te[j], 0, f)),
                  pl.BlockSpec((1, tf, D_MODEL), lambda j, f, te, tv: (te[j], f, 0))],
        out_specs=pl.BlockSpec((FFN_TILE, D_MODEL // 2), lambda j, f, te, tv: (j, 0)),
        scratch_shapes=[pltpu.VMEM((FFN_TILE, D_MODEL), F32), pltpu.VMEM((FFN_TILE, D_MODEL), BF16)])
    return pl.pallas_call(
        _expert_ffn_kernel, grid_spec=grid_spec,
        out_shape=jax.ShapeDtypeStruct((n_slots, D_MODEL // 2), jnp.int32),
        compiler_params=pltpu.CompilerParams(dimension_semantics=("arbitrary", "arbitrary"),
                                             vmem_limit_bytes=VMEM_LIMIT),
        name="expert_ffn",
    )(f_exp, f_valid, xs, wg, wu, wd)


def _combine_kernel(part_tiles, x1_ref, o1_ref, o2_ref, gate_ref, fnw_ref, *y_refs):
    i = pl.program_id(0)
    lo1, hi1 = _unpack_bf16_pairs(o1_ref[...])
    lo2, hi2 = _unpack_bf16_pairs(o2_ref[...])
    g1 = gate_ref[:, 0:1]
    g2 = gate_ref[:, 1:2]
    half = D_MODEL // 2
    x1 = x1_ref[...]
    fnw = fnw_ref[...]
    ya = x1[:, :half] + g1 * lo1 + g2 * lo2
    yb = x1[:, half:] + g1 * hi1 + g2 * hi2
    scale = lax.rsqrt((jnp.sum(ya * ya, axis=-1, keepdims=True) + jnp.sum(yb * yb, axis=-1, keepdims=True))
                      / D_MODEL + EPS)
    y = jnp.concatenate([ya * scale * fnw[:, :half], yb * scale * fnw[:, half:]], axis=1)
    lo_tile = 0
    for y_ref, tiles in zip(y_refs, part_tiles):
        @pl.when(jnp.logical_and(i >= lo_tile, i < lo_tile + tiles))
        def _(y_ref=y_ref):
            _store_part_tile(y_ref, y)
        lo_tile += tiles


def _combine(x1, o12, gates, fnw, part_shapes):
    n = x1.shape[0]
    parts = [jax.ShapeDtypeStruct(s, F32) for s in part_shapes]
    tm = _pick_tile(np.gcd.reduce([_part_rows(p) for p in parts]), TOKEN_TILES)
    n_tiles = n // tm
    out_specs, bounds = _token_parts_specs(parts, tm)
    part_tiles = [bounds[k + 1] - bounds[k] for k in range(len(parts))]
    return pl.pallas_call(
        functools.partial(_combine_kernel, part_tiles),
        grid=(n_tiles,),
        in_specs=[pl.BlockSpec((tm, D_MODEL), lambda i: (i, 0)),
                  pl.BlockSpec((tm, D_MODEL // 2), lambda i: (i, 0)),
                  pl.BlockSpec((tm, D_MODEL // 2), lambda i: (i + n_tiles, 0)),
                  pl.BlockSpec((tm, 128), lambda i: (i, 0)),
                  pl.BlockSpec((1, D_MODEL), lambda i: (0, 0))],
        out_specs=out_specs,
        out_shape=parts,
        compiler_params=pltpu.CompilerParams(dimension_semantics=("arbitrary",),
                                             vmem_limit_bytes=VMEM_LIMIT),
        name="combine",
    )(x1, o12, o12, gates, fnw)


def _moe(x, mixed, wo, nw, router_w, wg, wu, wd, fnw, part_shapes):
    i32 = jnp.int32
    n = x.shape[0]
    assert n % TOK_BLK == 0
    wrt = jnp.pad(router_w.T, ((0, 16 - N_EXPERTS), (0, 0))).astype(BF16)
    t = np.arange(TOK_BLK)
    upper = jnp.asarray((t[:, None] < t[None, :]).astype(np.float32), dtype=BF16)
    x1, hn_p, meta, count = _router(x, mixed, wo, nw, wrt, upper)

    n_slots = (-(-2 * n // FFN_TILE) + N_EXPERTS) * FFN_TILE
    counts = count[:N_EXPERTS, 0].astype(i32)
    gsize = (counts + FFN_TILE - 1) // FFN_TILE * FFN_TILE
    gend = jnp.cumsum(gsize)
    goff = gend - gsize
    fstart = jnp.arange(n_slots // FFN_TILE, dtype=i32) * FFN_TILE
    f_valid = (fstart < gend[-1]).astype(i32)
    f_exp = jnp.sum((fstart[:, None] >= gend[None, :]).astype(i32), axis=1)
    f_exp = jnp.minimum(f_exp, jnp.sum((gend[-1] - 1 >= gend).astype(i32)))
    expert_ids = jnp.arange(N_EXPERTS, dtype=i32)[:, None]
    top = meta[0:2].astype(i32)
    rank = meta[2:4].astype(i32)
    slot = jnp.stack([jnp.sum(jnp.where(top[k][None, :] == expert_ids, goff[:, None], 0), axis=0) + rank[k]
                      for k in range(2)]).reshape(-1)
    pad = gsize - counts
    cpad = jnp.cumsum(pad)
    j = jnp.arange(n_slots - 2 * n, dtype=i32)
    pad_e = jnp.sum((j[:, None] >= cpad[None, :]).astype(i32), axis=1)
    pad_base = jnp.sum(jnp.where(pad_e[:, None] == expert_ids.T, (goff + counts - (cpad - pad))[None, :], 0), axis=1)
    pad_slot = jnp.where(pad_e < N_EXPERTS, pad_base + j, gend[-1] + j - cpad[-1])

    xs_p = _sc_row_scatter(hn_p, jnp.concatenate([slot, pad_slot]))
    out_p = _expert_ffn(f_exp, f_valid, xs_p, wg, wu, wd)
    o12 = _sc_row_gather(out_p, slot)
    gates = _pad_lanes(meta[4:6].T, 128)
    return _combine(x1, o12, gates, fnw, part_shapes)


def _rope_table(seq_len, dec_len):
    half = ROPE_DIMS // 2
    pos = jnp.concatenate([jnp.arange(seq_len), PAST_LEN + jnp.arange(dec_len)]).astype(F32)
    inv_freq = jnp.power(jnp.float32(ROPE_THETA), -jnp.arange(half, dtype=F32) / half)
    lane = np.arange(128) % HEAD
    inv_lane = jnp.where(lane < ROPE_DIMS, inv_freq[lane % half], 0.0)
    ang = pos[:, None] * inv_lane[None, :]
    cos, sin = jnp.cos(ang), jnp.sin(ang)
    sin_a = jnp.where(lane < half, -sin, 0.0)
    sin_b = jnp.where((lane >= half) & (lane < ROPE_DIMS), sin, 0.0)
    return jnp.concatenate([cos, sin_a, sin_b], axis=1)


def _mixer_consts():
    t = np.arange(CHUNK)
    tri = (t[:, None] >= t[None, :])
    same = (t[:, None] // SUB) == (t[None, :] // SUB)
    c = np.arange(MIX_W)
    ind = (c[:, None] // HEAD) == (c[None, :] // HEAD)
    as_bf = lambda m: jnp.asarray(m.astype(np.float32), dtype=BF16)
    return [as_bf(tri), as_bf(tri & same), as_bf(same), as_bf(ind)]


def _mlstm_state_to_pairs(c, n):
    b = c.shape[0]
    eye = jnp.eye(2, dtype=F32)
    c6 = c.astype(F32).reshape(b, N_HEADS // 2, 2, HEAD, 1, HEAD) * eye[None, None, :, None, :, None]
    n6 = jnp.broadcast_to(n.astype(F32).reshape(b, N_HEADS // 2, 2, HEAD, 1, 1), c6.shape) * eye[None, None, :, None, :, None]
    return jnp.concatenate([c6.reshape(b, N_HEADS // 2, 2 * HEAD, 2 * HEAD),
                            n6.reshape(b, N_HEADS // 2, 2 * HEAD, 2 * HEAD)], axis=-1)


def _mlstm_state_from_pairs(s):
    b = s.shape[0]
    s7 = s.reshape(b, N_HEADS // 2, 2, HEAD, 2, 2, HEAD)
    c = jnp.stack([s7[:, :, a, :, 0, a, :] for a in range(2)], axis=2).reshape(b, N_HEADS, HEAD, HEAD)
    n = jnp.stack([s7[:, :, a, :, 1, a, 0] for a in range(2)], axis=2).reshape(b, N_HEADS, HEAD)
    return (c, n)


def _pad_lanes(v, width):
    return jnp.pad(v, ((0, 0), (0, width - v.shape[1])))


def kernel(x_prompt, x_sample, state_mlstm_C, state_mlstm_n, state_mlstm_m, state_hgrn_S, cache_swa_k, cache_swa_v, state_ssd_h, state_ssd_conv, norm1_w, w_in, ml_ig_b, ml_fg_b, ml_norm_w, hg_lb_logits, hg_norm_w, sw_sinks, ssd_conv_w, ssd_conv_b, ssd_dt_bias, ssd_A_log, ssd_D, ssd_norm_w, w_out, norm2_w, ffn_w_gate, ffn_w_up, ffn_w_down, moe_router, moe_w_gate, moe_w_up, moe_w_down, final_norm_w):
    depth = w_in.shape[0]
    bp, seq_len, _ = x_prompt.shape
    bs, dec_len, _ = x_sample.shape
    assert seq_len % CHUNK == 0 and dec_len == CHUNK and depth % 2 == 0
    n_seq = bp + bs
    chunks_per_prompt = seq_len // CHUNK

    assert bp == SEQ_GROUP and bs % SEQ_GROUP == 0
    x_parts = (x_prompt.reshape(bp, chunks_per_prompt, CHUNK, D_MODEL), x_sample.reshape(bs * dec_len, D_MODEL))
    part_shapes = tuple(p.shape for p in x_parts)
    rope_tab = _rope_table(seq_len, dec_len)
    consts = _mixer_consts()

    sm = jax.nn.softmax(hg_lb_logits.astype(F32), axis=0)
    hg_lb = jnp.cumsum(sm, axis=0) - sm[0]

    def with_prompt_zeros(a):
        return jnp.concatenate([jnp.zeros((bp,) + a.shape[1:], F32), a.astype(F32)], axis=0)

    state_outs = []
    for l in range(depth):
        lp = {
            "sinks": sw_sinks[l].astype(F32),
            "gbias": _pad_lanes(jnp.concatenate([ml_ig_b[l], ml_fg_b[l], ssd_dt_bias[l]])[None, :].astype(F32), 128),
            "alog": _pad_lanes(ssd_A_log[l][None, :].astype(F32), 128),
            "mlnw": ml_norm_w[l][None, :].astype(F32),
            "hglb": hg_lb[l][None, :],
            "hgnw": hg_norm_w[l][None, :].astype(F32),
            "convw": ssd_conv_w[l].astype(F32),
            "convb": ssd_conv_b[l][None, :].astype(F32),
            "ssdd": jnp.repeat(ssd_D[l].astype(F32), HEAD)[None, :],
            "ssdnw": ssd_norm_w[l][None, :].astype(F32),
        }
        c_aug = _mlstm_state_to_pairs(state_mlstm_C[l], state_mlstm_n[l])
        m_pad = jnp.pad(state_mlstm_m[l][:, None, :], ((0, 0), (0, 7), (0, 128 - N_HEADS)))
        s_t = jnp.swapaxes(state_hgrn_S[l], -1, -2)
        s_t = jnp.moveaxis(s_t, 1, 2).reshape(bs, HEAD, MIX_W)
        cv_pad = jnp.pad(state_ssd_conv[l], ((0, 0), (8 - (SSD_CONV - 1), 0), (0, 0)))
        init = [with_prompt_zeros(a) for a in (
            c_aug, m_pad, s_t, cache_swa_k[l].reshape(bs, WINDOW, 128), cache_swa_v[l].reshape(bs, WINDOW, 128),
            state_ssd_h[l], cv_pad)]

        proj = _inproj(x_parts, norm1_w[l][None, :], w_in[l])
        mixed, c_o, m_o, s_o, k_o, v_o, h_o, cv_o = _mixer(
            proj, rope_tab, init, lp, consts, chunks_per_prompt, n_seq)
        wo = w_out[l].astype(BF16)
        j = l // 2
        if l % 2 == 0:
            x_parts = (_dense_ffn(x_parts, mixed, wo, norm2_w[l][None, :], ffn_w_gate[j].astype(BF16),
                                  ffn_w_up[j].astype(BF16), ffn_w_down[j].astype(BF16)),)
        else:
            assert l == depth - 1 and len(x_parts) == 1
            y_parts = _moe(x_parts[0], mixed, wo, norm2_w[l][None, :], moe_router[j], moe_w_gate[j],
                           moe_w_up[j], moe_w_down[j], final_norm_w[None, :], part_shapes)
        s_back = jnp.moveaxis(s_o.reshape(n_seq, HEAD, N_HEADS, HEAD), 2, 1)
        state_outs.append(_mlstm_state_from_pairs(c_o) + (m_o[:, 0, :N_HEADS],
                           jnp.swapaxes(s_back, -1, -2),
                           k_o.reshape(n_seq, WINDOW, SW_KV, HEAD), v_o.reshape(n_seq, WINDOW, SW_KV, HEAD),
                           h_o, cv_o[:, 8 - (SSD_CONV - 1):, :]))

    y_prompt = y_parts[0].reshape(bp, seq_len, D_MODEL)
    y_sample = y_parts[1].reshape(bs, dec_len, D_MODEL)
    stacked = [jnp.stack([so[k] for so in state_outs]) for k in range(8)]
    return (y_prompt, y_sample) + tuple(s[:, :bp] for s in stacked) + tuple(s[:, bp:] for s in stacked)
```

```python
import functools

import numpy as np
import jax
import jax.numpy as jnp
from jax import lax
from jax.experimental import pallas as pl
from jax.experimental.pallas import tpu as pltpu
from jax.experimental.pallas import tpu_sc as plsc

F32 = jnp.float32
BF16 = jnp.bfloat16

D_MODEL = 1024
CHUNK = 64
EPS = 1e-6
N_HEADS = 4
HEAD = 64
MIX_W = N_HEADS * HEAD
SW_KV = 2
WINDOW = 128
ROPE_DIMS = 16
ROPE_THETA = 500000.0
SSD_N = 128
SSD_CONV = 4
SSD_CONV_DIM = 768
PAST_LEN = 4096
N_EXPERTS = 8
SUB = 16
SEQ_GROUP = 2
HG_SLICE = 4
TOKEN_TILES = (512, 256, 128)

C_MLQ, C_MLK, C_MLV, C_MLO = 0, 256, 512, 768
C_HGQ, C_HGF, C_HGI, C_HGG = 1024, 1280, 1536, 1792
C_SWQ, C_SWK, C_SWV = 2048, 2304, 2432
C_SDZ, C_SDX, C_GATE = 2560, 2816, 3584
D_PROJ = 3712
NEG = -1e30
VMEM_LIMIT = 56 * 1024 * 1024
SC_CORES = 2
SC_SUBCORES = 16
SC_MAX_GATHER_ROWS = 128


def _mm(a, b):
    return jnp.dot(a.astype(BF16), b.astype(BF16), preferred_element_type=F32)


def _mm_nt(a, b):
    return lax.dot_general(a.astype(BF16), b.astype(BF16), (((1,), (1,)), ((), ())),
                           preferred_element_type=F32)


def _mm_tn(a, b):
    return jnp.dot(a.T.astype(BF16), b.astype(BF16), preferred_element_type=F32)


def _mm_sel(sel, x):
    hi = x.astype(BF16)
    r1 = x - hi.astype(F32)
    mid = r1.astype(BF16)
    lo = (r1 - mid.astype(F32)).astype(BF16)
    return (jnp.dot(sel, hi, preferred_element_type=F32)
            + jnp.dot(sel, mid, preferred_element_type=F32)
            + jnp.dot(sel, lo, preferred_element_type=F32))


def _sigmoid(x):
    return 1.0 / (1.0 + jnp.exp(-x))


def _silu(x):
    return x * _sigmoid(x)


def _softplus(x):
    return jnp.maximum(x, 0.0) + jnp.log(1.0 + jnp.exp(-jnp.abs(x)))


def _rms(x, w):
    return x * lax.rsqrt(jnp.mean(x * x, axis=-1, keepdims=True) + EPS) * w


def _pick_tile(n, candidates):
    for c in candidates:
        if n % c == 0:
            return c
    raise ValueError(f"no tile for {n}")


def _token_parts_specs(parts, tm):
    bounds = np.cumsum([0] + [_part_rows(p) // tm for p in parts])
    specs = []
    for k, p in enumerate(parts):
        assert _part_rows(p) % tm == 0
        lo, hi = int(bounds[k]), int(bounds[k + 1])
        specs.append(_part_block_spec(p, tm, lo, hi - lo))
    return specs, [int(b) for b in bounds]


def _part_rows(p):
    return int(np.prod(p.shape[:-1]))


def _part_block_spec(p, tm, first_tile, n_tiles):
    idx = lambda i: jnp.clip(i - first_tile, 0, n_tiles - 1)
    if len(p.shape) == 2:
        return pl.BlockSpec((tm, p.shape[1]), lambda i: (idx(i), 0))
    grp, _, chunk, d = p.shape
    assert tm % (grp * chunk) == 0
    return pl.BlockSpec((grp, tm // (grp * chunk), chunk, d), lambda i: (0, idx(i), 0, 0))


def _load_part_tile(ref):
    if len(ref.shape) == 2:
        return ref[...]
    grp, n_chunks = ref.shape[:2]
    return jnp.concatenate([ref[s, c] for c in range(n_chunks) for s in range(grp)], axis=0)


def _store_part_tile(ref, x):
    if len(ref.shape) == 2:
        ref[...] = x
        return
    grp, n_chunks, chunk = ref.shape[:3]
    for c in range(n_chunks):
        for s in range(grp):
            r0 = (c * grp + s) * chunk
            ref[s, c] = x[r0:r0 + chunk]


def _read_token_tile(refs, bounds):
    i = pl.program_id(0)
    x = _load_part_tile(refs[-1])
    for k in range(len(refs) - 2, -1, -1):
        x = jnp.where(i < bounds[k + 1], _load_part_tile(refs[k]), x)
    return x


def _inproj_kernel(n_parts, bounds, *refs):
    x_refs = refs[:n_parts]
    nw_ref, w_ref, o_ref, wp_ref = refs[n_parts:]

    @pl.when(pl.program_id(0) == 0)
    def _():
        rows = 128
        for r0 in range(0, D_MODEL, rows):
            w = w_ref[r0:r0 + rows, :]
            wp_ref[r0:r0 + rows, 0:C_MLO] = w[:, 0:768].astype(BF16)
            wp_ref[r0:r0 + rows, C_MLO:C_GATE] = w[:, 776:3592].astype(BF16)
            gates = jnp.concatenate([w[:, 768:776], w[:, 3592:3596],
                                     jnp.zeros((rows, D_PROJ - C_GATE - 12), F32)], axis=1)
            wp_ref[r0:r0 + rows, C_GATE:D_PROJ] = gates.astype(BF16)

    xn = _rms(_read_token_tile(x_refs, bounds), nw_ref[...]).astype(BF16)
    o_ref[...] = jnp.dot(xn, wp_ref[...], preferred_element_type=F32)


def _inproj(x_parts, nw, w):
    n = sum(_part_rows(p) for p in x_parts)
    tm = _pick_tile(np.gcd.reduce([_part_rows(p) for p in x_parts]), TOKEN_TILES)
    x_specs, bounds = _token_parts_specs(x_parts, tm)
    d_in = w.shape[1]
    return pl.pallas_call(
        functools.partial(_inproj_kernel, len(x_parts), bounds),
        grid=(n // tm,),
        in_specs=x_specs + [pl.BlockSpec((1, D_MODEL), lambda i: (0, 0)),
                            pl.BlockSpec((D_MODEL, d_in), lambda i: (0, 0), pipeline_mode=pl.Buffered(1))],
        out_specs=pl.BlockSpec((tm, D_PROJ), lambda i: (i, 0)),
        out_shape=jax.ShapeDtypeStruct((n, D_PROJ), F32),
        scratch_shapes=[pltpu.VMEM((D_MODEL, D_PROJ), BF16)],
        compiler_params=pltpu.CompilerParams(dimension_semantics=("arbitrary",),
                                             vmem_limit_bytes=VMEM_LIMIT),
        name="inproj",
    )(*x_parts, nw, w)


def _head_norm(x, width):
    parts = []
    for g in range(x.shape[1] // width):
        xg = x[:, g * width:(g + 1) * width]
        parts.append(xg * lax.rsqrt(jnp.mean(xg * xg, axis=-1, keepdims=True) + EPS))
    return jnp.concatenate(parts, axis=1)


def _mixer_kernel(prompt_steps,
                  sinks_ref,
                  proj_ref, rope_ref, c0_ref, m0_ref, s0_ref, k0_ref, v0_ref, h0_ref, cv0_ref,
                  gbias_ref, alog_ref, mlnw_ref, hglb_ref, hgnw_ref,
                  convw_ref, convb_ref, ssdd_ref, ssdnw_ref,
                  tri_ref, triblk_ref, blkones_ref, ind_ref,
                  mixed_ref, c_ref, m_ref, s_ref, k_ref, v_ref, h_ref, cv_ref,
                  convbuf, bloc_s, kk_s, hv_s):
    i = pl.program_id(0)
    is_prompt = i < prompt_steps
    is_first = jnp.logical_or(jnp.logical_not(is_prompt), i == 0)
    n_valid = jnp.where(is_prompt, jnp.minimum(i * CHUNK, WINDOW), WINDOW)

    @pl.when(is_first)
    def _():
        c_ref[...] = c0_ref[...]
        m_ref[...] = m0_ref[...]
        s_ref[...] = s0_ref[...]
        k_ref[...] = k0_ref[...]
        v_ref[...] = v0_ref[...]
        h_ref[...] = h0_ref[...]
        cv_ref[...] = cv0_ref[...]

    members = [_mixer_chunk(g, n_valid, sinks_ref, proj_ref, rope_ref,
                            gbias_ref, alog_ref, mlnw_ref, hglb_ref, hgnw_ref,
                            convw_ref, convb_ref, ssdd_ref, ssdnw_ref,
                            tri_ref, triblk_ref, blkones_ref, ind_ref,
                            mixed_ref, c_ref, m_ref, s_ref, k_ref, v_ref, h_ref, cv_ref,
                            convbuf, bloc_s, kk_s, hv_s) for g in range(SEQ_GROUP)]
    for phase in zip(*[m[0] for m in members]):
        chains = [c for per_member in zip(*phase) for c in per_member]
        while chains:
            alive = []
            for chain in chains:
                if next(chain, _CHAIN_DONE) is not _CHAIN_DONE:
                    alive.append(chain)
            chains = alive
    for _, finish in members:
        finish()


_CHAIN_DONE = object()


def _mixer_chunk(g, n_valid, sinks_ref, proj_ref, rope_ref,
                 gbias_ref, alog_ref, mlnw_ref, hglb_ref, hgnw_ref,
                 convw_ref, convb_ref, ssdd_ref, ssdnw_ref,
                 tri_ref, triblk_ref, blkones_ref, ind_ref,
                 mixed_ref, c_ref, m_ref, s_ref, k_ref, v_ref, h_ref, cv_ref,
                 convbuf, bloc_s, kk_s, hv_s):
    rs = slice(g * CHUNK, (g + 1) * CHUNK)
    L = CHUNK
    row = lax.broadcasted_iota(jnp.int32, (L, L), 0)
    col = lax.broadcasted_iota(jnp.int32, (L, L), 1)
    causal = row >= col
    tri = tri_ref[...]

    gate = proj_ref[rs,C_GATE:C_GATE + 128] + gbias_ref[...]
    ig = gate
    lf = pltpu.roll(-_softplus(-gate), 124, 1)
    dtv = pltpu.roll(_softplus(gate), 120, 1)

    b = _mm_sel(tri, lf)
    a = ig - b
    cm = a
    rowg = lax.broadcasted_iota(jnp.int32, (L, 128), 0)
    for sh in (1, 2, 4, 8, 16, 32):
        cm = jnp.where(rowg >= sh, jnp.maximum(cm, pltpu.roll(cm, sh, 0)), cm)
    m_prev = m_ref[g, 0:1, :]
    m_t = b + jnp.maximum(m_prev, cm)
    inter = jnp.exp(b + m_prev - m_t)
    bm = b - m_t
    a_t = a.T
    m_last = m_t[L - 1:L, :]
    b_last = b[L - 1:L, :]
    ws = jnp.exp(b_last + a - m_last)
    decay = jnp.exp(b_last + m_prev - m_last)
    emt = jnp.exp(-m_t)
    m_ref[g, 0:1, :] = m_last

    lane2 = lax.broadcasted_iota(jnp.int32, (L, 2 * HEAD), 1)
    low = lane2 < HEAD
    causal2 = lax.broadcasted_iota(jnp.int32, (L, 2 * HEAD), 0) >= (lane2 & (HEAD - 1))
    ones_blk = ind_ref[0:2 * HEAD, 0:2 * HEAD]
    ones_rows = jnp.ones((L, 2 * HEAD), BF16)
    st_row = lax.broadcasted_iota(jnp.int32, (2 * HEAD, 4 * HEAD), 0) // HEAD
    st_lane = (lax.broadcasted_iota(jnp.int32, (2 * HEAD, 4 * HEAD), 1) // HEAD) % 2
    ml_out = [None] * (N_HEADS // 2)

    def pair_cols(x, p):
        return jnp.where(low, x[:, 2 * p:2 * p + 1], x[:, 2 * p + 1:2 * p + 2])

    def split_rows(x):
        return jnp.concatenate([jnp.where(low, x, 0.0), jnp.where(low, 0.0, x)], axis=0)

    def ml_pair(p):
        cs = slice(2 * p * HEAD, 2 * (p + 1) * HEAD)
        q = proj_ref[rs, C_MLQ + cs.start:C_MLQ + cs.stop]
        k = proj_ref[rs, C_MLK + cs.start:C_MLK + cs.stop] * (HEAD ** -0.5)
        v = proj_ref[rs, C_MLV + cs.start:C_MLV + cs.stop]
        qk = _mm_nt(q, split_rows(k))
        yield
        st = c_ref[g, p]
        from_state = _mm(q, st)
        yield
        a_row = jnp.concatenate([a_t[2 * p:2 * p + 1, :], a_t[2 * p + 1:2 * p + 2, :]], axis=1)
        w = jnp.where(causal2, jnp.exp(pair_cols(bm, p) + a_row), 0.0)
        rhs = jnp.concatenate([split_rows(v).astype(BF16), ones_blk], axis=1)
        gate_in = pair_cols(inter, p)
        nd = (jnp.dot((w * qk).astype(BF16), rhs, preferred_element_type=F32)
              + jnp.concatenate([gate_in, gate_in], axis=1) * from_state)
        yield
        kws = k * pair_cols(ws, p)
        upd = jnp.dot(kws.T.astype(BF16), jnp.concatenate([v.astype(BF16), ones_rows], axis=1),
                      preferred_element_type=F32)
        yield
        den = jnp.maximum(jnp.abs(nd[:, 2 * HEAD:]), pair_cols(emt, p))
        ml_out[p] = nd[:, :2 * HEAD] / den
        dec = jnp.where(st_row == 0, decay[:, 2 * p:2 * p + 1], decay[:, 2 * p + 1:2 * p + 2])
        c_ref[g, p] = dec * st + jnp.where(st_row == st_lane, upd, 0.0)

    def ml_finish():
        ml = _head_norm(jnp.concatenate(ml_out, axis=1), HEAD)
        mixed_ref[rs,0:MIX_W] = (_sigmoid(proj_ref[rs,C_MLO:C_MLO + MIX_W]) * ml
                                 * mlnw_ref[...]).astype(mixed_ref.dtype)

    lb = hglb_ref[...]
    fg = lb + (1.0 - lb) * _sigmoid(proj_ref[rs,C_HGF:C_HGF + MIX_W])
    hq = _silu(proj_ref[rs,C_HGQ:C_HGQ + MIX_W]) * (HEAD ** -0.5)
    kk = 1.0 - fg
    lfg = jnp.log(fg)
    hv = proj_ref[rs,C_HGI:C_HGI + MIX_W]
    bloc = _mm_sel(triblk_ref[...], lfg)
    blast = _mm_sel(blkones_ref[...], lfg)
    qe = hq * jnp.exp(bloc)
    kw = kk * jnp.exp(blast - bloc)
    bloc_s[g] = bloc
    kk_s[g] = kk
    hv_s[g] = hv
    ind = ind_ref[...]
    rows = lax.broadcasted_iota(jnp.int32, (SUB, MIX_W), 0)
    n_blk = L // SUB
    hg_state_part = [None] * n_blk
    hg_block_part = [None] * n_blk

    def hg_within_block(blk):
        r0 = blk * SUB
        q_blk = hq[r0:r0 + SUB]
        b_blk = bloc[r0:r0 + SUB]
        xs = []
        for s in range(SUB):
            r = r0 + s
            e = jnp.exp(jnp.minimum(b_blk - bloc_s[g, r:r + 1, :], 0.0))
            xs.append((q_blk * (kk_s[g, r:r + 1, :] * e)).astype(BF16))
            if s % HG_SLICE == HG_SLICE - 1:
                yield
        att = jnp.dot(jnp.concatenate(xs, axis=0), ind, preferred_element_type=F32)
        yield
        o_blk = att[0:SUB] * hv_s[g, r0:r0 + 1, :]
        for s in range(1, SUB):
            r = r0 + s
            o_blk = o_blk + jnp.where(rows >= s, att[s * SUB:(s + 1) * SUB], 0.0) * hv_s[g, r:r + 1, :]
            if s % HG_SLICE == HG_SLICE - 1:
                yield
        hg_block_part[blk] = o_blk

    def hg_state_chain():
        st = s_ref[g]
        for blk in range(n_blk):
            r0 = blk * SUB
            hg_state_part[blk] = jnp.concatenate(
                [_mm_nt(qe[r0:r0 + SUB, h * HEAD:(h + 1) * HEAD], st[:, h * HEAD:(h + 1) * HEAD])
                 for h in range(N_HEADS)], axis=1)
            upd = jnp.concatenate(
                [_mm_tn(hv[r0:r0 + SUB, h * HEAD:(h + 1) * HEAD], kw[r0:r0 + SUB, h * HEAD:(h + 1) * HEAD])
                 for h in range(N_HEADS)], axis=1)
            yield
            st = st * jnp.exp(blast[r0:r0 + 1, :]) + upd
        s_ref[g] = st

    def hg_finish():
        o = jnp.concatenate([hg_state_part[blk] + hg_block_part[blk] for blk in range(n_blk)], axis=0)
        mixed_ref[rs,MIX_W:2 * MIX_W] = (_head_norm(o, HEAD) * hgnw_ref[...]
                                         * _silu(proj_ref[rs,C_HGG:C_HGG + MIX_W])).astype(mixed_ref.dtype)

    @functools.lru_cache(maxsize=None)
    def sw_prep():
        cos = rope_ref[:, 0:128]
        sin_a = rope_ref[:, 128:256]
        sin_b = rope_ref[:, 256:384]

        def rope(x):
            return x * cos + pltpu.roll(x, 120, 1) * sin_a + pltpu.roll(x, 8, 1) * sin_b

        q_rot = [rope(proj_ref[rs,C_SWQ:C_SWQ + 128]), rope(proj_ref[rs,C_SWQ + 128:C_SWQ + 256])]
        k_rot = rope(proj_ref[rs,C_SWK:C_SWK + 128])
        v_cur = proj_ref[rs,C_SWV:C_SWV + 128]
        colw = lax.broadcasted_iota(jnp.int32, (2 * L, WINDOW), 1)
        row2 = lax.broadcasted_iota(jnp.int32, (2 * L, 1), 0)
        return q_rot, k_rot, v_cur, k_ref[g], v_ref[g], colw >= WINDOW - n_valid, row2

    sw_prep()
    sw_out = [None] * (2 * SW_KV)

    def sw_group(kv):
        q_rot, k_rot, v_cur, k_prev, v_prev, prev_ok, row2 = sw_prep()
        gs = slice(kv * HEAD, (kv + 1) * HEAD)
        q2 = jnp.concatenate([q_rot[kv][:, 0:HEAD], q_rot[kv][:, HEAD:2 * HEAD]], axis=0) * (HEAD ** -0.5)
        s_p = jnp.where(prev_ok, _mm_nt(q2, k_prev[:, gs]), NEG)
        yield
        s_c = _mm_nt(q2, k_rot[:, gs])
        yield
        sink = jnp.where(row2 < L, sinks_ref[2 * kv], sinks_ref[2 * kv + 1])
        mx = jnp.maximum(jnp.maximum(jnp.max(s_p, axis=-1, keepdims=True),
                                     jnp.max(s_c, axis=-1, keepdims=True)), sink)
        e_p = jnp.exp(s_p - mx)
        e_c = jnp.exp(s_c - mx)
        den = (jnp.sum(e_p, axis=-1, keepdims=True) + jnp.sum(e_c, axis=-1, keepdims=True)
               + jnp.exp(sink - mx))
        o_p = _mm(e_p, v_prev[:, gs])
        yield
        o = (o_p + _mm(e_c, v_cur[:, gs])) / den
        yield
        sw_out[2 * kv] = o[0:L]
        sw_out[2 * kv + 1] = o[L:2 * L]

    def sw_finish():
        _, k_rot, v_cur, k_prev, v_prev, _, _ = sw_prep()
        mixed_ref[rs,2 * MIX_W:3 * MIX_W] = jnp.concatenate(sw_out, axis=1).astype(mixed_ref.dtype)
        k_ref[g, 0:WINDOW - L, :] = k_prev[L:WINDOW]
        k_ref[g, WINDOW - L:WINDOW, :] = k_rot
        v_ref[g, 0:WINDOW - L, :] = v_prev[L:WINDOW]
        v_ref[g, WINDOW - L:WINDOW, :] = v_cur

    @functools.lru_cache(maxsize=None)
    def sd_prep():
        convbuf[g, 0:8, :] = cv_ref[g]
        convbuf[g, 8:8 + L, :] = proj_ref[rs,C_SDX:C_SDX + SSD_CONV_DIM]
        acc = convb_ref[...] + convbuf[g, 5:5 + L, :] * convw_ref[0:1, :]
        for j in range(1, SSD_CONV):
            acc = acc + convbuf[g, 5 + j:5 + j + L, :] * convw_ref[j:j + 1, :]
        xbc = _silu(acc)
        cv_ref[g] = convbuf[g, L:L + 8, :]
        a_neg = -jnp.exp(alog_ref[...])
        ca = _mm_sel(tri, dtv * a_neg)
        ca_last = ca[L - 1:L, :]
        return (xbc, xbc[:, 0:MIX_W], ca, ca.T, dtv.T, jnp.exp(ca), jnp.exp(ca_last - ca) * dtv,
                jnp.exp(ca_last))

    sd_prep()
    sd_out = [None] * (N_HEADS // 2)

    def sd_group(grp):
        xbc, xs_all, ca, ca_t, dt_t, eca, wss, sdec = sd_prep()
        bm_g = xbc[:, MIX_W + grp * SSD_N:MIX_W + (grp + 1) * SSD_N]
        cm_g = xbc[:, MIX_W + 2 * SSD_N + grp * SSD_N:MIX_W + 2 * SSD_N + (grp + 1) * SSD_N]
        e, o = 2 * grp, 2 * grp + 1
        x = xs_all[:, e * HEAD:(o + 1) * HEAD]
        cb2 = _mm_nt(cm_g, jnp.concatenate([bm_g, bm_g], axis=0))
        yield
        hst = h_ref[g, e:o + 1].reshape(2 * HEAD, SSD_N)
        from_state = _mm_nt(cm_g, hst)
        yield
        ca_row = jnp.concatenate([ca_t[e:e + 1, :], ca_t[o:o + 1, :]], axis=1)
        dt_row = jnp.concatenate([dt_t[e:e + 1, :], dt_t[o:o + 1, :]], axis=1)
        dec = jnp.where(causal2, jnp.exp(pair_cols(ca, grp) - ca_row), 0.0)
        sd_out[grp] = _mm(cb2 * dec * dt_row, split_rows(x)) + pair_cols(eca, grp) * from_state
        yield
        upd = _mm_tn(x * pair_cols(wss, grp), bm_g)
        yield
        keep = jnp.where(lax.broadcasted_iota(jnp.int32, (2 * HEAD, SSD_N), 0) < HEAD,
                         sdec[:, e:e + 1], sdec[:, o:o + 1])
        h_ref[g, e:o + 1] = (keep * hst + upd).reshape(2, HEAD, SSD_N)

    def sd_finish():
        xs_all = sd_prep()[1]
        y = jnp.concatenate(sd_out, axis=1) + ssdd_ref[...] * xs_all
        gated = y * _silu(proj_ref[rs,C_SDZ:C_SDZ + MIX_W])
        mixed_ref[rs,3 * MIX_W:4 * MIX_W] = (_head_norm(gated, 2 * HEAD)
                                             * ssdnw_ref[...]).astype(mixed_ref.dtype)

    chains = ([[hg_within_block(blk) for blk in range(n_blk)]]
              + [[ml_pair(p)] for p in range(N_HEADS // 2)] + [[hg_state_chain(), sd_group(0)]]
              + [[sd_group(1)]]
              + [[sw_group(kv)] for kv in range(SW_KV)])

    def finish():
        ml_finish()
        hg_finish()
        sw_finish()
        sd_finish()

    return chains, finish


def _mixer(proj, rope_tab, init, lp, consts, chunks_per_prompt, n_seq):
    n = proj.shape[0]
    rows = SEQ_GROUP * CHUNK
    n_steps = n // rows

    def group_of(i):
        return jnp.where(i < chunks_per_prompt, 0, i - chunks_per_prompt + 1)

    def rope_blk(i):
        return jnp.minimum(i, chunks_per_prompt)

    def per_seq(shape):
        nd = len(shape)
        return pl.BlockSpec((SEQ_GROUP,) + shape, lambda i, s: (group_of(i),) + (0,) * nd)

    def const(shape):
        nd = len(shape)
        return pl.BlockSpec(shape, lambda i, s: (0,) * nd)

    state_shapes = [(N_HEADS // 2, 2 * HEAD, 4 * HEAD), (8, 128), (HEAD, MIX_W), (WINDOW, 128), (WINDOW, 128),
                    (N_HEADS, HEAD, SSD_N), (8, SSD_CONV_DIM)]
    in_specs = ([pl.BlockSpec((rows, D_PROJ), lambda i, s: (i, 0)),
                 pl.BlockSpec((CHUNK, 384), lambda i, s: (rope_blk(i), 0))]
                + [per_seq(s) for s in state_shapes]
                + [const((1, 128)), const((1, 128)), const((1, MIX_W)), const((1, MIX_W)), const((1, MIX_W)),
                   const((SSD_CONV, SSD_CONV_DIM)), const((1, SSD_CONV_DIM)), const((1, MIX_W)),
                   const((1, MIX_W)),
                   const((CHUNK, CHUNK)), const((CHUNK, CHUNK)), const((CHUNK, CHUNK)), const((MIX_W, MIX_W))])
    out_specs = ([pl.BlockSpec((rows, D_MODEL), lambda i, s: (i, 0))]
                 + [per_seq(s) for s in state_shapes])
    out_shape = ([jax.ShapeDtypeStruct((n, D_MODEL), BF16)]
                 + [jax.ShapeDtypeStruct((n_seq,) + s, F32) for s in state_shapes])
    grid_spec = pltpu.PrefetchScalarGridSpec(
        num_scalar_prefetch=1, grid=(n_steps,), in_specs=in_specs, out_specs=out_specs,
        scratch_shapes=[pltpu.VMEM((SEQ_GROUP, CHUNK + 8, SSD_CONV_DIM), F32),
                        pltpu.VMEM((SEQ_GROUP, CHUNK, MIX_W), F32), pltpu.VMEM((SEQ_GROUP, CHUNK, MIX_W), F32),
                        pltpu.VMEM((SEQ_GROUP, CHUNK, MIX_W), F32)])
    return pl.pallas_call(
        functools.partial(_mixer_kernel, chunks_per_prompt),
        grid_spec=grid_spec, out_shape=out_shape,
        compiler_params=pltpu.CompilerParams(dimension_semantics=("arbitrary",),
                                             vmem_limit_bytes=VMEM_LIMIT),
        name="mixer",
    )(lp["sinks"], proj, rope_tab, *init,
      lp["gbias"], lp["alog"], lp["mlnw"], lp["hglb"], lp["hgnw"],
      lp["convw"], lp["convb"], lp["ssdd"], lp["ssdnw"], *consts)


def _dense_ffn_kernel(f_chunk, n_parts, bounds, *refs):
    x_refs = refs[:n_parts]
    mix_ref, wo_ref, nw_ref, wg_ref, wu_ref, wd_ref, o_ref = refs[n_parts:]
    x1 = _read_token_tile(x_refs, bounds) + jnp.dot(mix_ref[...], wo_ref[...], preferred_element_type=F32)
    hn = _rms(x1, nw_ref[...]).astype(BF16)
    o_ref[...] = x1
    for f0 in range(0, wg_ref.shape[1], f_chunk):
        g = jnp.dot(hn, wg_ref[:, f0:f0 + f_chunk], preferred_element_type=F32)
        u = jnp.dot(hn, wu_ref[:, f0:f0 + f_chunk], preferred_element_type=F32)
        act = (_silu(g) * u).astype(BF16)
        o_ref[...] += jnp.dot(act, wd_ref[f0:f0 + f_chunk, :], preferred_element_type=F32)


def _dense_ffn(x_parts, mixed, wo, nw, wg, wu, wd):
    n = mixed.shape[0]
    d_ff = wg.shape[1]
    tm = _pick_tile(np.gcd.reduce([_part_rows(p) for p in x_parts]), TOKEN_TILES)
    x_specs, bounds = _token_parts_specs(x_parts, tm)
    full = lambda shape: pl.BlockSpec(shape, lambda i: (0, 0))
    return pl.pallas_call(
        functools.partial(_dense_ffn_kernel, 256, len(x_parts), bounds),
        grid=(n // tm,),
        in_specs=x_specs + [
                  pl.BlockSpec((tm, D_MODEL), lambda i: (i, 0)),
                  full((D_MODEL, D_MODEL)), full((1, D_MODEL)),
                  full((D_MODEL, d_ff)), full((D_MODEL, d_ff)), full((d_ff, D_MODEL))],
        out_specs=pl.BlockSpec((tm, D_MODEL), lambda i: (i, 0)),
        out_shape=jax.ShapeDtypeStruct((n, D_MODEL), F32),
        compiler_params=pltpu.CompilerParams(dimension_semantics=("arbitrary",),
                                             vmem_limit_bytes=VMEM_LIMIT),
        name="dense_ffn",
    )(*x_parts, mixed, wo, nw, wg, wu, wd)


TOK_BLK = 256
FFN_TILE = 1024


def _pack_bf16_pairs(x):
    w = x.shape[1] // 2
    bits = pltpu.bitcast(x.astype(BF16).astype(F32), jnp.int32)
    return lax.shift_right_logical(bits[:, :w], 16) | bits[:, w:]


def _unpack_bf16_pairs(p):
    lo = pltpu.bitcast(lax.shift_left(p, 16), F32)
    hi = pltpu.bitcast(p & jnp.int32(-65536), F32)
    return lo, hi


def _router_kernel(x_ref, mix_ref, wo_ref, nw_ref, wrt_ref, upper_ref,
                   x1_ref, hn_ref, meta_ref, count_ref, carry):
    @pl.when(pl.program_id(0) == 0)
    def _():
        carry[...] = jnp.zeros_like(carry)

    x1 = x_ref[...] + jnp.dot(mix_ref[...], wo_ref[...], preferred_element_type=F32)
    x1_ref[...] = x1
    hn_f = _rms(x1, nw_ref[...])
    hn = hn_f.astype(BF16)
    hn_ref[...] = _pack_bf16_pairs(hn_f)
    logits = lax.dot_general(wrt_ref[...], hn, (((1,), (1,)), ((), ())),
                             preferred_element_type=F32)
    sub = lax.broadcasted_iota(jnp.int32, logits.shape, 0)
    l1 = jnp.where(sub < N_EXPERTS, logits, NEG)
    m1 = jnp.max(l1, axis=0, keepdims=True)
    i1 = jnp.min(jnp.where(l1 == m1, sub, 99), axis=0, keepdims=True)
    l2 = jnp.where(sub == i1, NEG, l1)
    m2 = jnp.max(l2, axis=0, keepdims=True)
    i2 = jnp.min(jnp.where(l2 == m2, sub, 99), axis=0, keepdims=True)
    e2 = jnp.exp(m2 - m1)
    g1 = 1.0 / (1.0 + e2)
    g2 = e2 / (1.0 + e2)
    sel = jnp.where(sub == i1, 1.0, jnp.where(sub == i2, 1.0, 0.0))
    before = jnp.dot(sel.astype(BF16), upper_ref[...], preferred_element_type=F32) + carry[:, 0:1]
    r1 = jnp.sum(jnp.where(sub == i1, before, 0.0), axis=0, keepdims=True)
    r2 = jnp.sum(jnp.where(sub == i2, before, 0.0), axis=0, keepdims=True)
    meta_ref[...] = jnp.concatenate([i1.astype(F32), i2.astype(F32), r1, r2, g1, g2,
                                     jnp.zeros((2, r1.shape[1]), F32)], axis=0)
    total = carry[...] + jnp.sum(sel, axis=1, keepdims=True)
    carry[...] = total
    count_ref[...] = total


def _router(x, mixed, wo, nw, wrt, upper):
    n = x.shape[0]
    tm = TOK_BLK
    full = lambda shape: pl.BlockSpec(shape, lambda i: (0,) * len(shape))
    tok = lambda w: pl.BlockSpec((tm, w), lambda i: (i, 0))
    return pl.pallas_call(
        _router_kernel,
        grid=(n // tm,),
        in_specs=[tok(D_MODEL), tok(D_MODEL), full((D_MODEL, D_MODEL)), full((1, D_MODEL)),
                  full((16, D_MODEL)), full((tm, tm))],
        out_specs=[tok(D_MODEL), tok(D_MODEL // 2),
                   pl.BlockSpec((8, tm), lambda i: (0, i)), full((16, 128))],
        out_shape=[jax.ShapeDtypeStruct((n, D_MODEL), F32), jax.ShapeDtypeStruct((n, D_MODEL // 2), jnp.int32),
                   jax.ShapeDtypeStruct((8, n), F32), jax.ShapeDtypeStruct((16, 128), F32)],
        scratch_shapes=[pltpu.VMEM((16, 128), F32)],
        compiler_params=pltpu.CompilerParams(dimension_semantics=("arbitrary",),
                                             vmem_limit_bytes=VMEM_LIMIT),
        name="router",
    )(x, mixed, wo, nw, wrt, upper)


def _sc_row_gather(table, idx):
    n_workers = SC_CORES * SC_SUBCORES
    b = idx.shape[0]
    d = table.shape[1]
    assert b % (8 * n_workers) == 0
    per_worker = b // n_workers
    chunk = max(c for c in range(8, SC_MAX_GATHER_ROWS + 1, 8) if per_worker % c == 0)
    mesh = plsc.VectorSubcoreMesh(core_axis_name="c", subcore_axis_name="s")

    @functools.partial(
        pl.kernel, mesh=mesh, out_type=jax.ShapeDtypeStruct((b, d), table.dtype),
        scratch_types=[pltpu.VMEM((chunk,), jnp.int32), pltpu.VMEM((chunk, d), table.dtype),
                       pltpu.SemaphoreType.DMA])
    def gather(table_hbm, idx_hbm, out_hbm, idx_v, rows_v, sem):
        worker = lax.axis_index("s") * SC_CORES + lax.axis_index("c")
        base = worker * per_worker

        @pl.loop(0, per_worker // chunk)
        def _(c):
            off = pl.multiple_of(base + c * chunk, 8)
            pltpu.sync_copy(idx_hbm.at[pl.ds(off, chunk)], idx_v)
            pltpu.async_copy(table_hbm.at[idx_v], rows_v, sem).wait()
            pltpu.sync_copy(rows_v, out_hbm.at[pl.ds(off, chunk)])

    return gather(table, idx)


def _sc_row_scatter(rows, idx):
    n_workers = SC_CORES * SC_SUBCORES
    b = n_out = idx.shape[0]
    v, d = rows.shape
    assert b % (8 * n_workers) == 0
    per_worker = b // n_workers
    chunk = max(c for c in range(8, SC_MAX_GATHER_ROWS + 1, 8) if per_worker % c == 0 and v % c == 0)
    mesh = plsc.VectorSubcoreMesh(core_axis_name="c", subcore_axis_name="s")

    @functools.partial(
        pl.kernel, mesh=mesh, out_type=jax.ShapeDtypeStruct((n_out, d), rows.dtype),
        scratch_types=[pltpu.VMEM((chunk,), jnp.int32), pltpu.VMEM((chunk, d), rows.dtype)])
    def scatter(rows_hbm, idx_hbm, out_hbm, idx_v, rows_v):
        worker = lax.axis_index("s") * SC_CORES + lax.axis_index("c")
        base = worker * per_worker

        @pl.loop(0, per_worker // chunk)
        def _(c):
            off = pl.multiple_of(base + c * chunk, 8)
            pltpu.sync_copy(idx_hbm.at[pl.ds(off, chunk)], idx_v)
            pltpu.sync_copy(rows_hbm.at[pl.ds(pl.multiple_of(lax.rem(off, v), 8), chunk)], rows_v)
            pltpu.sync_copy(rows_v, out_hbm.at[idx_v])

    return scatter(rows, idx)


def _expert_ffn_kernel(texp_ref, tvalid_ref, xs_ref, wg_ref, wu_ref, wd_ref, o_ref, acc_ref, xb_ref):
    j = pl.program_id(0)
    f = pl.program_id(1)
    last = f == pl.num_programs(1) - 1
    valid = tvalid_ref[j] == 1
    half = D_MODEL // 2

    @pl.when(jnp.logical_and(valid, f == 0))
    def _():
        lo, hi = _unpack_bf16_pairs(xs_ref[...])
        xb_ref[:, :half] = lo.astype(BF16)
        xb_ref[:, half:] = hi.astype(BF16)

        acc_ref[...] = jnp.zeros_like(acc_ref)

    @pl.when(valid)
    def _():
        xs = xb_ref[...]
        g = jnp.dot(xs, wg_ref[0], preferred_element_type=F32)
        u = jnp.dot(xs, wu_ref[0], preferred_element_type=F32)
        act = (_silu(g) * u).astype(BF16)
        acc_ref[...] += jnp.dot(act, wd_ref[0], preferred_element_type=F32)

        @pl.when(last)
        def _():
            o_ref[...] = _pack_bf16_pairs(acc_ref[...])

    @pl.when(jnp.logical_and(jnp.logical_not(valid), last))
    def _():
        o_ref[...] = jnp.zeros_like(o_ref)


def _expert_ffn(f_exp, f_valid, xs, wg, wu, wd):
    n_slots = xs.shape[0]
    d_ff = wg.shape[2]
    tf = _pick_tile(d_ff, (512, 256, 128))
    grid_spec = pltpu.PrefetchScalarGridSpec(
        num_scalar_prefetch=2, grid=(n_slots // FFN_TILE, d_ff // tf),
        in_specs=[pl.BlockSpec((FFN_TILE, D_MODEL // 2), lambda j, f, te, tv: (j, 0)),
                  pl.BlockSpec((1, D_MODEL, tf), lambda j, f, te, tv: (te[j], 0, f)),
                  pl.BlockSpec((1, D_MODEL, tf), lambda j, f, te, tv: (te[j], 0, f)),
                  pl.BlockSpec((1, tf, D_MODEL), lambda j, f, te, tv: (te[j], f, 0))],
        out_specs=pl.BlockSpec((FFN_TILE, D_MODEL // 2), lambda j, f, te, tv: (j, 0)),
        scratch_shapes=[pltpu.VMEM((FFN_TILE, D_MODEL), F32), pltpu.VMEM((FFN_TILE, D_MODEL), BF16)])
    return pl.pallas_call(
        _expert_ffn_kernel, grid_spec=grid_spec,
        out_shape=jax.ShapeDtypeStruct((n_slots, D_MODEL // 2), jnp.int32),
        compiler_params=pltpu.CompilerParams(dimension_semantics=("arbitrary", "arbitrary"),
                                             vmem_limit_bytes=VMEM_LIMIT),
        name="expert_ffn",
    )(f_exp, f_valid, xs, wg, wu, wd)


def _combine_kernel(part_tiles, x1_ref, o1_ref, o2_ref, gate_ref, fnw_ref, *y_refs):
    i = pl.program_id(0)
    lo1, hi1 = _unpack_bf16_pairs(o1_ref[...])
    lo2, hi2 = _unpack_bf16_pairs(o2_ref[...])
    g1 = gate_ref[:, 0:1]
    g2 = gate_ref[:, 1:2]
    half = D_MODEL // 2
    x1 = x1_ref[...]
    fnw = fnw_ref[...]
    ya = x1[:, :half] + g1 * lo1 + g2 * lo2
    yb = x1[:, half:] + g1 * hi1 + g2 * hi2
    scale = lax.rsqrt((jnp.sum(ya * ya, axis=-1, keepdims=True) + jnp.sum(yb * yb, axis=-1, keepdims=True))
                      / D_MODEL + EPS)
    y = jnp.concatenate([ya * scale * fnw[:, :half], yb * scale * fnw[:, half:]], axis=1)
    lo_tile = 0
    for y_ref, tiles in zip(y_refs, part_tiles):
        @pl.when(jnp.logical_and(i >= lo_tile, i < lo_tile + tiles))
        def _(y_ref=y_ref):
            _store_part_tile(y_ref, y)
        lo_tile += tiles


def _combine(x1, o12, gates, fnw, part_shapes):
    n = x1.shape[0]
    parts = [jax.ShapeDtypeStruct(s, F32) for s in part_shapes]
    tm = _pick_tile(np.gcd.reduce([_part_rows(p) for p in parts]), TOKEN_TILES)
    n_tiles = n // tm
    out_specs, bounds = _token_parts_specs(parts, tm)
    part_tiles = [bounds[k + 1] - bounds[k] for k in range(len(parts))]
    return pl.pallas_call(
        functools.partial(_combine_kernel, part_tiles),
        grid=(n_tiles,),
        in_specs=[pl.BlockSpec((tm, D_MODEL), lambda i: (i, 0)),
                  pl.BlockSpec((tm, D_MODEL // 2), lambda i: (i, 0)),
                  pl.BlockSpec((tm, D_MODEL // 2), lambda i: (i + n_tiles, 0)),
                  pl.BlockSpec((tm, 128), lambda i: (i, 0)),
                  pl.BlockSpec((1, D_MODEL), lambda i: (0, 0))],
        out_specs=out_specs,
        out_shape=parts,
        compiler_params=pltpu.CompilerParams(dimension_semantics=("arbitrary",),
                                             vmem_limit_bytes=VMEM_LIMIT),
        name="combine",
    )(x1, o12, o12, gates, fnw)


def _moe(x, mixed, wo, nw, router_w, wg, wu, wd, fnw, part_shapes):
    i32 = jnp.int32
    n = x.shape[0]
    assert n % TOK_BLK == 0
    wrt = jnp.pad(router_w.T, ((0, 16 - N_EXPERTS), (0, 0))).astype(BF16)
    t = np.arange(TOK_BLK)
    upper = jnp.asarray((t[:, None] < t[None, :]).astype(np.float32), dtype=BF16)
    x1, hn_p, meta, count = _router(x, mixed, wo, nw, wrt, upper)

    n_slots = (-(-2 * n // FFN_TILE) + N_EXPERTS) * FFN_TILE
    counts = count[:N_EXPERTS, 0].astype(i32)
    gsize = (counts + FFN_TILE - 1) // FFN_TILE * FFN_TILE
    gend = jnp.cumsum(gsize)
    goff = gend - gsize
    fstart = jnp.arange(n_slots // FFN_TILE, dtype=i32) * FFN_TILE
    f_valid = (fstart < gend[-1]).astype(i32)
    f_exp = jnp.sum((fstart[:, None] >= gend[None, :]).astype(i32), axis=1)
    f_exp = jnp.minimum(f_exp, jnp.sum((gend[-1] - 1 >= gend).astype(i32)))
    expert_ids = jnp.arange(N_EXPERTS, dtype=i32)[:, None]
    top = meta[0:2].astype(i32)
    rank = meta[2:4].astype(i32)
    slot = jnp.stack([jnp.sum(jnp.where(top[k][None, :] == expert_ids, goff[:, None], 0), axis=0) + rank[k]
                      for k in range(2)]).reshape(-1)
    pad = gsize - counts
    cpad = jnp.cumsum(pad)
    j = jnp.arange(n_slots - 2 * n, dtype=i32)
    pad_e = jnp.sum((j[:, None] >= cpad[None, :]).astype(i32), axis=1)
    pad_base = jnp.sum(jnp.where(pad_e[:, None] == expert_ids.T, (goff + counts - (cpad - pad))[None, :], 0), axis=1)
    pad_slot = jnp.where(pad_e < N_EXPERTS, pad_base + j, gend[-1] + j - cpad[-1])

    hn_p, wg, wu, wd = lax.optimization_barrier((hn_p, wg, wu, wd))
    xs_p = _sc_row_scatter(hn_p, jnp.concatenate([slot, pad_slot]))
    out_p = _expert_ffn(f_exp, f_valid, xs_p, wg.astype(BF16), wu.astype(BF16), wd.astype(BF16))
    o12 = _sc_row_gather(out_p, slot)
    gates = _pad_lanes(meta[4:6].T, 128)
    return _combine(x1, o12, gates, fnw, part_shapes)


def _rope_table(seq_len, dec_len):
    half = ROPE_DIMS // 2
    pos = jnp.concatenate([jnp.arange(seq_len), PAST_LEN + jnp.arange(dec_len)]).astype(F32)
    inv_freq = jnp.power(jnp.float32(ROPE_THETA), -jnp.arange(half, dtype=F32) / half)
    lane = np.arange(128) % HEAD
    inv_lane = jnp.where(lane < ROPE_DIMS, inv_freq[lane % half], 0.0)
    ang = pos[:, None] * inv_lane[None, :]
    cos, sin = jnp.cos(ang), jnp.sin(ang)
    sin_a = jnp.where(lane < half, -sin, 0.0)
    sin_b = jnp.where((lane >= half) & (lane < ROPE_DIMS), sin, 0.0)
    return jnp.concatenate([cos, sin_a, sin_b], axis=1)


def _mixer_consts():
    t = np.arange(CHUNK)
    tri = (t[:, None] >= t[None, :])
    same = (t[:, None] // SUB) == (t[None, :] // SUB)
    c = np.arange(MIX_W)
    ind = (c[:, None] // HEAD) == (c[None, :] // HEAD)
    as_bf = lambda m: jnp.asarray(m.astype(np.float32), dtype=BF16)
    return [as_bf(tri), as_bf(tri & same), as_bf(same), as_bf(ind)]


def _mlstm_state_to_pairs(c, n):
    c = c.astype(F32)
    n_b = jnp.broadcast_to(n.astype(F32)[..., None], c.shape)
    z = jnp.zeros_like(c[:, 0::2])
    top = jnp.concatenate([c[:, 0::2], z, n_b[:, 0::2], z], axis=-1)
    bottom = jnp.concatenate([z, c[:, 1::2], z, n_b[:, 1::2]], axis=-1)
    return jnp.concatenate([top, bottom], axis=-2)


def _mlstm_state_from_pairs(s):
    b = s.shape[0]
    c = jnp.stack([s[:, :, a * HEAD:(a + 1) * HEAD, a * HEAD:(a + 1) * HEAD] for a in range(2)], axis=2)
    n = jnp.stack([s[:, :, a * HEAD:(a + 1) * HEAD, (2 + a) * HEAD] for a in range(2)], axis=2)
    return (c.reshape(b, N_HEADS, HEAD, HEAD), n.reshape(b, N_HEADS, HEAD))


def _pad_lanes(v, width):
    return jnp.pad(v, ((0, 0), (0, width - v.shape[1])))


def kernel(x_prompt, x_sample, state_mlstm_C, state_mlstm_n, state_mlstm_m, state_hgrn_S, cache_swa_k, cache_swa_v, state_ssd_h, state_ssd_conv, norm1_w, w_in, ml_ig_b, ml_fg_b, ml_norm_w, hg_lb_logits, hg_norm_w, sw_sinks, ssd_conv_w, ssd_conv_b, ssd_dt_bias, ssd_A_log, ssd_D, ssd_norm_w, w_out, norm2_w, ffn_w_gate, ffn_w_up, ffn_w_down, moe_router, moe_w_gate, moe_w_up, moe_w_down, final_norm_w):
    depth = w_in.shape[0]
    bp, seq_len, _ = x_prompt.shape
    bs, dec_len, _ = x_sample.shape
    assert seq_len % CHUNK == 0 and dec_len == CHUNK and depth % 2 == 0
    n_seq = bp + bs
    chunks_per_prompt = seq_len // CHUNK

    assert bp == SEQ_GROUP and bs % SEQ_GROUP == 0
    x_parts = (x_prompt.reshape(bp, chunks_per_prompt, CHUNK, D_MODEL), x_sample.reshape(bs * dec_len, D_MODEL))
    part_shapes = tuple(p.shape for p in x_parts)
    rope_tab = _rope_table(seq_len, dec_len)
    consts = _mixer_consts()

    sm = jax.nn.softmax(hg_lb_logits.astype(F32), axis=0)
    hg_lb = jnp.cumsum(sm, axis=0) - sm[0]

    def with_prompt_zeros(a):
        return jnp.concatenate([jnp.zeros((bp,) + a.shape[1:], F32), a.astype(F32)], axis=0)

    state_outs = []
    for l in range(depth):
        lp = {
            "sinks": sw_sinks[l].astype(F32),
            "gbias": _pad_lanes(jnp.concatenate([ml_ig_b[l], ml_fg_b[l], ssd_dt_bias[l]])[None, :].astype(F32), 128),
            "alog": _pad_lanes(ssd_A_log[l][None, :].astype(F32), 128),
            "mlnw": ml_norm_w[l][None, :].astype(F32),
            "hglb": hg_lb[l][None, :],
            "hgnw": hg_norm_w[l][None, :].astype(F32),
            "convw": ssd_conv_w[l].astype(F32),
            "convb": ssd_conv_b[l][None, :].astype(F32),
            "ssdd": jnp.repeat(ssd_D[l].astype(F32), HEAD)[None, :],
            "ssdnw": ssd_norm_w[l][None, :].astype(F32),
        }
        c_aug = _mlstm_state_to_pairs(state_mlstm_C[l], state_mlstm_n[l])
        m_pad = jnp.pad(state_mlstm_m[l][:, None, :], ((0, 0), (0, 7), (0, 128 - N_HEADS)))
        s_t = jnp.swapaxes(state_hgrn_S[l], -1, -2)
        s_t = jnp.moveaxis(s_t, 1, 2).reshape(bs, HEAD, MIX_W)
        cv_pad = jnp.pad(state_ssd_conv[l], ((0, 0), (8 - (SSD_CONV - 1), 0), (0, 0)))
        init = [with_prompt_zeros(a) for a in (
            c_aug, m_pad, s_t, cache_swa_k[l].reshape(bs, WINDOW, 128), cache_swa_v[l].reshape(bs, WINDOW, 128),
            state_ssd_h[l], cv_pad)]

        proj = _inproj(x_parts, norm1_w[l][None, :], w_in[l])
        mixed, c_o, m_o, s_o, k_o, v_o, h_o, cv_o = _mixer(
            proj, rope_tab, init, lp, consts, chunks_per_prompt, n_seq)
        wo = w_out[l].astype(BF16)
        j = l // 2
        if l % 2 == 0:
            x_parts = (_dense_ffn(x_parts, mixed, wo, norm2_w[l][None, :], ffn_w_gate[j].astype(BF16),
                                  ffn_w_up[j].astype(BF16), ffn_w_down[j].astype(BF16)),)
        else:
            assert l == depth - 1 and len(x_parts) == 1
            y_parts = _moe(x_parts[0], mixed, wo, norm2_w[l][None, :], moe_router[j], moe_w_gate[j],
                           moe_w_up[j], moe_w_down[j], final_norm_w[None, :], part_shapes)
        s_back = jnp.moveaxis(s_o.reshape(n_seq, HEAD, N_HEADS, HEAD), 2, 1)
        state_outs.append(_mlstm_state_from_pairs(c_o) + (m_o[:, 0, :N_HEADS],
                           jnp.swapaxes(s_back, -1, -2),
                           k_o.reshape(n_seq, WINDOW, SW_KV, HEAD), v_o.reshape(n_seq, WINDOW, SW_KV, HEAD),
                           h_o, cv_o[:, 8 - (SSD_CONV - 1):, :]))

    y_prompt = y_parts[0].reshape(bp, seq_len, D_MODEL)
    y_sample = y_parts[1].reshape(bs, dec_len, D_MODEL)
    stacked = [jnp.stack([so[k] for so in state_outs]) for k in range(8)]
    return (y_prompt, y_sample) + tuple(s[:, :bp] for s in stacked) + tuple(s[:, bp:] for s in stacked)
```

```python
import functools

import numpy as np
import jax
import jax.numpy as jnp
from jax import lax
from jax.experimental import pallas as pl
from jax.experimental.pallas import tpu as pltpu
from jax.experimental.pallas import tpu_sc as plsc

F32 = jnp.float32
BF16 = jnp.bfloat16

D_MODEL = 1024
CHUNK = 64
EPS = 1e-6
N_HEADS = 4
HEAD = 64
MIX_W = N_HEADS * HEAD
SW_KV = 2
WINDOW = 128
ROPE_DIMS = 16
ROPE_THETA = 500000.0
SSD_N = 128
SSD_CONV = 4
SSD_CONV_DIM = 768
PAST_LEN = 4096
N_EXPERTS = 8
SUB = 16
SEQ_GROUP = 2
HG_SLICE = 4
TOKEN_TILES = (512, 256, 128)

C_MLQ, C_MLK, C_MLV, C_MLO = 0, 256, 512, 768
C_HGQ, C_HGF, C_HGI, C_HGG = 1024, 1280, 1536, 1792
C_SWQ, C_SWK, C_SWV = 2048, 2304, 2432
C_SDZ, C_SDX, C_GATE = 2560, 2816, 3584
D_PROJ = 3712
NEG = -1e30
VMEM_LIMIT = 56 * 1024 * 1024
SC_CORES = 2
SC_SUBCORES = 16
SC_MAX_GATHER_ROWS = 128


def _mm(a, b):
    return jnp.dot(a.astype(BF16), b.astype(BF16), preferred_element_type=F32)


def _mm_nt(a, b):
    return lax.dot_general(a.astype(BF16), b.astype(BF16), (((1,), (1,)), ((), ())),
                           preferred_element_type=F32)


def _mm_tn(a, b):
    return jnp.dot(a.T.astype(BF16), b.astype(BF16), preferred_element_type=F32)


def _mm_sel(sel, x):
    hi = x.astype(BF16)
    r1 = x - hi.astype(F32)
    mid = r1.astype(BF16)
    lo = (r1 - mid.astype(F32)).astype(BF16)
    return (jnp.dot(sel, hi, preferred_element_type=F32)
            + jnp.dot(sel, mid, preferred_element_type=F32)
            + jnp.dot(sel, lo, preferred_element_type=F32))


def _sigmoid(x):
    return 1.0 / (1.0 + jnp.exp(-x))


def _silu(x):
    return x * _sigmoid(x)


def _softplus(x):
    return jnp.maximum(x, 0.0) + jnp.log(1.0 + jnp.exp(-jnp.abs(x)))


def _rms(x, w):
    return x * lax.rsqrt(jnp.mean(x * x, axis=-1, keepdims=True) + EPS) * w


def _pick_tile(n, candidates):
    for c in candidates:
        if n % c == 0:
            return c
    raise ValueError(f"no tile for {n}")


def _token_parts_specs(parts, tm):
    bounds = np.cumsum([0] + [_part_rows(p) // tm for p in parts])
    specs = []
    for k, p in enumerate(parts):
        assert _part_rows(p) % tm == 0
        lo, hi = int(bounds[k]), int(bounds[k + 1])
        specs.append(_part_block_spec(p, tm, lo, hi - lo))
    return specs, [int(b) for b in bounds]


def _part_rows(p):
    return int(np.prod(p.shape[:-1]))


def _part_block_spec(p, tm, first_tile, n_tiles):
    idx = lambda i: jnp.clip(i - first_tile, 0, n_tiles - 1)
    if len(p.shape) == 2:
        return pl.BlockSpec((tm, p.shape[1]), lambda i: (idx(i), 0))
    grp, _, chunk, d = p.shape
    assert tm % (grp * chunk) == 0
    return pl.BlockSpec((grp, tm // (grp * chunk), chunk, d), lambda i: (0, idx(i), 0, 0))


def _load_part_tile(ref):
    if len(ref.shape) == 2:
        return ref[...]
    grp, n_chunks = ref.shape[:2]
    return jnp.concatenate([ref[s, c] for c in range(n_chunks) for s in range(grp)], axis=0)


def _store_part_tile(ref, x):
    if len(ref.shape) == 2:
        ref[...] = x
        return
    grp, n_chunks, chunk = ref.shape[:3]
    for c in range(n_chunks):
        for s in range(grp):
            r0 = (c * grp + s) * chunk
            ref[s, c] = x[r0:r0 + chunk]


def _read_token_tile(refs, bounds):
    i = pl.program_id(0)
    x = _load_part_tile(refs[-1])
    for k in range(len(refs) - 2, -1, -1):
        x = jnp.where(i < bounds[k + 1], _load_part_tile(refs[k]), x)
    return x


def _inproj_kernel(n_parts, bounds, *refs):
    x_refs = refs[:n_parts]
    nw_ref, w_ref, o_ref, wp_ref = refs[n_parts:]

    @pl.when(pl.program_id(0) == 0)
    def _():
        rows = 128
        for r0 in range(0, D_MODEL, rows):
            w = w_ref[r0:r0 + rows, :]
            wp_ref[r0:r0 + rows, 0:C_MLO] = w[:, 0:768].astype(BF16)
            wp_ref[r0:r0 + rows, C_MLO:C_GATE] = w[:, 776:3592].astype(BF16)
            gates = jnp.concatenate([w[:, 768:776], w[:, 3592:3596],
                                     jnp.zeros((rows, D_PROJ - C_GATE - 12), F32)], axis=1)
            wp_ref[r0:r0 + rows, C_GATE:D_PROJ] = gates.astype(BF16)

    xn = _rms(_read_token_tile(x_refs, bounds), nw_ref[...]).astype(BF16)
    o_ref[...] = jnp.dot(xn, wp_ref[...], preferred_element_type=F32)


def _inproj(x_parts, nw, w):
    n = sum(_part_rows(p) for p in x_parts)
    tm = _pick_tile(np.gcd.reduce([_part_rows(p) for p in x_parts]), TOKEN_TILES)
    x_specs, bounds = _token_parts_specs(x_parts, tm)
    d_in = w.shape[1]
    return pl.pallas_call(
        functools.partial(_inproj_kernel, len(x_parts), bounds),
        grid=(n // tm,),
        in_specs=x_specs + [pl.BlockSpec((1, D_MODEL), lambda i: (0, 0)),
                            pl.BlockSpec((D_MODEL, d_in), lambda i: (0, 0), pipeline_mode=pl.Buffered(1))],
        out_specs=pl.BlockSpec((tm, D_PROJ), lambda i: (i, 0)),
        out_shape=jax.ShapeDtypeStruct((n, D_PROJ), F32),
        scratch_shapes=[pltpu.VMEM((D_MODEL, D_PROJ), BF16)],
        compiler_params=pltpu.CompilerParams(dimension_semantics=("arbitrary",),
                                             vmem_limit_bytes=VMEM_LIMIT),
        name="inproj",
    )(*x_parts, nw, w)


def _head_norm(x, width):
    parts = []
    for g in range(x.shape[1] // width):
        xg = x[:, g * width:(g + 1) * width]
        parts.append(xg * lax.rsqrt(jnp.mean(xg * xg, axis=-1, keepdims=True) + EPS))
    return jnp.concatenate(parts, axis=1)


def _mixer_kernel(prompt_steps,
                  sinks_ref,
                  proj_ref, rope_ref, c0_ref, m0_ref, s0_ref, k0_ref, v0_ref, h0_ref, cv0_ref,
                  gbias_ref, alog_ref, mlnw_ref, hglb_ref, hgnw_ref,
                  convw_ref, convb_ref, ssdd_ref, ssdnw_ref,
                  tri_ref, triblk_ref, blkones_ref, ind_ref,
                  mixed_ref, c_ref, m_ref, s_ref, k_ref, v_ref, h_ref, cv_ref,
                  convbuf, bloc_s, kk_s, hv_s):
    i = pl.program_id(0)
    is_prompt = i < prompt_steps
    is_first = jnp.logical_or(jnp.logical_not(is_prompt), i == 0)
    n_valid = jnp.where(is_prompt, jnp.minimum(i * CHUNK, WINDOW), WINDOW)

    @pl.when(is_first)
    def _():
        c_ref[...] = c0_ref[...]
        m_ref[...] = m0_ref[...]
        s_ref[...] = s0_ref[...]
        k_ref[...] = k0_ref[...]
        v_ref[...] = v0_ref[...]
        h_ref[...] = h0_ref[...]
        cv_ref[...] = cv0_ref[...]

    members = [_mixer_chunk(g, n_valid, sinks_ref, proj_ref, rope_ref,
                            gbias_ref, alog_ref, mlnw_ref, hglb_ref, hgnw_ref,
                            convw_ref, convb_ref, ssdd_ref, ssdnw_ref,
                            tri_ref, triblk_ref, blkones_ref, ind_ref,
                            mixed_ref, c_ref, m_ref, s_ref, k_ref, v_ref, h_ref, cv_ref,
                            convbuf, bloc_s, kk_s, hv_s) for g in range(SEQ_GROUP)]
    for phase in zip(*[m[0] for m in members]):
        chains = [c for per_member in zip(*phase) for c in per_member]
        while chains:
            alive = []
            for chain in chains:
                if next(chain, _CHAIN_DONE) is not _CHAIN_DONE:
                    alive.append(chain)
            chains = alive
    for _, finish in members:
        finish()


_CHAIN_DONE = object()


def _mixer_chunk(g, n_valid, sinks_ref, proj_ref, rope_ref,
                 gbias_ref, alog_ref, mlnw_ref, hglb_ref, hgnw_ref,
                 convw_ref, convb_ref, ssdd_ref, ssdnw_ref,
                 tri_ref, triblk_ref, blkones_ref, ind_ref,
                 mixed_ref, c_ref, m_ref, s_ref, k_ref, v_ref, h_ref, cv_ref,
                 convbuf, bloc_s, kk_s, hv_s):
    rs = slice(g * CHUNK, (g + 1) * CHUNK)
    L = CHUNK
    row = lax.broadcasted_iota(jnp.int32, (L, L), 0)
    col = lax.broadcasted_iota(jnp.int32, (L, L), 1)
    causal = row >= col
    tri = tri_ref[...]

    gate = proj_ref[rs,C_GATE:C_GATE + 128] + gbias_ref[...]
    ig = gate
    lf = pltpu.roll(-_softplus(-gate), 124, 1)
    dtv = pltpu.roll(_softplus(gate), 120, 1)

    b = _mm_sel(tri, lf)
    a = ig - b
    cm = a
    rowg = lax.broadcasted_iota(jnp.int32, (L, 128), 0)
    for sh in (1, 2, 4, 8, 16, 32):
        cm = jnp.where(rowg >= sh, jnp.maximum(cm, pltpu.roll(cm, sh, 0)), cm)
    m_prev = m_ref[g, 0:1, :]
    m_t = b + jnp.maximum(m_prev, cm)
    inter = jnp.exp(b + m_prev - m_t)
    bm = b - m_t
    a_t = a.T
    m_last = m_t[L - 1:L, :]
    b_last = b[L - 1:L, :]
    ws = jnp.exp(b_last + a - m_last)
    decay = jnp.exp(b_last + m_prev - m_last)
    emt = jnp.exp(-m_t)
    m_ref[g, 0:1, :] = m_last

    lane2 = lax.broadcasted_iota(jnp.int32, (L, 2 * HEAD), 1)
    low = lane2 < HEAD
    causal2 = lax.broadcasted_iota(jnp.int32, (L, 2 * HEAD), 0) >= (lane2 & (HEAD - 1))
    ones_blk = ind_ref[0:2 * HEAD, 0:2 * HEAD]
    ones_rows = jnp.ones((L, 2 * HEAD), BF16)
    st_row = lax.broadcasted_iota(jnp.int32, (2 * HEAD, 4 * HEAD), 0) // HEAD
    st_lane = (lax.broadcasted_iota(jnp.int32, (2 * HEAD, 4 * HEAD), 1) // HEAD) % 2
    ml_out = [None] * (N_HEADS // 2)

    def pair_cols(x, p):
        return jnp.where(low, x[:, 2 * p:2 * p + 1], x[:, 2 * p + 1:2 * p + 2])

    def split_rows(x):
        return jnp.concatenate([jnp.where(low, x, 0.0), jnp.where(low, 0.0, x)], axis=0)

    def ml_pair(p):
        cs = slice(2 * p * HEAD, 2 * (p + 1) * HEAD)
        q = proj_ref[rs, C_MLQ + cs.start:C_MLQ + cs.stop]
        k = proj_ref[rs, C_MLK + cs.start:C_MLK + cs.stop] * (HEAD ** -0.5)
        v = proj_ref[rs, C_MLV + cs.start:C_MLV + cs.stop]
        qk = _mm_nt(q, split_rows(k))
        yield
        st = c_ref[g, p]
        from_state = _mm(q, st)
        yield
        a_row = jnp.concatenate([a_t[2 * p:2 * p + 1, :], a_t[2 * p + 1:2 * p + 2, :]], axis=1)
        w = jnp.where(causal2, jnp.exp(pair_cols(bm, p) + a_row), 0.0)
        rhs = jnp.concatenate([split_rows(v).astype(BF16), ones_blk], axis=1)
        gate_in = pair_cols(inter, p)
        nd = (jnp.dot((w * qk).astype(BF16), rhs, preferred_element_type=F32)
              + jnp.concatenate([gate_in, gate_in], axis=1) * from_state)
        yield
        kws = k * pair_cols(ws, p)
        upd = jnp.dot(kws.T.astype(BF16), jnp.concatenate([v.astype(BF16), ones_rows], axis=1),
                      preferred_element_type=F32)
        yield
        den = jnp.maximum(jnp.abs(nd[:, 2 * HEAD:]), pair_cols(emt, p))
        ml_out[p] = nd[:, :2 * HEAD] / den
        dec = jnp.where(st_row == 0, decay[:, 2 * p:2 * p + 1], decay[:, 2 * p + 1:2 * p + 2])
        c_ref[g, p] = dec * st + jnp.where(st_row == st_lane, upd, 0.0)

    def ml_finish():
        ml = _head_norm(jnp.concatenate(ml_out, axis=1), HEAD)
        mixed_ref[rs,0:MIX_W] = (_sigmoid(proj_ref[rs,C_MLO:C_MLO + MIX_W]) * ml
                                 * mlnw_ref[...]).astype(mixed_ref.dtype)

    lb = hglb_ref[...]
    fg = lb + (1.0 - lb) * _sigmoid(proj_ref[rs,C_HGF:C_HGF + MIX_W])
    hq = _silu(proj_ref[rs,C_HGQ:C_HGQ + MIX_W]) * (HEAD ** -0.5)
    kk = 1.0 - fg
    lfg = jnp.log(fg)
    hv = proj_ref[rs,C_HGI:C_HGI + MIX_W]
    bloc = _mm_sel(triblk_ref[...], lfg)
    blast = _mm_sel(blkones_ref[...], lfg)
    qe = hq * jnp.exp(bloc)
    kw = kk * jnp.exp(blast - bloc)
    bloc_s[g] = bloc
    kk_s[g] = kk
    hv_s[g] = hv
    ind = ind_ref[...]
    rows = lax.broadcasted_iota(jnp.int32, (SUB, MIX_W), 0)
    n_blk = L // SUB
    hg_state_part = [None] * n_blk
    hg_block_part = [None] * n_blk

    def hg_within_block(blk):
        r0 = blk * SUB
        q_blk = hq[r0:r0 + SUB]
        b_blk = bloc[r0:r0 + SUB]
        xs = []
        for s in range(SUB):
            r = r0 + s
            e = jnp.exp(jnp.minimum(b_blk - bloc_s[g, r:r + 1, :], 0.0))
            xs.append((q_blk * (kk_s[g, r:r + 1, :] * e)).astype(BF16))
            if s % HG_SLICE == HG_SLICE - 1:
                yield
        att = jnp.dot(jnp.concatenate(xs, axis=0), ind, preferred_element_type=F32)
        yield
        o_blk = att[0:SUB] * hv_s[g, r0:r0 + 1, :]
        for s in range(1, SUB):
            r = r0 + s
            o_blk = o_blk + jnp.where(rows >= s, att[s * SUB:(s + 1) * SUB], 0.0) * hv_s[g, r:r + 1, :]
            if s % HG_SLICE == HG_SLICE - 1:
                yield
        hg_block_part[blk] = o_blk

    def hg_state_chain():
        st = s_ref[g]
        for blk in range(n_blk):
            r0 = blk * SUB
            hg_state_part[blk] = jnp.concatenate(
                [_mm_nt(qe[r0:r0 + SUB, h * HEAD:(h + 1) * HEAD], st[:, h * HEAD:(h + 1) * HEAD])
                 for h in range(N_HEADS)], axis=1)
            upd = jnp.concatenate(
                [_mm_tn(hv[r0:r0 + SUB, h * HEAD:(h + 1) * HEAD], kw[r0:r0 + SUB, h * HEAD:(h + 1) * HEAD])
                 for h in range(N_HEADS)], axis=1)
            yield
            st = st * jnp.exp(blast[r0:r0 + 1, :]) + upd
        s_ref[g] = st

    def hg_finish():
        o = jnp.concatenate([hg_state_part[blk] + hg_block_part[blk] for blk in range(n_blk)], axis=0)
        mixed_ref[rs,MIX_W:2 * MIX_W] = (_head_norm(o, HEAD) * hgnw_ref[...]
                                         * _silu(proj_ref[rs,C_HGG:C_HGG + MIX_W])).astype(mixed_ref.dtype)

    @functools.lru_cache(maxsize=None)
    def sw_prep():
        cos = rope_ref[:, 0:128]
        sin_a = rope_ref[:, 128:256]
        sin_b = rope_ref[:, 256:384]

        def rope(x):
            return x * cos + pltpu.roll(x, 120, 1) * sin_a + pltpu.roll(x, 8, 1) * sin_b

        q_rot = [rope(proj_ref[rs,C_SWQ:C_SWQ + 128]), rope(proj_ref[rs,C_SWQ + 128:C_SWQ + 256])]
        k_rot = rope(proj_ref[rs,C_SWK:C_SWK + 128])
        v_cur = proj_ref[rs,C_SWV:C_SWV + 128]
        colw = lax.broadcasted_iota(jnp.int32, (2 * L, WINDOW), 1)
        row2 = lax.broadcasted_iota(jnp.int32, (2 * L, 1), 0)
        return q_rot, k_rot, v_cur, k_ref[g], v_ref[g], colw >= WINDOW - n_valid, row2

    sw_prep()
    sw_out = [None] * (2 * SW_KV)

    def sw_group(kv):
        q_rot, k_rot, v_cur, k_prev, v_prev, prev_ok, row2 = sw_prep()
        gs = slice(kv * HEAD, (kv + 1) * HEAD)
        q2 = jnp.concatenate([q_rot[kv][:, 0:HEAD], q_rot[kv][:, HEAD:2 * HEAD]], axis=0) * (HEAD ** -0.5)
        s_p = jnp.where(prev_ok, _mm_nt(q2, k_prev[:, gs]), NEG)
        yield
        s_c = _mm_nt(q2, k_rot[:, gs])
        yield
        sink = jnp.where(row2 < L, sinks_ref[2 * kv], sinks_ref[2 * kv + 1])
        mx = jnp.maximum(jnp.maximum(jnp.max(s_p, axis=-1, keepdims=True),
                                     jnp.max(s_c, axis=-1, keepdims=True)), sink)
        e_p = jnp.exp(s_p - mx)
        e_c = jnp.exp(s_c - mx)
        den = (jnp.sum(e_p, axis=-1, keepdims=True) + jnp.sum(e_c, axis=-1, keepdims=True)
               + jnp.exp(sink - mx))
        o_p = _mm(e_p, v_prev[:, gs])
        yield
        o = (o_p + _mm(e_c, v_cur[:, gs])) / den
        yield
        sw_out[2 * kv] = o[0:L]
        sw_out[2 * kv + 1] = o[L:2 * L]

    def sw_finish():
        _, k_rot, v_cur, k_prev, v_prev, _, _ = sw_prep()
        mixed_ref[rs,2 * MIX_W:3 * MIX_W] = jnp.concatenate(sw_out, axis=1).astype(mixed_ref.dtype)
        k_ref[g, 0:WINDOW - L, :] = k_prev[L:WINDOW]
        k_ref[g, WINDOW - L:WINDOW, :] = k_rot
        v_ref[g, 0:WINDOW - L, :] = v_prev[L:WINDOW]
        v_ref[g, WINDOW - L:WINDOW, :] = v_cur

    @functools.lru_cache(maxsize=None)
    def sd_prep():
        convbuf[g, 0:8, :] = cv_ref[g]
        convbuf[g, 8:8 + L, :] = proj_ref[rs,C_SDX:C_SDX + SSD_CONV_DIM]
        acc = convb_ref[...] + convbuf[g, 5:5 + L, :] * convw_ref[0:1, :]
        for j in range(1, SSD_CONV):
            acc = acc + convbuf[g, 5 + j:5 + j + L, :] * convw_ref[j:j + 1, :]
        xbc = _silu(acc)
        cv_ref[g] = convbuf[g, L:L + 8, :]
        a_neg = -jnp.exp(alog_ref[...])
        ca = _mm_sel(tri, dtv * a_neg)
        ca_last = ca[L - 1:L, :]
        return (xbc, xbc[:, 0:MIX_W], ca, ca.T, dtv.T, jnp.exp(ca), jnp.exp(ca_last - ca) * dtv,
                jnp.exp(ca_last))

    sd_prep()
    sd_out = [None] * (N_HEADS // 2)

    def sd_group(grp):
        xbc, xs_all, ca, ca_t, dt_t, eca, wss, sdec = sd_prep()
        bm_g = xbc[:, MIX_W + grp * SSD_N:MIX_W + (grp + 1) * SSD_N]
        cm_g = xbc[:, MIX_W + 2 * SSD_N + grp * SSD_N:MIX_W + 2 * SSD_N + (grp + 1) * SSD_N]
        e, o = 2 * grp, 2 * grp + 1
        x = xs_all[:, e * HEAD:(o + 1) * HEAD]
        cb2 = _mm_nt(cm_g, jnp.concatenate([bm_g, bm_g], axis=0))
        yield
        hst = h_ref[g, e:o + 1].reshape(2 * HEAD, SSD_N)
        from_state = _mm_nt(cm_g, hst)
        yield
        ca_row = jnp.concatenate([ca_t[e:e + 1, :], ca_t[o:o + 1, :]], axis=1)
        dt_row = jnp.concatenate([dt_t[e:e + 1, :], dt_t[o:o + 1, :]], axis=1)
        dec = jnp.where(causal2, jnp.exp(pair_cols(ca, grp) - ca_row), 0.0)
        sd_out[grp] = _mm(cb2 * dec * dt_row, split_rows(x)) + pair_cols(eca, grp) * from_state
        yield
        upd = _mm_tn(x * pair_cols(wss, grp), bm_g)
        yield
        keep = jnp.where(lax.broadcasted_iota(jnp.int32, (2 * HEAD, SSD_N), 0) < HEAD,
                         sdec[:, e:e + 1], sdec[:, o:o + 1])
        h_ref[g, e:o + 1] = (keep * hst + upd).reshape(2, HEAD, SSD_N)

    def sd_finish():
        xs_all = sd_prep()[1]
        y = jnp.concatenate(sd_out, axis=1) + ssdd_ref[...] * xs_all
        gated = y * _silu(proj_ref[rs,C_SDZ:C_SDZ + MIX_W])
        mixed_ref[rs,3 * MIX_W:4 * MIX_W] = (_head_norm(gated, 2 * HEAD)
                                             * ssdnw_ref[...]).astype(mixed_ref.dtype)

    chains = ([[hg_within_block(blk) for blk in range(n_blk)]]
              + [[ml_pair(p)] for p in range(N_HEADS // 2)] + [[hg_state_chain(), sd_group(0)]]
              + [[sd_group(1)]]
              + [[sw_group(kv)] for kv in range(SW_KV)])

    def finish():
        ml_finish()
        hg_finish()
        sw_finish()
        sd_finish()

    return chains, finish


def _mixer(proj, rope_tab, init, lp, consts, chunks_per_prompt, n_seq):
    n = proj.shape[0]
    rows = SEQ_GROUP * CHUNK
    n_steps = n // rows

    def group_of(i):
        return jnp.where(i < chunks_per_prompt, 0, i - chunks_per_prompt + 1)

    def rope_blk(i):
        return jnp.minimum(i, chunks_per_prompt)

    def per_seq(shape):
        nd = len(shape)
        return pl.BlockSpec((SEQ_GROUP,) + shape, lambda i, s: (group_of(i),) + (0,) * nd)

    def const(shape):
        nd = len(shape)
        return pl.BlockSpec(shape, lambda i, s: (0,) * nd)

    state_shapes = [(N_HEADS // 2, 2 * HEAD, 4 * HEAD), (8, 128), (HEAD, MIX_W), (WINDOW, 128), (WINDOW, 128),
                    (N_HEADS, HEAD, SSD_N), (8, SSD_CONV_DIM)]
    in_specs = ([pl.BlockSpec((rows, D_PROJ), lambda i, s: (i, 0)),
                 pl.BlockSpec((CHUNK, 384), lambda i, s: (rope_blk(i), 0))]
                + [per_seq(s) for s in state_shapes]
                + [const((1, 128)), const((1, 128)), const((1, MIX_W)), const((1, MIX_W)), const((1, MIX_W)),
                   const((SSD_CONV, SSD_CONV_DIM)), const((1, SSD_CONV_DIM)), const((1, MIX_W)),
                   const((1, MIX_W)),
                   const((CHUNK, CHUNK)), const((CHUNK, CHUNK)), const((CHUNK, CHUNK)), const((MIX_W, MIX_W))])
    out_specs = ([pl.BlockSpec((rows, D_MODEL), lambda i, s: (i, 0))]
                 + [per_seq(s) for s in state_shapes])
    out_shape = ([jax.ShapeDtypeStruct((n, D_MODEL), BF16)]
                 + [jax.ShapeDtypeStruct((n_seq,) + s, F32) for s in state_shapes])
    grid_spec = pltpu.PrefetchScalarGridSpec(
        num_scalar_prefetch=1, grid=(n_steps,), in_specs=in_specs, out_specs=out_specs,
        scratch_shapes=[pltpu.VMEM((SEQ_GROUP, CHUNK + 8, SSD_CONV_DIM), F32),
                        pltpu.VMEM((SEQ_GROUP, CHUNK, MIX_W), F32), pltpu.VMEM((SEQ_GROUP, CHUNK, MIX_W), F32),
                        pltpu.VMEM((SEQ_GROUP, CHUNK, MIX_W), F32)])
    return pl.pallas_call(
        functools.partial(_mixer_kernel, chunks_per_prompt),
        grid_spec=grid_spec, out_shape=out_shape,
        compiler_params=pltpu.CompilerParams(dimension_semantics=("arbitrary",),
                                             vmem_limit_bytes=VMEM_LIMIT),
        name="mixer",
    )(lp["sinks"], proj, rope_tab, *init,
      lp["gbias"], lp["alog"], lp["mlnw"], lp["hglb"], lp["hgnw"],
      lp["convw"], lp["convb"], lp["ssdd"], lp["ssdnw"], *consts)


def _dense_ffn_kernel(f_chunk, n_parts, bounds, *refs):
    x_refs = refs[:n_parts]
    mix_ref, wo_ref, nw_ref, wg_ref, wu_ref, wd_ref, o_ref = refs[n_parts:]
    x1 = _read_token_tile(x_refs, bounds) + jnp.dot(mix_ref[...], wo_ref[...], preferred_element_type=F32)
    hn = _rms(x1, nw_ref[...]).astype(BF16)
    o_ref[...] = x1
    for f0 in range(0, wg_ref.shape[1], f_chunk):
        g = jnp.dot(hn, wg_ref[:, f0:f0 + f_chunk], preferred_element_type=F32)
        u = jnp.dot(hn, wu_ref[:, f0:f0 + f_chunk], preferred_element_type=F32)
        act = (_silu(g) * u).astype(BF16)
        o_ref[...] += jnp.dot(act, wd_ref[f0:f0 + f_chunk, :], preferred_element_type=F32)


def _dense_ffn(x_parts, mixed, wo, nw, wg, wu, wd):
    n = mixed.shape[0]
    d_ff = wg.shape[1]
    tm = _pick_tile(np.gcd.reduce([_part_rows(p) for p in x_parts]), TOKEN_TILES)
    x_specs, bounds = _token_parts_specs(x_parts, tm)
    full = lambda shape: pl.BlockSpec(shape, lambda i: (0, 0))
    return pl.pallas_call(
        functools.partial(_dense_ffn_kernel, 256, len(x_parts), bounds),
        grid=(n // tm,),
        in_specs=x_specs + [
                  pl.BlockSpec((tm, D_MODEL), lambda i: (i, 0)),
                  full((D_MODEL, D_MODEL)), full((1, D_MODEL)),
                  full((D_MODEL, d_ff)), full((D_MODEL, d_ff)), full((d_ff, D_MODEL))],
        out_specs=pl.BlockSpec((tm, D_MODEL), lambda i: (i, 0)),
        out_shape=jax.ShapeDtypeStruct((n, D_MODEL), F32),
        compiler_params=pltpu.CompilerParams(dimension_semantics=("arbitrary",),
                                             vmem_limit_bytes=VMEM_LIMIT),
        name="dense_ffn",
    )(*x_parts, mixed, wo, nw, wg, wu, wd)


TOK_BLK = 256
FFN_TILE = 2048
FFN_COLS = 256


def _pack_bf16_pairs(x):
    w = x.shape[1] // 2
    bits = pltpu.bitcast(x.astype(BF16).astype(F32), jnp.int32)
    return lax.shift_right_logical(bits[:, :w], 16) | bits[:, w:]


def _unpack_bf16_pairs(p):
    lo = pltpu.bitcast(lax.shift_left(p, 16), F32)
    hi = pltpu.bitcast(p & jnp.int32(-65536), F32)
    return lo, hi


def _router_kernel(x_ref, mix_ref, wo_ref, nw_ref, wrt_ref, upper_ref,
                   x1_ref, hn_ref, meta_ref, count_ref, carry):
    @pl.when(pl.program_id(0) == 0)
    def _():
        carry[...] = jnp.zeros_like(carry)

    x1 = x_ref[...] + jnp.dot(mix_ref[...], wo_ref[...], preferred_element_type=F32)
    x1_ref[...] = x1
    hn_f = _rms(x1, nw_ref[...])
    hn = hn_f.astype(BF16)
    hn_ref[...] = _pack_bf16_pairs(hn_f)
    logits = lax.dot_general(wrt_ref[...], hn, (((1,), (1,)), ((), ())),
                             preferred_element_type=F32)
    sub = lax.broadcasted_iota(jnp.int32, logits.shape, 0)
    l1 = jnp.where(sub < N_EXPERTS, logits, NEG)
    m1 = jnp.max(l1, axis=0, keepdims=True)
    i1 = jnp.min(jnp.where(l1 == m1, sub, 99), axis=0, keepdims=True)
    l2 = jnp.where(sub == i1, NEG, l1)
    m2 = jnp.max(l2, axis=0, keepdims=True)
    i2 = jnp.min(jnp.where(l2 == m2, sub, 99), axis=0, keepdims=True)
    e2 = jnp.exp(m2 - m1)
    g1 = 1.0 / (1.0 + e2)
    g2 = e2 / (1.0 + e2)
    sel = jnp.where(sub == i1, 1.0, jnp.where(sub == i2, 1.0, 0.0))
    before = jnp.dot(sel.astype(BF16), upper_ref[...], preferred_element_type=F32) + carry[:, 0:1]
    r1 = jnp.sum(jnp.where(sub == i1, before, 0.0), axis=0, keepdims=True)
    r2 = jnp.sum(jnp.where(sub == i2, before, 0.0), axis=0, keepdims=True)
    meta_ref[...] = jnp.concatenate([i1.astype(F32), i2.astype(F32), r1, r2, g1, g2,
                                     jnp.zeros((2, r1.shape[1]), F32)], axis=0)
    total = carry[...] + jnp.sum(sel, axis=1, keepdims=True)
    carry[...] = total
    count_ref[...] = total


def _router(x, mixed, wo, nw, wrt, upper):
    n = x.shape[0]
    tm = TOK_BLK
    full = lambda shape: pl.BlockSpec(shape, lambda i: (0,) * len(shape))
    tok = lambda w: pl.BlockSpec((tm, w), lambda i: (i, 0))
    return pl.pallas_call(
        _router_kernel,
        grid=(n // tm,),
        in_specs=[tok(D_MODEL), tok(D_MODEL), full((D_MODEL, D_MODEL)), full((1, D_MODEL)),
                  full((16, D_MODEL)), full((tm, tm))],
        out_specs=[tok(D_MODEL), tok(D_MODEL // 2),
                   pl.BlockSpec((8, tm), lambda i: (0, i)), full((16, 128))],
        out_shape=[jax.ShapeDtypeStruct((n, D_MODEL), F32), jax.ShapeDtypeStruct((n, D_MODEL // 2), jnp.int32),
                   jax.ShapeDtypeStruct((8, n), F32), jax.ShapeDtypeStruct((16, 128), F32)],
        scratch_shapes=[pltpu.VMEM((16, 128), F32)],
        compiler_params=pltpu.CompilerParams(dimension_semantics=("arbitrary",),
                                             vmem_limit_bytes=VMEM_LIMIT),
        name="router",
    )(x, mixed, wo, nw, wrt, upper)


def _sc_row_gather(table, idx):
    n_workers = SC_CORES * SC_SUBCORES
    b = idx.shape[0]
    d = table.shape[1]
    assert b % (8 * n_workers) == 0
    per_worker = b // n_workers
    chunk = max(c for c in range(8, SC_MAX_GATHER_ROWS + 1, 8) if per_worker % c == 0)
    mesh = plsc.VectorSubcoreMesh(core_axis_name="c", subcore_axis_name="s")

    @functools.partial(
        pl.kernel, mesh=mesh, out_type=jax.ShapeDtypeStruct((b, d), table.dtype),
        scratch_types=[pltpu.VMEM((chunk,), jnp.int32), pltpu.VMEM((chunk, d), table.dtype),
                       pltpu.SemaphoreType.DMA])
    def gather(table_hbm, idx_hbm, out_hbm, idx_v, rows_v, sem):
        worker = lax.axis_index("s") * SC_CORES + lax.axis_index("c")
        base = worker * per_worker

        @pl.loop(0, per_worker // chunk)
        def _(c):
            off = pl.multiple_of(base + c * chunk, 8)
            pltpu.sync_copy(idx_hbm.at[pl.ds(off, chunk)], idx_v)
            pltpu.async_copy(table_hbm.at[idx_v], rows_v, sem).wait()
            pltpu.sync_copy(rows_v, out_hbm.at[pl.ds(off, chunk)])

    return gather(table, idx)


def _sc_row_scatter(rows, idx):
    n_workers = SC_CORES * SC_SUBCORES
    b = n_out = idx.shape[0]
    v, d = rows.shape
    assert b % (8 * n_workers) == 0
    per_worker = b // n_workers
    chunk = max(c for c in range(8, SC_MAX_GATHER_ROWS + 1, 8) if per_worker % c == 0 and v % c == 0)
    mesh = plsc.VectorSubcoreMesh(core_axis_name="c", subcore_axis_name="s")

    @functools.partial(
        pl.kernel, mesh=mesh, out_type=jax.ShapeDtypeStruct((n_out, d), rows.dtype),
        scratch_types=[pltpu.VMEM((chunk,), jnp.int32), pltpu.VMEM((chunk, d), rows.dtype)])
    def scatter(rows_hbm, idx_hbm, out_hbm, idx_v, rows_v):
        worker = lax.axis_index("s") * SC_CORES + lax.axis_index("c")
        base = worker * per_worker

        @pl.loop(0, per_worker // chunk)
        def _(c):
            off = pl.multiple_of(base + c * chunk, 8)
            pltpu.sync_copy(idx_hbm.at[pl.ds(off, chunk)], idx_v)
            pltpu.sync_copy(rows_hbm.at[pl.ds(pl.multiple_of(lax.rem(off, v), 8), chunk)], rows_v)
            pltpu.sync_copy(rows_v, out_hbm.at[idx_v])

    return scatter(rows, idx)


def _expert_ffn_kernel(texp_ref, tvalid_ref, xs_ref, wg_ref, wu_ref, wd_ref, o_ref, acc_ref, xb_ref):
    j = pl.program_id(0)
    f = pl.program_id(1)
    last = f == pl.num_programs(1) - 1
    valid = tvalid_ref[j] == 1
    half = D_MODEL // 2

    @pl.when(jnp.logical_and(valid, f == 0))
    def _():
        lo, hi = _unpack_bf16_pairs(xs_ref[...])
        xb_ref[:, :half] = lo.astype(BF16)
        xb_ref[:, half:] = hi.astype(BF16)

        acc_ref[...] = jnp.zeros_like(acc_ref)

    @pl.when(valid)
    def _():
        xs = xb_ref[...]
        g = jnp.dot(xs, wg_ref[0].astype(BF16), preferred_element_type=F32)
        u = jnp.dot(xs, wu_ref[0].astype(BF16), preferred_element_type=F32)
        act = (_silu(g) * u).astype(BF16)
        acc_ref[...] += jnp.dot(act, wd_ref[0].astype(BF16), preferred_element_type=F32)

        @pl.when(last)
        def _():
            o_ref[...] = _pack_bf16_pairs(acc_ref[...])

    @pl.when(jnp.logical_and(jnp.logical_not(valid), last))
    def _():
        o_ref[...] = jnp.zeros_like(o_ref)


def _expert_ffn(f_exp, f_valid, xs, wg, wu, wd):
    n_slots = xs.shape[0]
    d_ff = wg.shape[2]
    tf = _pick_tile(d_ff, (FFN_COLS, 128))
    grid_spec = pltpu.PrefetchScalarGridSpec(
        num_scalar_prefetch=2, grid=(n_slots // FFN_TILE, d_ff // tf),
        in_specs=[pl.BlockSpec((FFN_TILE, D_MODEL // 2), lambda j, f, te, tv: (j, 0)),
                  pl.BlockSpec((1, D_MODEL, tf), lambda j, f, te, tv: (te[j], 0, f)),
                  pl.BlockSpec((1, D_MODEL, tf), lambda j, f, te, tv: (te[j], 0, f)),
                  pl.BlockSpec((1, tf, D_MODEL), lambda j, f, te, tv: (te[j], f, 0))],
        out_specs=pl.BlockSpec((FFN_TILE, D_MODEL // 2), lambda j, f, te, tv: (j, 0)),
        scratch_shapes=[pltpu.VMEM((FFN_TILE, D_MODEL), F32), pltpu.VMEM((FFN_TILE, D_MODEL), BF16)])
    return pl.pallas_call(
        _expert_ffn_kernel, grid_spec=grid_spec,
        out_shape=jax.ShapeDtypeStruct((n_slots, D_MODEL // 2), jnp.int32),
        compiler_params=pltpu.CompilerParams(dimension_semantics=("arbitrary", "arbitrary"),
                                             vmem_limit_bytes=VMEM_LIMIT),
        name="expert_ffn",
    )(f_exp, f_valid, xs, wg, wu, wd)


def _combine_kernel(part_tiles, x1_ref, o1_ref, o2_ref, gate_ref, fnw_ref, *y_refs):
    i = pl.program_id(0)
    lo1, hi1 = _unpack_bf16_pairs(o1_ref[...])
    lo2, hi2 = _unpack_bf16_pairs(o2_ref[...])
    g1 = gate_ref[:, 0:1]
    g2 = gate_ref[:, 1:2]
    half = D_MODEL // 2
    x1 = x1_ref[...]
    fnw = fnw_ref[...]
    ya = x1[:, :half] + g1 * lo1 + g2 * lo2
    yb = x1[:, half:] + g1 * hi1 + g2 * hi2
    scale = lax.rsqrt((jnp.sum(ya * ya, axis=-1, keepdims=True) + jnp.sum(yb * yb, axis=-1, keepdims=True))
                      / D_MODEL + EPS)
    y = jnp.concatenate([ya * scale * fnw[:, :half], yb * scale * fnw[:, half:]], axis=1)
    lo_tile = 0
    for y_ref, tiles in zip(y_refs, part_tiles):
        @pl.when(jnp.logical_and(i >= lo_tile, i < lo_tile + tiles))
        def _(y_ref=y_ref):
            _store_part_tile(y_ref, y)
        lo_tile += tiles


def _combine(x1, o12, gates, fnw, part_shapes):
    n = x1.shape[0]
    parts = [jax.ShapeDtypeStruct(s, F32) for s in part_shapes]
    tm = _pick_tile(np.gcd.reduce([_part_rows(p) for p in parts]), TOKEN_TILES)
    n_tiles = n // tm
    out_specs, bounds = _token_parts_specs(parts, tm)
    part_tiles = [bounds[k + 1] - bounds[k] for k in range(len(parts))]
    return pl.pallas_call(
        functools.partial(_combine_kernel, part_tiles),
        grid=(n_tiles,),
        in_specs=[pl.BlockSpec((tm, D_MODEL), lambda i: (i, 0)),
                  pl.BlockSpec((tm, D_MODEL // 2), lambda i: (i, 0)),
                  pl.BlockSpec((tm, D_MODEL // 2), lambda i: (i + n_tiles, 0)),
                  pl.BlockSpec((tm, 128), lambda i: (i, 0)),
                  pl.BlockSpec((1, D_MODEL), lambda i: (0, 0))],
        out_specs=out_specs,
        out_shape=parts,
        compiler_params=pltpu.CompilerParams(dimension_semantics=("arbitrary",),
                                             vmem_limit_bytes=VMEM_LIMIT),
        name="combine",
    )(x1, o12, o12, gates, fnw)


def _moe(x, mixed, wo, nw, router_w, wg, wu, wd, fnw, part_shapes):
    i32 = jnp.int32
    n = x.shape[0]
    assert n % TOK_BLK == 0
    wrt = jnp.pad(router_w.T, ((0, 16 - N_EXPERTS), (0, 0))).astype(BF16)
    t = np.arange(TOK_BLK)
    upper = jnp.asarray((t[:, None] < t[None, :]).astype(np.float32), dtype=BF16)
    x1, hn_p, meta, count = _router(x, mixed, wo, nw, wrt, upper)

    n_slots = (-(-2 * n // FFN_TILE) + N_EXPERTS) * FFN_TILE
    counts = count[:N_EXPERTS, 0].astype(i32)
    gsize = (counts + FFN_TILE - 1) // FFN_TILE * FFN_TILE
    gend = jnp.cumsum(gsize)
    goff = gend - gsize
    fstart = jnp.arange(n_slots // FFN_TILE, dtype=i32) * FFN_TILE
    f_valid = (fstart < gend[-1]).astype(i32)
    f_exp = jnp.sum((fstart[:, None] >= gend[None, :]).astype(i32), axis=1)
    f_exp = jnp.minimum(f_exp, jnp.sum((gend[-1] - 1 >= gend).astype(i32)))
    expert_ids = jnp.arange(N_EXPERTS, dtype=i32)[:, None]
    top = meta[0:2].astype(i32)
    rank = meta[2:4].astype(i32)
    slot = jnp.stack([jnp.sum(jnp.where(top[k][None, :] == expert_ids, goff[:, None], 0), axis=0) + rank[k]
                      for k in range(2)]).reshape(-1)
    pad = gsize - counts
    cpad = jnp.cumsum(pad)
    j = jnp.arange(n_slots - 2 * n, dtype=i32)
    pad_e = jnp.sum((j[:, None] >= cpad[None, :]).astype(i32), axis=1)
    pad_base = jnp.sum(jnp.where(pad_e[:, None] == expert_ids.T, (goff + counts - (cpad - pad))[None, :], 0), axis=1)
    pad_slot = jnp.where(pad_e < N_EXPERTS, pad_base + j, gend[-1] + j - cpad[-1])

    xs_p = _sc_row_scatter(hn_p, jnp.concatenate([slot, pad_slot]))
    out_p = _expert_ffn(f_exp, f_valid, xs_p, wg, wu, wd)
    o12 = _sc_row_gather(out_p, slot)
    gates = _pad_lanes(meta[4:6].T, 128)
    return _combine(x1, o12, gates, fnw, part_shapes)


def _rope_table(seq_len, dec_len):
    half = ROPE_DIMS // 2
    pos = jnp.concatenate([jnp.arange(seq_len), PAST_LEN + jnp.arange(dec_len)]).astype(F32)
    inv_freq = jnp.power(jnp.float32(ROPE_THETA), -jnp.arange(half, dtype=F32) / half)
    lane = np.arange(128) % HEAD
    inv_lane = jnp.where(lane < ROPE_DIMS, inv_freq[lane % half], 0.0)
    ang = pos[:, None] * inv_lane[None, :]
    cos, sin = jnp.cos(ang), jnp.sin(ang)
    sin_a = jnp.where(lane < half, -sin, 0.0)
    sin_b = jnp.where((lane >= half) & (lane < ROPE_DIMS), sin, 0.0)
    return jnp.concatenate([cos, sin_a, sin_b], axis=1)


def _mixer_consts():
    t = np.arange(CHUNK)
    tri = (t[:, None] >= t[None, :])
    same = (t[:, None] // SUB) == (t[None, :] // SUB)
    c = np.arange(MIX_W)
    ind = (c[:, None] // HEAD) == (c[None, :] // HEAD)
    as_bf = lambda m: jnp.asarray(m.astype(np.float32), dtype=BF16)
    return [as_bf(tri), as_bf(tri & same), as_bf(same), as_bf(ind)]


def _mlstm_state_to_pairs(c, n):
    c = c.astype(F32)
    n_b = jnp.broadcast_to(n.astype(F32)[..., None], c.shape)
    z = jnp.zeros_like(c[:, 0::2])
    top = jnp.concatenate([c[:, 0::2], z, n_b[:, 0::2], z], axis=-1)
    bottom = jnp.concatenate([z, c[:, 1::2], z, n_b[:, 1::2]], axis=-1)
    return jnp.concatenate([top, bottom], axis=-2)


def _mlstm_state_from_pairs(s):
    b = s.shape[0]
    c = jnp.stack([s[:, :, a * HEAD:(a + 1) * HEAD, a * HEAD:(a + 1) * HEAD] for a in range(2)], axis=2)
    n = jnp.stack([s[:, :, a * HEAD:(a + 1) * HEAD, (2 + a) * HEAD] for a in range(2)], axis=2)
    return (c.reshape(b, N_HEADS, HEAD, HEAD), n.reshape(b, N_HEADS, HEAD))


def _pad_lanes(v, width):
    return jnp.pad(v, ((0, 0), (0, width - v.shape[1])))


def kernel(x_prompt, x_sample, state_mlstm_C, state_mlstm_n, state_mlstm_m, state_hgrn_S, cache_swa_k, cache_swa_v, state_ssd_h, state_ssd_conv, norm1_w, w_in, ml_ig_b, ml_fg_b, ml_norm_w, hg_lb_logits, hg_norm_w, sw_sinks, ssd_conv_w, ssd_conv_b, ssd_dt_bias, ssd_A_log, ssd_D, ssd_norm_w, w_out, norm2_w, ffn_w_gate, ffn_w_up, ffn_w_down, moe_router, moe_w_gate, moe_w_up, moe_w_down, final_norm_w):
    depth = w_in.shape[0]
    bp, seq_len, _ = x_prompt.shape
    bs, dec_len, _ = x_sample.shape
    assert seq_len % CHUNK == 0 and dec_len == CHUNK and depth % 2 == 0
    n_seq = bp + bs
    chunks_per_prompt = seq_len // CHUNK

    assert bp == SEQ_GROUP and bs % SEQ_GROUP == 0
    x_parts = (x_prompt.reshape(bp, chunks_per_prompt, CHUNK, D_MODEL), x_sample.reshape(bs * dec_len, D_MODEL))
    part_shapes = tuple(p.shape for p in x_parts)
    rope_tab = _rope_table(seq_len, dec_len)
    consts = _mixer_consts()

    sm = jax.nn.softmax(hg_lb_logits.astype(F32), axis=0)
    hg_lb = jnp.cumsum(sm, axis=0) - sm[0]

    def with_prompt_zeros(a):
        return jnp.concatenate([jnp.zeros((bp,) + a.shape[1:], F32), a.astype(F32)], axis=0)

    state_outs = []
    for l in range(depth):
        lp = {
            "sinks": sw_sinks[l].astype(F32),
            "gbias": _pad_lanes(jnp.concatenate([ml_ig_b[l], ml_fg_b[l], ssd_dt_bias[l]])[None, :].astype(F32), 128),
            "alog": _pad_lanes(ssd_A_log[l][None, :].astype(F32), 128),
            "mlnw": ml_norm_w[l][None, :].astype(F32),
            "hglb": hg_lb[l][None, :],
            "hgnw": hg_norm_w[l][None, :].astype(F32),
            "convw": ssd_conv_w[l].astype(F32),
            "convb": ssd_conv_b[l][None, :].astype(F32),
            "ssdd": jnp.repeat(ssd_D[l].astype(F32), HEAD)[None, :],
            "ssdnw": ssd_norm_w[l][None, :].astype(F32),
        }
        c_aug = _mlstm_state_to_pairs(state_mlstm_C[l], state_mlstm_n[l])
        m_pad = jnp.pad(state_mlstm_m[l][:, None, :], ((0, 0), (0, 7), (0, 128 - N_HEADS)))
        s_t = jnp.swapaxes(state_hgrn_S[l], -1, -2)
        s_t = jnp.moveaxis(s_t, 1, 2).reshape(bs, HEAD, MIX_W)
        cv_pad = jnp.pad(state_ssd_conv[l], ((0, 0), (8 - (SSD_CONV - 1), 0), (0, 0)))
        init = [with_prompt_zeros(a) for a in (
            c_aug, m_pad, s_t, cache_swa_k[l].reshape(bs, WINDOW, 128), cache_swa_v[l].reshape(bs, WINDOW, 128),
            state_ssd_h[l], cv_pad)]

        proj = _inproj(x_parts, norm1_w[l][None, :], w_in[l])
        mixed, c_o, m_o, s_o, k_o, v_o, h_o, cv_o = _mixer(
            proj, rope_tab, init, lp, consts, chunks_per_prompt, n_seq)
        wo = w_out[l].astype(BF16)
        j = l // 2
        if l % 2 == 0:
            x_parts = (_dense_ffn(x_parts, mixed, wo, norm2_w[l][None, :], ffn_w_gate[j].astype(BF16),
                                  ffn_w_up[j].astype(BF16), ffn_w_down[j].astype(BF16)),)
        else:
            assert l == depth - 1 and len(x_parts) == 1
            y_parts = _moe(x_parts[0], mixed, wo, norm2_w[l][None, :], moe_router[j], moe_w_gate[j],
                           moe_w_up[j], moe_w_down[j], final_norm_w[None, :], part_shapes)
        s_back = jnp.moveaxis(s_o.reshape(n_seq, HEAD, N_HEADS, HEAD), 2, 1)
        state_outs.append(_mlstm_state_from_pairs(c_o) + (m_o[:, 0, :N_HEADS],
                           jnp.swapaxes(s_back, -1, -2),
                           k_o.reshape(n_seq, WINDOW, SW_KV, HEAD), v_o.reshape(n_seq, WINDOW, SW_KV, HEAD),
                           h_o, cv_o[:, 8 - (SSD_CONV - 1):, :]))

    y_prompt = y_parts[0].reshape(bp, seq_len, D_MODEL)
    y_sample = y_parts[1].reshape(bs, dec_len, D_MODEL)
    stacked = [jnp.stack([so[k] for so in state_outs]) for k in range(8)]
    return (y_prompt, y_sample) + tuple(s[:, :bp] for s in stacked) + tuple(s[:, bp:] for s in stacked)
```

```python
import functools

import numpy as np
import jax
import jax.numpy as jnp
from jax import lax
from jax.experimental import pallas as pl
from jax.experimental.pallas import tpu as pltpu
from jax.experimental.pallas import tpu_sc as plsc

F32 = jnp.float32
BF16 = jnp.bfloat16

D_MODEL = 1024
CHUNK = 64
EPS = 1e-6
N_HEADS = 4
HEAD = 64
MIX_W = N_HEADS * HEAD
SW_KV = 2
WINDOW = 128
ROPE_DIMS = 16
ROPE_THETA = 500000.0
SSD_N = 128
SSD_CONV = 4
SSD_CONV_DIM = 768
PAST_LEN = 4096
N_EXPERTS = 8
SUB = 16
SEQ_GROUP = 2
HG_SLICE = 4
TOKEN_TILES = (512, 256, 128)

C_MLQ, C_MLK, C_MLV, C_MLO = 0, 256, 512, 768
C_HGQ, C_HGF, C_HGI, C_HGG = 1024, 1280, 1536, 1792
C_SWQ, C_SWK, C_SWV = 2048, 2304, 2432
C_SDZ, C_SDX, C_GATE = 2560, 2816, 3584
D_PROJ = 3712
W_IN_COLS = C_GATE + 3 * N_HEADS
NEG = -1e30
VMEM_LIMIT = 56 * 1024 * 1024
SC_CORES = 2
SC_SUBCORES = 16
SC_MAX_GATHER_ROWS = 128


def _mm(a, b):
    return jnp.dot(a.astype(BF16), b.astype(BF16), preferred_element_type=F32)


def _mm_nt(a, b):
    return lax.dot_general(a.astype(BF16), b.astype(BF16), (((1,), (1,)), ((), ())),
                           preferred_element_type=F32)


def _mm_tn(a, b):
    return jnp.dot(a.T.astype(BF16), b.astype(BF16), preferred_element_type=F32)


def _mm_sel(sel, x):
    hi = x.astype(BF16)
    r1 = x - hi.astype(F32)
    mid = r1.astype(BF16)
    lo = (r1 - mid.astype(F32)).astype(BF16)
    return (jnp.dot(sel, hi, preferred_element_type=F32)
            + jnp.dot(sel, mid, preferred_element_type=F32)
            + jnp.dot(sel, lo, preferred_element_type=F32))


def _sigmoid(x):
    return 1.0 / (1.0 + jnp.exp(-x))


def _silu(x):
    return x * _sigmoid(x)


def _softplus(x):
    return jnp.maximum(x, 0.0) + jnp.log(1.0 + jnp.exp(-jnp.abs(x)))


def _rms(x, w):
    return x * lax.rsqrt(jnp.mean(x * x, axis=-1, keepdims=True) + EPS) * w


def _pick_tile(n, candidates):
    for c in candidates:
        if n % c == 0:
            return c
    raise ValueError(f"no tile for {n}")


def _token_parts_specs(parts, tm):
    bounds = np.cumsum([0] + [_part_rows(p) // tm for p in parts])
    specs = []
    for k, p in enumerate(parts):
        assert _part_rows(p) % tm == 0
        lo, hi = int(bounds[k]), int(bounds[k + 1])
        specs.append(_part_block_spec(p, tm, lo, hi - lo))
    return specs, [int(b) for b in bounds]


def _part_rows(p):
    return int(np.prod(p.shape[:-1]))


def _part_block_spec(p, tm, first_tile, n_tiles):
    idx = lambda i: jnp.clip(i - first_tile, 0, n_tiles - 1)
    if len(p.shape) == 2:
        return pl.BlockSpec((tm, p.shape[1]), lambda i: (idx(i), 0))
    grp, _, chunk, d = p.shape
    assert tm % (grp * chunk) == 0
    return pl.BlockSpec((grp, tm // (grp * chunk), chunk, d), lambda i: (0, idx(i), 0, 0))


def _load_part_tile(ref):
    if len(ref.shape) == 2:
        return ref[...]
    grp, n_chunks = ref.shape[:2]
    return jnp.concatenate([ref[s, c] for c in range(n_chunks) for s in range(grp)], axis=0)


def _store_part_tile(ref, x):
    if len(ref.shape) == 2:
        ref[...] = x
        return
    grp, n_chunks, chunk = ref.shape[:3]
    for c in range(n_chunks):
        for s in range(grp):
            r0 = (c * grp + s) * chunk
            ref[s, c] = x[r0:r0 + chunk]


def _read_token_tile(refs, bounds):
    i = pl.program_id(0)
    x = _load_part_tile(refs[-1])
    for k in range(len(refs) - 2, -1, -1):
        x = jnp.where(i < bounds[k + 1], _load_part_tile(refs[k]), x)
    return x


def _inproj_kernel(n_parts, bounds, *refs):
    x_refs = refs[:n_parts]
    nw_ref, w_ref, o_ref, wp_ref = refs[n_parts:]

    @pl.when(pl.program_id(0) == 0)
    def _():
        rows = 128
        ml_gates = slice(C_MLO, C_MLO + 2 * N_HEADS)
        sd_dt = slice(W_IN_COLS - N_HEADS, W_IN_COLS)
        n_gates = 3 * N_HEADS
        for r0 in range(0, D_MODEL, rows):
            w = w_ref[0, r0:r0 + rows, :]
            wp_ref[r0:r0 + rows, 0:C_MLO] = w[:, 0:ml_gates.start].astype(BF16)
            wp_ref[r0:r0 + rows, C_MLO:C_GATE] = w[:, ml_gates.stop:sd_dt.start].astype(BF16)
            gates = jnp.concatenate([w[:, ml_gates], w[:, sd_dt],
                                     jnp.zeros((rows, D_PROJ - C_GATE - n_gates), F32)], axis=1)
            wp_ref[r0:r0 + rows, C_GATE:D_PROJ] = gates.astype(BF16)

    xn = _rms(_read_token_tile(x_refs, bounds), nw_ref[...]).astype(BF16)
    o_ref[...] = jnp.dot(xn, wp_ref[...], preferred_element_type=F32)


def _inproj(x_parts, nw, w_all, layer):
    n = sum(_part_rows(p) for p in x_parts)
    tm = _pick_tile(np.gcd.reduce([_part_rows(p) for p in x_parts]), TOKEN_TILES)
    x_specs, bounds = _token_parts_specs(x_parts, tm)
    assert w_all.shape[1:] == (D_MODEL, W_IN_COLS)
    return pl.pallas_call(
        functools.partial(_inproj_kernel, len(x_parts), bounds),
        grid=(n // tm,),
        in_specs=x_specs + [pl.BlockSpec((1, D_MODEL), lambda i: (0, 0)),
                            pl.BlockSpec((1, D_MODEL, W_IN_COLS), lambda i: (layer, 0, 0),
                                         pipeline_mode=pl.Buffered(1))],
        out_specs=pl.BlockSpec((tm, D_PROJ), lambda i: (i, 0)),
        out_shape=jax.ShapeDtypeStruct((n, D_PROJ), F32),
        scratch_shapes=[pltpu.VMEM((D_MODEL, D_PROJ), BF16)],
        compiler_params=pltpu.CompilerParams(dimension_semantics=("arbitrary",),
                                             vmem_limit_bytes=VMEM_LIMIT),
        name="inproj",
    )(*x_parts, nw, w_all)


def _head_norm(x, width):
    parts = []
    for g in range(x.shape[1] // width):
        xg = x[:, g * width:(g + 1) * width]
        parts.append(xg * lax.rsqrt(jnp.mean(xg * xg, axis=-1, keepdims=True) + EPS))
    return jnp.concatenate(parts, axis=1)


def _mixer_kernel(prompt_steps,
                  sinks_ref,
                  proj_ref, rope_ref, c0_ref, m0_ref, s0_ref, k0_ref, v0_ref, h0_ref, cv0_ref,
                  gbias_ref, alog_ref, mlnw_ref, hglb_ref, hgnw_ref,
                  convw_ref, convb_ref, ssdd_ref, ssdnw_ref,
                  tri_ref, triblk_ref, blkones_ref, ind_ref,
                  mixed_ref, c_ref, m_ref, s_ref, k_ref, v_ref, h_ref, cv_ref,
                  convbuf, bloc_s, kk_s, hv_s):
    i = pl.program_id(0)
    is_prompt = i < prompt_steps
    is_first = jnp.logical_or(jnp.logical_not(is_prompt), i == 0)
    n_valid = jnp.where(is_prompt, jnp.minimum(i * CHUNK, WINDOW), WINDOW)

    @pl.when(is_first)
    def _():
        c_ref[...] = c0_ref[...]
        m_ref[...] = m0_ref[...]
        s_ref[...] = s0_ref[...]
        k_ref[...] = k0_ref[...]
        v_ref[...] = v0_ref[...]
        h_ref[...] = h0_ref[...]
        cv_ref[...] = cv0_ref[...]

    members = [_mixer_chunk(g, n_valid, sinks_ref, proj_ref, rope_ref,
                            gbias_ref, alog_ref, mlnw_ref, hglb_ref, hgnw_ref,
                            convw_ref, convb_ref, ssdd_ref, ssdnw_ref,
                            tri_ref, triblk_ref, blkones_ref, ind_ref,
                            mixed_ref, c_ref, m_ref, s_ref, k_ref, v_ref, h_ref, cv_ref,
                            convbuf, bloc_s, kk_s, hv_s) for g in range(SEQ_GROUP)]
    for phase in zip(*[m[0] for m in members]):
        chains = [c for per_member in zip(*phase) for c in per_member]
        while chains:
            alive = []
            for chain in chains:
                if next(chain, _CHAIN_DONE) is not _CHAIN_DONE:
                    alive.append(chain)
            chains = alive
    for _, finish in members:
        finish()


_CHAIN_DONE = object()


def _mixer_chunk(g, n_valid, sinks_ref, proj_ref, rope_ref,
                 gbias_ref, alog_ref, mlnw_ref, hglb_ref, hgnw_ref,
                 convw_ref, convb_ref, ssdd_ref, ssdnw_ref,
                 tri_ref, triblk_ref, blkones_ref, ind_ref,
                 mixed_ref, c_ref, m_ref, s_ref, k_ref, v_ref, h_ref, cv_ref,
                 convbuf, bloc_s, kk_s, hv_s):
    rs = slice(g * CHUNK, (g + 1) * CHUNK)
    L = CHUNK
    row = lax.broadcasted_iota(jnp.int32, (L, L), 0)
    col = lax.broadcasted_iota(jnp.int32, (L, L), 1)
    causal = row >= col
    tri = tri_ref[...]

    gate = proj_ref[rs,C_GATE:C_GATE + 128] + gbias_ref[...]
    ig = gate
    lf = pltpu.roll(-_softplus(-gate), 124, 1)
    dtv = pltpu.roll(_softplus(gate), 120, 1)

    b = _mm_sel(tri, lf)
    a = ig - b
    cm = a
    rowg = lax.broadcasted_iota(jnp.int32, (L, 128), 0)
    for sh in (1, 2, 4, 8, 16, 32):
        cm = jnp.where(rowg >= sh, jnp.maximum(cm, pltpu.roll(cm, sh, 0)), cm)
    m_prev = m_ref[g, 0:1, :]
    m_t = b + jnp.maximum(m_prev, cm)
    inter = jnp.exp(b + m_prev - m_t)
    bm = b - m_t
    a_t = a.T
    m_last = m_t[L - 1:L, :]
    b_last = b[L - 1:L, :]
    ws = jnp.exp(b_last + a - m_last)
    decay = jnp.exp(b_last + m_prev - m_last)
    emt = jnp.exp(-m_t)
    m_ref[g, 0:1, :] = m_last

    lane2 = lax.broadcasted_iota(jnp.int32, (L, 2 * HEAD), 1)
    low = lane2 < HEAD
    causal2 = lax.broadcasted_iota(jnp.int32, (L, 2 * HEAD), 0) >= (lane2 & (HEAD - 1))
    ones_blk = ind_ref[0:2 * HEAD, 0:2 * HEAD]
    ones_rows = jnp.ones((L, 2 * HEAD), BF16)
    st_row = lax.broadcasted_iota(jnp.int32, (2 * HEAD, 4 * HEAD), 0) // HEAD
    st_lane = (lax.broadcasted_iota(jnp.int32, (2 * HEAD, 4 * HEAD), 1) // HEAD) % 2
    ml_out = [None] * (N_HEADS // 2)

    def pair_cols(x, p):
        return jnp.where(low, x[:, 2 * p:2 * p + 1], x[:, 2 * p + 1:2 * p + 2])

    def split_rows(x):
        return jnp.concatenate([jnp.where(low, x, 0.0), jnp.where(low, 0.0, x)], axis=0)

    def ml_pair(p):
        cs = slice(2 * p * HEAD, 2 * (p + 1) * HEAD)
        q = proj_ref[rs, C_MLQ + cs.start:C_MLQ + cs.stop]
        k = proj_ref[rs, C_MLK + cs.start:C_MLK + cs.stop] * (HEAD ** -0.5)
        v = proj_ref[rs, C_MLV + cs.start:C_MLV + cs.stop]
        qk = _mm_nt(q, split_rows(k))
        yield
        st = c_ref[g, p]
        from_state = _mm(q, st)
        yield
        a_row = jnp.concatenate([a_t[2 * p:2 * p + 1, :], a_t[2 * p + 1:2 * p + 2, :]], axis=1)
        w = jnp.where(causal2, jnp.exp(pair_cols(bm, p) + a_row), 0.0)
        rhs = jnp.concatenate([split_rows(v).astype(BF16), ones_blk], axis=1)
        gate_in = pair_cols(inter, p)
        nd = (jnp.dot((w * qk).astype(BF16), rhs, preferred_element_type=F32)
              + jnp.concatenate([gate_in, gate_in], axis=1) * from_state)
        yield
        kws = k * pair_cols(ws, p)
        upd = jnp.dot(kws.T.astype(BF16), jnp.concatenate([v.astype(BF16), ones_rows], axis=1),
                      preferred_element_type=F32)
        yield
        den = jnp.maximum(jnp.abs(nd[:, 2 * HEAD:]), pair_cols(emt, p))
        ml_out[p] = nd[:, :2 * HEAD] / den
        dec = jnp.where(st_row == 0, decay[:, 2 * p:2 * p + 1], decay[:, 2 * p + 1:2 * p + 2])
        c_ref[g, p] = dec * st + jnp.where(st_row == st_lane, upd, 0.0)

    def ml_finish():
        ml = _head_norm(jnp.concatenate(ml_out, axis=1), HEAD)
        mixed_ref[rs,0:MIX_W] = (_sigmoid(proj_ref[rs,C_MLO:C_MLO + MIX_W]) * ml
                                 * mlnw_ref[...]).astype(mixed_ref.dtype)

    lb = hglb_ref[...]
    fg = lb + (1.0 - lb) * _sigmoid(proj_ref[rs,C_HGF:C_HGF + MIX_W])
    hq = _silu(proj_ref[rs,C_HGQ:C_HGQ + MIX_W]) * (HEAD ** -0.5)
    kk = 1.0 - fg
    lfg = jnp.log(fg)
    hv = proj_ref[rs,C_HGI:C_HGI + MIX_W]
    bloc = _mm_sel(triblk_ref[...], lfg)
    blast = _mm_sel(blkones_ref[...], lfg)
    qe = hq * jnp.exp(bloc)
    kw = kk * jnp.exp(blast - bloc)
    bloc_s[g] = bloc
    kk_s[g] = kk
    hv_s[g] = hv
    ind = ind_ref[...]
    rows = lax.broadcasted_iota(jnp.int32, (SUB, MIX_W), 0)
    n_blk = L // SUB
    hg_state_part = [None] * n_blk
    hg_block_part = [None] * n_blk

    def hg_within_block(blk):
        r0 = blk * SUB
        q_blk = hq[r0:r0 + SUB]
        b_blk = bloc[r0:r0 + SUB]
        xs = []
        for s in range(SUB):
            r = r0 + s
            e = jnp.exp(jnp.minimum(b_blk - bloc_s[g, r:r + 1, :], 0.0))
            xs.append((q_blk * (kk_s[g, r:r + 1, :] * e)).astype(BF16))
            if s % HG_SLICE == HG_SLICE - 1:
                yield
        att = jnp.dot(jnp.concatenate(xs, axis=0), ind, preferred_element_type=F32)
        yield
        o_blk = att[0:SUB] * hv_s[g, r0:r0 + 1, :]
        for s in range(1, SUB):
            r = r0 + s
            o_blk = o_blk + jnp.where(rows >= s, att[s * SUB:(s + 1) * SUB], 0.0) * hv_s[g, r:r + 1, :]
            if s % HG_SLICE == HG_SLICE - 1:
                yield
        hg_block_part[blk] = o_blk

    def hg_state_chain():
        st = s_ref[g]
        for blk in range(n_blk):
            r0 = blk * SUB
            hg_state_part[blk] = jnp.concatenate(
                [_mm_nt(qe[r0:r0 + SUB, h * HEAD:(h + 1) * HEAD], st[:, h * HEAD:(h + 1) * HEAD])
                 for h in range(N_HEADS)], axis=1)
            upd = jnp.concatenate(
                [_mm_tn(hv[r0:r0 + SUB, h * HEAD:(h + 1) * HEAD], kw[r0:r0 + SUB, h * HEAD:(h + 1) * HEAD])
                 for h in range(N_HEADS)], axis=1)
            yield
            st = st * jnp.exp(blast[r0:r0 + 1, :]) + upd
        s_ref[g] = st

    def hg_finish():
        o = jnp.concatenate([hg_state_part[blk] + hg_block_part[blk] for blk in range(n_blk)], axis=0)
        mixed_ref[rs,MIX_W:2 * MIX_W] = (_head_norm(o, HEAD) * hgnw_ref[...]
                                         * _silu(proj_ref[rs,C_HGG:C_HGG + MIX_W])).astype(mixed_ref.dtype)

    @functools.lru_cache(maxsize=None)
    def sw_prep():
        cos = rope_ref[:, 0:128]
        sin_a = rope_ref[:, 128:256]
        sin_b = rope_ref[:, 256:384]

        def rope(x):
            return x * cos + pltpu.roll(x, 120, 1) * sin_a + pltpu.roll(x, 8, 1) * sin_b

        q_rot = [rope(proj_ref[rs,C_SWQ:C_SWQ + 128]), rope(proj_ref[rs,C_SWQ + 128:C_SWQ + 256])]
        k_rot = rope(proj_ref[rs,C_SWK:C_SWK + 128])
        v_cur = proj_ref[rs,C_SWV:C_SWV + 128]
        colw = lax.broadcasted_iota(jnp.int32, (2 * L, WINDOW), 1)
        row2 = lax.broadcasted_iota(jnp.int32, (2 * L, 1), 0)
        return q_rot, k_rot, v_cur, k_ref[g], v_ref[g], colw >= WINDOW - n_valid, row2

    sw_prep()
    sw_out = [None] * (2 * SW_KV)

    def sw_group(kv):
        q_rot, k_rot, v_cur, k_prev, v_prev, prev_ok, row2 = sw_prep()
        gs = slice(kv * HEAD, (kv + 1) * HEAD)
        q2 = jnp.concatenate([q_rot[kv][:, 0:HEAD], q_rot[kv][:, HEAD:2 * HEAD]], axis=0) * (HEAD ** -0.5)
        s_p = jnp.where(prev_ok, _mm_nt(q2, k_prev[:, gs]), NEG)
        yield
        s_c = _mm_nt(q2, k_rot[:, gs])
        yield
        sink = jnp.where(row2 < L, sinks_ref[2 * kv], sinks_ref[2 * kv + 1])
        mx = jnp.maximum(jnp.maximum(jnp.max(s_p, axis=-1, keepdims=True),
                                     jnp.max(s_c, axis=-1, keepdims=True)), sink)
        e_p = jnp.exp(s_p - mx)
        e_c = jnp.exp(s_c - mx)
        den = (jnp.sum(e_p, axis=-1, keepdims=True) + jnp.sum(e_c, axis=-1, keepdims=True)
               + jnp.exp(sink - mx))
        o_p = _mm(e_p, v_prev[:, gs])
        yield
        o = (o_p + _mm(e_c, v_cur[:, gs])) / den
        yield
        sw_out[2 * kv] = o[0:L]
        sw_out[2 * kv + 1] = o[L:2 * L]

    def sw_finish():
        _, k_rot, v_cur, k_prev, v_prev, _, _ = sw_prep()
        mixed_ref[rs,2 * MIX_W:3 * MIX_W] = jnp.concatenate(sw_out, axis=1).astype(mixed_ref.dtype)
        k_ref[g, 0:WINDOW - L, :] = k_prev[L:WINDOW]
        k_ref[g, WINDOW - L:WINDOW, :] = k_rot
        v_ref[g, 0:WINDOW - L, :] = v_prev[L:WINDOW]
        v_ref[g, WINDOW - L:WINDOW, :] = v_cur

    @functools.lru_cache(maxsize=None)
    def sd_prep():
        convbuf[g, 0:8, :] = cv_ref[g]
        convbuf[g, 8:8 + L, :] = proj_ref[rs,C_SDX:C_SDX + SSD_CONV_DIM]
        acc = convb_ref[...] + convbuf[g, 5:5 + L, :] * convw_ref[0:1, :]
        for j in range(1, SSD_CONV):
            acc = acc + convbuf[g, 5 + j:5 + j + L, :] * convw_ref[j:j + 1, :]
        xbc = _silu(acc)
        cv_ref[g] = convbuf[g, L:L + 8, :]
        a_neg = -jnp.exp(alog_ref[...])
        ca = _mm_sel(tri, dtv * a_neg)
        ca_last = ca[L - 1:L, :]
        return (xbc, xbc[:, 0:MIX_W], ca, ca.T, dtv.T, jnp.exp(ca), jnp.exp(ca_last - ca) * dtv,
                jnp.exp(ca_last))

    sd_prep()
    sd_out = [None] * (N_HEADS // 2)

    def sd_group(grp):
        xbc, xs_all, ca, ca_t, dt_t, eca, wss, sdec = sd_prep()
        bm_g = xbc[:, MIX_W + grp * SSD_N:MIX_W + (grp + 1) * SSD_N]
        cm_g = xbc[:, MIX_W + 2 * SSD_N + grp * SSD_N:MIX_W + 2 * SSD_N + (grp + 1) * SSD_N]
        e, o = 2 * grp, 2 * grp + 1
        x = xs_all[:, e * HEAD:(o + 1) * HEAD]
        cb2 = _mm_nt(cm_g, jnp.concatenate([bm_g, bm_g], axis=0))
        yield
        hst = h_ref[g, e:o + 1].reshape(2 * HEAD, SSD_N)
        from_state = _mm_nt(cm_g, hst)
        yield
        ca_row = jnp.concatenate([ca_t[e:e + 1, :], ca_t[o:o + 1, :]], axis=1)
        dt_row = jnp.concatenate([dt_t[e:e + 1, :], dt_t[o:o + 1, :]], axis=1)
        dec = jnp.where(causal2, jnp.exp(pair_cols(ca, grp) - ca_row), 0.0)
        sd_out[grp] = _mm(cb2 * dec * dt_row, split_rows(x)) + pair_cols(eca, grp) * from_state
        yield
        upd = _mm_tn(x * pair_cols(wss, grp), bm_g)
        yield
        keep = jnp.where(lax.broadcasted_iota(jnp.int32, (2 * HEAD, SSD_N), 0) < HEAD,
                         sdec[:, e:e + 1], sdec[:, o:o + 1])
        h_ref[g, e:o + 1] = (keep * hst + upd).reshape(2, HEAD, SSD_N)

    def sd_finish():
        xs_all = sd_prep()[1]
        y = jnp.concatenate(sd_out, axis=1) + ssdd_ref[...] * xs_all
        gated = y * _silu(proj_ref[rs,C_SDZ:C_SDZ + MIX_W])
        mixed_ref[rs,3 * MIX_W:4 * MIX_W] = (_head_norm(gated, 2 * HEAD)
                                             * ssdnw_ref[...]).astype(mixed_ref.dtype)

    chains = ([[hg_within_block(blk) for blk in range(n_blk)]]
              + [[ml_pair(p)] for p in range(N_HEADS // 2)] + [[hg_state_chain(), sd_group(0)]]
              + [[sd_group(1)]]
              + [[sw_group(kv)] for kv in range(SW_KV)])

    def finish():
        ml_finish()
        hg_finish()
        sw_finish()
        sd_finish()

    return chains, finish


def _mixer(proj, rope_tab, init, lp, consts, chunks_per_prompt, n_seq):
    n = proj.shape[0]
    rows = SEQ_GROUP * CHUNK
    n_steps = n // rows

    def group_of(i):
        return jnp.where(i < chunks_per_prompt, 0, i - chunks_per_prompt + 1)

    def rope_blk(i):
        return jnp.minimum(i, chunks_per_prompt)

    def per_seq(shape):
        nd = len(shape)
        return pl.BlockSpec((SEQ_GROUP,) + shape, lambda i, s: (group_of(i),) + (0,) * nd)

    def const(shape):
        nd = len(shape)
        return pl.BlockSpec(shape, lambda i, s: (0,) * nd)

    state_shapes = [(N_HEADS // 2, 2 * HEAD, 4 * HEAD), (8, 128), (HEAD, MIX_W), (WINDOW, 128), (WINDOW, 128),
                    (N_HEADS, HEAD, SSD_N), (8, SSD_CONV_DIM)]
    in_specs = ([pl.BlockSpec((rows, D_PROJ), lambda i, s: (i, 0)),
                 pl.BlockSpec((CHUNK, 384), lambda i, s: (rope_blk(i), 0))]
                + [per_seq(s) for s in state_shapes]
                + [const((1, 128)), const((1, 128)), const((1, MIX_W)), const((1, MIX_W)), const((1, MIX_W)),
                   const((SSD_CONV, SSD_CONV_DIM)), const((1, SSD_CONV_DIM)), const((1, MIX_W)),
                   const((1, MIX_W)),
                   const((CHUNK, CHUNK)), const((CHUNK, CHUNK)), const((CHUNK, CHUNK)), const((MIX_W, MIX_W))])
    out_specs = ([pl.BlockSpec((rows, D_MODEL), lambda i, s: (i, 0))]
                 + [per_seq(s) for s in state_shapes])
    out_shape = ([jax.ShapeDtypeStruct((n, D_MODEL), BF16)]
                 + [jax.ShapeDtypeStruct((n_seq,) + s, F32) for s in state_shapes])
    grid_spec = pltpu.PrefetchScalarGridSpec(
        num_scalar_prefetch=1, grid=(n_steps,), in_specs=in_specs, out_specs=out_specs,
        scratch_shapes=[pltpu.VMEM((SEQ_GROUP, CHUNK + 8, SSD_CONV_DIM), F32),
                        pltpu.VMEM((SEQ_GROUP, CHUNK, MIX_W), F32), pltpu.VMEM((SEQ_GROUP, CHUNK, MIX_W), F32),
                        pltpu.VMEM((SEQ_GROUP, CHUNK, MIX_W), F32)])
    return pl.pallas_call(
        functools.partial(_mixer_kernel, chunks_per_prompt),
        grid_spec=grid_spec, out_shape=out_shape,
        compiler_params=pltpu.CompilerParams(dimension_semantics=("arbitrary",),
                                             vmem_limit_bytes=VMEM_LIMIT),
        name="mixer",
    )(lp["sinks"], proj, rope_tab, *init,
      lp["gbias"], lp["alog"], lp["mlnw"], lp["hglb"], lp["hgnw"],
      lp["convw"], lp["convb"], lp["ssdd"], lp["ssdnw"], *consts)


def _dense_ffn_kernel(f_chunk, n_parts, bounds, *refs):
    x_refs = refs[:n_parts]
    mix_ref, wo_ref, nw_ref, wg_ref, wu_ref, wd_ref, o_ref = refs[n_parts:]
    x1 = _read_token_tile(x_refs, bounds) + jnp.dot(mix_ref[...], wo_ref[...], preferred_element_type=F32)
    hn = _rms(x1, nw_ref[...]).astype(BF16)
    o_ref[...] = x1
    for f0 in range(0, wg_ref.shape[1], f_chunk):
        g = jnp.dot(hn, wg_ref[:, f0:f0 + f_chunk], preferred_element_type=F32)
        u = jnp.dot(hn, wu_ref[:, f0:f0 + f_chunk], preferred_element_type=F32)
        act = (_silu(g) * u).astype(BF16)
        o_ref[...] += jnp.dot(act, wd_ref[f0:f0 + f_chunk, :], preferred_element_type=F32)


def _dense_ffn(x_parts, mixed, wo, nw, wg, wu, wd):
    n = mixed.shape[0]
    d_ff = wg.shape[1]
    tm = _pick_tile(np.gcd.reduce([_part_rows(p) for p in x_parts]), TOKEN_TILES)
    x_specs, bounds = _token_parts_specs(x_parts, tm)
    full = lambda shape: pl.BlockSpec(shape, lambda i: (0, 0))
    return pl.pallas_call(
        functools.partial(_dense_ffn_kernel, 256, len(x_parts), bounds),
        grid=(n // tm,),
        in_specs=x_specs + [
                  pl.BlockSpec((tm, D_MODEL), lambda i: (i, 0)),
                  full((D_MODEL, D_MODEL)), full((1, D_MODEL)),
                  full((D_MODEL, d_ff)), full((D_MODEL, d_ff)), full((d_ff, D_MODEL))],
        out_specs=pl.BlockSpec((tm, D_MODEL), lambda i: (i, 0)),
        out_shape=jax.ShapeDtypeStruct((n, D_MODEL), F32),
        compiler_params=pltpu.CompilerParams(dimension_semantics=("arbitrary",),
                                             vmem_limit_bytes=VMEM_LIMIT),
        name="dense_ffn",
    )(*x_parts, mixed, wo, nw, wg, wu, wd)


TOK_BLK = 256
FFN_TILE = 1024


def _pack_bf16_pairs(x):
    w = x.shape[1] // 2
    bits = pltpu.bitcast(x.astype(BF16).astype(F32), jnp.int32)
    return lax.shift_right_logical(bits[:, :w], 16) | bits[:, w:]


def _unpack_bf16_pairs(p):
    lo = pltpu.bitcast(lax.shift_left(p, 16), F32)
    hi = pltpu.bitcast(p & jnp.int32(-65536), F32)
    return lo, hi


def _router_kernel(x_ref, mix_ref, wo_ref, nw_ref, wrt_ref, upper_ref,
                   x1_ref, hn_ref, meta_ref, gate_ref, count_ref, carry):
    @pl.when(pl.program_id(0) == 0)
    def _():
        carry[...] = jnp.zeros_like(carry)

    x1 = x_ref[...] + jnp.dot(mix_ref[...], wo_ref[...], preferred_element_type=F32)
    x1_ref[...] = x1
    hn_f = _rms(x1, nw_ref[...])
    hn = hn_f.astype(BF16)
    hn_ref[...] = _pack_bf16_pairs(hn_f)
    logits = lax.dot_general(wrt_ref[...], hn, (((1,), (1,)), ((), ())),
                             preferred_element_type=F32)
    sub = lax.broadcasted_iota(jnp.int32, logits.shape, 0)
    l1 = jnp.where(sub < N_EXPERTS, logits, NEG)
    m1 = jnp.max(l1, axis=0, keepdims=True)
    i1 = jnp.min(jnp.where(l1 == m1, sub, 99), axis=0, keepdims=True)
    l2 = jnp.where(sub == i1, NEG, l1)
    m2 = jnp.max(l2, axis=0, keepdims=True)
    i2 = jnp.min(jnp.where(l2 == m2, sub, 99), axis=0, keepdims=True)
    e2 = jnp.exp(m2 - m1)
    g1 = 1.0 / (1.0 + e2)
    g2 = e2 / (1.0 + e2)
    sel = jnp.where(sub == i1, 1.0, jnp.where(sub == i2, 1.0, 0.0))
    before = jnp.dot(sel.astype(BF16), upper_ref[...], preferred_element_type=F32) + carry[:, 0:1]
    r1 = jnp.sum(jnp.where(sub == i1, before, 0.0), axis=0, keepdims=True)
    r2 = jnp.sum(jnp.where(sub == i2, before, 0.0), axis=0, keepdims=True)
    meta_ref[...] = jnp.concatenate([i1.astype(F32), i2.astype(F32), r1, r2,
                                     jnp.zeros((4, r1.shape[1]), F32)], axis=0)
    gate_ref[...] = jnp.concatenate([g1, g2, jnp.zeros((126, r1.shape[1]), F32)], axis=0).T
    total = carry[...] + jnp.sum(sel, axis=1, keepdims=True)
    carry[...] = total
    count_ref[...] = total


def _router(x, mixed, wo, nw, wrt, upper):
    n = x.shape[0]
    tm = TOK_BLK
    full = lambda shape: pl.BlockSpec(shape, lambda i: (0,) * len(shape))
    tok = lambda w: pl.BlockSpec((tm, w), lambda i: (i, 0))
    return pl.pallas_call(
        _router_kernel,
        grid=(n // tm,),
        in_specs=[tok(D_MODEL), tok(D_MODEL), full((D_MODEL, D_MODEL)), full((1, D_MODEL)),
                  full((16, D_MODEL)), full((tm, tm))],
        out_specs=[tok(D_MODEL), tok(D_MODEL // 2),
                   pl.BlockSpec((8, tm), lambda i: (0, i)), tok(128), full((16, 128))],
        out_shape=[jax.ShapeDtypeStruct((n, D_MODEL), F32), jax.ShapeDtypeStruct((n, D_MODEL // 2), jnp.int32),
                   jax.ShapeDtypeStruct((8, n), F32), jax.ShapeDtypeStruct((n, 128), F32),
                   jax.ShapeDtypeStruct((16, 128), F32)],
        scratch_shapes=[pltpu.VMEM((16, 128), F32)],
        compiler_params=pltpu.CompilerParams(dimension_semantics=("arbitrary",),
                                             vmem_limit_bytes=VMEM_LIMIT),
        name="router",
    )(x, mixed, wo, nw, wrt, upper)


def _sc_row_gather(table, idx):
    n_workers = SC_CORES * SC_SUBCORES
    b = idx.shape[0]
    d = table.shape[1]
    assert b % (8 * n_workers) == 0
    per_worker = b // n_workers
    chunk = max(c for c in range(8, SC_MAX_GATHER_ROWS + 1, 8) if per_worker % c == 0)
    mesh = plsc.VectorSubcoreMesh(core_axis_name="c", subcore_axis_name="s")

    @functools.partial(
        pl.kernel, mesh=mesh, out_type=jax.ShapeDtypeStruct((b, d), table.dtype),
        scratch_types=[pltpu.VMEM((chunk,), jnp.int32), pltpu.VMEM((chunk, d), table.dtype),
                       pltpu.SemaphoreType.DMA])
    def gather(table_hbm, idx_hbm, out_hbm, idx_v, rows_v, sem):
        worker = lax.axis_index("s") * SC_CORES + lax.axis_index("c")
        base = worker * per_worker

        @pl.loop(0, per_worker // chunk)
        def _(c):
            off = pl.multiple_of(base + c * chunk, 8)
            pltpu.sync_copy(idx_hbm.at[pl.ds(off, chunk)], idx_v)
            pltpu.async_copy(table_hbm.at[idx_v], rows_v, sem).wait()
            pltpu.sync_copy(rows_v, out_hbm.at[pl.ds(off, chunk)])

    return gather(table, idx)


def _sc_row_scatter(rows, idx):
    n_workers = SC_CORES * SC_SUBCORES
    b = n_out = idx.shape[0]
    v, d = rows.shape
    assert b % (8 * n_workers) == 0
    per_worker = b // n_workers
    chunk = max(c for c in range(8, SC_MAX_GATHER_ROWS + 1, 8) if per_worker % c == 0 and v % c == 0)
    mesh = plsc.VectorSubcoreMesh(core_axis_name="c", subcore_axis_name="s")

    @functools.partial(
        pl.kernel, mesh=mesh, out_type=jax.ShapeDtypeStruct((n_out, d), rows.dtype),
        scratch_types=[pltpu.VMEM((chunk,), jnp.int32), pltpu.VMEM((chunk, d), rows.dtype)])
    def scatter(rows_hbm, idx_hbm, out_hbm, idx_v, rows_v):
        worker = lax.axis_index("s") * SC_CORES + lax.axis_index("c")
        base = worker * per_worker

        @pl.loop(0, per_worker // chunk)
        def _(c):
            off = pl.multiple_of(base + c * chunk, 8)
            pltpu.sync_copy(idx_hbm.at[pl.ds(off, chunk)], idx_v)
            pltpu.sync_copy(rows_hbm.at[pl.ds(pl.multiple_of(lax.rem(off, v), 8), chunk)], rows_v)
            pltpu.sync_copy(rows_v, out_hbm.at[idx_v])

    return scatter(rows, idx)


def _expert_ffn_kernel(texp_ref, tvalid_ref, xs_ref, wg_ref, wu_ref, wd_ref, o_ref, acc_ref, xb_ref):
    j = pl.program_id(0)
    f = pl.program_id(1)
    last = f == pl.num_programs(1) - 1
    valid = tvalid_ref[j] == 1
    half = D_MODEL // 2

    @pl.when(jnp.logical_and(valid, f == 0))
    def _():
        lo, hi = _unpack_bf16_pairs(xs_ref[...])
        xb_ref[:, :half] = lo.astype(BF16)
        xb_ref[:, half:] = hi.astype(BF16)

        acc_ref[...] = jnp.zeros_like(acc_ref)

    @pl.when(valid)
    def _():
        xs = xb_ref[...]
        g = jnp.dot(xs, wg_ref[0], preferred_element_type=F32)
        u = jnp.dot(xs, wu_ref[0].astype(BF16), preferred_element_type=F32)
        act = (_silu(g) * u).astype(BF16)
        acc_ref[...] += jnp.dot(act, wd_ref[0], preferred_element_type=F32)

        @pl.when(last)
        def _():
            o_ref[...] = _pack_bf16_pairs(acc_ref[...])

    @pl.when(jnp.logical_and(jnp.logical_not(valid), last))
    def _():
        o_ref[...] = jnp.zeros_like(o_ref)


def _expert_ffn(f_exp, f_valid, xs, wg, wu, wd):
    n_slots = xs.shape[0]
    d_ff = wg.shape[2]
    tf = _pick_tile(d_ff, (512, 256, 128))
    grid_spec = pltpu.PrefetchScalarGridSpec(
        num_scalar_prefetch=2, grid=(n_slots // FFN_TILE, d_ff // tf),
        in_specs=[pl.BlockSpec((FFN_TILE, D_MODEL // 2), lambda j, f, te, tv: (j, 0)),
                  pl.BlockSpec((1, D_MODEL, tf), lambda j, f, te, tv: (te[j], 0, f)),
                  pl.BlockSpec((1, D_MODEL, tf), lambda j, f, te, tv: (te[j], 0, f)),
                  pl.BlockSpec((1, tf, D_MODEL), lambda j, f, te, tv: (te[j], f, 0))],
        out_specs=pl.BlockSpec((FFN_TILE, D_MODEL // 2), lambda j, f, te, tv: (j, 0)),
        scratch_shapes=[pltpu.VMEM((FFN_TILE, D_MODEL), F32), pltpu.VMEM((FFN_TILE, D_MODEL), BF16)])
    return pl.pallas_call(
        _expert_ffn_kernel, grid_spec=grid_spec,
        out_shape=jax.ShapeDtypeStruct((n_slots, D_MODEL // 2), jnp.int32),
        compiler_params=pltpu.CompilerParams(dimension_semantics=("arbitrary", "arbitrary"),
                                             vmem_limit_bytes=VMEM_LIMIT),
        name="expert_ffn",
    )(f_exp, f_valid, xs, wg, wu, wd)


def _combine_kernel(part_tiles, x1_ref, o1_ref, o2_ref, gate_ref, fnw_ref, *y_refs):
    i = pl.program_id(0)
    lo1, hi1 = _unpack_bf16_pairs(o1_ref[...])
    lo2, hi2 = _unpack_bf16_pairs(o2_ref[...])
    g1 = gate_ref[:, 0:1]
    g2 = gate_ref[:, 1:2]
    half = D_MODEL // 2
    x1 = x1_ref[...]
    fnw = fnw_ref[...]
    ya = x1[:, :half] + g1 * lo1 + g2 * lo2
    yb = x1[:, half:] + g1 * hi1 + g2 * hi2
    scale = lax.rsqrt((jnp.sum(ya * ya, axis=-1, keepdims=True) + jnp.sum(yb * yb, axis=-1, keepdims=True))
                      / D_MODEL + EPS)
    y = jnp.concatenate([ya * scale * fnw[:, :half], yb * scale * fnw[:, half:]], axis=1)
    lo_tile = 0
    for y_ref, tiles in zip(y_refs, part_tiles):
        @pl.when(jnp.logical_and(i >= lo_tile, i < lo_tile + tiles))
        def _(y_ref=y_ref):
            _store_part_tile(y_ref, y)
        lo_tile += tiles


def _combine(x1, o12, gates, fnw, part_shapes):
    n = x1.shape[0]
    parts = [jax.ShapeDtypeStruct(s, F32) for s in part_shapes]
    tm = _pick_tile(np.gcd.reduce([_part_rows(p) for p in parts]), TOKEN_TILES)
    n_tiles = n // tm
    out_specs, bounds = _token_parts_specs(parts, tm)
    part_tiles = [bounds[k + 1] - bounds[k] for k in range(len(parts))]
    return pl.pallas_call(
        functools.partial(_combine_kernel, part_tiles),
        grid=(n_tiles,),
        in_specs=[pl.BlockSpec((tm, D_MODEL), lambda i: (i, 0)),
                  pl.BlockSpec((tm, D_MODEL // 2), lambda i: (i, 0)),
                  pl.BlockSpec((tm, D_MODEL // 2), lambda i: (i + n_tiles, 0)),
                  pl.BlockSpec((tm, 128), lambda i: (i, 0)),
                  pl.BlockSpec((1, D_MODEL), lambda i: (0, 0))],
        out_specs=out_specs,
        out_shape=parts,
        compiler_params=pltpu.CompilerParams(dimension_semantics=("arbitrary",),
                                             vmem_limit_bytes=VMEM_LIMIT),
        name="combine",
    )(x1, o12, o12, gates, fnw)


def _moe(x, mixed, wo, nw, router_w, wg, wu, wd, fnw, part_shapes):
    i32 = jnp.int32
    n = x.shape[0]
    assert n % TOK_BLK == 0
    wrt = jnp.pad(router_w.T, ((0, 16 - N_EXPERTS), (0, 0))).astype(BF16)
    t = np.arange(TOK_BLK)
    upper = jnp.asarray((t[:, None] < t[None, :]).astype(np.float32), dtype=BF16)
    x1, hn_p, meta, gates, count = _router(x, mixed, wo, nw, wrt, upper)

    n_slots = (-(-2 * n // FFN_TILE) + N_EXPERTS) * FFN_TILE
    counts = count[:N_EXPERTS, 0].astype(i32)
    gsize = (counts + FFN_TILE - 1) // FFN_TILE * FFN_TILE
    gend = jnp.cumsum(gsize)
    goff = gend - gsize
    fstart = jnp.arange(n_slots // FFN_TILE, dtype=i32) * FFN_TILE
    f_valid = (fstart < gend[-1]).astype(i32)
    f_exp = jnp.sum((fstart[:, None] >= gend[None, :]).astype(i32), axis=1)
    f_exp = jnp.minimum(f_exp, jnp.sum((gend[-1] - 1 >= gend).astype(i32)))
    expert_ids = jnp.arange(N_EXPERTS, dtype=i32)[:, None]
    top = meta[0:2].astype(i32)
    rank = meta[2:4].astype(i32)
    slot = jnp.stack([jnp.sum(jnp.where(top[k][None, :] == expert_ids, goff[:, None], 0), axis=0) + rank[k]
                      for k in range(2)]).reshape(-1)
    pad = gsize - counts
    cpad = jnp.cumsum(pad)
    j = jnp.arange(n_slots - 2 * n, dtype=i32)
    pad_e = jnp.sum((j[:, None] >= cpad[None, :]).astype(i32), axis=1)
    pad_base = jnp.sum(jnp.where(pad_e[:, None] == expert_ids.T, (goff + counts - (cpad - pad))[None, :], 0), axis=1)
    pad_slot = jnp.where(pad_e < N_EXPERTS, pad_base + j, gend[-1] + j - cpad[-1])

    hn_p, wg, wd = lax.optimization_barrier((hn_p, wg, wd))
    xs_p = _sc_row_scatter(hn_p, jnp.concatenate([slot, pad_slot]))
    out_p = _expert_ffn(f_exp, f_valid, xs_p, wg.astype(BF16), wu, wd.astype(BF16))
    o12 = _sc_row_gather(out_p, slot)
    return _combine(x1, o12, gates, fnw, part_shapes)


def _rope_table(seq_len, dec_len):
    half = ROPE_DIMS // 2
    pos = jnp.concatenate([jnp.arange(seq_len), PAST_LEN + jnp.arange(dec_len)]).astype(F32)
    inv_freq = jnp.power(jnp.float32(ROPE_THETA), -jnp.arange(half, dtype=F32) / half)
    ang = pos[:, None] * inv_freq[None, :]
    cs = jnp.concatenate([jnp.cos(ang), jnp.sin(ang)], axis=1)
    lane = np.arange(128) % HEAD
    sel = np.zeros((2 * half, 3 * 128), np.float32)
    for l in range(128):
        if lane[l] < ROPE_DIMS:
            sel[lane[l] % half, l] = 1.0
            if lane[l] < half:
                sel[half + lane[l], 128 + l] = -1.0
            else:
                sel[half + lane[l] - half, 256 + l] = 1.0
    one = np.concatenate([(lane >= ROPE_DIMS).astype(np.float32), np.zeros(256, np.float32)])
    return jnp.dot(cs, jnp.asarray(sel), precision=lax.Precision.HIGHEST) + jnp.asarray(one)[None, :]


def _mixer_consts():
    t = np.arange(CHUNK)
    tri = (t[:, None] >= t[None, :])
    same = (t[:, None] // SUB) == (t[None, :] // SUB)
    c = np.arange(MIX_W)
    ind = (c[:, None] // HEAD) == (c[None, :] // HEAD)
    as_bf = lambda m: jnp.asarray(m.astype(np.float32), dtype=BF16)
    return [as_bf(tri), as_bf(tri & same), as_bf(same), as_bf(ind)]


def _mlstm_state_to_pairs(c, n):
    c = c.astype(F32)
    n_b = jnp.broadcast_to(n.astype(F32)[..., None], c.shape)
    z = jnp.zeros_like(c[:, 0::2])
    top = jnp.concatenate([c[:, 0::2], z, n_b[:, 0::2], z], axis=-1)
    bottom = jnp.concatenate([z, c[:, 1::2], z, n_b[:, 1::2]], axis=-1)
    return jnp.concatenate([top, bottom], axis=-2)


def _mlstm_state_from_pairs(s):
    b = s.shape[0]
    c = jnp.stack([s[:, :, a * HEAD:(a + 1) * HEAD, a * HEAD:(a + 1) * HEAD] for a in range(2)], axis=2)
    n = jnp.stack([s[:, :, a * HEAD:(a + 1) * HEAD, (2 + a) * HEAD] for a in range(2)], axis=2)
    return (c.reshape(b, N_HEADS, HEAD, HEAD), n.reshape(b, N_HEADS, HEAD))


def _pad_lanes(v, width):
    return jnp.pad(v, ((0, 0), (0, width - v.shape[1])))


def kernel(x_prompt, x_sample, state_mlstm_C, state_mlstm_n, state_mlstm_m, state_hgrn_S, cache_swa_k, cache_swa_v, state_ssd_h, state_ssd_conv, norm1_w, w_in, ml_ig_b, ml_fg_b, ml_norm_w, hg_lb_logits, hg_norm_w, sw_sinks, ssd_conv_w, ssd_conv_b, ssd_dt_bias, ssd_A_log, ssd_D, ssd_norm_w, w_out, norm2_w, ffn_w_gate, ffn_w_up, ffn_w_down, moe_router, moe_w_gate, moe_w_up, moe_w_down, final_norm_w):
    depth = w_in.shape[0]
    bp, seq_len, _ = x_prompt.shape
    bs, dec_len, _ = x_sample.shape
    assert seq_len % CHUNK == 0 and dec_len == CHUNK and depth % 2 == 0
    n_seq = bp + bs
    chunks_per_prompt = seq_len // CHUNK

    assert bp == SEQ_GROUP and bs % SEQ_GROUP == 0
    x_parts = (x_prompt.reshape(bp, chunks_per_prompt, CHUNK, D_MODEL), x_sample.reshape(bs * dec_len, D_MODEL))
    part_shapes = tuple(p.shape for p in x_parts)
    rope_tab = _rope_table(seq_len, dec_len)
    consts = _mixer_consts()

    sm = jax.nn.softmax(hg_lb_logits.astype(F32), axis=0)
    hg_lb = jnp.cumsum(sm, axis=0) - sm[0]

    def with_prompt_zeros(a):
        return jnp.concatenate([jnp.zeros((bp,) + a.shape[1:], F32), a.astype(F32)], axis=0)

    state_outs = []
    for l in range(depth):
        lp = {
            "sinks": sw_sinks[l].astype(F32),
            "gbias": _pad_lanes(jnp.concatenate([ml_ig_b[l], ml_fg_b[l], ssd_dt_bias[l]])[None, :].astype(F32), 128),
            "alog": _pad_lanes(ssd_A_log[l][None, :].astype(F32), 128),
            "mlnw": ml_norm_w[l][None, :].astype(F32),
            "hglb": hg_lb[l][None, :],
            "hgnw": hg_norm_w[l][None, :].astype(F32),
            "convw": ssd_conv_w[l].astype(F32),
            "convb": ssd_conv_b[l][None, :].astype(F32),
            "ssdd": jnp.repeat(ssd_D[l].astype(F32), HEAD)[None, :],
            "ssdnw": ssd_norm_w[l][None, :].astype(F32),
        }
        c_aug = _mlstm_state_to_pairs(state_mlstm_C[l], state_mlstm_n[l])
        m_pad = jnp.pad(state_mlstm_m[l][:, None, :], ((0, 0), (0, 7), (0, 128 - N_HEADS)))
        s_t = jnp.swapaxes(state_hgrn_S[l], -1, -2)
        s_t = jnp.moveaxis(s_t, 1, 2).reshape(bs, HEAD, MIX_W)
        cv_pad = jnp.pad(state_ssd_conv[l], ((0, 0), (8 - (SSD_CONV - 1), 0), (0, 0)))
        init = [with_prompt_zeros(a) for a in (
            c_aug, m_pad, s_t, cache_swa_k[l].reshape(bs, WINDOW, 128), cache_swa_v[l].reshape(bs, WINDOW, 128),
            state_ssd_h[l], cv_pad)]

        proj = _inproj(x_parts, norm1_w[l][None, :], w_in, l)
        mixed, c_o, m_o, s_o, k_o, v_o, h_o, cv_o = _mixer(
            proj, rope_tab, init, lp, consts, chunks_per_prompt, n_seq)
        wo = w_out[l].astype(BF16)
        j = l // 2
        if l % 2 == 0:
            x_parts = (_dense_ffn(x_parts, mixed, wo, norm2_w[l][None, :], ffn_w_gate[j].astype(BF16),
                                  ffn_w_up[j].astype(BF16), ffn_w_down[j].astype(BF16)),)
        else:
            assert l == depth - 1 and len(x_parts) == 1
            y_parts = _moe(x_parts[0], mixed, wo, norm2_w[l][None, :], moe_router[j], moe_w_gate[j],
                           moe_w_up[j], moe_w_down[j], final_norm_w[None, :], part_shapes)
        s_back = jnp.moveaxis(s_o.reshape(n_seq, HEAD, N_HEADS, HEAD), 2, 1)
        state_outs.append(_mlstm_state_from_pairs(c_o) + (m_o[:, 0, :N_HEADS],
                           jnp.swapaxes(s_back, -1, -2),
                           k_o.reshape(n_seq, WINDOW, SW_KV, HEAD), v_o.reshape(n_seq, WINDOW, SW_KV, HEAD),
                           h_o, cv_o[:, 8 - (SSD_CONV - 1):, :]))

    y_prompt = y_parts[0].reshape(bp, seq_len, D_MODEL)
    y_sample = y_parts[1].reshape(bs, dec_len, D_MODEL)
    stacked = [jnp.stack([so[k] for so in state_outs]) for k in range(8)]
    return (y_prompt, y_sample) + tuple(s[:, :bp] for s in stacked) + tuple(s[:, bp:] for s in stacked)
```

```python
import functools

import numpy as np
import jax
import jax.numpy as jnp
from jax import lax
from jax.experimental import pallas as pl
from jax.experimental.pallas import tpu as pltpu
from jax.experimental.pallas import tpu_sc as plsc

F32 = jnp.float32
BF16 = jnp.bfloat16

D_MODEL = 1024
CHUNK = 64
EPS = 1e-6
N_HEADS = 4
HEAD = 64
MIX_W = N_HEADS * HEAD
SW_KV = 2
WINDOW = 128
ROPE_DIMS = 16
ROPE_THETA = 500000.0
SSD_N = 128
SSD_CONV = 4
SSD_CONV_DIM = 768
PAST_LEN = 4096
N_EXPERTS = 8
SUB = 16
SEQ_GROUP = 2
HG_SLICE = 4
TOKEN_TILES = (512, 256, 128)

C_MLQ, C_MLK, C_MLV, C_MLO = 0, 256, 512, 768
C_HGQ, C_HGF, C_HGI, C_HGG = 1024, 1280, 1536, 1792
C_SWQ, C_SWK, C_SWV = 2048, 2304, 2432
C_SDZ, C_SDX, C_GATE = 2560, 2816, 3584
D_PROJ = 3712
W_IN_COLS = C_GATE + 3 * N_HEADS
NEG = -1e30
VMEM_LIMIT = 56 * 1024 * 1024
SC_CORES = 2
SC_SUBCORES = 16
SC_MAX_GATHER_ROWS = 128


def _mm(a, b):
    return jnp.dot(a.astype(BF16), b.astype(BF16), preferred_element_type=F32)


def _mm_nt(a, b):
    return lax.dot_general(a.astype(BF16), b.astype(BF16), (((1,), (1,)), ((), ())),
                           preferred_element_type=F32)


def _mm_tn(a, b):
    return jnp.dot(a.T.astype(BF16), b.astype(BF16), preferred_element_type=F32)


def _mm_sel(sel, x):
    hi = x.astype(BF16)
    r1 = x - hi.astype(F32)
    mid = r1.astype(BF16)
    lo = (r1 - mid.astype(F32)).astype(BF16)
    return (jnp.dot(sel, hi, preferred_element_type=F32)
            + jnp.dot(sel, mid, preferred_element_type=F32)
            + jnp.dot(sel, lo, preferred_element_type=F32))


def _sigmoid(x):
    return 1.0 / (1.0 + jnp.exp(-x))


def _silu(x):
    return x * _sigmoid(x)


def _softplus(x):
    return jnp.maximum(x, 0.0) + jnp.log(1.0 + jnp.exp(-jnp.abs(x)))


def _rms(x, w):
    return x * lax.rsqrt(jnp.mean(x * x, axis=-1, keepdims=True) + EPS) * w


def _pick_tile(n, candidates):
    for c in candidates:
        if n % c == 0:
            return c
    raise ValueError(f"no tile for {n}")


def _token_parts_specs(parts, tm):
    bounds = np.cumsum([0] + [_part_rows(p) // tm for p in parts])
    specs = []
    for k, p in enumerate(parts):
        assert _part_rows(p) % tm == 0
        lo, hi = int(bounds[k]), int(bounds[k + 1])
        specs.append(_part_block_spec(p, tm, lo, hi - lo))
    return specs, [int(b) for b in bounds]


def _part_rows(p):
    return int(np.prod(p.shape[:-1]))


def _part_block_spec(p, tm, first_tile, n_tiles):
    idx = lambda i: jnp.clip(i - first_tile, 0, n_tiles - 1)
    if len(p.shape) == 2:
        return pl.BlockSpec((tm, p.shape[1]), lambda i: (idx(i), 0))
    grp, _, chunk, d = p.shape
    assert tm % (grp * chunk) == 0
    return pl.BlockSpec((grp, tm // (grp * chunk), chunk, d), lambda i: (0, idx(i), 0, 0))


def _load_part_tile(ref):
    if len(ref.shape) == 2:
        return ref[...]
    grp, n_chunks = ref.shape[:2]
    return jnp.concatenate([ref[s, c] for c in range(n_chunks) for s in range(grp)], axis=0)


def _store_part_tile(ref, x):
    if len(ref.shape) == 2:
        ref[...] = x
        return
    grp, n_chunks, chunk = ref.shape[:3]
    for c in range(n_chunks):
        for s in range(grp):
            r0 = (c * grp + s) * chunk
            ref[s, c] = x[r0:r0 + chunk]


def _read_token_tile(refs, bounds):
    i = pl.program_id(0)
    x = _load_part_tile(refs[-1])
    for k in range(len(refs) - 2, -1, -1):
        x = jnp.where(i < bounds[k + 1], _load_part_tile(refs[k]), x)
    return x


def _inproj_kernel(n_parts, bounds, *refs):
    x_refs = refs[:n_parts]
    nw_ref, w_ref, o_ref, wp_ref = refs[n_parts:]

    @pl.when(pl.program_id(0) == 0)
    def _():
        rows = 128
        ml_gates = slice(C_MLO, C_MLO + 2 * N_HEADS)
        sd_dt = slice(W_IN_COLS - N_HEADS, W_IN_COLS)
        n_gates = 3 * N_HEADS
        for r0 in range(0, D_MODEL, rows):
            w = w_ref[0, r0:r0 + rows, :]
            wp_ref[r0:r0 + rows, 0:C_MLO] = w[:, 0:ml_gates.start].astype(BF16)
            wp_ref[r0:r0 + rows, C_MLO:C_GATE] = w[:, ml_gates.stop:sd_dt.start].astype(BF16)
            gates = jnp.concatenate([w[:, ml_gates], w[:, sd_dt],
                                     jnp.zeros((rows, D_PROJ - C_GATE - n_gates), F32)], axis=1)
            wp_ref[r0:r0 + rows, C_GATE:D_PROJ] = gates.astype(BF16)

    xn = _rms(_read_token_tile(x_refs, bounds), nw_ref[...]).astype(BF16)
    o_ref[...] = jnp.dot(xn, wp_ref[...], preferred_element_type=F32)


def _inproj(x_parts, nw, w_all, layer):
    n = sum(_part_rows(p) for p in x_parts)
    tm = _pick_tile(np.gcd.reduce([_part_rows(p) for p in x_parts]), TOKEN_TILES)
    x_specs, bounds = _token_parts_specs(x_parts, tm)
    assert w_all.shape[1:] == (D_MODEL, W_IN_COLS)
    return pl.pallas_call(
        functools.partial(_inproj_kernel, len(x_parts), bounds),
        grid=(n // tm,),
        in_specs=x_specs + [pl.BlockSpec((1, D_MODEL), lambda i: (0, 0)),
                            pl.BlockSpec((1, D_MODEL, W_IN_COLS), lambda i: (layer, 0, 0),
                                         pipeline_mode=pl.Buffered(1))],
        out_specs=pl.BlockSpec((tm, D_PROJ), lambda i: (i, 0)),
        out_shape=jax.ShapeDtypeStruct((n, D_PROJ), F32),
        scratch_shapes=[pltpu.VMEM((D_MODEL, D_PROJ), BF16)],
        compiler_params=pltpu.CompilerParams(dimension_semantics=("arbitrary",),
                                             vmem_limit_bytes=VMEM_LIMIT),
        name="inproj",
    )(*x_parts, nw, w_all)


def _head_norm(x, width):
    parts = []
    for g in range(x.shape[1] // width):
        xg = x[:, g * width:(g + 1) * width]
        parts.append(xg * lax.rsqrt(jnp.mean(xg * xg, axis=-1, keepdims=True) + EPS))
    return jnp.concatenate(parts, axis=1)


def _mixer_kernel(prompt_steps,
                  sinks_ref,
                  proj_ref, rope_ref, c0_ref, m0_ref, s0_ref, k0_ref, v0_ref, h0_ref, cv0_ref,
                  gbias_ref, alog_ref, mlnw_ref, hglb_ref, hgnw_ref,
                  convw_ref, convb_ref, ssdd_ref, ssdnw_ref,
                  tri_ref, triblk_ref, blkones_ref, ind_ref,
                  mixed_ref, c_ref, m_ref, s_ref, k_ref, v_ref, h_ref, cv_ref,
                  convbuf, bloc_s, kk_s, hv_s):
    i = pl.program_id(0)
    is_prompt = i < prompt_steps
    is_first = jnp.logical_or(jnp.logical_not(is_prompt), i == 0)
    n_valid = jnp.where(is_prompt, jnp.minimum(i * CHUNK, WINDOW), WINDOW)

    @pl.when(is_first)
    def _():
        c_ref[...] = c0_ref[...]
        m_ref[...] = m0_ref[...]
        s_ref[...] = s0_ref[...]
        k_ref[...] = k0_ref[...]
        v_ref[...] = v0_ref[...]
        h_ref[...] = h0_ref[...]
        cv_ref[...] = cv0_ref[...]

    members = [_mixer_chunk(g, n_valid, sinks_ref, proj_ref, rope_ref,
                            gbias_ref, alog_ref, mlnw_ref, hglb_ref, hgnw_ref,
                            convw_ref, convb_ref, ssdd_ref, ssdnw_ref,
                            tri_ref, triblk_ref, blkones_ref, ind_ref,
                            mixed_ref, c_ref, m_ref, s_ref, k_ref, v_ref, h_ref, cv_ref,
                            convbuf, bloc_s, kk_s, hv_s) for g in range(SEQ_GROUP)]
    for phase in zip(*[m[0] for m in members]):
        chains = [c for per_member in zip(*phase) for c in per_member]
        while chains:
            alive = []
            for chain in chains:
                if next(chain, _CHAIN_DONE) is not _CHAIN_DONE:
                    alive.append(chain)
            chains = alive
    for _, finish in members:
        finish()


_CHAIN_DONE = object()


def _mixer_chunk(g, n_valid, sinks_ref, proj_ref, rope_ref,
                 gbias_ref, alog_ref, mlnw_ref, hglb_ref, hgnw_ref,
                 convw_ref, convb_ref, ssdd_ref, ssdnw_ref,
                 tri_ref, triblk_ref, blkones_ref, ind_ref,
                 mixed_ref, c_ref, m_ref, s_ref, k_ref, v_ref, h_ref, cv_ref,
                 convbuf, bloc_s, kk_s, hv_s):
    rs = slice(g * CHUNK, (g + 1) * CHUNK)
    L = CHUNK
    row = lax.broadcasted_iota(jnp.int32, (L, L), 0)
    col = lax.broadcasted_iota(jnp.int32, (L, L), 1)
    causal = row >= col
    tri = tri_ref[...]

    gate = proj_ref[rs,C_GATE:C_GATE + 128] + gbias_ref[...]
    ig = gate
    lf = pltpu.roll(-_softplus(-gate), 124, 1)
    dtv = pltpu.roll(_softplus(gate), 120, 1)

    b = _mm_sel(tri, lf)
    a = ig - b
    cm = a
    rowg = lax.broadcasted_iota(jnp.int32, (L, 128), 0)
    for sh in (1, 2, 4, 8, 16, 32):
        cm = jnp.where(rowg >= sh, jnp.maximum(cm, pltpu.roll(cm, sh, 0)), cm)
    m_prev = m_ref[g, 0:1, :]
    m_t = b + jnp.maximum(m_prev, cm)
    inter = jnp.exp(b + m_prev - m_t)
    bm = b - m_t
    a_t = a.T
    m_last = m_t[L - 1:L, :]
    b_last = b[L - 1:L, :]
    ws = jnp.exp(b_last + a - m_last)
    decay = jnp.exp(b_last + m_prev - m_last)
    emt = jnp.exp(-m_t)
    m_ref[g, 0:1, :] = m_last

    lane2 = lax.broadcasted_iota(jnp.int32, (L, 2 * HEAD), 1)
    low = lane2 < HEAD
    causal2 = lax.broadcasted_iota(jnp.int32, (L, 2 * HEAD), 0) >= (lane2 & (HEAD - 1))
    ones_blk = ind_ref[0:2 * HEAD, 0:2 * HEAD]
    ones_rows = jnp.ones((L, 2 * HEAD), BF16)
    st_row = lax.broadcasted_iota(jnp.int32, (2 * HEAD, 4 * HEAD), 0) // HEAD
    st_lane = (lax.broadcasted_iota(jnp.int32, (2 * HEAD, 4 * HEAD), 1) // HEAD) % 2
    ml_out = [None] * (N_HEADS // 2)

    def pair_cols(x, p):
        return jnp.where(low, x[:, 2 * p:2 * p + 1], x[:, 2 * p + 1:2 * p + 2])

    def split_rows(x):
        return jnp.concatenate([jnp.where(low, x, 0.0), jnp.where(low, 0.0, x)], axis=0)

    def ml_pair(p):
        cs = slice(2 * p * HEAD, 2 * (p + 1) * HEAD)
        q = proj_ref[rs, C_MLQ + cs.start:C_MLQ + cs.stop]
        k = proj_ref[rs, C_MLK + cs.start:C_MLK + cs.stop] * (HEAD ** -0.5)
        v = proj_ref[rs, C_MLV + cs.start:C_MLV + cs.stop]
        qk = _mm_nt(q, split_rows(k))
        yield
        st = c_ref[g, p]
        from_state = _mm(q, st)
        yield
        a_row = jnp.concatenate([a_t[2 * p:2 * p + 1, :], a_t[2 * p + 1:2 * p + 2, :]], axis=1)
        w = jnp.where(causal2, jnp.exp(pair_cols(bm, p) + a_row), 0.0)
        rhs = jnp.concatenate([split_rows(v).astype(BF16), ones_blk], axis=1)
        gate_in = pair_cols(inter, p)
        nd = (jnp.dot((w * qk).astype(BF16), rhs, preferred_element_type=F32)
              + jnp.concatenate([gate_in, gate_in], axis=1) * from_state)
        yield
        kws = k * pair_cols(ws, p)
        upd = jnp.dot(kws.T.astype(BF16), jnp.concatenate([v.astype(BF16), ones_rows], axis=1),
                      preferred_element_type=F32)
        yield
        den = jnp.maximum(jnp.abs(nd[:, 2 * HEAD:]), pair_cols(emt, p))
        ml_out[p] = nd[:, :2 * HEAD] / den
        dec = jnp.where(st_row == 0, decay[:, 2 * p:2 * p + 1], decay[:, 2 * p + 1:2 * p + 2])
        c_ref[g, p] = dec * st + jnp.where(st_row == st_lane, upd, 0.0)

    def ml_finish():
        ml = _head_norm(jnp.concatenate(ml_out, axis=1), HEAD)
        mixed_ref[rs,0:MIX_W] = (_sigmoid(proj_ref[rs,C_MLO:C_MLO + MIX_W]) * ml
                                 * mlnw_ref[...]).astype(mixed_ref.dtype)

    lb = hglb_ref[...]
    fg = lb + (1.0 - lb) * _sigmoid(proj_ref[rs,C_HGF:C_HGF + MIX_W])
    hq = _silu(proj_ref[rs,C_HGQ:C_HGQ + MIX_W]) * (HEAD ** -0.5)
    kk = 1.0 - fg
    lfg = jnp.log(fg)
    hv = proj_ref[rs,C_HGI:C_HGI + MIX_W]
    bloc = _mm_sel(triblk_ref[...], lfg)
    blast = _mm_sel(blkones_ref[...], lfg)
    qe = hq * jnp.exp(bloc)
    kw = kk * jnp.exp(blast - bloc)
    bloc_s[g] = bloc
    kk_s[g] = kk
    hv_s[g] = hv
    ind = ind_ref[...]
    rows = lax.broadcasted_iota(jnp.int32, (SUB, MIX_W), 0)
    n_blk = L // SUB
    hg_state_part = [None] * n_blk
    hg_block_part = [None] * n_blk

    def hg_within_block(blk):
        r0 = blk * SUB
        q_blk = hq[r0:r0 + SUB]
        b_blk = bloc[r0:r0 + SUB]
        xs = []
        for s in range(SUB):
            r = r0 + s
            e = jnp.exp(jnp.minimum(b_blk - bloc_s[g, r:r + 1, :], 0.0))
            xs.append((q_blk * (kk_s[g, r:r + 1, :] * e)).astype(BF16))
            if s % HG_SLICE == HG_SLICE - 1:
                yield
        att = jnp.dot(jnp.concatenate(xs, axis=0), ind, preferred_element_type=F32)
        yield
        o_blk = att[0:SUB] * hv_s[g, r0:r0 + 1, :]
        for s in range(1, SUB):
            r = r0 + s
            o_blk = o_blk + jnp.where(rows >= s, att[s * SUB:(s + 1) * SUB], 0.0) * hv_s[g, r:r + 1, :]
            if s % HG_SLICE == HG_SLICE - 1:
                yield
        hg_block_part[blk] = o_blk

    def hg_state_chain():
        st = s_ref[g]
        for blk in range(n_blk):
            r0 = blk * SUB
            hg_state_part[blk] = jnp.concatenate(
                [_mm_nt(qe[r0:r0 + SUB, h * HEAD:(h + 1) * HEAD], st[:, h * HEAD:(h + 1) * HEAD])
                 for h in range(N_HEADS)], axis=1)
            upd = jnp.concatenate(
                [_mm_tn(hv[r0:r0 + SUB, h * HEAD:(h + 1) * HEAD], kw[r0:r0 + SUB, h * HEAD:(h + 1) * HEAD])
                 for h in range(N_HEADS)], axis=1)
            yield
            st = st * jnp.exp(blast[r0:r0 + 1, :]) + upd
        s_ref[g] = st

    def hg_finish():
        o = jnp.concatenate([hg_state_part[blk] + hg_block_part[blk] for blk in range(n_blk)], axis=0)
        mixed_ref[rs,MIX_W:2 * MIX_W] = (_head_norm(o, HEAD) * hgnw_ref[...]
                                         * _silu(proj_ref[rs,C_HGG:C_HGG + MIX_W])).astype(mixed_ref.dtype)

    @functools.lru_cache(maxsize=None)
    def sw_prep():
        cos = rope_ref[:, 0:128]
        sin_a = rope_ref[:, 128:256]
        sin_b = rope_ref[:, 256:384]

        def rope(x):
            return x * cos + pltpu.roll(x, 120, 1) * sin_a + pltpu.roll(x, 8, 1) * sin_b

        q_rot = [rope(proj_ref[rs,C_SWQ:C_SWQ + 128]), rope(proj_ref[rs,C_SWQ + 128:C_SWQ + 256])]
        k_rot = rope(proj_ref[rs,C_SWK:C_SWK + 128])
        v_cur = proj_ref[rs,C_SWV:C_SWV + 128]
        colw = lax.broadcasted_iota(jnp.int32, (2 * L, WINDOW), 1)
        row2 = lax.broadcasted_iota(jnp.int32, (2 * L, 1), 0)
        return q_rot, k_rot, v_cur, k_ref[g], v_ref[g], colw >= WINDOW - n_valid, row2

    sw_prep()
    sw_out = [None] * (2 * SW_KV)

    def sw_group(kv):
        q_rot, k_rot, v_cur, k_prev, v_prev, prev_ok, row2 = sw_prep()
        gs = slice(kv * HEAD, (kv + 1) * HEAD)
        q2 = jnp.concatenate([q_rot[kv][:, 0:HEAD], q_rot[kv][:, HEAD:2 * HEAD]], axis=0) * (HEAD ** -0.5)
        s_p = jnp.where(prev_ok, _mm_nt(q2, k_prev[:, gs]), NEG)
        yield
        s_c = _mm_nt(q2, k_rot[:, gs])
        yield
        sink = jnp.where(row2 < L, sinks_ref[2 * kv], sinks_ref[2 * kv + 1])
        mx = jnp.maximum(jnp.maximum(jnp.max(s_p, axis=-1, keepdims=True),
                                     jnp.max(s_c, axis=-1, keepdims=True)), sink)
        e_p = jnp.exp(s_p - mx)
        e_c = jnp.exp(s_c - mx)
        den = (jnp.sum(e_p, axis=-1, keepdims=True) + jnp.sum(e_c, axis=-1, keepdims=True)
               + jnp.exp(sink - mx))
        o_p = _mm(e_p, v_prev[:, gs])
        yield
        o = (o_p + _mm(e_c, v_cur[:, gs])) / den
        yield
        sw_out[2 * kv] = o[0:L]
        sw_out[2 * kv + 1] = o[L:2 * L]

    def sw_finish():
        _, k_rot, v_cur, k_prev, v_prev, _, _ = sw_prep()
        mixed_ref[rs,2 * MIX_W:3 * MIX_W] = jnp.concatenate(sw_out, axis=1).astype(mixed_ref.dtype)
        k_ref[g, 0:WINDOW - L, :] = k_prev[L:WINDOW]
        k_ref[g, WINDOW - L:WINDOW, :] = k_rot
        v_ref[g, 0:WINDOW - L, :] = v_prev[L:WINDOW]
        v_ref[g, WINDOW - L:WINDOW, :] = v_cur

    @functools.lru_cache(maxsize=None)
    def sd_prep():
        convbuf[g, 0:8, :] = cv_ref[g]
        convbuf[g, 8:8 + L, :] = proj_ref[rs,C_SDX:C_SDX + SSD_CONV_DIM]
        acc = convb_ref[...] + convbuf[g, 5:5 + L, :] * convw_ref[0:1, :]
        for j in range(1, SSD_CONV):
            acc = acc + convbuf[g, 5 + j:5 + j + L, :] * convw_ref[j:j + 1, :]
        xbc = _silu(acc)
        cv_ref[g] = convbuf[g, L:L + 8, :]
        a_neg = -jnp.exp(alog_ref[...])
        ca = _mm_sel(tri, dtv * a_neg)
        ca_last = ca[L - 1:L, :]
        return (xbc, xbc[:, 0:MIX_W], ca, ca.T, dtv.T, jnp.exp(ca), jnp.exp(ca_last - ca) * dtv,
                jnp.exp(ca_last))

    sd_prep()
    sd_out = [None] * (N_HEADS // 2)

    def sd_group(grp):
        xbc, xs_all, ca, ca_t, dt_t, eca, wss, sdec = sd_prep()
        bm_g = xbc[:, MIX_W + grp * SSD_N:MIX_W + (grp + 1) * SSD_N]
        cm_g = xbc[:, MIX_W + 2 * SSD_N + grp * SSD_N:MIX_W + 2 * SSD_N + (grp + 1) * SSD_N]
        e, o = 2 * grp, 2 * grp + 1
        x = xs_all[:, e * HEAD:(o + 1) * HEAD]
        cb2 = _mm_nt(cm_g, jnp.concatenate([bm_g, bm_g], axis=0))
        yield
        hst = h_ref[g, e:o + 1].reshape(2 * HEAD, SSD_N)
        from_state = _mm_nt(cm_g, hst)
        yield
        ca_row = jnp.concatenate([ca_t[e:e + 1, :], ca_t[o:o + 1, :]], axis=1)
        dt_row = jnp.concatenate([dt_t[e:e + 1, :], dt_t[o:o + 1, :]], axis=1)
        dec = jnp.where(causal2, jnp.exp(pair_cols(ca, grp) - ca_row), 0.0)
        sd_out[grp] = _mm(cb2 * dec * dt_row, split_rows(x)) + pair_cols(eca, grp) * from_state
        yield
        upd = _mm_tn(x * pair_cols(wss, grp), bm_g)
        yield
        keep = jnp.where(lax.broadcasted_iota(jnp.int32, (2 * HEAD, SSD_N), 0) < HEAD,
                         sdec[:, e:e + 1], sdec[:, o:o + 1])
        h_ref[g, e:o + 1] = (keep * hst + upd).reshape(2, HEAD, SSD_N)

    def sd_finish():
        xs_all = sd_prep()[1]
        y = jnp.concatenate(sd_out, axis=1) + ssdd_ref[...] * xs_all
        gated = y * _silu(proj_ref[rs,C_SDZ:C_SDZ + MIX_W])
        mixed_ref[rs,3 * MIX_W:4 * MIX_W] = (_head_norm(gated, 2 * HEAD)
                                             * ssdnw_ref[...]).astype(mixed_ref.dtype)

    chains = ([[hg_within_block(blk) for blk in range(n_blk)]]
              + [[ml_pair(p)] for p in range(N_HEADS // 2)] + [[hg_state_chain(), sd_group(0)]]
              + [[sd_group(1)]]
              + [[sw_group(kv)] for kv in range(SW_KV)])

    def finish():
        ml_finish()
        hg_finish()
        sw_finish()
        sd_finish()

    return chains, finish


def _mixer(proj, rope_tab, init, lp, consts, chunks_per_prompt, n_seq):
    n = proj.shape[0]
    rows = SEQ_GROUP * CHUNK
    n_steps = n // rows

    def group_of(i):
        return jnp.where(i < chunks_per_prompt, 0, i - chunks_per_prompt + 1)

    def rope_blk(i):
        return jnp.minimum(i, chunks_per_prompt)

    def per_seq(shape):
        nd = len(shape)
        return pl.BlockSpec((SEQ_GROUP,) + shape, lambda i, s: (group_of(i),) + (0,) * nd)

    def const(shape):
        nd = len(shape)
        return pl.BlockSpec(shape, lambda i, s: (0,) * nd)

    state_shapes = [(N_HEADS // 2, 2 * HEAD, 4 * HEAD), (8, 128), (HEAD, MIX_W), (WINDOW, 128), (WINDOW, 128),
                    (N_HEADS, HEAD, SSD_N), (8, SSD_CONV_DIM)]
    in_specs = ([pl.BlockSpec((rows, D_PROJ), lambda i, s: (i, 0)),
                 pl.BlockSpec((CHUNK, 384), lambda i, s: (rope_blk(i), 0))]
                + [per_seq(s) for s in state_shapes]
                + [const((1, 128)), const((1, 128)), const((1, MIX_W)), const((1, MIX_W)), const((1, MIX_W)),
                   const((SSD_CONV, SSD_CONV_DIM)), const((1, SSD_CONV_DIM)), const((1, MIX_W)),
                   const((1, MIX_W)),
                   const((CHUNK, CHUNK)), const((CHUNK, CHUNK)), const((CHUNK, CHUNK)), const((MIX_W, MIX_W))])
    out_specs = ([pl.BlockSpec((rows, D_MODEL), lambda i, s: (i, 0))]
                 + [per_seq(s) for s in state_shapes])
    out_shape = ([jax.ShapeDtypeStruct((n, D_MODEL), BF16)]
                 + [jax.ShapeDtypeStruct((n_seq,) + s, F32) for s in state_shapes])
    grid_spec = pltpu.PrefetchScalarGridSpec(
        num_scalar_prefetch=1, grid=(n_steps,), in_specs=in_specs, out_specs=out_specs,
        scratch_shapes=[pltpu.VMEM((SEQ_GROUP, CHUNK + 8, SSD_CONV_DIM), F32),
                        pltpu.VMEM((SEQ_GROUP, CHUNK, MIX_W), F32), pltpu.VMEM((SEQ_GROUP, CHUNK, MIX_W), F32),
                        pltpu.VMEM((SEQ_GROUP, CHUNK, MIX_W), F32)])
    return pl.pallas_call(
        functools.partial(_mixer_kernel, chunks_per_prompt),
        grid_spec=grid_spec, out_shape=out_shape,
        compiler_params=pltpu.CompilerParams(dimension_semantics=("arbitrary",),
                                             vmem_limit_bytes=VMEM_LIMIT),
        name="mixer",
    )(lp["sinks"], proj, rope_tab, *init,
      lp["gbias"], lp["alog"], lp["mlnw"], lp["hglb"], lp["hgnw"],
      lp["convw"], lp["convb"], lp["ssdd"], lp["ssdnw"], *consts)


def _dense_ffn_kernel(f_chunk, n_parts, bounds, *refs):
    x_refs = refs[:n_parts]
    mix_ref, wo_ref, nw_ref, wg_ref, wu_ref, wd_ref, o_ref = refs[n_parts:]
    x1 = _read_token_tile(x_refs, bounds) + jnp.dot(mix_ref[...], wo_ref[...], preferred_element_type=F32)
    hn = _rms(x1, nw_ref[...]).astype(BF16)
    o_ref[...] = x1
    for f0 in range(0, wg_ref.shape[1], f_chunk):
        g = jnp.dot(hn, wg_ref[:, f0:f0 + f_chunk], preferred_element_type=F32)
        u = jnp.dot(hn, wu_ref[:, f0:f0 + f_chunk], preferred_element_type=F32)
        act = (_silu(g) * u).astype(BF16)
        o_ref[...] += jnp.dot(act, wd_ref[f0:f0 + f_chunk, :], preferred_element_type=F32)


def _dense_ffn(x_parts, mixed, wo, nw, wg, wu, wd):
    n = mixed.shape[0]
    d_ff = wg.shape[1]
    tm = _pick_tile(np.gcd.reduce([_part_rows(p) for p in x_parts]), TOKEN_TILES)
    x_specs, bounds = _token_parts_specs(x_parts, tm)
    full = lambda shape: pl.BlockSpec(shape, lambda i: (0, 0))
    return pl.pallas_call(
        functools.partial(_dense_ffn_kernel, 256, len(x_parts), bounds),
        grid=(n // tm,),
        in_specs=x_specs + [
                  pl.BlockSpec((tm, D_MODEL), lambda i: (i, 0)),
                  full((D_MODEL, D_MODEL)), full((1, D_MODEL)),
                  full((D_MODEL, d_ff)), full((D_MODEL, d_ff)), full((d_ff, D_MODEL))],
        out_specs=pl.BlockSpec((tm, D_MODEL), lambda i: (i, 0)),
        out_shape=jax.ShapeDtypeStruct((n, D_MODEL), F32),
        compiler_params=pltpu.CompilerParams(dimension_semantics=("arbitrary",),
                                             vmem_limit_bytes=VMEM_LIMIT),
        name="dense_ffn",
    )(*x_parts, mixed, wo, nw, wg, wu, wd)


TOK_BLK = 256
FFN_TILE = 1024


def _pack_bf16_pairs(x):
    w = x.shape[1] // 2
    bits = pltpu.bitcast(x.astype(BF16).astype(F32), jnp.int32)
    return lax.shift_right_logical(bits[:, :w], 16) | bits[:, w:]


def _unpack_bf16_pairs(p):
    lo = pltpu.bitcast(lax.shift_left(p, 16), F32)
    hi = pltpu.bitcast(p & jnp.int32(-65536), F32)
    return lo, hi


def _router_kernel(x_ref, mix_ref, wo_ref, nw_ref, wrt_ref, upper_ref,
                   x1_ref, hn_ref, meta_ref, gate_ref, count_ref, carry):
    @pl.when(pl.program_id(0) == 0)
    def _():
        carry[...] = jnp.zeros_like(carry)

    x1 = x_ref[...] + jnp.dot(mix_ref[...], wo_ref[...], preferred_element_type=F32)
    x1_ref[...] = x1
    hn_f = _rms(x1, nw_ref[...])
    hn = hn_f.astype(BF16)
    hn_ref[...] = _pack_bf16_pairs(hn_f)
    logits = lax.dot_general(wrt_ref[...], hn, (((1,), (1,)), ((), ())),
                             preferred_element_type=F32)
    sub = lax.broadcasted_iota(jnp.int32, logits.shape, 0)
    l1 = jnp.where(sub < N_EXPERTS, logits, NEG)
    m1 = jnp.max(l1, axis=0, keepdims=True)
    i1 = jnp.min(jnp.where(l1 == m1, sub, 99), axis=0, keepdims=True)
    l2 = jnp.where(sub == i1, NEG, l1)
    m2 = jnp.max(l2, axis=0, keepdims=True)
    i2 = jnp.min(jnp.where(l2 == m2, sub, 99), axis=0, keepdims=True)
    e2 = jnp.exp(m2 - m1)
    g1 = 1.0 / (1.0 + e2)
    g2 = e2 / (1.0 + e2)
    sel = jnp.where(sub == i1, 1.0, jnp.where(sub == i2, 1.0, 0.0))
    before = jnp.dot(sel.astype(BF16), upper_ref[...], preferred_element_type=F32) + carry[:, 0:1]
    r1 = jnp.sum(jnp.where(sub == i1, before, 0.0), axis=0, keepdims=True)
    r2 = jnp.sum(jnp.where(sub == i2, before, 0.0), axis=0, keepdims=True)
    meta_ref[...] = jnp.concatenate([i1.astype(F32), i2.astype(F32), r1, r2,
                                     jnp.zeros((4, r1.shape[1]), F32)], axis=0)
    gate_ref[...] = jnp.concatenate([g1, g2, jnp.zeros((126, r1.shape[1]), F32)], axis=0).T
    total = carry[...] + jnp.sum(sel, axis=1, keepdims=True)
    carry[...] = total
    count_ref[...] = total


def _router(x, mixed, wo, nw, wrt, upper):
    n = x.shape[0]
    tm = TOK_BLK
    full = lambda shape: pl.BlockSpec(shape, lambda i: (0,) * len(shape))
    tok = lambda w: pl.BlockSpec((tm, w), lambda i: (i, 0))
    return pl.pallas_call(
        _router_kernel,
        grid=(n // tm,),
        in_specs=[tok(D_MODEL), tok(D_MODEL), full((D_MODEL, D_MODEL)), full((1, D_MODEL)),
                  full((16, D_MODEL)), full((tm, tm))],
        out_specs=[tok(D_MODEL), tok(D_MODEL // 2),
                   pl.BlockSpec((8, tm), lambda i: (0, i)), tok(128), full((16, 128))],
        out_shape=[jax.ShapeDtypeStruct((n, D_MODEL), F32), jax.ShapeDtypeStruct((n, D_MODEL // 2), jnp.int32),
                   jax.ShapeDtypeStruct((8, n), F32), jax.ShapeDtypeStruct((n, 128), F32),
                   jax.ShapeDtypeStruct((16, 128), F32)],
        scratch_shapes=[pltpu.VMEM((16, 128), F32)],
        compiler_params=pltpu.CompilerParams(dimension_semantics=("arbitrary",),
                                             vmem_limit_bytes=VMEM_LIMIT),
        name="router",
    )(x, mixed, wo, nw, wrt, upper)


def _sc_row_gather(table, idx):
    n_workers = SC_CORES * SC_SUBCORES
    b = idx.shape[0]
    d = table.shape[1]
    assert b % (8 * n_workers) == 0
    per_worker = b // n_workers
    chunk = max(c for c in range(8, SC_MAX_GATHER_ROWS + 1, 8) if per_worker % c == 0)
    mesh = plsc.VectorSubcoreMesh(core_axis_name="c", subcore_axis_name="s")

    @functools.partial(
        pl.kernel, mesh=mesh, out_type=jax.ShapeDtypeStruct((b, d), table.dtype),
        scratch_types=[pltpu.VMEM((chunk,), jnp.int32), pltpu.VMEM((chunk, d), table.dtype),
                       pltpu.SemaphoreType.DMA])
    def gather(table_hbm, idx_hbm, out_hbm, idx_v, rows_v, sem):
        worker = lax.axis_index("s") * SC_CORES + lax.axis_index("c")
        base = worker * per_worker

        @pl.loop(0, per_worker // chunk)
        def _(c):
            off = pl.multiple_of(base + c * chunk, 8)
            pltpu.sync_copy(idx_hbm.at[pl.ds(off, chunk)], idx_v)
            pltpu.async_copy(table_hbm.at[idx_v], rows_v, sem).wait()
            pltpu.sync_copy(rows_v, out_hbm.at[pl.ds(off, chunk)])

    return gather(table, idx)


def _sc_row_scatter(rows, idx):
    n_workers = SC_CORES * SC_SUBCORES
    b = n_out = idx.shape[0]
    v, d = rows.shape
    assert b % (8 * n_workers) == 0
    per_worker = b // n_workers
    chunk = max(c for c in range(8, SC_MAX_GATHER_ROWS + 1, 8) if per_worker % c == 0 and v % c == 0)
    mesh = plsc.VectorSubcoreMesh(core_axis_name="c", subcore_axis_name="s")

    @functools.partial(
        pl.kernel, mesh=mesh, out_type=jax.ShapeDtypeStruct((n_out, d), rows.dtype),
        scratch_types=[pltpu.VMEM((chunk,), jnp.int32), pltpu.VMEM((chunk, d), rows.dtype)])
    def scatter(rows_hbm, idx_hbm, out_hbm, idx_v, rows_v):
        worker = lax.axis_index("s") * SC_CORES + lax.axis_index("c")
        base = worker * per_worker

        @pl.loop(0, per_worker // chunk)
        def _(c):
            off = pl.multiple_of(base + c * chunk, 8)
            pltpu.sync_copy(idx_hbm.at[pl.ds(off, chunk)], idx_v)
            pltpu.sync_copy(rows_hbm.at[pl.ds(pl.multiple_of(lax.rem(off, v), 8), chunk)], rows_v)
            pltpu.sync_copy(rows_v, out_hbm.at[idx_v])

    return scatter(rows, idx)


def _expert_ffn_kernel(texp_ref, tvalid_ref, xs_ref, wg_ref, wu_ref, wd_ref, o_ref, acc_ref, xb_ref):
    j = pl.program_id(0)
    f = pl.program_id(1)
    last = f == pl.num_programs(1) - 1
    valid = tvalid_ref[j] == 1
    half = D_MODEL // 2

    @pl.when(jnp.logical_and(valid, f == 0))
    def _():
        lo, hi = _unpack_bf16_pairs(xs_ref[...])
        xb_ref[:, :half] = lo.astype(BF16)
        xb_ref[:, half:] = hi.astype(BF16)

        acc_ref[...] = jnp.zeros_like(acc_ref)

    @pl.when(valid)
    def _():
        xs = xb_ref[...]
        g = jnp.dot(xs, wg_ref[0], preferred_element_type=F32)
        u = jnp.dot(xs, wu_ref[0].astype(BF16), preferred_element_type=F32)
        act = (_silu(g) * u).astype(BF16)
        acc_ref[...] += jnp.dot(act, wd_ref[0].astype(BF16), preferred_element_type=F32)

        @pl.when(last)
        def _():
            o_ref[...] = _pack_bf16_pairs(acc_ref[...])

    @pl.when(jnp.logical_and(jnp.logical_not(valid), last))
    def _():
        o_ref[...] = jnp.zeros_like(o_ref)


def _expert_ffn(f_exp, f_valid, xs, wg, wu, wd):
    n_slots = xs.shape[0]
    d_ff = wg.shape[2]
    tf = _pick_tile(d_ff, (512, 256, 128))
    grid_spec = pltpu.PrefetchScalarGridSpec(
        num_scalar_prefetch=2, grid=(n_slots // FFN_TILE, d_ff // tf),
        in_specs=[pl.BlockSpec((FFN_TILE, D_MODEL // 2), lambda j, f, te, tv: (j, 0)),
                  pl.BlockSpec((1, D_MODEL, tf), lambda j, f, te, tv: (te[j], 0, f)),
                  pl.BlockSpec((1, D_MODEL, tf), lambda j, f, te, tv: (te[j], 0, f)),
                  pl.BlockSpec((1, tf, D_MODEL), lambda j, f, te, tv: (te[j], f, 0))],
        out_specs=pl.BlockSpec((FFN_TILE, D_MODEL // 2), lambda j, f, te, tv: (j, 0)),
        scratch_shapes=[pltpu.VMEM((FFN_TILE, D_MODEL), F32), pltpu.VMEM((FFN_TILE, D_MODEL), BF16)])
    return pl.pallas_call(
        _expert_ffn_kernel, grid_spec=grid_spec,
        out_shape=jax.ShapeDtypeStruct((n_slots, D_MODEL // 2), jnp.int32),
        compiler_params=pltpu.CompilerParams(dimension_semantics=("arbitrary", "arbitrary"),
                                             vmem_limit_bytes=VMEM_LIMIT),
        name="expert_ffn",
    )(f_exp, f_valid, xs, wg, wu, wd)


def _combine_kernel(part_tiles, x1_ref, o1_ref, o2_ref, gate_ref, fnw_ref, *y_refs):
    i = pl.program_id(0)
    lo1, hi1 = _unpack_bf16_pairs(o1_ref[...])
    lo2, hi2 = _unpack_bf16_pairs(o2_ref[...])
    g1 = gate_ref[:, 0:1]
    g2 = gate_ref[:, 1:2]
    half = D_MODEL // 2
    x1 = x1_ref[...]
    fnw = fnw_ref[...]
    ya = x1[:, :half] + g1 * lo1 + g2 * lo2
    yb = x1[:, half:] + g1 * hi1 + g2 * hi2
    scale = lax.rsqrt((jnp.sum(ya * ya, axis=-1, keepdims=True) + jnp.sum(yb * yb, axis=-1, keepdims=True))
                      / D_MODEL + EPS)
    y = jnp.concatenate([ya * scale * fnw[:, :half], yb * scale * fnw[:, half:]], axis=1)
    lo_tile = 0
    for y_ref, tiles in zip(y_refs, part_tiles):
        @pl.when(jnp.logical_and(i >= lo_tile, i < lo_tile + tiles))
        def _(y_ref=y_ref):
            _store_part_tile(y_ref, y)
        lo_tile += tiles


def _combine(x1, o12, gates, fnw, part_shapes):
    n = x1.shape[0]
    parts = [jax.ShapeDtypeStruct(s, F32) for s in part_shapes]
    tm = _pick_tile(np.gcd.reduce([_part_rows(p) for p in parts]), TOKEN_TILES)
    n_tiles = n // tm
    out_specs, bounds = _token_parts_specs(parts, tm)
    part_tiles = [bounds[k + 1] - bounds[k] for k in range(len(parts))]
    return pl.pallas_call(
        functools.partial(_combine_kernel, part_tiles),
        grid=(n_tiles,),
        in_specs=[pl.BlockSpec((tm, D_MODEL), lambda i: (i, 0)),
                  pl.BlockSpec((tm, D_MODEL // 2), lambda i: (i, 0)),
                  pl.BlockSpec((tm, D_MODEL // 2), lambda i: (i + n_tiles, 0)),
                  pl.BlockSpec((tm, 128), lambda i: (i, 0)),
                  pl.BlockSpec((1, D_MODEL), lambda i: (0, 0))],
        out_specs=out_specs,
        out_shape=parts,
        compiler_params=pltpu.CompilerParams(dimension_semantics=("arbitrary",),
                                             vmem_limit_bytes=VMEM_LIMIT),
        name="combine",
    )(x1, o12, o12, gates, fnw)


def _moe(x, mixed, wo, nw, router_w, wg, wu, wd, fnw, part_shapes):
    i32 = jnp.int32
    n = x.shape[0]
    assert n % TOK_BLK == 0
    wrt = jnp.pad(router_w.T, ((0, 16 - N_EXPERTS), (0, 0))).astype(BF16)
    t = np.arange(TOK_BLK)
    upper = jnp.asarray((t[:, None] < t[None, :]).astype(np.float32), dtype=BF16)
    x1, hn_p, meta, gates, count = _router(x, mixed, wo, nw, wrt, upper)

    n_slots = (-(-2 * n // FFN_TILE) + N_EXPERTS) * FFN_TILE
    counts = count[:N_EXPERTS, 0].astype(i32)
    gsize = (counts + FFN_TILE - 1) // FFN_TILE * FFN_TILE
    gend = jnp.cumsum(gsize)
    goff = gend - gsize
    fstart = jnp.arange(n_slots // FFN_TILE, dtype=i32) * FFN_TILE
    f_valid = (fstart < gend[-1]).astype(i32)
    f_exp = jnp.sum((fstart[:, None] >= gend[None, :]).astype(i32), axis=1)
    f_exp = jnp.minimum(f_exp, jnp.sum((gend[-1] - 1 >= gend).astype(i32)))
    expert_ids = jnp.arange(N_EXPERTS, dtype=i32)[:, None]
    top = meta[0:2].astype(i32)
    rank = meta[2:4].astype(i32)
    slot = jnp.stack([jnp.sum(jnp.where(top[k][None, :] == expert_ids, goff[:, None], 0), axis=0) + rank[k]
                      for k in range(2)]).reshape(-1)
    pad = gsize - counts
    cpad = jnp.cumsum(pad)
    j = jnp.arange(n_slots - 2 * n, dtype=i32)
    pad_e = jnp.sum((j[:, None] >= cpad[None, :]).astype(i32), axis=1)
    pad_base = jnp.sum(jnp.where(pad_e[:, None] == expert_ids.T, (goff + counts - (cpad - pad))[None, :], 0), axis=1)
    pad_slot = jnp.where(pad_e < N_EXPERTS, pad_base + j, gend[-1] + j - cpad[-1])

    hn_p, wg = lax.optimization_barrier((hn_p, wg))
    xs_p = _sc_row_scatter(hn_p, jnp.concatenate([slot, pad_slot]))
    out_p = _expert_ffn(f_exp, f_valid, xs_p, wg.astype(BF16), wu, wd)
    o12 = _sc_row_gather(out_p, slot)
    return _combine(x1, o12, gates, fnw, part_shapes)


def _rope_table(seq_len, dec_len):
    half = ROPE_DIMS // 2
    pos = jnp.concatenate([jnp.arange(seq_len), PAST_LEN + jnp.arange(dec_len)]).astype(F32)
    inv_freq = jnp.power(jnp.float32(ROPE_THETA), -jnp.arange(half, dtype=F32) / half)
    lane = np.arange(128) % HEAD
    inv_lane = jnp.where(lane < ROPE_DIMS, inv_freq[lane % half], 0.0)
    ang = pos[:, None] * inv_lane[None, :]
    cos, sin = jnp.cos(ang), jnp.sin(ang)
    sin_a = jnp.where(lane < half, -sin, 0.0)
    sin_b = jnp.where((lane >= half) & (lane < ROPE_DIMS), sin, 0.0)
    return jnp.concatenate([cos, sin_a, sin_b], axis=1)


def _mixer_consts():
    t = np.arange(CHUNK)
    tri = (t[:, None] >= t[None, :])
    same = (t[:, None] // SUB) == (t[None, :] // SUB)
    c = np.arange(MIX_W)
    ind = (c[:, None] // HEAD) == (c[None, :] // HEAD)
    as_bf = lambda m: jnp.asarray(m.astype(np.float32), dtype=BF16)
    return [as_bf(tri), as_bf(tri & same), as_bf(same), as_bf(ind)]


def _mlstm_state_to_pairs(c, n):
    c = c.astype(F32)
    n_b = jnp.broadcast_to(n.astype(F32)[..., None], c.shape)
    z = jnp.zeros_like(c[:, 0::2])
    top = jnp.concatenate([c[:, 0::2], z, n_b[:, 0::2], z], axis=-1)
    bottom = jnp.concatenate([z, c[:, 1::2], z, n_b[:, 1::2]], axis=-1)
    return jnp.concatenate([top, bottom], axis=-2)


def _mlstm_state_from_pairs(s):
    b = s.shape[0]
    c = jnp.stack([s[:, :, a * HEAD:(a + 1) * HEAD, a * HEAD:(a + 1) * HEAD] for a in range(2)], axis=2)
    n = jnp.stack([s[:, :, a * HEAD:(a + 1) * HEAD, (2 + a) * HEAD] for a in range(2)], axis=2)
    return (c.reshape(b, N_HEADS, HEAD, HEAD), n.reshape(b, N_HEADS, HEAD))


def _pad_lanes(v, width):
    return jnp.pad(v, ((0, 0), (0, width - v.shape[1])))


def kernel(x_prompt, x_sample, state_mlstm_C, state_mlstm_n, state_mlstm_m, state_hgrn_S, cache_swa_k, cache_swa_v, state_ssd_h, state_ssd_conv, norm1_w, w_in, ml_ig_b, ml_fg_b, ml_norm_w, hg_lb_logits, hg_norm_w, sw_sinks, ssd_conv_w, ssd_conv_b, ssd_dt_bias, ssd_A_log, ssd_D, ssd_norm_w, w_out, norm2_w, ffn_w_gate, ffn_w_up, ffn_w_down, moe_router, moe_w_gate, moe_w_up, moe_w_down, final_norm_w):
    depth = w_in.shape[0]
    bp, seq_len, _ = x_prompt.shape
    bs, dec_len, _ = x_sample.shape
    assert seq_len % CHUNK == 0 and dec_len == CHUNK and depth % 2 == 0
    n_seq = bp + bs
    chunks_per_prompt = seq_len // CHUNK

    assert bp == SEQ_GROUP and bs % SEQ_GROUP == 0
    x_parts = (x_prompt.reshape(bp, chunks_per_prompt, CHUNK, D_MODEL), x_sample.reshape(bs * dec_len, D_MODEL))
    part_shapes = tuple(p.shape for p in x_parts)
    rope_tab = _rope_table(seq_len, dec_len)
    consts = _mixer_consts()

    sm = jax.nn.softmax(hg_lb_logits.astype(F32), axis=0)
    hg_lb = jnp.cumsum(sm, axis=0) - sm[0]

    def with_prompt_zeros(a):
        return jnp.concatenate([jnp.zeros((bp,) + a.shape[1:], F32), a.astype(F32)], axis=0)

    state_outs = []
    for l in range(depth):
        lp = {
            "sinks": sw_sinks[l].astype(F32),
            "gbias": _pad_lanes(jnp.concatenate([ml_ig_b[l], ml_fg_b[l], ssd_dt_bias[l]])[None, :].astype(F32), 128),
            "alog": _pad_lanes(ssd_A_log[l][None, :].astype(F32), 128),
            "mlnw": ml_norm_w[l][None, :].astype(F32),
            "hglb": hg_lb[l][None, :],
            "hgnw": hg_norm_w[l][None, :].astype(F32),
            "convw": ssd_conv_w[l].astype(F32),
            "convb": ssd_conv_b[l][None, :].astype(F32),
            "ssdd": jnp.repeat(ssd_D[l].astype(F32), HEAD)[None, :],
            "ssdnw": ssd_norm_w[l][None, :].astype(F32),
        }
        c_aug = _mlstm_state_to_pairs(state_mlstm_C[l], state_mlstm_n[l])
        m_pad = jnp.pad(state_mlstm_m[l][:, None, :], ((0, 0), (0, 7), (0, 128 - N_HEADS)))
        s_t = jnp.swapaxes(state_hgrn_S[l], -1, -2)
        s_t = jnp.moveaxis(s_t, 1, 2).reshape(bs, HEAD, MIX_W)
        cv_pad = jnp.pad(state_ssd_conv[l], ((0, 0), (8 - (SSD_CONV - 1), 0), (0, 0)))
        init = [with_prompt_zeros(a) for a in (
            c_aug, m_pad, s_t, cache_swa_k[l].reshape(bs, WINDOW, 128), cache_swa_v[l].reshape(bs, WINDOW, 128),
            state_ssd_h[l], cv_pad)]

        proj = _inproj(x_parts, norm1_w[l][None, :], w_in, l)
        mixed, c_o, m_o, s_o, k_o, v_o, h_o, cv_o = _mixer(
            proj, rope_tab, init, lp, consts, chunks_per_prompt, n_seq)
        wo = w_out[l].astype(BF16)
        j = l // 2
        if l % 2 == 0:
            x_parts = (_dense_ffn(x_parts, mixed, wo, norm2_w[l][None, :], ffn_w_gate[j].astype(BF16),
                                  ffn_w_up[j].astype(BF16), ffn_w_down[j].astype(BF16)),)
        else:
            assert l == depth - 1 and len(x_parts) == 1
            y_parts = _moe(x_parts[0], mixed, wo, norm2_w[l][None, :], moe_router[j], moe_w_gate[j],
                           moe_w_up[j], moe_w_down[j], final_norm_w[None, :], part_shapes)
        s_back = jnp.moveaxis(s_o.reshape(n_seq, HEAD, N_HEADS, HEAD), 2, 1)
        state_outs.append(_mlstm_state_from_pairs(c_o) + (m_o[:, 0, :N_HEADS],
                           jnp.swapaxes(s_back, -1, -2),
                           k_o.reshape(n_seq, WINDOW, SW_KV, HEAD), v_o.reshape(n_seq, WINDOW, SW_KV, HEAD),
                           h_o, cv_o[:, 8 - (SSD_CONV - 1):, :]))

    y_prompt = y_parts[0].reshape(bp, seq_len, D_MODEL)
    y_sample = y_parts[1].reshape(bs, dec_len, D_MODEL)
    stacked = [jnp.stack([so[k] for so in state_outs]) for k in range(8)]
    return (y_prompt, y_sample) + tuple(s[:, :bp] for s in stacked) + tuple(s[:, bp:] for s in stacked)
```

```python
import functools

import numpy as np
import jax
import jax.numpy as jnp
from jax import lax
from jax.experimental import pallas as pl
from jax.experimental.pallas import tpu as pltpu
from jax.experimental.pallas import tpu_sc as plsc

F32 = jnp.float32
BF16 = jnp.bfloat16

D_MODEL = 1024
CHUNK = 64
EPS = 1e-6
N_HEADS = 4
HEAD = 64
MIX_W = N_HEADS * HEAD
SW_KV = 2
WINDOW = 128
ROPE_DIMS = 16
ROPE_THETA = 500000.0
SSD_N = 128
SSD_CONV = 4
SSD_CONV_DIM = 768
PAST_LEN = 4096
N_EXPERTS = 8
SUB = 16
SEQ_GROUP = 2
HG_SLICE = 4
TOKEN_TILES = (512, 256, 128)

C_MLQ, C_MLK, C_MLV, C_MLO = 0, 256, 512, 768
C_HGQ, C_HGF, C_HGI, C_HGG = 1024, 1280, 1536, 1792
C_SWQ, C_SWK, C_SWV = 2048, 2304, 2432
C_SDZ, C_SDX, C_GATE = 2560, 2816, 3584
D_PROJ = 3712
W_IN_COLS = C_GATE + 3 * N_HEADS
NEG = -1e30
V7X_VMEM_BYTES = 64 * 1024 * 1024
VMEM_LIMIT = V7X_VMEM_BYTES - 8 * 1024 * 1024
SC_CORES = 2
SC_SUBCORES = 16
SC_MAX_GATHER_ROWS = 128


def _mm(a, b):
    return jnp.dot(a.astype(BF16), b.astype(BF16), preferred_element_type=F32)


def _mm_nt(a, b):
    return lax.dot_general(a.astype(BF16), b.astype(BF16), (((1,), (1,)), ((), ())),
                           preferred_element_type=F32)


def _mm_tn(a, b):
    return jnp.dot(a.T.astype(BF16), b.astype(BF16), preferred_element_type=F32)


def _mm_sel(sel, x):
    hi = x.astype(BF16)
    r1 = x - hi.astype(F32)
    mid = r1.astype(BF16)
    lo = (r1 - mid.astype(F32)).astype(BF16)
    return (jnp.dot(sel, hi, preferred_element_type=F32)
            + jnp.dot(sel, mid, preferred_element_type=F32)
            + jnp.dot(sel, lo, preferred_element_type=F32))


def _sigmoid(x):
    return 1.0 / (1.0 + jnp.exp(-x))


def _silu(x):
    return x * _sigmoid(x)


def _softplus(x):
    return jnp.maximum(x, 0.0) + jnp.log(1.0 + jnp.exp(-jnp.abs(x)))


def _rms(x, w):
    return x * lax.rsqrt(jnp.mean(x * x, axis=-1, keepdims=True) + EPS) * w


def _pick_tile(n, candidates):
    for c in candidates:
        if n % c == 0:
            return c
    raise ValueError(f"no tile for {n}")


def _token_parts_specs(parts, tm):
    bounds = np.cumsum([0] + [_part_rows(p) // tm for p in parts])
    specs = []
    for k, p in enumerate(parts):
        assert _part_rows(p) % tm == 0
        lo, hi = int(bounds[k]), int(bounds[k + 1])
        specs.append(_part_block_spec(p, tm, lo, hi - lo))
    return specs, [int(b) for b in bounds]


def _part_rows(p):
    return int(np.prod(p.shape[:-1]))


def _part_block_spec(p, tm, first_tile, n_tiles):
    idx = lambda i: jnp.clip(i - first_tile, 0, n_tiles - 1)
    if len(p.shape) == 2:
        return pl.BlockSpec((tm, p.shape[1]), lambda i: (idx(i), 0))
    grp, _, chunk, d = p.shape
    assert tm % (grp * chunk) == 0
    return pl.BlockSpec((grp, tm // (grp * chunk), chunk, d), lambda i: (0, idx(i), 0, 0))


def _load_part_tile(ref):
    if len(ref.shape) == 2:
        return ref[...]
    grp, n_chunks = ref.shape[:2]
    return jnp.concatenate([ref[s, c] for c in range(n_chunks) for s in range(grp)], axis=0)


def _store_part_tile(ref, x):
    if len(ref.shape) == 2:
        ref[...] = x
        return
    grp, n_chunks, chunk = ref.shape[:3]
    for c in range(n_chunks):
        for s in range(grp):
            r0 = (c * grp + s) * chunk
            ref[s, c] = x[r0:r0 + chunk]


def _read_token_tile(refs, bounds):
    i = pl.program_id(0)
    x = _load_part_tile(refs[-1])
    for k in range(len(refs) - 2, -1, -1):
        x = jnp.where(i < bounds[k + 1], _load_part_tile(refs[k]), x)
    return x


def _inproj_kernel(n_parts, bounds, *refs):
    x_refs = refs[:n_parts]
    nw_ref, w_ref, o_ref, wp_ref = refs[n_parts:]

    @pl.when(pl.program_id(0) == 0)
    def _():
        rows = 128
        ml_gates = slice(C_MLO, C_MLO + 2 * N_HEADS)
        sd_dt = slice(W_IN_COLS - N_HEADS, W_IN_COLS)
        n_gates = 3 * N_HEADS
        for r0 in range(0, D_MODEL, rows):
            w = w_ref[0, r0:r0 + rows, :]
            wp_ref[r0:r0 + rows, 0:C_MLO] = w[:, 0:ml_gates.start].astype(BF16)
            wp_ref[r0:r0 + rows, C_MLO:C_GATE] = w[:, ml_gates.stop:sd_dt.start].astype(BF16)
            gates = jnp.concatenate([w[:, ml_gates], w[:, sd_dt],
                                     jnp.zeros((rows, D_PROJ - C_GATE - n_gates), F32)], axis=1)
            wp_ref[r0:r0 + rows, C_GATE:D_PROJ] = gates.astype(BF16)

    xn = _rms(_read_token_tile(x_refs, bounds), nw_ref[...]).astype(BF16)
    o_ref[...] = jnp.dot(xn, wp_ref[...], preferred_element_type=F32)


def _inproj(x_parts, nw, w_all, layer):
    n = sum(_part_rows(p) for p in x_parts)
    tm = _pick_tile(np.gcd.reduce([_part_rows(p) for p in x_parts]), TOKEN_TILES)
    x_specs, bounds = _token_parts_specs(x_parts, tm)
    assert w_all.shape[1:] == (D_MODEL, W_IN_COLS)
    return pl.pallas_call(
        functools.partial(_inproj_kernel, len(x_parts), bounds),
        grid=(n // tm,),
        in_specs=x_specs + [pl.BlockSpec((1, D_MODEL), lambda i: (0, 0)),
                            pl.BlockSpec((1, D_MODEL, W_IN_COLS), lambda i: (layer, 0, 0),
                                         pipeline_mode=pl.Buffered(1))],
        out_specs=pl.BlockSpec((tm, D_PROJ), lambda i: (i, 0)),
        out_shape=jax.ShapeDtypeStruct((n, D_PROJ), F32),
        scratch_shapes=[pltpu.VMEM((D_MODEL, D_PROJ), BF16)],
        compiler_params=pltpu.CompilerParams(dimension_semantics=("arbitrary",),
                                             vmem_limit_bytes=VMEM_LIMIT),
        name="inproj",
    )(*x_parts, nw, w_all)


def _head_norm(x, width):
    parts = []
    for g in range(x.shape[1] // width):
        xg = x[:, g * width:(g + 1) * width]
        parts.append(xg * lax.rsqrt(jnp.mean(xg * xg, axis=-1, keepdims=True) + EPS))
    return jnp.concatenate(parts, axis=1)


def _mixer_kernel(prompt_steps,
                  sinks_ref,
                  proj_ref, rope_ref, c0_ref, m0_ref, s0_ref, k0_ref, v0_ref, h0_ref, cv0_ref,
                  gbias_ref, alog_ref, mlnw_ref, hglb_ref, hgnw_ref,
                  convw_ref, convb_ref, ssdd_ref, ssdnw_ref,
                  tri_ref, triblk_ref, blkones_ref, ind_ref,
                  mixed_ref, c_ref, m_ref, s_ref, k_ref, v_ref, h_ref, cv_ref,
                  convbuf, bloc_s, kk_s, hv_s):
    i = pl.program_id(0)
    is_prompt = i < prompt_steps
    is_first = jnp.logical_or(jnp.logical_not(is_prompt), i == 0)
    n_valid = jnp.where(is_prompt, jnp.minimum(i * CHUNK, WINDOW), WINDOW)

    @pl.when(is_first)
    def _():
        c_ref[...] = c0_ref[...]
        m_ref[...] = m0_ref[...]
        s_ref[...] = s0_ref[...]
        k_ref[...] = k0_ref[...]
        v_ref[...] = v0_ref[...]
        h_ref[...] = h0_ref[...]
        cv_ref[...] = cv0_ref[...]

    members = [_mixer_chunk(g, n_valid, sinks_ref, proj_ref, rope_ref,
                            gbias_ref, alog_ref, mlnw_ref, hglb_ref, hgnw_ref,
                            convw_ref, convb_ref, ssdd_ref, ssdnw_ref,
                            tri_ref, triblk_ref, blkones_ref, ind_ref,
                            mixed_ref, c_ref, m_ref, s_ref, k_ref, v_ref, h_ref, cv_ref,
                            convbuf, bloc_s, kk_s, hv_s) for g in range(SEQ_GROUP)]
    for phase in zip(*[m[0] for m in members]):
        chains = [c for per_member in zip(*phase) for c in per_member]
        while chains:
            alive = []
            for chain in chains:
                if next(chain, _CHAIN_DONE) is not _CHAIN_DONE:
                    alive.append(chain)
            chains = alive
    for _, finish in members:
        finish()


_CHAIN_DONE = object()


def _mixer_chunk(g, n_valid, sinks_ref, proj_ref, rope_ref,
                 gbias_ref, alog_ref, mlnw_ref, hglb_ref, hgnw_ref,
                 convw_ref, convb_ref, ssdd_ref, ssdnw_ref,
                 tri_ref, triblk_ref, blkones_ref, ind_ref,
                 mixed_ref, c_ref, m_ref, s_ref, k_ref, v_ref, h_ref, cv_ref,
                 convbuf, bloc_s, kk_s, hv_s):
    rs = slice(g * CHUNK, (g + 1) * CHUNK)
    L = CHUNK
    row = lax.broadcasted_iota(jnp.int32, (L, L), 0)
    col = lax.broadcasted_iota(jnp.int32, (L, L), 1)
    causal = row >= col
    tri = tri_ref[...]

    gate = proj_ref[rs,C_GATE:C_GATE + 128] + gbias_ref[...]
    ig = gate
    lf = pltpu.roll(-_softplus(-gate), 124, 1)
    dtv = pltpu.roll(_softplus(gate), 120, 1)

    b = _mm_sel(tri, lf)
    a = ig - b
    cm = a
    rowg = lax.broadcasted_iota(jnp.int32, (L, 128), 0)
    for sh in (1, 2, 4, 8, 16, 32):
        cm = jnp.where(rowg >= sh, jnp.maximum(cm, pltpu.roll(cm, sh, 0)), cm)
    m_prev = m_ref[g, 0:1, :]
    m_t = b + jnp.maximum(m_prev, cm)
    inter = jnp.exp(b + m_prev - m_t)
    bm = b - m_t
    a_t = a.T
    m_last = m_t[L - 1:L, :]
    b_last = b[L - 1:L, :]
    ws = jnp.exp(b_last + a - m_last)
    decay = jnp.exp(b_last + m_prev - m_last)
    emt = jnp.exp(-m_t)
    m_ref[g, 0:1, :] = m_last

    lane2 = lax.broadcasted_iota(jnp.int32, (L, 2 * HEAD), 1)
    low = lane2 < HEAD
    causal2 = lax.broadcasted_iota(jnp.int32, (L, 2 * HEAD), 0) >= (lane2 & (HEAD - 1))
    ones_blk = ind_ref[0:2 * HEAD, 0:2 * HEAD]
    ones_rows = jnp.ones((L, 2 * HEAD), BF16)
    st_row = lax.broadcasted_iota(jnp.int32, (2 * HEAD, 4 * HEAD), 0) // HEAD
    st_lane = (lax.broadcasted_iota(jnp.int32, (2 * HEAD, 4 * HEAD), 1) // HEAD) % 2
    ml_out = [None] * (N_HEADS // 2)

    def pair_cols(x, p):
        return jnp.where(low, x[:, 2 * p:2 * p + 1], x[:, 2 * p + 1:2 * p + 2])

    def split_rows(x):
        return jnp.concatenate([jnp.where(low, x, 0.0), jnp.where(low, 0.0, x)], axis=0)

    def ml_pair(p):
        cs = slice(2 * p * HEAD, 2 * (p + 1) * HEAD)
        q = proj_ref[rs, C_MLQ + cs.start:C_MLQ + cs.stop]
        k = proj_ref[rs, C_MLK + cs.start:C_MLK + cs.stop] * (HEAD ** -0.5)
        v = proj_ref[rs, C_MLV + cs.start:C_MLV + cs.stop]
        qk = _mm_nt(q, split_rows(k))
        yield
        st = c_ref[g, p]
        from_state = _mm(q, st)
        yield
        a_row = jnp.concatenate([a_t[2 * p:2 * p + 1, :], a_t[2 * p + 1:2 * p + 2, :]], axis=1)
        w = jnp.where(causal2, jnp.exp(pair_cols(bm, p) + a_row), 0.0)
        rhs = jnp.concatenate([split_rows(v).astype(BF16), ones_blk], axis=1)
        gate_in = pair_cols(inter, p)
        nd = (jnp.dot((w * qk).astype(BF16), rhs, preferred_element_type=F32)
              + jnp.concatenate([gate_in, gate_in], axis=1) * from_state)
        yield
        kws = k * pair_cols(ws, p)
        upd = jnp.dot(kws.T.astype(BF16), jnp.concatenate([v.astype(BF16), ones_rows], axis=1),
                      preferred_element_type=F32)
        yield
        den = jnp.maximum(jnp.abs(nd[:, 2 * HEAD:]), pair_cols(emt, p))
        ml_out[p] = nd[:, :2 * HEAD] / den
        dec = jnp.where(st_row == 0, decay[:, 2 * p:2 * p + 1], decay[:, 2 * p + 1:2 * p + 2])
        c_ref[g, p] = dec * st + jnp.where(st_row == st_lane, upd, 0.0)

    def ml_finish():
        ml = _head_norm(jnp.concatenate(ml_out, axis=1), HEAD)
        mixed_ref[rs,0:MIX_W] = (_sigmoid(proj_ref[rs,C_MLO:C_MLO + MIX_W]) * ml
                                 * mlnw_ref[...]).astype(mixed_ref.dtype)

    lb = hglb_ref[...]
    fg = lb + (1.0 - lb) * _sigmoid(proj_ref[rs,C_HGF:C_HGF + MIX_W])
    hq = _silu(proj_ref[rs,C_HGQ:C_HGQ + MIX_W]) * (HEAD ** -0.5)
    kk = 1.0 - fg
    lfg = jnp.log(fg)
    hv = proj_ref[rs,C_HGI:C_HGI + MIX_W]
    bloc = _mm_sel(triblk_ref[...], lfg)
    blast = _mm_sel(blkones_ref[...], lfg)
    qe = hq * jnp.exp(bloc)
    kw = kk * jnp.exp(blast - bloc)
    bloc_s[g] = bloc
    kk_s[g] = kk
    hv_s[g] = hv
    ind = ind_ref[...]
    rows = lax.broadcasted_iota(jnp.int32, (SUB, MIX_W), 0)
    n_blk = L // SUB
    hg_state_part = [None] * n_blk
    hg_block_part = [None] * n_blk

    def hg_within_block(blk):
        r0 = blk * SUB
        q_blk = hq[r0:r0 + SUB]
        b_blk = bloc[r0:r0 + SUB]
        xs = []
        for s in range(SUB):
            r = r0 + s
            e = jnp.exp(jnp.minimum(b_blk - bloc_s[g, r:r + 1, :], 0.0))
            xs.append((q_blk * (kk_s[g, r:r + 1, :] * e)).astype(BF16))
            if s % HG_SLICE == HG_SLICE - 1:
                yield
        att = jnp.dot(jnp.concatenate(xs, axis=0), ind, preferred_element_type=F32)
        yield
        o_blk = att[0:SUB] * hv_s[g, r0:r0 + 1, :]
        for s in range(1, SUB):
            r = r0 + s
            o_blk = o_blk + jnp.where(rows >= s, att[s * SUB:(s + 1) * SUB], 0.0) * hv_s[g, r:r + 1, :]
            if s % HG_SLICE == HG_SLICE - 1:
                yield
        hg_block_part[blk] = o_blk

    def hg_state_chain():
        st = s_ref[g]
        for blk in range(n_blk):
            r0 = blk * SUB
            hg_state_part[blk] = jnp.concatenate(
                [_mm_nt(qe[r0:r0 + SUB, h * HEAD:(h + 1) * HEAD], st[:, h * HEAD:(h + 1) * HEAD])
                 for h in range(N_HEADS)], axis=1)
            upd = jnp.concatenate(
                [_mm_tn(hv[r0:r0 + SUB, h * HEAD:(h + 1) * HEAD], kw[r0:r0 + SUB, h * HEAD:(h + 1) * HEAD])
                 for h in range(N_HEADS)], axis=1)
            yield
            st = st * jnp.exp(blast[r0:r0 + 1, :]) + upd
        s_ref[g] = st

    def hg_finish():
        o = jnp.concatenate([hg_state_part[blk] + hg_block_part[blk] for blk in range(n_blk)], axis=0)
        mixed_ref[rs,MIX_W:2 * MIX_W] = (_head_norm(o, HEAD) * hgnw_ref[...]
                                         * _silu(proj_ref[rs,C_HGG:C_HGG + MIX_W])).astype(mixed_ref.dtype)

    @functools.lru_cache(maxsize=None)
    def sw_prep():
        cos = rope_ref[:, 0:128]
        sin_a = rope_ref[:, 128:256]
        sin_b = rope_ref[:, 256:384]

        def rope(x):
            return x * cos + pltpu.roll(x, 120, 1) * sin_a + pltpu.roll(x, 8, 1) * sin_b

        q_rot = [rope(proj_ref[rs,C_SWQ:C_SWQ + 128]), rope(proj_ref[rs,C_SWQ + 128:C_SWQ + 256])]
        k_rot = rope(proj_ref[rs,C_SWK:C_SWK + 128])
        v_cur = proj_ref[rs,C_SWV:C_SWV + 128]
        colw = lax.broadcasted_iota(jnp.int32, (2 * L, WINDOW), 1)
        row2 = lax.broadcasted_iota(jnp.int32, (2 * L, 1), 0)
        return q_rot, k_rot, v_cur, k_ref[g], v_ref[g], colw >= WINDOW - n_valid, row2

    sw_prep()
    sw_out = [None] * (2 * SW_KV)

    def sw_group(kv):
        q_rot, k_rot, v_cur, k_prev, v_prev, prev_ok, row2 = sw_prep()
        gs = slice(kv * HEAD, (kv + 1) * HEAD)
        q2 = jnp.concatenate([q_rot[kv][:, 0:HEAD], q_rot[kv][:, HEAD:2 * HEAD]], axis=0) * (HEAD ** -0.5)
        s_p = jnp.where(prev_ok, _mm_nt(q2, k_prev[:, gs]), NEG)
        yield
        s_c = _mm_nt(q2, k_rot[:, gs])
        yield
        sink = jnp.where(row2 < L, sinks_ref[2 * kv], sinks_ref[2 * kv + 1])
        mx = jnp.maximum(jnp.maximum(jnp.max(s_p, axis=-1, keepdims=True),
                                     jnp.max(s_c, axis=-1, keepdims=True)), sink)
        e_p = jnp.exp(s_p - mx)
        e_c = jnp.exp(s_c - mx)
        den = (jnp.sum(e_p, axis=-1, keepdims=True) + jnp.sum(e_c, axis=-1, keepdims=True)
               + jnp.exp(sink - mx))
        o_p = _mm(e_p, v_prev[:, gs])
        yield
        o = (o_p + _mm(e_c, v_cur[:, gs])) / den
        yield
        sw_out[2 * kv] = o[0:L]
        sw_out[2 * kv + 1] = o[L:2 * L]

    def sw_finish():
        _, k_rot, v_cur, k_prev, v_prev, _, _ = sw_prep()
        mixed_ref[rs,2 * MIX_W:3 * MIX_W] = jnp.concatenate(sw_out, axis=1).astype(mixed_ref.dtype)
        k_ref[g, 0:WINDOW - L, :] = k_prev[L:WINDOW]
        k_ref[g, WINDOW - L:WINDOW, :] = k_rot
        v_ref[g, 0:WINDOW - L, :] = v_prev[L:WINDOW]
        v_ref[g, WINDOW - L:WINDOW, :] = v_cur

    @functools.lru_cache(maxsize=None)
    def sd_prep():
        convbuf[g, 0:8, :] = cv_ref[g]
        convbuf[g, 8:8 + L, :] = proj_ref[rs,C_SDX:C_SDX + SSD_CONV_DIM]
        acc = convb_ref[...] + convbuf[g, 5:5 + L, :] * convw_ref[0:1, :]
        for j in range(1, SSD_CONV):
            acc = acc + convbuf[g, 5 + j:5 + j + L, :] * convw_ref[j:j + 1, :]
        xbc = _silu(acc)
        cv_ref[g] = convbuf[g, L:L + 8, :]
        a_neg = -jnp.exp(alog_ref[...])
        ca = _mm_sel(tri, dtv * a_neg)
        ca_last = ca[L - 1:L, :]
        return (xbc, xbc[:, 0:MIX_W], ca, ca.T, dtv.T, jnp.exp(ca), jnp.exp(ca_last - ca) * dtv,
                jnp.exp(ca_last))

    sd_prep()
    sd_out = [None] * (N_HEADS // 2)

    def sd_group(grp):
        xbc, xs_all, ca, ca_t, dt_t, eca, wss, sdec = sd_prep()
        bm_g = xbc[:, MIX_W + grp * SSD_N:MIX_W + (grp + 1) * SSD_N]
        cm_g = xbc[:, MIX_W + 2 * SSD_N + grp * SSD_N:MIX_W + 2 * SSD_N + (grp + 1) * SSD_N]
        e, o = 2 * grp, 2 * grp + 1
        x = xs_all[:, e * HEAD:(o + 1) * HEAD]
        cb2 = _mm_nt(cm_g, jnp.concatenate([bm_g, bm_g], axis=0))
        yield
        hst = h_ref[g, e:o + 1].reshape(2 * HEAD, SSD_N)
        from_state = _mm_nt(cm_g, hst)
        yield
        ca_row = jnp.concatenate([ca_t[e:e + 1, :], ca_t[o:o + 1, :]], axis=1)
        dt_row = jnp.concatenate([dt_t[e:e + 1, :], dt_t[o:o + 1, :]], axis=1)
        dec = jnp.where(causal2, jnp.exp(pair_cols(ca, grp) - ca_row), 0.0)
        sd_out[grp] = _mm(cb2 * dec * dt_row, split_rows(x)) + pair_cols(eca, grp) * from_state
        yield
        upd = _mm_tn(x * pair_cols(wss, grp), bm_g)
        yield
        keep = jnp.where(lax.broadcasted_iota(jnp.int32, (2 * HEAD, SSD_N), 0) < HEAD,
                         sdec[:, e:e + 1], sdec[:, o:o + 1])
        h_ref[g, e:o + 1] = (keep * hst + upd).reshape(2, HEAD, SSD_N)

    def sd_finish():
        xs_all = sd_prep()[1]
        y = jnp.concatenate(sd_out, axis=1) + ssdd_ref[...] * xs_all
        gated = y * _silu(proj_ref[rs,C_SDZ:C_SDZ + MIX_W])
        mixed_ref[rs,3 * MIX_W:4 * MIX_W] = (_head_norm(gated, 2 * HEAD)
                                             * ssdnw_ref[...]).astype(mixed_ref.dtype)

    chains = ([[hg_within_block(blk) for blk in range(n_blk)]]
              + [[ml_pair(p)] for p in range(N_HEADS // 2)] + [[hg_state_chain(), sd_group(0)]]
              + [[sd_group(1)]]
              + [[sw_group(kv)] for kv in range(SW_KV)])

    def finish():
        ml_finish()
        hg_finish()
        sw_finish()
        sd_finish()

    return chains, finish


def _mixer(proj, rope_tab, init, lp, consts, chunks_per_prompt, n_seq):
    n = proj.shape[0]
    rows = SEQ_GROUP * CHUNK
    n_steps = n // rows

    def group_of(i):
        return jnp.where(i < chunks_per_prompt, 0, i - chunks_per_prompt + 1)

    def rope_blk(i):
        return jnp.minimum(i, chunks_per_prompt)

    def per_seq(shape):
        nd = len(shape)
        return pl.BlockSpec((SEQ_GROUP,) + shape, lambda i, s: (group_of(i),) + (0,) * nd)

    def const(shape):
        nd = len(shape)
        return pl.BlockSpec(shape, lambda i, s: (0,) * nd)

    state_shapes = [(N_HEADS // 2, 2 * HEAD, 4 * HEAD), (8, 128), (HEAD, MIX_W), (WINDOW, 128), (WINDOW, 128),
                    (N_HEADS, HEAD, SSD_N), (8, SSD_CONV_DIM)]
    in_specs = ([pl.BlockSpec((rows, D_PROJ), lambda i, s: (i, 0)),
                 pl.BlockSpec((CHUNK, 384), lambda i, s: (rope_blk(i), 0))]
                + [per_seq(s) for s in state_shapes]
                + [const((1, 128)), const((1, 128)), const((1, MIX_W)), const((1, MIX_W)), const((1, MIX_W)),
                   const((SSD_CONV, SSD_CONV_DIM)), const((1, SSD_CONV_DIM)), const((1, MIX_W)),
                   const((1, MIX_W)),
                   const((CHUNK, CHUNK)), const((CHUNK, CHUNK)), const((CHUNK, CHUNK)), const((MIX_W, MIX_W))])
    out_specs = ([pl.BlockSpec((rows, D_MODEL), lambda i, s: (i, 0))]
                 + [per_seq(s) for s in state_shapes])
    out_shape = ([jax.ShapeDtypeStruct((n, D_MODEL), BF16)]
                 + [jax.ShapeDtypeStruct((n_seq,) + s, F32) for s in state_shapes])
    grid_spec = pltpu.PrefetchScalarGridSpec(
        num_scalar_prefetch=1, grid=(n_steps,), in_specs=in_specs, out_specs=out_specs,
        scratch_shapes=[pltpu.VMEM((SEQ_GROUP, CHUNK + 8, SSD_CONV_DIM), F32),
                        pltpu.VMEM((SEQ_GROUP, CHUNK, MIX_W), F32), pltpu.VMEM((SEQ_GROUP, CHUNK, MIX_W), F32),
                        pltpu.VMEM((SEQ_GROUP, CHUNK, MIX_W), F32)])
    return pl.pallas_call(
        functools.partial(_mixer_kernel, chunks_per_prompt),
        grid_spec=grid_spec, out_shape=out_shape,
        compiler_params=pltpu.CompilerParams(dimension_semantics=("arbitrary",),
                                             vmem_limit_bytes=VMEM_LIMIT),
        name="mixer",
    )(lp["sinks"], proj, rope_tab, *init,
      lp["gbias"], lp["alog"], lp["mlnw"], lp["hglb"], lp["hgnw"],
      lp["convw"], lp["convb"], lp["ssdd"], lp["ssdnw"], *consts)


def _dense_ffn_kernel(f_chunk, n_parts, bounds, *refs):
    x_refs = refs[:n_parts]
    mix_ref, wo_ref, nw_ref, wg_ref, wu_ref, wd_ref, o_ref = refs[n_parts:]
    x1 = _read_token_tile(x_refs, bounds) + jnp.dot(mix_ref[...], wo_ref[...], preferred_element_type=F32)
    hn = _rms(x1, nw_ref[...]).astype(BF16)
    o_ref[...] = x1
    for f0 in range(0, wg_ref.shape[1], f_chunk):
        g = jnp.dot(hn, wg_ref[:, f0:f0 + f_chunk], preferred_element_type=F32)
        u = jnp.dot(hn, wu_ref[:, f0:f0 + f_chunk], preferred_element_type=F32)
        act = (_silu(g) * u).astype(BF16)
        o_ref[...] += jnp.dot(act, wd_ref[f0:f0 + f_chunk, :], preferred_element_type=F32)


def _dense_ffn(x_parts, mixed, wo, nw, wg, wu, wd):
    n = mixed.shape[0]
    d_ff = wg.shape[1]
    tm = _pick_tile(np.gcd.reduce([_part_rows(p) for p in x_parts]), TOKEN_TILES)
    x_specs, bounds = _token_parts_specs(x_parts, tm)
    full = lambda shape: pl.BlockSpec(shape, lambda i: (0, 0))
    return pl.pallas_call(
        functools.partial(_dense_ffn_kernel, 256, len(x_parts), bounds),
        grid=(n // tm,),
        in_specs=x_specs + [
                  pl.BlockSpec((tm, D_MODEL), lambda i: (i, 0)),
                  full((D_MODEL, D_MODEL)), full((1, D_MODEL)),
                  full((D_MODEL, d_ff)), full((D_MODEL, d_ff)), full((d_ff, D_MODEL))],
        out_specs=pl.BlockSpec((tm, D_MODEL), lambda i: (i, 0)),
        out_shape=jax.ShapeDtypeStruct((n, D_MODEL), F32),
        compiler_params=pltpu.CompilerParams(dimension_semantics=("arbitrary",),
                                             vmem_limit_bytes=VMEM_LIMIT),
        name="dense_ffn",
    )(*x_parts, mixed, wo, nw, wg, wu, wd)


TOK_BLK = 1024
FFN_TILE = 1024


def _pack_bf16_pairs(x):
    w = x.shape[1] // 2
    bits = pltpu.bitcast(x.astype(BF16).astype(F32), jnp.int32)
    return lax.shift_right_logical(bits[:, :w], 16) | bits[:, w:]


def _unpack_bf16_pairs(p):
    lo = pltpu.bitcast(lax.shift_left(p, 16), F32)
    hi = pltpu.bitcast(p & jnp.int32(-65536), F32)
    return lo, hi


def _router_kernel(x_ref, mix_ref, wo_ref, nw_ref, wrt_ref, upper_ref,
                   x1_ref, hn_ref, meta_ref, gate_ref, count_ref, carry):
    @pl.when(pl.program_id(0) == 0)
    def _():
        carry[...] = jnp.zeros_like(carry)

    x1 = x_ref[...] + jnp.dot(mix_ref[...], wo_ref[...], preferred_element_type=F32)
    x1_ref[...] = x1
    hn_f = _rms(x1, nw_ref[...])
    hn = hn_f.astype(BF16)
    hn_ref[...] = _pack_bf16_pairs(hn_f)
    logits = lax.dot_general(wrt_ref[...], hn, (((1,), (1,)), ((), ())),
                             preferred_element_type=F32)
    sub = lax.broadcasted_iota(jnp.int32, logits.shape, 0)
    l1 = jnp.where(sub < N_EXPERTS, logits, NEG)
    m1 = jnp.max(l1, axis=0, keepdims=True)
    i1 = jnp.min(jnp.where(l1 == m1, sub, 99), axis=0, keepdims=True)
    l2 = jnp.where(sub == i1, NEG, l1)
    m2 = jnp.max(l2, axis=0, keepdims=True)
    i2 = jnp.min(jnp.where(l2 == m2, sub, 99), axis=0, keepdims=True)
    e2 = jnp.exp(m2 - m1)
    g1 = 1.0 / (1.0 + e2)
    g2 = e2 / (1.0 + e2)
    sel = jnp.where(sub == i1, 1.0, jnp.where(sub == i2, 1.0, 0.0))
    before = jnp.dot(sel.astype(BF16), upper_ref[...], preferred_element_type=F32) + carry[:, 0:1]
    r1 = jnp.sum(jnp.where(sub == i1, before, 0.0), axis=0, keepdims=True)
    r2 = jnp.sum(jnp.where(sub == i2, before, 0.0), axis=0, keepdims=True)
    meta_ref[...] = jnp.concatenate([i1.astype(F32), i2.astype(F32), r1, r2,
                                     jnp.zeros((4, r1.shape[1]), F32)], axis=0)
    gate_ref[...] = jnp.concatenate([g1, g2, jnp.zeros((126, r1.shape[1]), F32)], axis=0).T
    total = carry[...] + jnp.sum(sel, axis=1, keepdims=True)
    carry[...] = total
    count_ref[...] = total


def _router(x, mixed, wo, nw, wrt, upper):
    n = x.shape[0]
    tm = TOK_BLK
    full = lambda shape: pl.BlockSpec(shape, lambda i: (0,) * len(shape))
    tok = lambda w: pl.BlockSpec((tm, w), lambda i: (i, 0))
    return pl.pallas_call(
        _router_kernel,
        grid=(n // tm,),
        in_specs=[tok(D_MODEL), tok(D_MODEL), full((D_MODEL, D_MODEL)), full((1, D_MODEL)),
                  full((16, D_MODEL)), full((tm, tm))],
        out_specs=[tok(D_MODEL), tok(D_MODEL // 2),
                   pl.BlockSpec((8, tm), lambda i: (0, i)), tok(128), full((16, 128))],
        out_shape=[jax.ShapeDtypeStruct((n, D_MODEL), F32), jax.ShapeDtypeStruct((n, D_MODEL // 2), jnp.int32),
                   jax.ShapeDtypeStruct((8, n), F32), jax.ShapeDtypeStruct((n, 128), F32),
                   jax.ShapeDtypeStruct((16, 128), F32)],
        scratch_shapes=[pltpu.VMEM((16, 128), F32)],
        compiler_params=pltpu.CompilerParams(dimension_semantics=("arbitrary",),
                                             vmem_limit_bytes=VMEM_LIMIT),
        name="router",
    )(x, mixed, wo, nw, wrt, upper)


def _sc_row_gather(table, idx):
    n_workers = SC_CORES * SC_SUBCORES
    b = idx.shape[0]
    d = table.shape[1]
    assert b % (8 * n_workers) == 0
    per_worker = b // n_workers
    chunk = max(c for c in range(8, SC_MAX_GATHER_ROWS + 1, 8) if per_worker % c == 0)
    mesh = plsc.VectorSubcoreMesh(core_axis_name="c", subcore_axis_name="s")

    @functools.partial(
        pl.kernel, mesh=mesh, out_type=jax.ShapeDtypeStruct((b, d), table.dtype),
        scratch_types=[pltpu.VMEM((chunk,), jnp.int32), pltpu.VMEM((chunk, d), table.dtype),
                       pltpu.SemaphoreType.DMA])
    def gather(table_hbm, idx_hbm, out_hbm, idx_v, rows_v, sem):
        worker = lax.axis_index("s") * SC_CORES + lax.axis_index("c")
        base = worker * per_worker

        @pl.loop(0, per_worker // chunk)
        def _(c):
            off = pl.multiple_of(base + c * chunk, 8)
            pltpu.sync_copy(idx_hbm.at[pl.ds(off, chunk)], idx_v)
            pltpu.async_copy(table_hbm.at[idx_v], rows_v, sem).wait()
            pltpu.sync_copy(rows_v, out_hbm.at[pl.ds(off, chunk)])

    return gather(table, idx)


def _sc_row_scatter(rows, idx):
    n_workers = SC_CORES * SC_SUBCORES
    b = n_out = idx.shape[0]
    v, d = rows.shape
    assert b % (8 * n_workers) == 0
    per_worker = b // n_workers
    chunk = max(c for c in range(8, SC_MAX_GATHER_ROWS + 1, 8) if per_worker % c == 0 and v % c == 0)
    mesh = plsc.VectorSubcoreMesh(core_axis_name="c", subcore_axis_name="s")

    @functools.partial(
        pl.kernel, mesh=mesh, out_type=jax.ShapeDtypeStruct((n_out, d), rows.dtype),
        scratch_types=[pltpu.VMEM((chunk,), jnp.int32), pltpu.VMEM((chunk, d), rows.dtype)])
    def scatter(rows_hbm, idx_hbm, out_hbm, idx_v, rows_v):
        worker = lax.axis_index("s") * SC_CORES + lax.axis_index("c")
        base = worker * per_worker

        @pl.loop(0, per_worker // chunk)
        def _(c):
            off = pl.multiple_of(base + c * chunk, 8)
            pltpu.sync_copy(idx_hbm.at[pl.ds(off, chunk)], idx_v)
            pltpu.sync_copy(rows_hbm.at[pl.ds(pl.multiple_of(lax.rem(off, v), 8), chunk)], rows_v)
            pltpu.sync_copy(rows_v, out_hbm.at[idx_v])

    return scatter(rows, idx)


def _expert_ffn_kernel(texp_ref, tvalid_ref, xs_ref, wg_ref, wu_ref, wd_ref, o_ref, acc_ref, xb_ref):
    j = pl.program_id(0)
    f = pl.program_id(1)
    last = f == pl.num_programs(1) - 1
    valid = tvalid_ref[j] == 1
    half = D_MODEL // 2

    @pl.when(jnp.logical_and(valid, f == 0))
    def _():
        lo, hi = _unpack_bf16_pairs(xs_ref[...])
        xb_ref[:, :half] = lo.astype(BF16)
        xb_ref[:, half:] = hi.astype(BF16)

        acc_ref[...] = jnp.zeros_like(acc_ref)

    @pl.when(valid)
    def _():
        xs = xb_ref[...]
        g = jnp.dot(xs, wg_ref[0], preferred_element_type=F32)
        u = jnp.dot(xs, wu_ref[0].astype(BF16), preferred_element_type=F32)
        act = (_silu(g) * u).astype(BF16)
        acc_ref[...] += jnp.dot(act, wd_ref[0].astype(BF16), preferred_element_type=F32)

        @pl.when(last)
        def _():
            o_ref[...] = _pack_bf16_pairs(acc_ref[...])

    @pl.when(jnp.logical_and(jnp.logical_not(valid), last))
    def _():
        o_ref[...] = jnp.zeros_like(o_ref)


def _expert_ffn(f_exp, f_valid, xs, wg, wu, wd):
    n_slots = xs.shape[0]
    d_ff = wg.shape[2]
    tf = _pick_tile(d_ff, (512, 256, 128))
    grid_spec = pltpu.PrefetchScalarGridSpec(
        num_scalar_prefetch=2, grid=(n_slots // FFN_TILE, d_ff // tf),
        in_specs=[pl.BlockSpec((FFN_TILE, D_MODEL // 2), lambda j, f, te, tv: (j, 0)),
                  pl.BlockSpec((1, D_MODEL, tf), lambda j, f, te, tv: (te[j], 0, f)),
                  pl.BlockSpec((1, D_MODEL, tf), lambda j, f, te, tv: (te[j], 0, f)),
                  pl.BlockSpec((1, tf, D_MODEL), lambda j, f, te, tv: (te[j], f, 0))],
        out_specs=pl.BlockSpec((FFN_TILE, D_MODEL // 2), lambda j, f, te, tv: (j, 0)),
        scratch_shapes=[pltpu.VMEM((FFN_TILE, D_MODEL), F32), pltpu.VMEM((FFN_TILE, D_MODEL), BF16)])
    return pl.pallas_call(
        _expert_ffn_kernel, grid_spec=grid_spec,
        out_shape=jax.ShapeDtypeStruct((n_slots, D_MODEL // 2), jnp.int32),
        compiler_params=pltpu.CompilerParams(dimension_semantics=("arbitrary", "arbitrary"),
                                             vmem_limit_bytes=VMEM_LIMIT),
        name="expert_ffn",
    )(f_exp, f_valid, xs, wg, wu, wd)


def _combine_kernel(part_tiles, x1_ref, o1_ref, o2_ref, gate_ref, fnw_ref, *y_refs):
    i = pl.program_id(0)
    lo1, hi1 = _unpack_bf16_pairs(o1_ref[...])
    lo2, hi2 = _unpack_bf16_pairs(o2_ref[...])
    g1 = gate_ref[:, 0:1]
    g2 = gate_ref[:, 1:2]
    half = D_MODEL // 2
    x1 = x1_ref[...]
    fnw = fnw_ref[...]
    ya = x1[:, :half] + g1 * lo1 + g2 * lo2
    yb = x1[:, half:] + g1 * hi1 + g2 * hi2
    scale = lax.rsqrt((jnp.sum(ya * ya, axis=-1, keepdims=True) + jnp.sum(yb * yb, axis=-1, keepdims=True))
                      / D_MODEL + EPS)
    y = jnp.concatenate([ya * scale * fnw[:, :half], yb * scale * fnw[:, half:]], axis=1)
    lo_tile = 0
    for y_ref, tiles in zip(y_refs, part_tiles):
        @pl.when(jnp.logical_and(i >= lo_tile, i < lo_tile + tiles))
        def _(y_ref=y_ref):
            _store_part_tile(y_ref, y)
        lo_tile += tiles


def _combine(x1, o12, gates, fnw, part_shapes):
    n = x1.shape[0]
    parts = [jax.ShapeDtypeStruct(s, F32) for s in part_shapes]
    tm = _pick_tile(np.gcd.reduce([_part_rows(p) for p in parts]), TOKEN_TILES)
    n_tiles = n // tm
    out_specs, bounds = _token_parts_specs(parts, tm)
    part_tiles = [bounds[k + 1] - bounds[k] for k in range(len(parts))]
    return pl.pallas_call(
        functools.partial(_combine_kernel, part_tiles),
        grid=(n_tiles,),
        in_specs=[pl.BlockSpec((tm, D_MODEL), lambda i: (i, 0)),
                  pl.BlockSpec((tm, D_MODEL // 2), lambda i: (i, 0)),
                  pl.BlockSpec((tm, D_MODEL // 2), lambda i: (i + n_tiles, 0)),
                  pl.BlockSpec((tm, 128), lambda i: (i, 0)),
                  pl.BlockSpec((1, D_MODEL), lambda i: (0, 0))],
        out_specs=out_specs,
        out_shape=parts,
        compiler_params=pltpu.CompilerParams(dimension_semantics=("arbitrary",),
                                             vmem_limit_bytes=VMEM_LIMIT),
        name="combine",
    )(x1, o12, o12, gates, fnw)


def _moe(x, mixed, wo, nw, router_w, wg, wu, wd, fnw, part_shapes):
    i32 = jnp.int32
    n = x.shape[0]
    assert n % TOK_BLK == 0
    wrt = jnp.pad(router_w.T, ((0, 16 - N_EXPERTS), (0, 0))).astype(BF16)
    t = np.arange(TOK_BLK)
    upper = jnp.asarray((t[:, None] < t[None, :]).astype(np.float32), dtype=BF16)
    x1, hn_p, meta, gates, count = _router(x, mixed, wo, nw, wrt, upper)

    n_slots = (-(-2 * n // FFN_TILE) + N_EXPERTS) * FFN_TILE
    counts = count[:N_EXPERTS, 0].astype(i32)
    gsize = (counts + FFN_TILE - 1) // FFN_TILE * FFN_TILE
    gend = jnp.cumsum(gsize)
    goff = gend - gsize
    fstart = jnp.arange(n_slots // FFN_TILE, dtype=i32) * FFN_TILE
    f_valid = (fstart < gend[-1]).astype(i32)
    f_exp = jnp.sum((fstart[:, None] >= gend[None, :]).astype(i32), axis=1)
    f_exp = jnp.minimum(f_exp, jnp.sum((gend[-1] - 1 >= gend).astype(i32)))
    expert_ids = jnp.arange(N_EXPERTS, dtype=i32)[:, None]
    top = meta[0:2].astype(i32)
    rank = meta[2:4].astype(i32)
    slot = jnp.stack([jnp.sum(jnp.where(top[k][None, :] == expert_ids, goff[:, None], 0), axis=0) + rank[k]
                      for k in range(2)]).reshape(-1)
    pad = gsize - counts
    cpad = jnp.cumsum(pad)
    j = jnp.arange(n_slots - 2 * n, dtype=i32)
    pad_e = jnp.sum((j[:, None] >= cpad[None, :]).astype(i32), axis=1)
    pad_base = jnp.sum(jnp.where(pad_e[:, None] == expert_ids.T, (goff + counts - (cpad - pad))[None, :], 0), axis=1)
    pad_slot = jnp.where(pad_e < N_EXPERTS, pad_base + j, gend[-1] + j - cpad[-1])

    hn_p, wg = lax.optimization_barrier((hn_p, wg))
    xs_p = _sc_row_scatter(hn_p, jnp.concatenate([slot, pad_slot]))
    out_p = _expert_ffn(f_exp, f_valid, xs_p, wg.astype(BF16), wu, wd)
    o12 = _sc_row_gather(out_p, slot)
    return _combine(x1, o12, gates, fnw, part_shapes)


def _rope_table(seq_len, dec_len):
    half = ROPE_DIMS // 2
    pos = jnp.concatenate([jnp.arange(seq_len), PAST_LEN + jnp.arange(dec_len)]).astype(F32)
    inv_freq = jnp.power(jnp.float32(ROPE_THETA), -jnp.arange(half, dtype=F32) / half)
    lane = np.arange(128) % HEAD
    inv_lane = jnp.where(lane < ROPE_DIMS, inv_freq[lane % half], 0.0)
    ang = pos[:, None] * inv_lane[None, :]
    cos, sin = jnp.cos(ang), jnp.sin(ang)
    sin_a = jnp.where(lane < half, -sin, 0.0)
    sin_b = jnp.where((lane >= half) & (lane < ROPE_DIMS), sin, 0.0)
    return jnp.concatenate([cos, sin_a, sin_b], axis=1)


def _mixer_consts():
    t = np.arange(CHUNK)
    tri = (t[:, None] >= t[None, :])
    same = (t[:, None] // SUB) == (t[None, :] // SUB)
    c = np.arange(MIX_W)
    ind = (c[:, None] // HEAD) == (c[None, :] // HEAD)
    as_bf = lambda m: jnp.asarray(m.astype(np.float32), dtype=BF16)
    return [as_bf(tri), as_bf(tri & same), as_bf(same), as_bf(ind)]


def _mlstm_state_to_pairs(c, n):
    c = c.astype(F32)
    n_b = jnp.broadcast_to(n.astype(F32)[..., None], c.shape)
    z = jnp.zeros_like(c[:, 0::2])
    top = jnp.concatenate([c[:, 0::2], z, n_b[:, 0::2], z], axis=-1)
    bottom = jnp.concatenate([z, c[:, 1::2], z, n_b[:, 1::2]], axis=-1)
    return jnp.concatenate([top, bottom], axis=-2)


def _mlstm_state_from_pairs(s):
    b = s.shape[0]
    c = jnp.stack([s[:, :, a * HEAD:(a + 1) * HEAD, a * HEAD:(a + 1) * HEAD] for a in range(2)], axis=2)
    n = jnp.stack([s[:, :, a * HEAD:(a + 1) * HEAD, (2 + a) * HEAD] for a in range(2)], axis=2)
    return (c.reshape(b, N_HEADS, HEAD, HEAD), n.reshape(b, N_HEADS, HEAD))


def _pad_lanes(v, width):
    return jnp.pad(v, ((0, 0), (0, width - v.shape[1])))


def kernel(x_prompt, x_sample, state_mlstm_C, state_mlstm_n, state_mlstm_m, state_hgrn_S, cache_swa_k, cache_swa_v, state_ssd_h, state_ssd_conv, norm1_w, w_in, ml_ig_b, ml_fg_b, ml_norm_w, hg_lb_logits, hg_norm_w, sw_sinks, ssd_conv_w, ssd_conv_b, ssd_dt_bias, ssd_A_log, ssd_D, ssd_norm_w, w_out, norm2_w, ffn_w_gate, ffn_w_up, ffn_w_down, moe_router, moe_w_gate, moe_w_up, moe_w_down, final_norm_w):
    depth = w_in.shape[0]
    bp, seq_len, _ = x_prompt.shape
    bs, dec_len, _ = x_sample.shape
    assert seq_len % CHUNK == 0 and dec_len == CHUNK and depth % 2 == 0
    n_seq = bp + bs
    chunks_per_prompt = seq_len // CHUNK

    assert bp == SEQ_GROUP and bs % SEQ_GROUP == 0
    x_parts = (x_prompt.reshape(bp, chunks_per_prompt, CHUNK, D_MODEL), x_sample.reshape(bs * dec_len, D_MODEL))
    part_shapes = tuple(p.shape for p in x_parts)
    rope_tab = _rope_table(seq_len, dec_len)
    consts = _mixer_consts()

    sm = jax.nn.softmax(hg_lb_logits.astype(F32), axis=0)
    hg_lb = jnp.cumsum(sm, axis=0) - sm[0]

    def with_prompt_zeros(a):
        return jnp.concatenate([jnp.zeros((bp,) + a.shape[1:], F32), a.astype(F32)], axis=0)

    state_outs = []
    for l in range(depth):
        lp = {
            "sinks": sw_sinks[l].astype(F32),
            "gbias": _pad_lanes(jnp.concatenate([ml_ig_b[l], ml_fg_b[l], ssd_dt_bias[l]])[None, :].astype(F32), 128),
            "alog": _pad_lanes(ssd_A_log[l][None, :].astype(F32), 128),
            "mlnw": ml_norm_w[l][None, :].astype(F32),
            "hglb": hg_lb[l][None, :],
            "hgnw": hg_norm_w[l][None, :].astype(F32),
            "convw": ssd_conv_w[l].astype(F32),
            "convb": ssd_conv_b[l][None, :].astype(F32),
            "ssdd": jnp.repeat(ssd_D[l].astype(F32), HEAD)[None, :],
            "ssdnw": ssd_norm_w[l][None, :].astype(F32),
        }
        c_aug = _mlstm_state_to_pairs(state_mlstm_C[l], state_mlstm_n[l])
        m_pad = jnp.pad(state_mlstm_m[l][:, None, :], ((0, 0), (0, 7), (0, 128 - N_HEADS)))
        s_t = jnp.swapaxes(state_hgrn_S[l], -1, -2)
        s_t = jnp.moveaxis(s_t, 1, 2).reshape(bs, HEAD, MIX_W)
        cv_pad = jnp.pad(state_ssd_conv[l], ((0, 0), (8 - (SSD_CONV - 1), 0), (0, 0)))
        init = [with_prompt_zeros(a) for a in (
            c_aug, m_pad, s_t, cache_swa_k[l].reshape(bs, WINDOW, 128), cache_swa_v[l].reshape(bs, WINDOW, 128),
            state_ssd_h[l], cv_pad)]

        proj = _inproj(x_parts, norm1_w[l][None, :], w_in, l)
        mixed, c_o, m_o, s_o, k_o, v_o, h_o, cv_o = _mixer(
            proj, rope_tab, init, lp, consts, chunks_per_prompt, n_seq)
        wo = w_out[l].astype(BF16)
        j = l // 2
        if l % 2 == 0:
            x_parts = (_dense_ffn(x_parts, mixed, wo, norm2_w[l][None, :], ffn_w_gate[j].astype(BF16),
                                  ffn_w_up[j].astype(BF16), ffn_w_down[j].astype(BF16)),)
        else:
            assert l == depth - 1 and len(x_parts) == 1
            y_parts = _moe(x_parts[0], mixed, wo, norm2_w[l][None, :], moe_router[j], moe_w_gate[j],
                           moe_w_up[j], moe_w_down[j], final_norm_w[None, :], part_shapes)
        s_back = jnp.moveaxis(s_o.reshape(n_seq, HEAD, N_HEADS, HEAD), 2, 1)
        state_outs.append(_mlstm_state_from_pairs(c_o) + (m_o[:, 0, :N_HEADS],
                           jnp.swapaxes(s_back, -1, -2),
                           k_o.reshape(n_seq, WINDOW, SW_KV, HEAD), v_o.reshape(n_seq, WINDOW, SW_KV, HEAD),
                           h_o, cv_o[:, 8 - (SSD_CONV - 1):, :]))

    y_prompt = y_parts[0].reshape(bp, seq_len, D_MODEL)
    y_sample = y_parts[1].reshape(bs, dec_len, D_MODEL)
    stacked = [jnp.stack([so[k] for so in state_outs]) for k in range(8)]
    return (y_prompt, y_sample) + tuple(s[:, :bp] for s in stacked) + tuple(s[:, bp:] for s in stacked)
```

```python
import functools

import numpy as np
import jax
import jax.numpy as jnp
from jax import lax
from jax.experimental import pallas as pl
from jax.experimental.pallas import tpu as pltpu
from jax.experimental.pallas import tpu_sc as plsc

F32 = jnp.float32
BF16 = jnp.bfloat16

D_MODEL = 1024
CHUNK = 64
EPS = 1e-6
N_HEADS = 4
HEAD = 64
MIX_W = N_HEADS * HEAD
SW_KV = 2
WINDOW = 128
ROPE_DIMS = 16
ROPE_THETA = 500000.0
SSD_N = 128
SSD_CONV = 4
SSD_CONV_DIM = 768
PAST_LEN = 4096
N_EXPERTS = 8
SUB = 16
SEQ_GROUP = 2
HG_SLICE = 4
TOKEN_TILES = (512, 256, 128)

C_MLQ, C_MLK, C_MLV, C_MLO = 0, 256, 512, 768
C_HGQ, C_HGF, C_HGI, C_HGG = 1024, 1280, 1536, 1792
C_SWQ, C_SWK, C_SWV = 2048, 2304, 2432
C_SDZ, C_SDX, C_GATE = 2560, 2816, 3584
D_PROJ = 3712
W_IN_COLS = C_GATE + 3 * N_HEADS
NEG = -1e30
V7X_VMEM_BYTES = 64 * 1024 * 1024
VMEM_LIMIT = V7X_VMEM_BYTES - 8 * 1024 * 1024
SC_CORES = 2
SC_SUBCORES = 16
SC_MAX_GATHER_ROWS = 128


def _mm(a, b):
    return jnp.dot(a.astype(BF16), b.astype(BF16), preferred_element_type=F32)


def _mm_nt(a, b):
    return lax.dot_general(a.astype(BF16), b.astype(BF16), (((1,), (1,)), ((), ())),
                           preferred_element_type=F32)


def _mm_tn(a, b):
    return jnp.dot(a.T.astype(BF16), b.astype(BF16), preferred_element_type=F32)


def _mm_sel(sel, x):
    hi = x.astype(BF16)
    r1 = x - hi.astype(F32)
    mid = r1.astype(BF16)
    lo = (r1 - mid.astype(F32)).astype(BF16)
    return (jnp.dot(sel, hi, preferred_element_type=F32)
            + jnp.dot(sel, mid, preferred_element_type=F32)
            + jnp.dot(sel, lo, preferred_element_type=F32))


def _sigmoid(x):
    return 1.0 / (1.0 + jnp.exp(-x))


def _silu(x):
    return x * _sigmoid(x)


def _softplus(x):
    return jnp.maximum(x, 0.0) + jnp.log(1.0 + jnp.exp(-jnp.abs(x)))


def _rms(x, w):
    return x * lax.rsqrt(jnp.mean(x * x, axis=-1, keepdims=True) + EPS) * w


def _pick_tile(n, candidates):
    for c in candidates:
        if n % c == 0:
            return c
    raise ValueError(f"no tile for {n}")


def _token_parts_specs(parts, tm):
    bounds = np.cumsum([0] + [_part_rows(p) // tm for p in parts])
    specs = []
    for k, p in enumerate(parts):
        assert _part_rows(p) % tm == 0
        lo, hi = int(bounds[k]), int(bounds[k + 1])
        specs.append(_part_block_spec(p, tm, lo, hi - lo))
    return specs, [int(b) for b in bounds]


def _part_rows(p):
    return int(np.prod(p.shape[:-1]))


def _part_block_spec(p, tm, first_tile, n_tiles):
    idx = lambda i: jnp.clip(i - first_tile, 0, n_tiles - 1)
    if len(p.shape) == 2:
        return pl.BlockSpec((tm, p.shape[1]), lambda i: (idx(i), 0))
    grp, _, chunk, d = p.shape
    assert tm % (grp * chunk) == 0
    return pl.BlockSpec((grp, tm // (grp * chunk), chunk, d), lambda i: (0, idx(i), 0, 0))


def _load_part_tile(ref):
    if len(ref.shape) == 2:
        return ref[...]
    grp, n_chunks = ref.shape[:2]
    return jnp.concatenate([ref[s, c] for c in range(n_chunks) for s in range(grp)], axis=0)


def _store_part_tile(ref, x):
    if len(ref.shape) == 2:
        ref[...] = x
        return
    grp, n_chunks, chunk = ref.shape[:3]
    for c in range(n_chunks):
        for s in range(grp):
            r0 = (c * grp + s) * chunk
            ref[s, c] = x[r0:r0 + chunk]


def _read_token_tile(refs, bounds):
    i = pl.program_id(0)
    x = _load_part_tile(refs[-1])
    for k in range(len(refs) - 2, -1, -1):
        x = jnp.where(i < bounds[k + 1], _load_part_tile(refs[k]), x)
    return x


def _inproj_kernel(n_parts, bounds, *refs):
    x_refs = refs[:n_parts]
    nw_ref, w_ref, o_ref, wp_ref = refs[n_parts:]

    @pl.when(pl.program_id(0) == 0)
    def _():
        rows = 128
        ml_gates = slice(C_MLO, C_MLO + 2 * N_HEADS)
        sd_dt = slice(W_IN_COLS - N_HEADS, W_IN_COLS)
        n_gates = 3 * N_HEADS
        for r0 in range(0, D_MODEL, rows):
            w = w_ref[0, r0:r0 + rows, :]
            wp_ref[r0:r0 + rows, 0:C_MLO] = w[:, 0:ml_gates.start].astype(BF16)
            wp_ref[r0:r0 + rows, C_MLO:C_GATE] = w[:, ml_gates.stop:sd_dt.start].astype(BF16)
            gates = jnp.concatenate([w[:, ml_gates], w[:, sd_dt],
                                     jnp.zeros((rows, D_PROJ - C_GATE - n_gates), F32)], axis=1)
            wp_ref[r0:r0 + rows, C_GATE:D_PROJ] = gates.astype(BF16)

    xn = _rms(_read_token_tile(x_refs, bounds), nw_ref[...]).astype(BF16)
    o_ref[...] = jnp.dot(xn, wp_ref[...], preferred_element_type=F32)


def _inproj(x_parts, nw, w_all, layer):
    n = sum(_part_rows(p) for p in x_parts)
    tm = _pick_tile(np.gcd.reduce([_part_rows(p) for p in x_parts]), TOKEN_TILES)
    x_specs, bounds = _token_parts_specs(x_parts, tm)
    assert w_all.shape[1:] == (D_MODEL, W_IN_COLS)
    return pl.pallas_call(
        functools.partial(_inproj_kernel, len(x_parts), bounds),
        grid=(n // tm,),
        in_specs=x_specs + [pl.BlockSpec((1, D_MODEL), lambda i: (0, 0)),
                            pl.BlockSpec((1, D_MODEL, W_IN_COLS), lambda i: (layer, 0, 0),
                                         pipeline_mode=pl.Buffered(1))],
        out_specs=pl.BlockSpec((tm, D_PROJ), lambda i: (i, 0)),
        out_shape=jax.ShapeDtypeStruct((n, D_PROJ), F32),
        scratch_shapes=[pltpu.VMEM((D_MODEL, D_PROJ), BF16)],
        compiler_params=pltpu.CompilerParams(dimension_semantics=("arbitrary",),
                                             vmem_limit_bytes=VMEM_LIMIT),
        name="inproj",
    )(*x_parts, nw, w_all)


def _head_norm(x, width):
    parts = []
    for g in range(x.shape[1] // width):
        xg = x[:, g * width:(g + 1) * width]
        parts.append(xg * lax.rsqrt(jnp.mean(xg * xg, axis=-1, keepdims=True) + EPS))
    return jnp.concatenate(parts, axis=1)


def _mixer_kernel(prompt_steps,
                  sinks_ref,
                  proj_ref, rope_ref, c0_ref, m0_ref, s0_ref, k0_ref, v0_ref, h0_ref, cv0_ref,
                  gbias_ref, alog_ref, mlnw_ref, hglb_ref, hgnw_ref,
                  convw_ref, convb_ref, ssdd_ref, ssdnw_ref,
                  tri_ref, triblk_ref, blkones_ref, ind_ref,
                  mixed_ref, c_ref, m_ref, s_ref, k_ref, v_ref, h_ref, cv_ref,
                  convbuf, bloc_s, kk_s, hv_s):
    i = pl.program_id(0)
    is_prompt = i < prompt_steps
    is_first = jnp.logical_or(jnp.logical_not(is_prompt), i == 0)
    n_valid = jnp.where(is_prompt, jnp.minimum(i * CHUNK, WINDOW), WINDOW)

    @pl.when(is_first)
    def _():
        c_ref[...] = c0_ref[...]
        m_ref[...] = m0_ref[...]
        s_ref[...] = s0_ref[...]
        k_ref[...] = k0_ref[...]
        v_ref[...] = v0_ref[...]
        h_ref[...] = h0_ref[...]
        cv_ref[...] = cv0_ref[...]

    members = [_mixer_chunk(g, n_valid, sinks_ref, proj_ref, rope_ref,
                            gbias_ref, alog_ref, mlnw_ref, hglb_ref, hgnw_ref,
                            convw_ref, convb_ref, ssdd_ref, ssdnw_ref,
                            tri_ref, triblk_ref, blkones_ref, ind_ref,
                            mixed_ref, c_ref, m_ref, s_ref, k_ref, v_ref, h_ref, cv_ref,
                            convbuf, bloc_s, kk_s, hv_s) for g in range(SEQ_GROUP)]
    for phase in zip(*[m[0] for m in members]):
        chains = [c for per_member in zip(*phase) for c in per_member]
        while chains:
            alive = []
            for chain in chains:
                if next(chain, _CHAIN_DONE) is not _CHAIN_DONE:
                    alive.append(chain)
            chains = alive
    for _, finish in members:
        finish()


_CHAIN_DONE = object()


def _mixer_chunk(g, n_valid, sinks_ref, proj_ref, rope_ref,
                 gbias_ref, alog_ref, mlnw_ref, hglb_ref, hgnw_ref,
                 convw_ref, convb_ref, ssdd_ref, ssdnw_ref,
                 tri_ref, triblk_ref, blkones_ref, ind_ref,
                 mixed_ref, c_ref, m_ref, s_ref, k_ref, v_ref, h_ref, cv_ref,
                 convbuf, bloc_s, kk_s, hv_s):
    rs = slice(g * CHUNK, (g + 1) * CHUNK)
    L = CHUNK
    row = lax.broadcasted_iota(jnp.int32, (L, L), 0)
    col = lax.broadcasted_iota(jnp.int32, (L, L), 1)
    causal = row >= col
    tri = tri_ref[...]

    gate = proj_ref[rs,C_GATE:C_GATE + 128] + gbias_ref[...]
    ig = gate
    lf = pltpu.roll(-_softplus(-gate), 124, 1)
    dtv = pltpu.roll(_softplus(gate), 120, 1)

    b = _mm_sel(tri, lf)
    a = ig - b
    cm = a
    rowg = lax.broadcasted_iota(jnp.int32, (L, 128), 0)
    for sh in (1, 2, 4, 8, 16, 32):
        cm = jnp.where(rowg >= sh, jnp.maximum(cm, pltpu.roll(cm, sh, 0)), cm)
    m_prev = m_ref[g, 0:1, :]
    m_t = b + jnp.maximum(m_prev, cm)
    inter = jnp.exp(b + m_prev - m_t)
    bm = b - m_t
    a_t = a.T
    m_last = m_t[L - 1:L, :]
    b_last = b[L - 1:L, :]
    ws = jnp.exp(b_last + a - m_last)
    decay = jnp.exp(b_last + m_prev - m_last)
    emt = jnp.exp(-m_t)
    m_ref[g, 0:1, :] = m_last

    lane2 = lax.broadcasted_iota(jnp.int32, (L, 2 * HEAD), 1)
    low = lane2 < HEAD
    causal2 = lax.broadcasted_iota(jnp.int32, (L, 2 * HEAD), 0) >= (lane2 & (HEAD - 1))
    ones_blk = ind_ref[0:2 * HEAD, 0:2 * HEAD]
    ones_rows = jnp.ones((L, 2 * HEAD), BF16)
    st_row = lax.broadcasted_iota(jnp.int32, (2 * HEAD, 4 * HEAD), 0) // HEAD
    st_lane = (lax.broadcasted_iota(jnp.int32, (2 * HEAD, 4 * HEAD), 1) // HEAD) % 2
    ml_out = [None] * (N_HEADS // 2)

    def pair_cols(x, p):
        return jnp.where(low, x[:, 2 * p:2 * p + 1], x[:, 2 * p + 1:2 * p + 2])

    def split_rows(x):
        return jnp.concatenate([jnp.where(low, x, 0.0), jnp.where(low, 0.0, x)], axis=0)

    def ml_pair(p):
        cs = slice(2 * p * HEAD, 2 * (p + 1) * HEAD)
        q = proj_ref[rs, C_MLQ + cs.start:C_MLQ + cs.stop]
        k = proj_ref[rs, C_MLK + cs.start:C_MLK + cs.stop] * (HEAD ** -0.5)
        v = proj_ref[rs, C_MLV + cs.start:C_MLV + cs.stop]
        qk = _mm_nt(q, split_rows(k))
        yield
        st = c_ref[g, p]
        from_state = _mm(q, st)
        yield
        a_row = jnp.concatenate([a_t[2 * p:2 * p + 1, :], a_t[2 * p + 1:2 * p + 2, :]], axis=1)
        w = jnp.where(causal2, jnp.exp(pair_cols(bm, p) + a_row), 0.0)
        rhs = jnp.concatenate([split_rows(v).astype(BF16), ones_blk], axis=1)
        gate_in = pair_cols(inter, p)
        nd = (jnp.dot((w * qk).astype(BF16), rhs, preferred_element_type=F32)
              + jnp.concatenate([gate_in, gate_in], axis=1) * from_state)
        yield
        kws = k * pair_cols(ws, p)
        upd = jnp.dot(kws.T.astype(BF16), jnp.concatenate([v.astype(BF16), ones_rows], axis=1),
                      preferred_element_type=F32)
        yield
        den = jnp.maximum(jnp.abs(nd[:, 2 * HEAD:]), pair_cols(emt, p))
        ml_out[p] = nd[:, :2 * HEAD] / den
        dec = jnp.where(st_row == 0, decay[:, 2 * p:2 * p + 1], decay[:, 2 * p + 1:2 * p + 2])
        c_ref[g, p] = dec * st + jnp.where(st_row == st_lane, upd, 0.0)

    def ml_finish():
        ml = _head_norm(jnp.concatenate(ml_out, axis=1), HEAD)
        mixed_ref[rs,0:MIX_W] = (_sigmoid(proj_ref[rs,C_MLO:C_MLO + MIX_W]) * ml
                                 * mlnw_ref[...]).astype(mixed_ref.dtype)

    lb = hglb_ref[...]
    fg = lb + (1.0 - lb) * _sigmoid(proj_ref[rs,C_HGF:C_HGF + MIX_W])
    hq = _silu(proj_ref[rs,C_HGQ:C_HGQ + MIX_W]) * (HEAD ** -0.5)
    kk = 1.0 - fg
    lfg = jnp.log(fg)
    hv = proj_ref[rs,C_HGI:C_HGI + MIX_W]
    bloc = _mm_sel(triblk_ref[...], lfg)
    blast = _mm_sel(blkones_ref[...], lfg)
    qe = hq * jnp.exp(bloc)
    kw = kk * jnp.exp(blast - bloc)
    bloc_s[g] = bloc
    kk_s[g] = kk
    hv_s[g] = hv
    ind = ind_ref[...]
    rows = lax.broadcasted_iota(jnp.int32, (SUB, MIX_W), 0)
    n_blk = L // SUB
    hg_state_part = [None] * n_blk
    hg_block_part = [None] * n_blk

    def hg_within_block(blk):
        r0 = blk * SUB
        q_blk = hq[r0:r0 + SUB]
        b_blk = bloc[r0:r0 + SUB]
        xs = []
        for s in range(SUB):
            r = r0 + s
            e = jnp.exp(jnp.minimum(b_blk - bloc_s[g, r:r + 1, :], 0.0))
            xs.append((q_blk * (kk_s[g, r:r + 1, :] * e)).astype(BF16))
            if s % HG_SLICE == HG_SLICE - 1:
                yield
        att = jnp.dot(jnp.concatenate(xs, axis=0), ind, preferred_element_type=F32)
        yield
        o_blk = att[0:SUB] * hv_s[g, r0:r0 + 1, :]
        for s in range(1, SUB):
            r = r0 + s
            o_blk = o_blk + jnp.where(rows >= s, att[s * SUB:(s + 1) * SUB], 0.0) * hv_s[g, r:r + 1, :]
            if s % HG_SLICE == HG_SLICE - 1:
                yield
        hg_block_part[blk] = o_blk

    def hg_state_chain():
        st = s_ref[g]
        for blk in range(n_blk):
            r0 = blk * SUB
            hg_state_part[blk] = jnp.concatenate(
                [_mm_nt(qe[r0:r0 + SUB, h * HEAD:(h + 1) * HEAD], st[:, h * HEAD:(h + 1) * HEAD])
                 for h in range(N_HEADS)], axis=1)
            upd = jnp.concatenate(
                [_mm_tn(hv[r0:r0 + SUB, h * HEAD:(h + 1) * HEAD], kw[r0:r0 + SUB, h * HEAD:(h + 1) * HEAD])
                 for h in range(N_HEADS)], axis=1)
            yield
            st = st * jnp.exp(blast[r0:r0 + 1, :]) + upd
        s_ref[g] = st

    def hg_finish():
        o = jnp.concatenate([hg_state_part[blk] + hg_block_part[blk] for blk in range(n_blk)], axis=0)
        mixed_ref[rs,MIX_W:2 * MIX_W] = (_head_norm(o, HEAD) * hgnw_ref[...]
                                         * _silu(proj_ref[rs,C_HGG:C_HGG + MIX_W])).astype(mixed_ref.dtype)

    @functools.lru_cache(maxsize=None)
    def sw_prep():
        cos = rope_ref[:, 0:128]
        sin_a = rope_ref[:, 128:256]
        sin_b = rope_ref[:, 256:384]

        def rope(x):
            return x * cos + pltpu.roll(x, 120, 1) * sin_a + pltpu.roll(x, 8, 1) * sin_b

        q_rot = [rope(proj_ref[rs,C_SWQ:C_SWQ + 128]), rope(proj_ref[rs,C_SWQ + 128:C_SWQ + 256])]
        k_rot = rope(proj_ref[rs,C_SWK:C_SWK + 128])
        v_cur = proj_ref[rs,C_SWV:C_SWV + 128]
        colw = lax.broadcasted_iota(jnp.int32, (2 * L, WINDOW), 1)
        row2 = lax.broadcasted_iota(jnp.int32, (2 * L, 1), 0)
        return q_rot, k_rot, v_cur, k_ref[g], v_ref[g], colw >= WINDOW - n_valid, row2

    sw_prep()
    sw_out = [None] * (2 * SW_KV)

    def sw_group(kv):
        q_rot, k_rot, v_cur, k_prev, v_prev, prev_ok, row2 = sw_prep()
        gs = slice(kv * HEAD, (kv + 1) * HEAD)
        q2 = jnp.concatenate([q_rot[kv][:, 0:HEAD], q_rot[kv][:, HEAD:2 * HEAD]], axis=0) * (HEAD ** -0.5)
        s_p = jnp.where(prev_ok, _mm_nt(q2, k_prev[:, gs]), NEG)
        yield
        s_c = _mm_nt(q2, k_rot[:, gs])
        yield
        sink = jnp.where(row2 < L, sinks_ref[2 * kv], sinks_ref[2 * kv + 1])
        mx = jnp.maximum(jnp.maximum(jnp.max(s_p, axis=-1, keepdims=True),
                                     jnp.max(s_c, axis=-1, keepdims=True)), sink)
        e_p = jnp.exp(s_p - mx)
        e_c = jnp.exp(s_c - mx)
        den = (jnp.sum(e_p, axis=-1, keepdims=True) + jnp.sum(e_c, axis=-1, keepdims=True)
               + jnp.exp(sink - mx))
        o_p = _mm(e_p, v_prev[:, gs])
        yield
        o = (o_p + _mm(e_c, v_cur[:, gs])) / den
        yield
        sw_out[2 * kv] = o[0:L]
        sw_out[2 * kv + 1] = o[L:2 * L]

    def sw_finish():
        _, k_rot, v_cur, k_prev, v_prev, _, _ = sw_prep()
        mixed_ref[rs,2 * MIX_W:3 * MIX_W] = jnp.concatenate(sw_out, axis=1).astype(mixed_ref.dtype)
        k_ref[g, 0:WINDOW - L, :] = k_prev[L:WINDOW]
        k_ref[g, WINDOW - L:WINDOW, :] = k_rot
        v_ref[g, 0:WINDOW - L, :] = v_prev[L:WINDOW]
        v_ref[g, WINDOW - L:WINDOW, :] = v_cur

    @functools.lru_cache(maxsize=None)
    def sd_prep():
        convbuf[g, 0:8, :] = cv_ref[g]
        convbuf[g, 8:8 + L, :] = proj_ref[rs,C_SDX:C_SDX + SSD_CONV_DIM]
        acc = convb_ref[...] + convbuf[g, 5:5 + L, :] * convw_ref[0:1, :]
        for j in range(1, SSD_CONV):
            acc = acc + convbuf[g, 5 + j:5 + j + L, :] * convw_ref[j:j + 1, :]
        xbc = _silu(acc)
        cv_ref[g] = convbuf[g, L:L + 8, :]
        a_neg = -jnp.exp(alog_ref[...])
        ca = _mm_sel(tri, dtv * a_neg)
        ca_last = ca[L - 1:L, :]
        return (xbc, xbc[:, 0:MIX_W], ca, ca.T, dtv.T, jnp.exp(ca), jnp.exp(ca_last - ca) * dtv,
                jnp.exp(ca_last))

    sd_prep()
    sd_out = [None] * (N_HEADS // 2)

    def sd_group(grp):
        xbc, xs_all, ca, ca_t, dt_t, eca, wss, sdec = sd_prep()
        bm_g = xbc[:, MIX_W + grp * SSD_N:MIX_W + (grp + 1) * SSD_N]
        cm_g = xbc[:, MIX_W + 2 * SSD_N + grp * SSD_N:MIX_W + 2 * SSD_N + (grp + 1) * SSD_N]
        e, o = 2 * grp, 2 * grp + 1
        x = xs_all[:, e * HEAD:(o + 1) * HEAD]
        cb2 = _mm_nt(cm_g, jnp.concatenate([bm_g, bm_g], axis=0))
        yield
        hst = h_ref[g, e:o + 1].reshape(2 * HEAD, SSD_N)
        from_state = _mm_nt(cm_g, hst)
        yield
        ca_row = jnp.concatenate([ca_t[e:e + 1, :], ca_t[o:o + 1, :]], axis=1)
        dt_row = jnp.concatenate([dt_t[e:e + 1, :], dt_t[o:o + 1, :]], axis=1)
        dec = jnp.where(causal2, jnp.exp(pair_cols(ca, grp) - ca_row), 0.0)
        sd_out[grp] = _mm(cb2 * dec * dt_row, split_rows(x)) + pair_cols(eca, grp) * from_state
        yield
        upd = _mm_tn(x * pair_cols(wss, grp), bm_g)
        yield
        keep = jnp.where(lax.broadcasted_iota(jnp.int32, (2 * HEAD, SSD_N), 0) < HEAD,
                         sdec[:, e:e + 1], sdec[:, o:o + 1])
        h_ref[g, e:o + 1] = (keep * hst + upd).reshape(2, HEAD, SSD_N)

    def sd_finish():
        xs_all = sd_prep()[1]
        y = jnp.concatenate(sd_out, axis=1) + ssdd_ref[...] * xs_all
        gated = y * _silu(proj_ref[rs,C_SDZ:C_SDZ + MIX_W])
        mixed_ref[rs,3 * MIX_W:4 * MIX_W] = (_head_norm(gated, 2 * HEAD)
                                             * ssdnw_ref[...]).astype(mixed_ref.dtype)

    chains = ([[hg_within_block(blk) for blk in range(n_blk)]]
              + [[ml_pair(p)] for p in range(N_HEADS // 2)] + [[hg_state_chain(), sd_group(0)]]
              + [[sd_group(1)]]
              + [[sw_group(kv)] for kv in range(SW_KV)])

    def finish():
        ml_finish()
        hg_finish()
        sw_finish()
        sd_finish()

    return chains, finish


def _mixer(proj, rope_tab, init, lp, consts, chunks_per_prompt, n_seq):
    n = proj.shape[0]
    rows = SEQ_GROUP * CHUNK
    n_steps = n // rows

    def group_of(i):
        return jnp.where(i < chunks_per_prompt, 0, i - chunks_per_prompt + 1)

    def rope_blk(i):
        return jnp.minimum(i, chunks_per_prompt)

    def per_seq(shape):
        nd = len(shape)
        return pl.BlockSpec((SEQ_GROUP,) + shape, lambda i, s: (group_of(i),) + (0,) * nd)

    def const(shape):
        nd = len(shape)
        return pl.BlockSpec(shape, lambda i, s: (0,) * nd)

    state_shapes = [(N_HEADS // 2, 2 * HEAD, 4 * HEAD), (8, 128), (HEAD, MIX_W), (WINDOW, 128), (WINDOW, 128),
                    (N_HEADS, HEAD, SSD_N), (8, SSD_CONV_DIM)]
    in_specs = ([pl.BlockSpec((rows, D_PROJ), lambda i, s: (i, 0)),
                 pl.BlockSpec((CHUNK, 384), lambda i, s: (rope_blk(i), 0))]
                + [per_seq(s) for s in state_shapes]
                + [const((1, 128)), const((1, 128)), const((1, MIX_W)), const((1, MIX_W)), const((1, MIX_W)),
                   const((SSD_CONV, SSD_CONV_DIM)), const((1, SSD_CONV_DIM)), const((1, MIX_W)),
                   const((1, MIX_W)),
                   const((CHUNK, CHUNK)), const((CHUNK, CHUNK)), const((CHUNK, CHUNK)), const((MIX_W, MIX_W))])
    out_specs = ([pl.BlockSpec((rows, D_MODEL), lambda i, s: (i, 0))]
                 + [per_seq(s) for s in state_shapes])
    out_shape = ([jax.ShapeDtypeStruct((n, D_MODEL), BF16)]
                 + [jax.ShapeDtypeStruct((n_seq,) + s, F32) for s in state_shapes])
    grid_spec = pltpu.PrefetchScalarGridSpec(
        num_scalar_prefetch=1, grid=(n_steps,), in_specs=in_specs, out_specs=out_specs,
        scratch_shapes=[pltpu.VMEM((SEQ_GROUP, CHUNK + 8, SSD_CONV_DIM), F32),
                        pltpu.VMEM((SEQ_GROUP, CHUNK, MIX_W), F32), pltpu.VMEM((SEQ_GROUP, CHUNK, MIX_W), F32),
                        pltpu.VMEM((SEQ_GROUP, CHUNK, MIX_W), F32)])
    return pl.pallas_call(
        functools.partial(_mixer_kernel, chunks_per_prompt),
        grid_spec=grid_spec, out_shape=out_shape,
        compiler_params=pltpu.CompilerParams(dimension_semantics=("arbitrary",),
                                             vmem_limit_bytes=VMEM_LIMIT),
        name="mixer",
    )(lp["sinks"], proj, rope_tab, *init,
      lp["gbias"], lp["alog"], lp["mlnw"], lp["hglb"], lp["hgnw"],
      lp["convw"], lp["convb"], lp["ssdd"], lp["ssdnw"], *consts)


def _dense_ffn_kernel(f_chunk, n_parts, bounds, *refs):
    x_refs = refs[:n_parts]
    mix_ref, wo_ref, nw_ref, wg_ref, wu_ref, wd_ref, o_ref = refs[n_parts:]
    x1 = _read_token_tile(x_refs, bounds) + jnp.dot(mix_ref[...], wo_ref[...], preferred_element_type=F32)
    hn = _rms(x1, nw_ref[...]).astype(BF16)
    o_ref[...] = x1
    for f0 in range(0, wg_ref.shape[1], f_chunk):
        g = jnp.dot(hn, wg_ref[:, f0:f0 + f_chunk], preferred_element_type=F32)
        u = jnp.dot(hn, wu_ref[:, f0:f0 + f_chunk], preferred_element_type=F32)
        act = (_silu(g) * u).astype(BF16)
        o_ref[...] += jnp.dot(act, wd_ref[f0:f0 + f_chunk, :], preferred_element_type=F32)


def _dense_ffn(x_parts, mixed, wo, nw, wg, wu, wd):
    n = mixed.shape[0]
    d_ff = wg.shape[1]
    tm = _pick_tile(np.gcd.reduce([_part_rows(p) for p in x_parts]), TOKEN_TILES)
    x_specs, bounds = _token_parts_specs(x_parts, tm)
    full = lambda shape: pl.BlockSpec(shape, lambda i: (0, 0))
    return pl.pallas_call(
        functools.partial(_dense_ffn_kernel, 256, len(x_parts), bounds),
        grid=(n // tm,),
        in_specs=x_specs + [
                  pl.BlockSpec((tm, D_MODEL), lambda i: (i, 0)),
                  full((D_MODEL, D_MODEL)), full((1, D_MODEL)),
                  full((D_MODEL, d_ff)), full((D_MODEL, d_ff)), full((d_ff, D_MODEL))],
        out_specs=pl.BlockSpec((tm, D_MODEL), lambda i: (i, 0)),
        out_shape=jax.ShapeDtypeStruct((n, D_MODEL), F32),
        compiler_params=pltpu.CompilerParams(dimension_semantics=("arbitrary",),
                                             vmem_limit_bytes=VMEM_LIMIT),
        name="dense_ffn",
    )(*x_parts, mixed, wo, nw, wg, wu, wd)


TOK_BLK = 1024
FFN_TILE = 1024


def _pack_bf16_pairs(x):
    w = x.shape[1] // 2
    bits = pltpu.bitcast(x.astype(BF16).astype(F32), jnp.int32)
    return lax.shift_right_logical(bits[:, :w], 16) | bits[:, w:]


def _unpack_bf16_pairs(p):
    lo = pltpu.bitcast(lax.shift_left(p, 16), F32)
    hi = pltpu.bitcast(p & jnp.int32(-65536), F32)
    return lo, hi


def _router_kernel(x_ref, mix_ref, wo_ref, nw_ref, wrt_ref, upper_ref,
                   x1_ref, hn_ref, meta_ref, gate_ref, count_ref, carry):
    @pl.when(pl.program_id(0) == 0)
    def _():
        carry[...] = jnp.zeros_like(carry)

    x1 = x_ref[...] + jnp.dot(mix_ref[...], wo_ref[...], preferred_element_type=F32)
    x1_ref[...] = x1
    hn_f = _rms(x1, nw_ref[...])
    hn = hn_f.astype(BF16)
    hn_ref[...] = _pack_bf16_pairs(hn_f)
    logits = lax.dot_general(wrt_ref[...], hn, (((1,), (1,)), ((), ())),
                             preferred_element_type=F32)
    sub = lax.broadcasted_iota(jnp.int32, logits.shape, 0)
    l1 = jnp.where(sub < N_EXPERTS, logits, NEG)
    m1 = jnp.max(l1, axis=0, keepdims=True)
    i1 = jnp.min(jnp.where(l1 == m1, sub, 99), axis=0, keepdims=True)
    l2 = jnp.where(sub == i1, NEG, l1)
    m2 = jnp.max(l2, axis=0, keepdims=True)
    i2 = jnp.min(jnp.where(l2 == m2, sub, 99), axis=0, keepdims=True)
    e2 = jnp.exp(m2 - m1)
    g1 = 1.0 / (1.0 + e2)
    g2 = e2 / (1.0 + e2)
    sel = jnp.where(sub == i1, 1.0, jnp.where(sub == i2, 1.0, 0.0))
    before = jnp.dot(sel.astype(BF16), upper_ref[...], preferred_element_type=F32) + carry[:, 0:1]
    r1 = jnp.sum(jnp.where(sub == i1, before, 0.0), axis=0, keepdims=True)
    r2 = jnp.sum(jnp.where(sub == i2, before, 0.0), axis=0, keepdims=True)
    meta_ref[...] = jnp.concatenate([i1.astype(F32), i2.astype(F32), r1, r2,
                                     jnp.zeros((4, r1.shape[1]), F32)], axis=0)
    gate_ref[...] = jnp.concatenate([g1, g2, jnp.zeros((126, r1.shape[1]), F32)], axis=0).T
    total = carry[...] + jnp.sum(sel, axis=1, keepdims=True)
    carry[...] = total
    count_ref[...] = total


def _router(x, mixed, wo, nw, wrt, upper):
    n = x.shape[0]
    tm = TOK_BLK
    full = lambda shape: pl.BlockSpec(shape, lambda i: (0,) * len(shape))
    tok = lambda w: pl.BlockSpec((tm, w), lambda i: (i, 0))
    return pl.pallas_call(
        _router_kernel,
        grid=(n // tm,),
        in_specs=[tok(D_MODEL), tok(D_MODEL), full((D_MODEL, D_MODEL)), full((1, D_MODEL)),
                  full((16, D_MODEL)), full((tm, tm))],
        out_specs=[tok(D_MODEL), tok(D_MODEL // 2),
                   pl.BlockSpec((8, tm), lambda i: (0, i)), tok(128), full((16, 128))],
        out_shape=[jax.ShapeDtypeStruct((n, D_MODEL), F32), jax.ShapeDtypeStruct((n, D_MODEL // 2), jnp.int32),
                   jax.ShapeDtypeStruct((8, n), F32), jax.ShapeDtypeStruct((n, 128), F32),
                   jax.ShapeDtypeStruct((16, 128), F32)],
        scratch_shapes=[pltpu.VMEM((16, 128), F32)],
        compiler_params=pltpu.CompilerParams(dimension_semantics=("arbitrary",),
                                             vmem_limit_bytes=VMEM_LIMIT),
        name="router",
    )(x, mixed, wo, nw, wrt, upper)


def _sc_row_gather(table, idx):
    n_workers = SC_CORES * SC_SUBCORES
    b = idx.shape[0]
    d = table.shape[1]
    assert b % (8 * n_workers) == 0
    per_worker = b // n_workers
    chunk = max(c for c in range(8, SC_MAX_GATHER_ROWS + 1, 8) if per_worker % c == 0)
    mesh = plsc.VectorSubcoreMesh(core_axis_name="c", subcore_axis_name="s")

    @functools.partial(
        pl.kernel, mesh=mesh, out_type=jax.ShapeDtypeStruct((b, d), table.dtype),
        scratch_types=[pltpu.VMEM((chunk,), jnp.int32), pltpu.VMEM((chunk, d), table.dtype),
                       pltpu.SemaphoreType.DMA])
    def gather(table_hbm, idx_hbm, out_hbm, idx_v, rows_v, sem):
        worker = lax.axis_index("s") * SC_CORES + lax.axis_index("c")
        base = worker * per_worker

        @pl.loop(0, per_worker // chunk)
        def _(c):
            off = pl.multiple_of(base + c * chunk, 8)
            pltpu.sync_copy(idx_hbm.at[pl.ds(off, chunk)], idx_v)
            pltpu.async_copy(table_hbm.at[idx_v], rows_v, sem).wait()
            pltpu.sync_copy(rows_v, out_hbm.at[pl.ds(off, chunk)])

    return gather(table, idx)


def _sc_row_scatter(rows, idx):
    n_workers = SC_CORES * SC_SUBCORES
    b = n_out = idx.shape[0]
    v, d = rows.shape
    assert b % (8 * n_workers) == 0
    per_worker = b // n_workers
    chunk = max(c for c in range(8, SC_MAX_GATHER_ROWS + 1, 8) if per_worker % c == 0 and v % c == 0)
    mesh = plsc.VectorSubcoreMesh(core_axis_name="c", subcore_axis_name="s")

    @functools.partial(
        pl.kernel, mesh=mesh, out_type=jax.ShapeDtypeStruct((n_out, d), rows.dtype),
        scratch_types=[pltpu.VMEM((chunk,), jnp.int32), pltpu.VMEM((chunk, d), rows.dtype)])
    def scatter(rows_hbm, idx_hbm, out_hbm, idx_v, rows_v):
        worker = lax.axis_index("s") * SC_CORES + lax.axis_index("c")
        base = worker * per_worker

        @pl.loop(0, per_worker // chunk)
        def _(c):
            off = pl.multiple_of(base + c * chunk, 8)
            pltpu.sync_copy(idx_hbm.at[pl.ds(off, chunk)], idx_v)
            pltpu.sync_copy(rows_hbm.at[pl.ds(pl.multiple_of(lax.rem(off, v), 8), chunk)], rows_v)
            pltpu.sync_copy(rows_v, out_hbm.at[idx_v])

    return scatter(rows, idx)


def _expert_ffn_kernel(texp_ref, tvalid_ref, xs_ref, wg_ref, wu_ref, wd_ref, o_ref, acc_ref, xb_ref):
    j = pl.program_id(0)
    f = pl.program_id(1)
    last = f == pl.num_programs(1) - 1
    valid = tvalid_ref[j] == 1
    half = D_MODEL // 2

    @pl.when(jnp.logical_and(valid, f == 0))
    def _():
        lo, hi = _unpack_bf16_pairs(xs_ref[...])
        xb_ref[:, :half] = lo.astype(BF16)
        xb_ref[:, half:] = hi.astype(BF16)

        acc_ref[...] = jnp.zeros_like(acc_ref)

    @pl.when(valid)
    def _():
        xs = xb_ref[...]
        g = jnp.dot(xs, wg_ref[0], preferred_element_type=F32)
        u = jnp.dot(xs, wu_ref[0].astype(BF16), preferred_element_type=F32)
        act = (_silu(g) * u).astype(BF16)
        acc_ref[...] += jnp.dot(act, wd_ref[0].astype(BF16), preferred_element_type=F32)

        @pl.when(last)
        def _():
            o_ref[...] = _pack_bf16_pairs(acc_ref[...])

    @pl.when(jnp.logical_and(jnp.logical_not(valid), last))
    def _():
        o_ref[...] = jnp.zeros_like(o_ref)


def _expert_ffn(f_exp, f_valid, xs, wg, wu, wd):
    n_slots = xs.shape[0]
    d_ff = wg.shape[2]
    tf = _pick_tile(d_ff, (512, 256, 128))
    grid_spec = pltpu.PrefetchScalarGridSpec(
        num_scalar_prefetch=2, grid=(n_slots // FFN_TILE, d_ff // tf),
        in_specs=[pl.BlockSpec((FFN_TILE, D_MODEL // 2), lambda j, f, te, tv: (j, 0)),
                  pl.BlockSpec((1, D_MODEL, tf), lambda j, f, te, tv: (te[j], 0, f)),
                  pl.BlockSpec((1, D_MODEL, tf), lambda j, f, te, tv: (te[j], 0, f)),
                  pl.BlockSpec((1, tf, D_MODEL), lambda j, f, te, tv: (te[j], f, 0))],
        out_specs=pl.BlockSpec((FFN_TILE, D_MODEL // 2), lambda j, f, te, tv: (j, 0)),
        scratch_shapes=[pltpu.VMEM((FFN_TILE, D_MODEL), F32), pltpu.VMEM((FFN_TILE, D_MODEL), BF16)])
    return pl.pallas_call(
        _expert_ffn_kernel, grid_spec=grid_spec,
        out_shape=jax.ShapeDtypeStruct((n_slots, D_MODEL // 2), jnp.int32),
        compiler_params=pltpu.CompilerParams(dimension_semantics=("arbitrary", "arbitrary"),
                                             vmem_limit_bytes=VMEM_LIMIT),
        name="expert_ffn",
    )(f_exp, f_valid, xs, wg, wu, wd)


def _combine_kernel(part_tiles, x1_ref, o1_ref, o2_ref, gate_ref, fnw_ref, *y_refs):
    i = pl.program_id(0)
    lo1, hi1 = _unpack_bf16_pairs(o1_ref[...])
    lo2, hi2 = _unpack_bf16_pairs(o2_ref[...])
    g1 = gate_ref[:, 0:1]
    g2 = gate_ref[:, 1:2]
    half = D_MODEL // 2
    x1 = x1_ref[...]
    fnw = fnw_ref[...]
    ya = x1[:, :half] + g1 * lo1 + g2 * lo2
    yb = x1[:, half:] + g1 * hi1 + g2 * hi2
    scale = lax.rsqrt((jnp.sum(ya * ya, axis=-1, keepdims=True) + jnp.sum(yb * yb, axis=-1, keepdims=True))
                      / D_MODEL + EPS)
    y = jnp.concatenate([ya * scale * fnw[:, :half], yb * scale * fnw[:, half:]], axis=1)
    lo_tile = 0
    for y_ref, tiles in zip(y_refs, part_tiles):
        @pl.when(jnp.logical_and(i >= lo_tile, i < lo_tile + tiles))
        def _(y_ref=y_ref):
            _store_part_tile(y_ref, y)
        lo_tile += tiles


def _combine(x1, o12, gates, fnw, part_shapes):
    n = x1.shape[0]
    parts = [jax.ShapeDtypeStruct(s, F32) for s in part_shapes]
    tm = _pick_tile(np.gcd.reduce([_part_rows(p) for p in parts]), TOKEN_TILES)
    n_tiles = n // tm
    out_specs, bounds = _token_parts_specs(parts, tm)
    part_tiles = [bounds[k + 1] - bounds[k] for k in range(len(parts))]
    return pl.pallas_call(
        functools.partial(_combine_kernel, part_tiles),
        grid=(n_tiles,),
        in_specs=[pl.BlockSpec((tm, D_MODEL), lambda i: (i, 0)),
                  pl.BlockSpec((tm, D_MODEL // 2), lambda i: (i, 0)),
                  pl.BlockSpec((tm, D_MODEL // 2), lambda i: (i + n_tiles, 0)),
                  pl.BlockSpec((tm, 128), lambda i: (i, 0)),
                  pl.BlockSpec((1, D_MODEL), lambda i: (0, 0))],
        out_specs=out_specs,
        out_shape=parts,
        compiler_params=pltpu.CompilerParams(dimension_semantics=("arbitrary",),
                                             vmem_limit_bytes=VMEM_LIMIT),
        name="combine",
    )(x1, o12, o12, gates, fnw)


def _moe(x, mixed, wo, nw, router_w, wg, wu, wd, fnw, part_shapes):
    i32 = jnp.int32
    n = x.shape[0]
    assert n % TOK_BLK == 0
    wrt = jnp.pad(router_w.T, ((0, 16 - N_EXPERTS), (0, 0))).astype(BF16)
    t = np.arange(TOK_BLK)
    upper = jnp.asarray((t[:, None] < t[None, :]).astype(np.float32), dtype=BF16)
    x1, hn_p, meta, gates, count = _router(x, mixed, wo, nw, wrt, upper)

    sc_quantum = int(np.lcm(FFN_TILE, SC_CORES * SC_SUBCORES * SC_MAX_GATHER_ROWS))
    n_slots = -(-(-(-2 * n // FFN_TILE) + N_EXPERTS) * FFN_TILE // sc_quantum) * sc_quantum
    counts = count[:N_EXPERTS, 0].astype(i32)
    gsize = (counts + FFN_TILE - 1) // FFN_TILE * FFN_TILE
    gend = jnp.cumsum(gsize)
    goff = gend - gsize
    fstart = jnp.arange(n_slots // FFN_TILE, dtype=i32) * FFN_TILE
    f_valid = (fstart < gend[-1]).astype(i32)
    f_exp = jnp.sum((fstart[:, None] >= gend[None, :]).astype(i32), axis=1)
    f_exp = jnp.minimum(f_exp, jnp.sum((gend[-1] - 1 >= gend).astype(i32)))
    expert_ids = jnp.arange(N_EXPERTS, dtype=i32)[:, None]
    top = meta[0:2].astype(i32)
    rank = meta[2:4].astype(i32)
    slot = jnp.stack([jnp.sum(jnp.where(top[k][None, :] == expert_ids, goff[:, None], 0), axis=0) + rank[k]
                      for k in range(2)]).reshape(-1)
    pad = gsize - counts
    cpad = jnp.cumsum(pad)
    j = jnp.arange(n_slots - 2 * n, dtype=i32)
    pad_e = jnp.sum((j[:, None] >= cpad[None, :]).astype(i32), axis=1)
    pad_base = jnp.sum(jnp.where(pad_e[:, None] == expert_ids.T, (goff + counts - (cpad - pad))[None, :], 0), axis=1)
    pad_slot = jnp.where(pad_e < N_EXPERTS, pad_base + j, gend[-1] + j - cpad[-1])

    hn_p, wg = lax.optimization_barrier((hn_p, wg))
    xs_p = _sc_row_scatter(hn_p, jnp.concatenate([slot, pad_slot]))
    out_p = _expert_ffn(f_exp, f_valid, xs_p, wg.astype(BF16), wu, wd)
    o12 = _sc_row_gather(out_p, slot)
    return _combine(x1, o12, gates, fnw, part_shapes)


def _rope_table(seq_len, dec_len):
    half = ROPE_DIMS // 2
    pos = jnp.concatenate([jnp.arange(seq_len), PAST_LEN + jnp.arange(dec_len)]).astype(F32)
    inv_freq = jnp.power(jnp.float32(ROPE_THETA), -jnp.arange(half, dtype=F32) / half)
    lane = np.arange(128) % HEAD
    inv_lane = jnp.where(lane < ROPE_DIMS, inv_freq[lane % half], 0.0)
    ang = pos[:, None] * inv_lane[None, :]
    cos, sin = jnp.cos(ang), jnp.sin(ang)
    sin_a = jnp.where(lane < half, -sin, 0.0)
    sin_b = jnp.where((lane >= half) & (lane < ROPE_DIMS), sin, 0.0)
    return jnp.concatenate([cos, sin_a, sin_b], axis=1)


def _mixer_consts():
    t = np.arange(CHUNK)
    tri = (t[:, None] >= t[None, :])
    same = (t[:, None] // SUB) == (t[None, :] // SUB)
    c = np.arange(MIX_W)
    ind = (c[:, None] // HEAD) == (c[None, :] // HEAD)
    as_bf = lambda m: jnp.asarray(m.astype(np.float32), dtype=BF16)
    return [as_bf(tri), as_bf(tri & same), as_bf(same), as_bf(ind)]


def _mlstm_state_to_pairs(c, n):
    c = c.astype(F32)
    n_b = jnp.broadcast_to(n.astype(F32)[..., None], c.shape)
    z = jnp.zeros_like(c[:, 0::2])
    top = jnp.concatenate([c[:, 0::2], z, n_b[:, 0::2], z], axis=-1)
    bottom = jnp.concatenate([z, c[:, 1::2], z, n_b[:, 1::2]], axis=-1)
    return jnp.concatenate([top, bottom], axis=-2)


def _mlstm_state_from_pairs(s):
    b = s.shape[0]
    c = jnp.stack([s[:, :, a * HEAD:(a + 1) * HEAD, a * HEAD:(a + 1) * HEAD] for a in range(2)], axis=2)
    n = jnp.stack([s[:, :, a * HEAD:(a + 1) * HEAD, (2 + a) * HEAD] for a in range(2)], axis=2)
    return (c.reshape(b, N_HEADS, HEAD, HEAD), n.reshape(b, N_HEADS, HEAD))


def _pad_lanes(v, width):
    return jnp.pad(v, ((0, 0), (0, width - v.shape[1])))


def kernel(x_prompt, x_sample, state_mlstm_C, state_mlstm_n, state_mlstm_m, state_hgrn_S, cache_swa_k, cache_swa_v, state_ssd_h, state_ssd_conv, norm1_w, w_in, ml_ig_b, ml_fg_b, ml_norm_w, hg_lb_logits, hg_norm_w, sw_sinks, ssd_conv_w, ssd_conv_b, ssd_dt_bias, ssd_A_log, ssd_D, ssd_norm_w, w_out, norm2_w, ffn_w_gate, ffn_w_up, ffn_w_down, moe_router, moe_w_gate, moe_w_up, moe_w_down, final_norm_w):
    depth = w_in.shape[0]
    bp, seq_len, _ = x_prompt.shape
    bs, dec_len, _ = x_sample.shape
    assert seq_len % CHUNK == 0 and dec_len == CHUNK and depth % 2 == 0
    n_seq = bp + bs
    chunks_per_prompt = seq_len // CHUNK

    assert bp == SEQ_GROUP and bs % SEQ_GROUP == 0
    x_parts = (x_prompt.reshape(bp, chunks_per_prompt, CHUNK, D_MODEL), x_sample.reshape(bs * dec_len, D_MODEL))
    part_shapes = tuple(p.shape for p in x_parts)
    rope_tab = _rope_table(seq_len, dec_len)
    consts = _mixer_consts()

    sm = jax.nn.softmax(hg_lb_logits.astype(F32), axis=0)
    hg_lb = jnp.cumsum(sm, axis=0) - sm[0]

    def with_prompt_zeros(a):
        return jnp.concatenate([jnp.zeros((bp,) + a.shape[1:], F32), a.astype(F32)], axis=0)

    state_outs = []
    for l in range(depth):
        lp = {
            "sinks": sw_sinks[l].astype(F32),
            "gbias": _pad_lanes(jnp.concatenate([ml_ig_b[l], ml_fg_b[l], ssd_dt_bias[l]])[None, :].astype(F32), 128),
            "alog": _pad_lanes(ssd_A_log[l][None, :].astype(F32), 128),
            "mlnw": ml_norm_w[l][None, :].astype(F32),
            "hglb": hg_lb[l][None, :],
            "hgnw": hg_norm_w[l][None, :].astype(F32),
            "convw": ssd_conv_w[l].astype(F32),
            "convb": ssd_conv_b[l][None, :].astype(F32),
            "ssdd": jnp.repeat(ssd_D[l].astype(F32), HEAD)[None, :],
            "ssdnw": ssd_norm_w[l][None, :].astype(F32),
        }
        c_aug = _mlstm_state_to_pairs(state_mlstm_C[l], state_mlstm_n[l])
        m_pad = jnp.pad(state_mlstm_m[l][:, None, :], ((0, 0), (0, 7), (0, 128 - N_HEADS)))
        s_t = jnp.swapaxes(state_hgrn_S[l], -1, -2)
        s_t = jnp.moveaxis(s_t, 1, 2).reshape(bs, HEAD, MIX_W)
        cv_pad = jnp.pad(state_ssd_conv[l], ((0, 0), (8 - (SSD_CONV - 1), 0), (0, 0)))
        init = [with_prompt_zeros(a) for a in (
            c_aug, m_pad, s_t, cache_swa_k[l].reshape(bs, WINDOW, 128), cache_swa_v[l].reshape(bs, WINDOW, 128),
            state_ssd_h[l], cv_pad)]

        proj = _inproj(x_parts, norm1_w[l][None, :], w_in, l)
        mixed, c_o, m_o, s_o, k_o, v_o, h_o, cv_o = _mixer(
            proj, rope_tab, init, lp, consts, chunks_per_prompt, n_seq)
        wo = w_out[l].astype(BF16)
        j = l // 2
        if l % 2 == 0:
            x_parts = (_dense_ffn(x_parts, mixed, wo, norm2_w[l][None, :], ffn_w_gate[j].astype(BF16),
                                  ffn_w_up[j].astype(BF16), ffn_w_down[j].astype(BF16)),)
        else:
            assert l == depth - 1 and len(x_parts) == 1
            y_parts = _moe(x_parts[0], mixed, wo, norm2_w[l][None, :], moe_router[j], moe_w_gate[j],
                           moe_w_up[j], moe_w_down[j], final_norm_w[None, :], part_shapes)
        s_back = jnp.moveaxis(s_o.reshape(n_seq, HEAD, N_HEADS, HEAD), 2, 1)
        state_outs.append(_mlstm_state_from_pairs(c_o) + (m_o[:, 0, :N_HEADS],
                           jnp.swapaxes(s_back, -1, -2),
                           k_o.reshape(n_seq, WINDOW, SW_KV, HEAD), v_o.reshape(n_seq, WINDOW, SW_KV, HEAD),
                           h_o, cv_o[:, 8 - (SSD_CONV - 1):, :]))

    y_prompt = y_parts[0].reshape(bp, seq_len, D_MODEL)
    y_sample = y_parts[1].reshape(bs, dec_len, D_MODEL)
    stacked = [jnp.stack([so[k] for so in state_outs]) for k in range(8)]
    return (y_prompt, y_sample) + tuple(s[:, :bp] for s in stacked) + tuple(s[:, bp:] for s in stacked)
```

```python
import functools

import numpy as np
import jax
import jax.numpy as jnp
from jax import lax
from jax.experimental import pallas as pl
from jax.experimental.pallas import tpu as pltpu
from jax.experimental.pallas import tpu_sc as plsc

F32 = jnp.float32
BF16 = jnp.bfloat16

D_MODEL = 1024
CHUNK = 64
EPS = 1e-6
N_HEADS = 4
HEAD = 64
MIX_W = N_HEADS * HEAD
SW_KV = 2
WINDOW = 128
ROPE_DIMS = 16
ROPE_THETA = 500000.0
SSD_N = 128
SSD_CONV = 4
SSD_CONV_DIM = 768
PAST_LEN = 4096
N_EXPERTS = 8
SUB = 16
SEQ_GROUP = 2
HG_SLICE = 4
TOKEN_TILES = (512, 256, 128)

C_MLQ, C_MLK, C_MLV, C_MLO = 0, 256, 512, 768
C_HGQ, C_HGF, C_HGI, C_HGG = 1024, 1280, 1536, 1792
C_SWQ, C_SWK, C_SWV = 2048, 2304, 2432
C_SDZ, C_SDX, C_GATE = 2560, 2816, 3584
D_PROJ = 3712
W_IN_COLS = C_GATE + 3 * N_HEADS
NEG = -1e30
V7X_VMEM_BYTES = 64 * 1024 * 1024
VMEM_LIMIT = V7X_VMEM_BYTES - 8 * 1024 * 1024
SC_CORES = 2
SC_SUBCORES = 16
SC_MAX_GATHER_ROWS = 128


def _mm(a, b):
    return jnp.dot(a.astype(BF16), b.astype(BF16), preferred_element_type=F32)


def _mm_nt(a, b):
    return lax.dot_general(a.astype(BF16), b.astype(BF16), (((1,), (1,)), ((), ())),
                           preferred_element_type=F32)


def _mm_tn(a, b):
    return jnp.dot(a.T.astype(BF16), b.astype(BF16), preferred_element_type=F32)


def _mm_sel(sel, x):
    hi = x.astype(BF16)
    r1 = x - hi.astype(F32)
    mid = r1.astype(BF16)
    lo = (r1 - mid.astype(F32)).astype(BF16)
    return (jnp.dot(sel, hi, preferred_element_type=F32)
            + jnp.dot(sel, mid, preferred_element_type=F32)
            + jnp.dot(sel, lo, preferred_element_type=F32))


def _sigmoid(x):
    return 1.0 / (1.0 + jnp.exp(-x))


def _silu(x):
    return x * _sigmoid(x)


def _softplus(x):
    return jnp.maximum(x, 0.0) + jnp.log(1.0 + jnp.exp(-jnp.abs(x)))


def _rms(x, w):
    return x * lax.rsqrt(jnp.mean(x * x, axis=-1, keepdims=True) + EPS) * w


def _pick_tile(n, candidates):
    for c in candidates:
        if n % c == 0:
            return c
    raise ValueError(f"no tile for {n}")


def _token_parts_specs(parts, tm):
    bounds = np.cumsum([0] + [_part_rows(p) // tm for p in parts])
    specs = []
    for k, p in enumerate(parts):
        assert _part_rows(p) % tm == 0
        lo, hi = int(bounds[k]), int(bounds[k + 1])
        specs.append(_part_block_spec(p, tm, lo, hi - lo))
    return specs, [int(b) for b in bounds]


def _part_rows(p):
    return int(np.prod(p.shape[:-1]))


def _part_block_spec(p, tm, first_tile, n_tiles):
    idx = lambda i: jnp.clip(i - first_tile, 0, n_tiles - 1)
    if len(p.shape) == 2:
        return pl.BlockSpec((tm, p.shape[1]), lambda i: (idx(i), 0))
    grp, _, chunk, d = p.shape
    assert tm % (grp * chunk) == 0
    return pl.BlockSpec((grp, tm // (grp * chunk), chunk, d), lambda i: (0, idx(i), 0, 0))


def _load_part_tile(ref):
    if len(ref.shape) == 2:
        return ref[...]
    grp, n_chunks = ref.shape[:2]
    return jnp.concatenate([ref[s, c] for c in range(n_chunks) for s in range(grp)], axis=0)


def _store_part_tile(ref, x):
    if len(ref.shape) == 2:
        ref[...] = x
        return
    grp, n_chunks, chunk = ref.shape[:3]
    for c in range(n_chunks):
        for s in range(grp):
            r0 = (c * grp + s) * chunk
            ref[s, c] = x[r0:r0 + chunk]


def _read_token_tile(refs, bounds):
    i = pl.program_id(0)
    x = _load_part_tile(refs[-1])
    for k in range(len(refs) - 2, -1, -1):
        x = jnp.where(i < bounds[k + 1], _load_part_tile(refs[k]), x)
    return x


def _inproj_kernel(n_parts, bounds, *refs):
    x_refs = refs[:n_parts]
    nw_ref, w_ref, o_ref, wp_ref = refs[n_parts:]

    @pl.when(pl.program_id(0) == 0)
    def _():
        rows = 128
        ml_gates = slice(C_MLO, C_MLO + 2 * N_HEADS)
        sd_dt = slice(W_IN_COLS - N_HEADS, W_IN_COLS)
        n_gates = 3 * N_HEADS
        for r0 in range(0, D_MODEL, rows):
            w = w_ref[0, r0:r0 + rows, :]
            wp_ref[r0:r0 + rows, 0:C_MLO] = w[:, 0:ml_gates.start].astype(BF16)
            wp_ref[r0:r0 + rows, C_MLO:C_GATE] = w[:, ml_gates.stop:sd_dt.start].astype(BF16)
            gates = jnp.concatenate([w[:, ml_gates], w[:, sd_dt],
                                     jnp.zeros((rows, D_PROJ - C_GATE - n_gates), F32)], axis=1)
            wp_ref[r0:r0 + rows, C_GATE:D_PROJ] = gates.astype(BF16)

    xn = _rms(_read_token_tile(x_refs, bounds), nw_ref[...]).astype(BF16)
    o_ref[...] = jnp.dot(xn, wp_ref[...], preferred_element_type=F32)


def _inproj(x_parts, nw, w_all, layer):
    n = sum(_part_rows(p) for p in x_parts)
    tm = _pick_tile(np.gcd.reduce([_part_rows(p) for p in x_parts]), TOKEN_TILES)
    x_specs, bounds = _token_parts_specs(x_parts, tm)
    assert w_all.shape[1:] == (D_MODEL, W_IN_COLS)
    return pl.pallas_call(
        functools.partial(_inproj_kernel, len(x_parts), bounds),
        grid=(n // tm,),
        in_specs=x_specs + [pl.BlockSpec((1, D_MODEL), lambda i: (0, 0)),
                            pl.BlockSpec((1, D_MODEL, W_IN_COLS), lambda i: (layer, 0, 0),
                                         pipeline_mode=pl.Buffered(1))],
        out_specs=pl.BlockSpec((tm, D_PROJ), lambda i: (i, 0)),
        out_shape=jax.ShapeDtypeStruct((n, D_PROJ), F32),
        scratch_shapes=[pltpu.VMEM((D_MODEL, D_PROJ), BF16)],
        compiler_params=pltpu.CompilerParams(dimension_semantics=("arbitrary",),
                                             vmem_limit_bytes=VMEM_LIMIT),
        name="inproj",
    )(*x_parts, nw, w_all)


def _head_norm(x, width):
    parts = []
    for g in range(x.shape[1] // width):
        xg = x[:, g * width:(g + 1) * width]
        parts.append(xg * lax.rsqrt(jnp.mean(xg * xg, axis=-1, keepdims=True) + EPS))
    return jnp.concatenate(parts, axis=1)


def _mixer_kernel(prompt_steps,
                  sinks_ref,
                  proj_ref, rope_ref, c0_ref, m0_ref, s0_ref, k0_ref, v0_ref, h0_ref, cv0_ref,
                  gbias_ref, alog_ref, mlnw_ref, hglb_ref, hgnw_ref,
                  convw_ref, convb_ref, ssdd_ref, ssdnw_ref,
                  tri_ref, triblk_ref, blkones_ref, ind_ref,
                  mixed_ref, c_ref, m_ref, s_ref, k_ref, v_ref, h_ref, cv_ref,
                  convbuf, bloc_s, kk_s, hv_s):
    i = pl.program_id(0)
    is_prompt = i < prompt_steps
    n_valid = jnp.where(is_prompt, jnp.minimum(i * CHUNK, WINDOW), WINDOW)
    state_refs = (c_ref, m_ref, s_ref, k_ref, v_ref, h_ref, cv_ref)

    @pl.when(i == 0)
    def _():
        for ref in state_refs:
            ref[...] = jnp.zeros_like(ref)

    @pl.when(jnp.logical_not(is_prompt))
    def _():
        for ref, init_ref in zip(state_refs, (c0_ref, m0_ref, s0_ref, k0_ref, v0_ref, h0_ref, cv0_ref)):
            ref[...] = init_ref[...]

    members = [_mixer_chunk(g, n_valid, sinks_ref, proj_ref, rope_ref,
                            gbias_ref, alog_ref, mlnw_ref, hglb_ref, hgnw_ref,
                            convw_ref, convb_ref, ssdd_ref, ssdnw_ref,
                            tri_ref, triblk_ref, blkones_ref, ind_ref,
                            mixed_ref, c_ref, m_ref, s_ref, k_ref, v_ref, h_ref, cv_ref,
                            convbuf, bloc_s, kk_s, hv_s) for g in range(SEQ_GROUP)]
    for phase in zip(*[m[0] for m in members]):
        chains = [c for per_member in zip(*phase) for c in per_member]
        while chains:
            alive = []
            for chain in chains:
                if next(chain, _CHAIN_DONE) is not _CHAIN_DONE:
                    alive.append(chain)
            chains = alive
    for _, finish in members:
        finish()


_CHAIN_DONE = object()


def _mixer_chunk(g, n_valid, sinks_ref, proj_ref, rope_ref,
                 gbias_ref, alog_ref, mlnw_ref, hglb_ref, hgnw_ref,
                 convw_ref, convb_ref, ssdd_ref, ssdnw_ref,
                 tri_ref, triblk_ref, blkones_ref, ind_ref,
                 mixed_ref, c_ref, m_ref, s_ref, k_ref, v_ref, h_ref, cv_ref,
                 convbuf, bloc_s, kk_s, hv_s):
    rs = slice(g * CHUNK, (g + 1) * CHUNK)
    L = CHUNK
    row = lax.broadcasted_iota(jnp.int32, (L, L), 0)
    col = lax.broadcasted_iota(jnp.int32, (L, L), 1)
    causal = row >= col
    tri = tri_ref[...]

    gate = proj_ref[rs,C_GATE:C_GATE + 128] + gbias_ref[...]
    ig = gate
    lf = pltpu.roll(-_softplus(-gate), 124, 1)
    dtv = pltpu.roll(_softplus(gate), 120, 1)

    b = _mm_sel(tri, lf)
    a = ig - b
    cm = a
    rowg = lax.broadcasted_iota(jnp.int32, (L, 128), 0)
    for sh in (1, 2, 4, 8, 16, 32):
        cm = jnp.where(rowg >= sh, jnp.maximum(cm, pltpu.roll(cm, sh, 0)), cm)
    m_prev = m_ref[g, 0:1, :]
    m_t = b + jnp.maximum(m_prev, cm)
    inter = jnp.exp(b + m_prev - m_t)
    bm = b - m_t
    a_t = a.T
    m_last = m_t[L - 1:L, :]
    b_last = b[L - 1:L, :]
    ws = jnp.exp(b_last + a - m_last)
    decay = jnp.exp(b_last + m_prev - m_last)
    emt = jnp.exp(-m_t)
    m_ref[g, 0:1, :] = m_last

    lane2 = lax.broadcasted_iota(jnp.int32, (L, 2 * HEAD), 1)
    low = lane2 < HEAD
    causal2 = lax.broadcasted_iota(jnp.int32, (L, 2 * HEAD), 0) >= (lane2 & (HEAD - 1))
    ones_blk = ind_ref[0:2 * HEAD, 0:2 * HEAD]
    ones_rows = jnp.ones((L, 2 * HEAD), BF16)
    st_row = lax.broadcasted_iota(jnp.int32, (2 * HEAD, 4 * HEAD), 0) // HEAD
    st_lane = (lax.broadcasted_iota(jnp.int32, (2 * HEAD, 4 * HEAD), 1) // HEAD) % 2
    ml_out = [None] * (N_HEADS // 2)

    def pair_cols(x, p):
        return jnp.where(low, x[:, 2 * p:2 * p + 1], x[:, 2 * p + 1:2 * p + 2])

    def split_rows(x):
        return jnp.concatenate([jnp.where(low, x, 0.0), jnp.where(low, 0.0, x)], axis=0)

    def ml_pair(p):
        cs = slice(2 * p * HEAD, 2 * (p + 1) * HEAD)
        q = proj_ref[rs, C_MLQ + cs.start:C_MLQ + cs.stop]
        k = proj_ref[rs, C_MLK + cs.start:C_MLK + cs.stop] * (HEAD ** -0.5)
        v = proj_ref[rs, C_MLV + cs.start:C_MLV + cs.stop]
        qk = _mm_nt(q, split_rows(k))
        yield
        st = c_ref[g, p]
        from_state = _mm(q, st)
        yield
        a_row = jnp.concatenate([a_t[2 * p:2 * p + 1, :], a_t[2 * p + 1:2 * p + 2, :]], axis=1)
        w = jnp.where(causal2, jnp.exp(pair_cols(bm, p) + a_row), 0.0)
        rhs = jnp.concatenate([split_rows(v).astype(BF16), ones_blk], axis=1)
        gate_in = pair_cols(inter, p)
        nd = (jnp.dot((w * qk).astype(BF16), rhs, preferred_element_type=F32)
              + jnp.concatenate([gate_in, gate_in], axis=1) * from_state)
        yield
        kws = k * pair_cols(ws, p)
        upd = jnp.dot(kws.T.astype(BF16), jnp.concatenate([v.astype(BF16), ones_rows], axis=1),
                      preferred_element_type=F32)
        yield
        den = jnp.maximum(jnp.abs(nd[:, 2 * HEAD:]), pair_cols(emt, p))
        ml_out[p] = nd[:, :2 * HEAD] / den
        dec = jnp.where(st_row == 0, decay[:, 2 * p:2 * p + 1], decay[:, 2 * p + 1:2 * p + 2])
        c_ref[g, p] = dec * st + jnp.where(st_row == st_lane, upd, 0.0)

    def ml_finish():
        ml = _head_norm(jnp.concatenate(ml_out, axis=1), HEAD)
        mixed_ref[rs,0:MIX_W] = (_sigmoid(proj_ref[rs,C_MLO:C_MLO + MIX_W]) * ml
                                 * mlnw_ref[...]).astype(mixed_ref.dtype)

    lb = hglb_ref[...]
    fg = lb + (1.0 - lb) * _sigmoid(proj_ref[rs,C_HGF:C_HGF + MIX_W])
    hq = _silu(proj_ref[rs,C_HGQ:C_HGQ + MIX_W]) * (HEAD ** -0.5)
    kk = 1.0 - fg
    lfg = jnp.log(fg)
    hv = proj_ref[rs,C_HGI:C_HGI + MIX_W]
    bloc = _mm_sel(triblk_ref[...], lfg)
    blast = _mm_sel(blkones_ref[...], lfg)
    qe = hq * jnp.exp(bloc)
    kw = kk * jnp.exp(blast - bloc)
    bloc_s[g] = bloc
    kk_s[g] = kk
    hv_s[g] = hv
    ind = ind_ref[...]
    rows = lax.broadcasted_iota(jnp.int32, (SUB, MIX_W), 0)
    n_blk = L // SUB
    hg_state_part = [None] * n_blk
    hg_block_part = [None] * n_blk

    def hg_within_block(blk):
        r0 = blk * SUB
        q_blk = hq[r0:r0 + SUB]
        b_blk = bloc[r0:r0 + SUB]
        xs = []
        for s in range(SUB):
            r = r0 + s
            e = jnp.exp(jnp.minimum(b_blk - bloc_s[g, r:r + 1, :], 0.0))
            xs.append((q_blk * (kk_s[g, r:r + 1, :] * e)).astype(BF16))
            if s % HG_SLICE == HG_SLICE - 1:
                yield
        att = jnp.dot(jnp.concatenate(xs, axis=0), ind, preferred_element_type=F32)
        yield
        o_blk = att[0:SUB] * hv_s[g, r0:r0 + 1, :]
        for s in range(1, SUB):
            r = r0 + s
            o_blk = o_blk + jnp.where(rows >= s, att[s * SUB:(s + 1) * SUB], 0.0) * hv_s[g, r:r + 1, :]
            if s % HG_SLICE == HG_SLICE - 1:
                yield
        hg_block_part[blk] = o_blk

    def hg_state_chain():
        st = s_ref[g]
        for blk in range(n_blk):
            r0 = blk * SUB
            hg_state_part[blk] = jnp.concatenate(
                [_mm_nt(qe[r0:r0 + SUB, h * HEAD:(h + 1) * HEAD], st[:, h * HEAD:(h + 1) * HEAD])
                 for h in range(N_HEADS)], axis=1)
            upd = jnp.concatenate(
                [_mm_tn(hv[r0:r0 + SUB, h * HEAD:(h + 1) * HEAD], kw[r0:r0 + SUB, h * HEAD:(h + 1) * HEAD])
                 for h in range(N_HEADS)], axis=1)
            yield
            st = st * jnp.exp(blast[r0:r0 + 1, :]) + upd
        s_ref[g] = st

    def hg_finish():
        o = jnp.concatenate([hg_state_part[blk] + hg_block_part[blk] for blk in range(n_blk)], axis=0)
        mixed_ref[rs,MIX_W:2 * MIX_W] = (_head_norm(o, HEAD) * hgnw_ref[...]
                                         * _silu(proj_ref[rs,C_HGG:C_HGG + MIX_W])).astype(mixed_ref.dtype)

    @functools.lru_cache(maxsize=None)
    def sw_prep():
        cos = rope_ref[:, 0:128]
        sin_a = rope_ref[:, 128:256]
        sin_b = rope_ref[:, 256:384]

        def rope(x):
            return x * cos + pltpu.roll(x, 120, 1) * sin_a + pltpu.roll(x, 8, 1) * sin_b

        q_rot = [rope(proj_ref[rs,C_SWQ:C_SWQ + 128]), rope(proj_ref[rs,C_SWQ + 128:C_SWQ + 256])]
        k_rot = rope(proj_ref[rs,C_SWK:C_SWK + 128])
        v_cur = proj_ref[rs,C_SWV:C_SWV + 128]
        colw = lax.broadcasted_iota(jnp.int32, (2 * L, WINDOW), 1)
        row2 = lax.broadcasted_iota(jnp.int32, (2 * L, 1), 0)
        return q_rot, k_rot, v_cur, k_ref[g], v_ref[g], colw >= WINDOW - n_valid, row2

    sw_prep()
    sw_out = [None] * (2 * SW_KV)

    def sw_group(kv):
        q_rot, k_rot, v_cur, k_prev, v_prev, prev_ok, row2 = sw_prep()
        gs = slice(kv * HEAD, (kv + 1) * HEAD)
        q2 = jnp.concatenate([q_rot[kv][:, 0:HEAD], q_rot[kv][:, HEAD:2 * HEAD]], axis=0) * (HEAD ** -0.5)
        s_p = jnp.where(prev_ok, _mm_nt(q2, k_prev[:, gs]), NEG)
        yield
        s_c = _mm_nt(q2, k_rot[:, gs])
        yield
        sink = jnp.where(row2 < L, sinks_ref[2 * kv], sinks_ref[2 * kv + 1])
        mx = jnp.maximum(jnp.maximum(jnp.max(s_p, axis=-1, keepdims=True),
                                     jnp.max(s_c, axis=-1, keepdims=True)), sink)
        e_p = jnp.exp(s_p - mx)
        e_c = jnp.exp(s_c - mx)
        den = (jnp.sum(e_p, axis=-1, keepdims=True) + jnp.sum(e_c, axis=-1, keepdims=True)
               + jnp.exp(sink - mx))
        o_p = _mm(e_p, v_prev[:, gs])
        yield
        o = (o_p + _mm(e_c, v_cur[:, gs])) / den
        yield
        sw_out[2 * kv] = o[0:L]
        sw_out[2 * kv + 1] = o[L:2 * L]

    def sw_finish():
        _, k_rot, v_cur, k_prev, v_prev, _, _ = sw_prep()
        mixed_ref[rs,2 * MIX_W:3 * MIX_W] = jnp.concatenate(sw_out, axis=1).astype(mixed_ref.dtype)
        k_ref[g, 0:WINDOW - L, :] = k_prev[L:WINDOW]
        k_ref[g, WINDOW - L:WINDOW, :] = k_rot
        v_ref[g, 0:WINDOW - L, :] = v_prev[L:WINDOW]
        v_ref[g, WINDOW - L:WINDOW, :] = v_cur

    @functools.lru_cache(maxsize=None)
    def sd_prep():
        convbuf[g, 0:8, :] = cv_ref[g]
        convbuf[g, 8:8 + L, :] = proj_ref[rs,C_SDX:C_SDX + SSD_CONV_DIM]
        acc = convb_ref[...] + convbuf[g, 5:5 + L, :] * convw_ref[0:1, :]
        for j in range(1, SSD_CONV):
            acc = acc + convbuf[g, 5 + j:5 + j + L, :] * convw_ref[j:j + 1, :]
        xbc = _silu(acc)
        cv_ref[g] = convbuf[g, L:L + 8, :]
        a_neg = -jnp.exp(alog_ref[...])
        ca = _mm_sel(tri, dtv * a_neg)
        ca_last = ca[L - 1:L, :]
        return (xbc, xbc[:, 0:MIX_W], ca, ca.T, dtv.T, jnp.exp(ca), jnp.exp(ca_last - ca) * dtv,
                jnp.exp(ca_last))

    sd_prep()
    sd_out = [None] * (N_HEADS // 2)

    def sd_group(grp):
        xbc, xs_all, ca, ca_t, dt_t, eca, wss, sdec = sd_prep()
        bm_g = xbc[:, MIX_W + grp * SSD_N:MIX_W + (grp + 1) * SSD_N]
        cm_g = xbc[:, MIX_W + 2 * SSD_N + grp * SSD_N:MIX_W + 2 * SSD_N + (grp + 1) * SSD_N]
        e, o = 2 * grp, 2 * grp + 1
        x = xs_all[:, e * HEAD:(o + 1) * HEAD]
        cb2 = _mm_nt(cm_g, jnp.concatenate([bm_g, bm_g], axis=0))
        yield
        hst = h_ref[g, e:o + 1].reshape(2 * HEAD, SSD_N)
        from_state = _mm_nt(cm_g, hst)
        yield
        ca_row = jnp.concatenate([ca_t[e:e + 1, :], ca_t[o:o + 1, :]], axis=1)
        dt_row = jnp.concatenate([dt_t[e:e + 1, :], dt_t[o:o + 1, :]], axis=1)
        dec = jnp.where(causal2, jnp.exp(pair_cols(ca, grp) - ca_row), 0.0)
        sd_out[grp] = _mm(cb2 * dec * dt_row, split_rows(x)) + pair_cols(eca, grp) * from_state
        yield
        upd = _mm_tn(x * pair_cols(wss, grp), bm_g)
        yield
        keep = jnp.where(lax.broadcasted_iota(jnp.int32, (2 * HEAD, SSD_N), 0) < HEAD,
                         sdec[:, e:e + 1], sdec[:, o:o + 1])
        h_ref[g, e:o + 1] = (keep * hst + upd).reshape(2, HEAD, SSD_N)

    def sd_finish():
        xs_all = sd_prep()[1]
        y = jnp.concatenate(sd_out, axis=1) + ssdd_ref[...] * xs_all
        gated = y * _silu(proj_ref[rs,C_SDZ:C_SDZ + MIX_W])
        mixed_ref[rs,3 * MIX_W:4 * MIX_W] = (_head_norm(gated, 2 * HEAD)
                                             * ssdnw_ref[...]).astype(mixed_ref.dtype)

    chains = ([[hg_within_block(blk) for blk in range(n_blk)]]
              + [[ml_pair(p)] for p in range(N_HEADS // 2)] + [[hg_state_chain(), sd_group(0)]]
              + [[sd_group(1)]]
              + [[sw_group(kv)] for kv in range(SW_KV)])

    def finish():
        ml_finish()
        hg_finish()
        sw_finish()
        sd_finish()

    return chains, finish


def _mixer(proj, rope_tab, init, lp, consts, chunks_per_prompt, n_seq):
    n = proj.shape[0]
    rows = SEQ_GROUP * CHUNK
    n_steps = n // rows

    def group_of(i):
        return jnp.where(i < chunks_per_prompt, 0, i - chunks_per_prompt + 1)

    def rope_blk(i):
        return jnp.minimum(i, chunks_per_prompt)

    def per_seq(shape):
        nd = len(shape)
        return pl.BlockSpec((SEQ_GROUP,) + shape, lambda i, s: (group_of(i),) + (0,) * nd)

    def per_sample_seq(shape):
        nd = len(shape)
        return pl.BlockSpec((SEQ_GROUP,) + shape, lambda i, s: (jnp.maximum(group_of(i) - 1, 0),) + (0,) * nd)

    def const(shape):
        nd = len(shape)
        return pl.BlockSpec(shape, lambda i, s: (0,) * nd)

    state_shapes = [(N_HEADS // 2, 2 * HEAD, 4 * HEAD), (8, 128), (HEAD, MIX_W), (WINDOW, 128), (WINDOW, 128),
                    (N_HEADS, HEAD, SSD_N), (8, SSD_CONV_DIM)]
    in_specs = ([pl.BlockSpec((rows, D_PROJ), lambda i, s: (i, 0)),
                 pl.BlockSpec((CHUNK, 384), lambda i, s: (rope_blk(i), 0))]
                + [per_sample_seq(s) for s in state_shapes]
                + [const((1, 128)), const((1, 128)), const((1, MIX_W)), const((1, MIX_W)), const((1, MIX_W)),
                   const((SSD_CONV, SSD_CONV_DIM)), const((1, SSD_CONV_DIM)), const((1, MIX_W)),
                   const((1, MIX_W)),
                   const((CHUNK, CHUNK)), const((CHUNK, CHUNK)), const((CHUNK, CHUNK)), const((MIX_W, MIX_W))])
    out_specs = ([pl.BlockSpec((rows, D_MODEL), lambda i, s: (i, 0))]
                 + [per_seq(s) for s in state_shapes])
    out_shape = ([jax.ShapeDtypeStruct((n, D_MODEL), BF16)]
                 + [jax.ShapeDtypeStruct((n_seq,) + s, F32) for s in state_shapes])
    grid_spec = pltpu.PrefetchScalarGridSpec(
        num_scalar_prefetch=1, grid=(n_steps,), in_specs=in_specs, out_specs=out_specs,
        scratch_shapes=[pltpu.VMEM((SEQ_GROUP, CHUNK + 8, SSD_CONV_DIM), F32),
                        pltpu.VMEM((SEQ_GROUP, CHUNK, MIX_W), F32), pltpu.VMEM((SEQ_GROUP, CHUNK, MIX_W), F32),
                        pltpu.VMEM((SEQ_GROUP, CHUNK, MIX_W), F32)])
    return pl.pallas_call(
        functools.partial(_mixer_kernel, chunks_per_prompt),
        grid_spec=grid_spec, out_shape=out_shape,
        compiler_params=pltpu.CompilerParams(dimension_semantics=("arbitrary",),
                                             vmem_limit_bytes=VMEM_LIMIT),
        name="mixer",
    )(lp["sinks"], proj, rope_tab, *init,
      lp["gbias"], lp["alog"], lp["mlnw"], lp["hglb"], lp["hgnw"],
      lp["convw"], lp["convb"], lp["ssdd"], lp["ssdnw"], *consts)


def _dense_ffn_kernel(f_chunk, n_parts, bounds, *refs):
    x_refs = refs[:n_parts]
    mix_ref, wo_ref, nw_ref, wg_ref, wu_ref, wd_ref, o_ref = refs[n_parts:]
    x1 = _read_token_tile(x_refs, bounds) + jnp.dot(mix_ref[...], wo_ref[...], preferred_element_type=F32)
    hn = _rms(x1, nw_ref[...]).astype(BF16)
    o_ref[...] = x1
    for f0 in range(0, wg_ref.shape[1], f_chunk):
        g = jnp.dot(hn, wg_ref[:, f0:f0 + f_chunk], preferred_element_type=F32)
        u = jnp.dot(hn, wu_ref[:, f0:f0 + f_chunk], preferred_element_type=F32)
        act = (_silu(g) * u).astype(BF16)
        o_ref[...] += jnp.dot(act, wd_ref[f0:f0 + f_chunk, :], preferred_element_type=F32)


def _dense_ffn(x_parts, mixed, wo, nw, wg, wu, wd):
    n = mixed.shape[0]
    d_ff = wg.shape[1]
    tm = _pick_tile(np.gcd.reduce([_part_rows(p) for p in x_parts]), TOKEN_TILES)
    x_specs, bounds = _token_parts_specs(x_parts, tm)
    full = lambda shape: pl.BlockSpec(shape, lambda i: (0, 0))
    return pl.pallas_call(
        functools.partial(_dense_ffn_kernel, 256, len(x_parts), bounds),
        grid=(n // tm,),
        in_specs=x_specs + [
                  pl.BlockSpec((tm, D_MODEL), lambda i: (i, 0)),
                  full((D_MODEL, D_MODEL)), full((1, D_MODEL)),
                  full((D_MODEL, d_ff)), full((D_MODEL, d_ff)), full((d_ff, D_MODEL))],
        out_specs=pl.BlockSpec((tm, D_MODEL), lambda i: (i, 0)),
        out_shape=jax.ShapeDtypeStruct((n, D_MODEL), F32),
        compiler_params=pltpu.CompilerParams(dimension_semantics=("arbitrary",),
                                             vmem_limit_bytes=VMEM_LIMIT),
        name="dense_ffn",
    )(*x_parts, mixed, wo, nw, wg, wu, wd)


TOK_BLK = 1024
FFN_TILE = 1024


def _pack_bf16_pairs(x):
    w = x.shape[1] // 2
    bits = pltpu.bitcast(x.astype(BF16).astype(F32), jnp.int32)
    return lax.shift_right_logical(bits[:, :w], 16) | bits[:, w:]


def _unpack_bf16_pairs(p):
    lo = pltpu.bitcast(lax.shift_left(p, 16), F32)
    hi = pltpu.bitcast(p & jnp.int32(-65536), F32)
    return lo, hi


def _router_kernel(x_ref, mix_ref, wo_ref, nw_ref, wrt_ref, upper_ref,
                   x1_ref, hn_ref, meta_ref, gate_ref, count_ref, carry):
    @pl.when(pl.program_id(0) == 0)
    def _():
        carry[...] = jnp.zeros_like(carry)

    x1 = x_ref[...] + jnp.dot(mix_ref[...], wo_ref[...], preferred_element_type=F32)
    x1_ref[...] = x1
    hn_f = _rms(x1, nw_ref[...])
    hn = hn_f.astype(BF16)
    hn_ref[...] = _pack_bf16_pairs(hn_f)
    logits = lax.dot_general(wrt_ref[...], hn, (((1,), (1,)), ((), ())),
                             preferred_element_type=F32)
    sub = lax.broadcasted_iota(jnp.int32, logits.shape, 0)
    l1 = jnp.where(sub < N_EXPERTS, logits, NEG)
    m1 = jnp.max(l1, axis=0, keepdims=True)
    i1 = jnp.min(jnp.where(l1 == m1, sub, 99), axis=0, keepdims=True)
    l2 = jnp.where(sub == i1, NEG, l1)
    m2 = jnp.max(l2, axis=0, keepdims=True)
    i2 = jnp.min(jnp.where(l2 == m2, sub, 99), axis=0, keepdims=True)
    e2 = jnp.exp(m2 - m1)
    g1 = 1.0 / (1.0 + e2)
    g2 = e2 / (1.0 + e2)
    sel = jnp.where(sub == i1, 1.0, jnp.where(sub == i2, 1.0, 0.0))
    before = jnp.dot(sel.astype(BF16), upper_ref[...], preferred_element_type=F32) + carry[:, 0:1]
    r1 = jnp.sum(jnp.where(sub == i1, before, 0.0), axis=0, keepdims=True)
    r2 = jnp.sum(jnp.where(sub == i2, before, 0.0), axis=0, keepdims=True)
    meta_ref[...] = jnp.concatenate([i1.astype(F32), i2.astype(F32), r1, r2,
                                     jnp.zeros((4, r1.shape[1]), F32)], axis=0)
    gate_ref[...] = jnp.concatenate([g1, g2, jnp.zeros((126, r1.shape[1]), F32)], axis=0).T
    total = carry[...] + jnp.sum(sel, axis=1, keepdims=True)
    carry[...] = total
    count_ref[...] = total


def _router(x, mixed, wo, nw, wrt, upper):
    n = x.shape[0]
    tm = TOK_BLK
    full = lambda shape: pl.BlockSpec(shape, lambda i: (0,) * len(shape))
    tok = lambda w: pl.BlockSpec((tm, w), lambda i: (i, 0))
    return pl.pallas_call(
        _router_kernel,
        grid=(n // tm,),
        in_specs=[tok(D_MODEL), tok(D_MODEL), full((D_MODEL, D_MODEL)), full((1, D_MODEL)),
                  full((16, D_MODEL)), full((tm, tm))],
        out_specs=[tok(D_MODEL), tok(D_MODEL // 2),
                   pl.BlockSpec((8, tm), lambda i: (0, i)), tok(128), full((16, 128))],
        out_shape=[jax.ShapeDtypeStruct((n, D_MODEL), F32), jax.ShapeDtypeStruct((n, D_MODEL // 2), jnp.int32),
                   jax.ShapeDtypeStruct((8, n), F32), jax.ShapeDtypeStruct((n, 128), F32),
                   jax.ShapeDtypeStruct((16, 128), F32)],
        scratch_shapes=[pltpu.VMEM((16, 128), F32)],
        compiler_params=pltpu.CompilerParams(dimension_semantics=("arbitrary",),
                                             vmem_limit_bytes=VMEM_LIMIT),
        name="router",
    )(x, mixed, wo, nw, wrt, upper)


def _sc_row_gather(table, idx):
    n_workers = SC_CORES * SC_SUBCORES
    b = idx.shape[0]
    d = table.shape[1]
    assert b % (8 * n_workers) == 0
    per_worker = b // n_workers
    chunk = max(c for c in range(8, SC_MAX_GATHER_ROWS + 1, 8) if per_worker % c == 0)
    mesh = plsc.VectorSubcoreMesh(core_axis_name="c", subcore_axis_name="s")

    @functools.partial(
        pl.kernel, mesh=mesh, out_type=jax.ShapeDtypeStruct((b, d), table.dtype),
        scratch_types=[pltpu.VMEM((chunk,), jnp.int32), pltpu.VMEM((chunk, d), table.dtype),
                       pltpu.SemaphoreType.DMA])
    def gather(table_hbm, idx_hbm, out_hbm, idx_v, rows_v, sem):
        worker = lax.axis_index("s") * SC_CORES + lax.axis_index("c")
        base = worker * per_worker

        @pl.loop(0, per_worker // chunk)
        def _(c):
            off = pl.multiple_of(base + c * chunk, 8)
            pltpu.sync_copy(idx_hbm.at[pl.ds(off, chunk)], idx_v)
            pltpu.async_copy(table_hbm.at[idx_v], rows_v, sem).wait()
            pltpu.sync_copy(rows_v, out_hbm.at[pl.ds(off, chunk)])

    return gather(table, idx)


def _sc_row_scatter(rows, idx):
    n_workers = SC_CORES * SC_SUBCORES
    b = n_out = idx.shape[0]
    v, d = rows.shape
    assert b % (8 * n_workers) == 0
    per_worker = b // n_workers
    chunk = max(c for c in range(8, SC_MAX_GATHER_ROWS + 1, 8) if per_worker % c == 0 and v % c == 0)
    mesh = plsc.VectorSubcoreMesh(core_axis_name="c", subcore_axis_name="s")

    @functools.partial(
        pl.kernel, mesh=mesh, out_type=jax.ShapeDtypeStruct((n_out, d), rows.dtype),
        scratch_types=[pltpu.VMEM((chunk,), jnp.int32), pltpu.VMEM((chunk, d), rows.dtype)])
    def scatter(rows_hbm, idx_hbm, out_hbm, idx_v, rows_v):
        worker = lax.axis_index("s") * SC_CORES + lax.axis_index("c")
        base = worker * per_worker

        @pl.loop(0, per_worker // chunk)
        def _(c):
            off = pl.multiple_of(base + c * chunk, 8)
            pltpu.sync_copy(idx_hbm.at[pl.ds(off, chunk)], idx_v)
            pltpu.sync_copy(rows_hbm.at[pl.ds(pl.multiple_of(lax.rem(off, v), 8), chunk)], rows_v)
            pltpu.sync_copy(rows_v, out_hbm.at[idx_v])

    return scatter(rows, idx)


def _expert_ffn_kernel(texp_ref, tvalid_ref, xs_ref, wg_ref, wu_ref, wd_ref, o_ref, acc_ref, xb_ref):
    j = pl.program_id(0)
    f = pl.program_id(1)
    last = f == pl.num_programs(1) - 1
    valid = tvalid_ref[j] == 1
    half = D_MODEL // 2

    @pl.when(jnp.logical_and(valid, f == 0))
    def _():
        lo, hi = _unpack_bf16_pairs(xs_ref[...])
        xb_ref[:, :half] = lo.astype(BF16)
        xb_ref[:, half:] = hi.astype(BF16)

        acc_ref[...] = jnp.zeros_like(acc_ref)

    @pl.when(valid)
    def _():
        xs = xb_ref[...]
        g = jnp.dot(xs, wg_ref[0], preferred_element_type=F32)
        u = jnp.dot(xs, wu_ref[0].astype(BF16), preferred_element_type=F32)
        act = (_silu(g) * u).astype(BF16)
        acc_ref[...] += jnp.dot(act, wd_ref[0].astype(BF16), preferred_element_type=F32)

        @pl.when(last)
        def _():
            o_ref[...] = _pack_bf16_pairs(acc_ref[...])

    @pl.when(jnp.logical_and(jnp.logical_not(valid), last))
    def _():
        o_ref[...] = jnp.zeros_like(o_ref)


def _expert_ffn(f_exp, f_valid, xs, wg, wu, wd):
    n_slots = xs.shape[0]
    d_ff = wg.shape[2]
    tf = _pick_tile(d_ff, (512, 256, 128))
    grid_spec = pltpu.PrefetchScalarGridSpec(
        num_scalar_prefetch=2, grid=(n_slots // FFN_TILE, d_ff // tf),
        in_specs=[pl.BlockSpec((FFN_TILE, D_MODEL // 2), lambda j, f, te, tv: (j, 0)),
                  pl.BlockSpec((1, D_MODEL, tf), lambda j, f, te, tv: (te[j], 0, f)),
                  pl.BlockSpec((1, D_MODEL, tf), lambda j, f, te, tv: (te[j], 0, f)),
                  pl.BlockSpec((1, tf, D_MODEL), lambda j, f, te, tv: (te[j], f, 0))],
        out_specs=pl.BlockSpec((FFN_TILE, D_MODEL // 2), lambda j, f, te, tv: (j, 0)),
        scratch_shapes=[pltpu.VMEM((FFN_TILE, D_MODEL), F32), pltpu.VMEM((FFN_TILE, D_MODEL), BF16)])
    return pl.pallas_call(
        _expert_ffn_kernel, grid_spec=grid_spec,
        out_shape=jax.ShapeDtypeStruct((n_slots, D_MODEL // 2), jnp.int32),
        compiler_params=pltpu.CompilerParams(dimension_semantics=("arbitrary", "arbitrary"),
                                             vmem_limit_bytes=VMEM_LIMIT),
        name="expert_ffn",
    )(f_exp, f_valid, xs, wg, wu, wd)


def _combine_kernel(part_tiles, x1_ref, o1_ref, o2_ref, gate_ref, fnw_ref, *y_refs):
    i = pl.program_id(0)
    lo1, hi1 = _unpack_bf16_pairs(o1_ref[...])
    lo2, hi2 = _unpack_bf16_pairs(o2_ref[...])
    g1 = gate_ref[:, 0:1]
    g2 = gate_ref[:, 1:2]
    half = D_MODEL // 2
    x1 = x1_ref[...]
    fnw = fnw_ref[...]
    ya = x1[:, :half] + g1 * lo1 + g2 * lo2
    yb = x1[:, half:] + g1 * hi1 + g2 * hi2
    scale = lax.rsqrt((jnp.sum(ya * ya, axis=-1, keepdims=True) + jnp.sum(yb * yb, axis=-1, keepdims=True))
                      / D_MODEL + EPS)
    y = jnp.concatenate([ya * scale * fnw[:, :half], yb * scale * fnw[:, half:]], axis=1)
    lo_tile = 0
    for y_ref, tiles in zip(y_refs, part_tiles):
        @pl.when(jnp.logical_and(i >= lo_tile, i < lo_tile + tiles))
        def _(y_ref=y_ref):
            _store_part_tile(y_ref, y)
        lo_tile += tiles


def _combine(x1, o12, gates, fnw, part_shapes):
    n = x1.shape[0]
    parts = [jax.ShapeDtypeStruct(s, F32) for s in part_shapes]
    tm = _pick_tile(np.gcd.reduce([_part_rows(p) for p in parts]), TOKEN_TILES)
    n_tiles = n // tm
    out_specs, bounds = _token_parts_specs(parts, tm)
    part_tiles = [bounds[k + 1] - bounds[k] for k in range(len(parts))]
    return pl.pallas_call(
        functools.partial(_combine_kernel, part_tiles),
        grid=(n_tiles,),
        in_specs=[pl.BlockSpec((tm, D_MODEL), lambda i: (i, 0)),
                  pl.BlockSpec((tm, D_MODEL // 2), lambda i: (i, 0)),
                  pl.BlockSpec((tm, D_MODEL // 2), lambda i: (i + n_tiles, 0)),
                  pl.BlockSpec((tm, 128), lambda i: (i, 0)),
                  pl.BlockSpec((1, D_MODEL), lambda i: (0, 0))],
        out_specs=out_specs,
        out_shape=parts,
        compiler_params=pltpu.CompilerParams(dimension_semantics=("arbitrary",),
                                             vmem_limit_bytes=VMEM_LIMIT),
        name="combine",
    )(x1, o12, o12, gates, fnw)


def _moe(x, mixed, wo, nw, router_w, wg, wu, wd, fnw, part_shapes):
    i32 = jnp.int32
    n = x.shape[0]
    assert n % TOK_BLK == 0
    wrt = jnp.pad(router_w.T, ((0, 16 - N_EXPERTS), (0, 0))).astype(BF16)
    t = np.arange(TOK_BLK)
    upper = jnp.asarray((t[:, None] < t[None, :]).astype(np.float32), dtype=BF16)
    x1, hn_p, meta, gates, count = _router(x, mixed, wo, nw, wrt, upper)

    n_slots = (-(-2 * n // FFN_TILE) + N_EXPERTS) * FFN_TILE
    counts = count[:N_EXPERTS, 0].astype(i32)
    gsize = (counts + FFN_TILE - 1) // FFN_TILE * FFN_TILE
    gend = jnp.cumsum(gsize)
    goff = gend - gsize
    fstart = jnp.arange(n_slots // FFN_TILE, dtype=i32) * FFN_TILE
    f_valid = (fstart < gend[-1]).astype(i32)
    f_exp = jnp.sum((fstart[:, None] >= gend[None, :]).astype(i32), axis=1)
    f_exp = jnp.minimum(f_exp, jnp.sum((gend[-1] - 1 >= gend).astype(i32)))
    expert_ids = jnp.arange(N_EXPERTS, dtype=i32)[:, None]
    top = meta[0:2].astype(i32)
    rank = meta[2:4].astype(i32)
    slot = jnp.stack([jnp.sum(jnp.where(top[k][None, :] == expert_ids, goff[:, None], 0), axis=0) + rank[k]
                      for k in range(2)]).reshape(-1)
    pad = gsize - counts
    cpad = jnp.cumsum(pad)
    j = jnp.arange(n_slots - 2 * n, dtype=i32)
    pad_e = jnp.sum((j[:, None] >= cpad[None, :]).astype(i32), axis=1)
    pad_base = jnp.sum(jnp.where(pad_e[:, None] == expert_ids.T, (goff + counts - (cpad - pad))[None, :], 0), axis=1)
    pad_slot = jnp.where(pad_e < N_EXPERTS, pad_base + j, gend[-1] + j - cpad[-1])

    hn_p, wg = lax.optimization_barrier((hn_p, wg))
    xs_p = _sc_row_scatter(hn_p, jnp.concatenate([slot, pad_slot]))
    out_p = _expert_ffn(f_exp, f_valid, xs_p, wg.astype(BF16), wu, wd)
    o12 = _sc_row_gather(out_p, slot)
    return _combine(x1, o12, gates, fnw, part_shapes)


def _rope_table(seq_len, dec_len):
    half = ROPE_DIMS // 2
    pos = jnp.concatenate([jnp.arange(seq_len), PAST_LEN + jnp.arange(dec_len)]).astype(F32)
    inv_freq = jnp.power(jnp.float32(ROPE_THETA), -jnp.arange(half, dtype=F32) / half)
    lane = np.arange(128) % HEAD
    inv_lane = jnp.where(lane < ROPE_DIMS, inv_freq[lane % half], 0.0)
    ang = pos[:, None] * inv_lane[None, :]
    cos, sin = jnp.cos(ang), jnp.sin(ang)
    sin_a = jnp.where(lane < half, -sin, 0.0)
    sin_b = jnp.where((lane >= half) & (lane < ROPE_DIMS), sin, 0.0)
    return jnp.concatenate([cos, sin_a, sin_b], axis=1)


def _mixer_consts():
    t = np.arange(CHUNK)
    tri = (t[:, None] >= t[None, :])
    same = (t[:, None] // SUB) == (t[None, :] // SUB)
    c = np.arange(MIX_W)
    ind = (c[:, None] // HEAD) == (c[None, :] // HEAD)
    as_bf = lambda m: jnp.asarray(m.astype(np.float32), dtype=BF16)
    return [as_bf(tri), as_bf(tri & same), as_bf(same), as_bf(ind)]


def _mlstm_state_to_pairs(c, n):
    c = c.astype(F32)
    n_b = jnp.broadcast_to(n.astype(F32)[..., None], c.shape)
    z = jnp.zeros_like(c[:, 0::2])
    top = jnp.concatenate([c[:, 0::2], z, n_b[:, 0::2], z], axis=-1)
    bottom = jnp.concatenate([z, c[:, 1::2], z, n_b[:, 1::2]], axis=-1)
    return jnp.concatenate([top, bottom], axis=-2)


def _mlstm_state_from_pairs(s):
    b = s.shape[0]
    c = jnp.stack([s[:, :, a * HEAD:(a + 1) * HEAD, a * HEAD:(a + 1) * HEAD] for a in range(2)], axis=2)
    n = jnp.stack([s[:, :, a * HEAD:(a + 1) * HEAD, (2 + a) * HEAD] for a in range(2)], axis=2)
    return (c.reshape(b, N_HEADS, HEAD, HEAD), n.reshape(b, N_HEADS, HEAD))


def _pad_lanes(v, width):
    return jnp.pad(v, ((0, 0), (0, width - v.shape[1])))


def kernel(x_prompt, x_sample, state_mlstm_C, state_mlstm_n, state_mlstm_m, state_hgrn_S, cache_swa_k, cache_swa_v, state_ssd_h, state_ssd_conv, norm1_w, w_in, ml_ig_b, ml_fg_b, ml_norm_w, hg_lb_logits, hg_norm_w, sw_sinks, ssd_conv_w, ssd_conv_b, ssd_dt_bias, ssd_A_log, ssd_D, ssd_norm_w, w_out, norm2_w, ffn_w_gate, ffn_w_up, ffn_w_down, moe_router, moe_w_gate, moe_w_up, moe_w_down, final_norm_w):
    depth = w_in.shape[0]
    bp, seq_len, _ = x_prompt.shape
    bs, dec_len, _ = x_sample.shape
    assert seq_len % CHUNK == 0 and dec_len == CHUNK and depth % 2 == 0
    n_seq = bp + bs
    chunks_per_prompt = seq_len // CHUNK

    assert bp == SEQ_GROUP and bs % SEQ_GROUP == 0
    x_parts = (x_prompt.reshape(bp, chunks_per_prompt, CHUNK, D_MODEL), x_sample.reshape(bs * dec_len, D_MODEL))
    part_shapes = tuple(p.shape for p in x_parts)
    rope_tab = _rope_table(seq_len, dec_len)
    consts = _mixer_consts()

    sm = jax.nn.softmax(hg_lb_logits.astype(F32), axis=0)
    hg_lb = jnp.cumsum(sm, axis=0) - sm[0]

    state_outs = []
    for l in range(depth):
        lp = {
            "sinks": sw_sinks[l].astype(F32),
            "gbias": _pad_lanes(jnp.concatenate([ml_ig_b[l], ml_fg_b[l], ssd_dt_bias[l]])[None, :].astype(F32), 128),
            "alog": _pad_lanes(ssd_A_log[l][None, :].astype(F32), 128),
            "mlnw": ml_norm_w[l][None, :].astype(F32),
            "hglb": hg_lb[l][None, :],
            "hgnw": hg_norm_w[l][None, :].astype(F32),
            "convw": ssd_conv_w[l].astype(F32),
            "convb": ssd_conv_b[l][None, :].astype(F32),
            "ssdd": jnp.repeat(ssd_D[l].astype(F32), HEAD)[None, :],
            "ssdnw": ssd_norm_w[l][None, :].astype(F32),
        }
        c_aug = _mlstm_state_to_pairs(state_mlstm_C[l], state_mlstm_n[l])
        m_pad = jnp.pad(state_mlstm_m[l][:, None, :], ((0, 0), (0, 7), (0, 128 - N_HEADS)))
        s_t = jnp.swapaxes(state_hgrn_S[l], -1, -2)
        s_t = jnp.moveaxis(s_t, 1, 2).reshape(bs, HEAD, MIX_W)
        cv_pad = jnp.pad(state_ssd_conv[l], ((0, 0), (8 - (SSD_CONV - 1), 0), (0, 0)))
        init = [a.astype(F32) for a in (
            c_aug, m_pad, s_t, cache_swa_k[l].reshape(bs, WINDOW, 128), cache_swa_v[l].reshape(bs, WINDOW, 128),
            state_ssd_h[l], cv_pad)]

        proj = _inproj(x_parts, norm1_w[l][None, :], w_in, l)
        mixed, c_o, m_o, s_o, k_o, v_o, h_o, cv_o = _mixer(
            proj, rope_tab, init, lp, consts, chunks_per_prompt, n_seq)
        wo = w_out[l].astype(BF16)
        j = l // 2
        if l % 2 == 0:
            x_parts = (_dense_ffn(x_parts, mixed, wo, norm2_w[l][None, :], ffn_w_gate[j].astype(BF16),
                                  ffn_w_up[j].astype(BF16), ffn_w_down[j].astype(BF16)),)
        else:
            assert l == depth - 1 and len(x_parts) == 1
            y_parts = _moe(x_parts[0], mixed, wo, norm2_w[l][None, :], moe_router[j], moe_w_gate[j],
                           moe_w_up[j], moe_w_down[j], final_norm_w[None, :], part_shapes)
        s_back = jnp.moveaxis(s_o.reshape(n_seq, HEAD, N_HEADS, HEAD), 2, 1)
        state_outs.append(_mlstm_state_from_pairs(c_o) + (m_o[:, 0, :N_HEADS],
                           jnp.swapaxes(s_back, -1, -2),
                           k_o.reshape(n_seq, WINDOW, SW_KV, HEAD), v_o.reshape(n_seq, WINDOW, SW_KV, HEAD),
                           h_o, cv_o[:, 8 - (SSD_CONV - 1):, :]))

    y_prompt = y_parts[0].reshape(bp, seq_len, D_MODEL)
    y_sample = y_parts[1].reshape(bs, dec_len, D_MODEL)
    stacked = [jnp.stack([so[k] for so in state_outs]) for k in range(8)]
    return (y_prompt, y_sample) + tuple(s[:, :bp] for s in stacked) + tuple(s[:, bp:] for s in stacked)
```

```python
import functools

import numpy as np
import jax
import jax.numpy as jnp
from jax import lax
from jax.experimental import pallas as pl
from jax.experimental.pallas import tpu as pltpu
from jax.experimental.pallas import tpu_sc as plsc

F32 = jnp.float32
BF16 = jnp.bfloat16

D_MODEL = 1024
CHUNK = 64
EPS = 1e-6
N_HEADS = 4
HEAD = 64
MIX_W = N_HEADS * HEAD
SW_KV = 2
WINDOW = 128
ROPE_DIMS = 16
ROPE_THETA = 500000.0
SSD_N = 128
SSD_CONV = 4
SSD_CONV_DIM = 768
PAST_LEN = 4096
N_EXPERTS = 8
SUB = 16
SEQ_GROUP = 2
HG_SLICE = 4
TOKEN_TILES = (512, 256, 128)

C_MLQ, C_MLK, C_MLV, C_MLO = 0, 256, 512, 768
C_HGQ, C_HGF, C_HGI, C_HGG = 1024, 1280, 1536, 1792
C_SWQ, C_SWK, C_SWV = 2048, 2304, 2432
C_SDZ, C_SDX, C_GATE = 2560, 2816, 3584
D_PROJ = 3712
W_IN_COLS = C_GATE + 3 * N_HEADS
NEG = -1e30
V7X_VMEM_BYTES = 64 * 1024 * 1024
VMEM_LIMIT = V7X_VMEM_BYTES - 8 * 1024 * 1024
SC_CORES = 2
SC_SUBCORES = 16
SC_MAX_GATHER_ROWS = 128


def _mm(a, b):
    return jnp.dot(a.astype(BF16), b.astype(BF16), preferred_element_type=F32)


def _mm_nt(a, b):
    return lax.dot_general(a.astype(BF16), b.astype(BF16), (((1,), (1,)), ((), ())),
                           preferred_element_type=F32)


def _mm_tn(a, b):
    return jnp.dot(a.T.astype(BF16), b.astype(BF16), preferred_element_type=F32)


def _mm_sel(sel, x):
    hi = x.astype(BF16)
    r1 = x - hi.astype(F32)
    mid = r1.astype(BF16)
    lo = (r1 - mid.astype(F32)).astype(BF16)
    return (jnp.dot(sel, hi, preferred_element_type=F32)
            + jnp.dot(sel, mid, preferred_element_type=F32)
            + jnp.dot(sel, lo, preferred_element_type=F32))


def _sigmoid(x):
    return 1.0 / (1.0 + jnp.exp(-x))


def _silu(x):
    return x * _sigmoid(x)


def _softplus(x):
    return jnp.maximum(x, 0.0) + jnp.log(1.0 + jnp.exp(-jnp.abs(x)))


def _rms(x, w):
    return x * lax.rsqrt(jnp.mean(x * x, axis=-1, keepdims=True) + EPS) * w


def _pick_tile(n, candidates):
    for c in candidates:
        if n % c == 0:
            return c
    raise ValueError(f"no tile for {n}")


def _token_parts_specs(parts, tm):
    bounds = np.cumsum([0] + [_part_rows(p) // tm for p in parts])
    specs = []
    for k, p in enumerate(parts):
        assert _part_rows(p) % tm == 0
        lo, hi = int(bounds[k]), int(bounds[k + 1])
        specs.append(_part_block_spec(p, tm, lo, hi - lo))
    return specs, [int(b) for b in bounds]


def _part_rows(p):
    return int(np.prod(p.shape[:-1]))


def _part_block_spec(p, tm, first_tile, n_tiles):
    idx = lambda i: jnp.clip(i - first_tile, 0, n_tiles - 1)
    if len(p.shape) == 2:
        return pl.BlockSpec((tm, p.shape[1]), lambda i: (idx(i), 0))
    grp, _, chunk, d = p.shape
    assert tm % (grp * chunk) == 0
    return pl.BlockSpec((grp, tm // (grp * chunk), chunk, d), lambda i: (0, idx(i), 0, 0))


def _load_part_tile(ref):
    if len(ref.shape) == 2:
        return ref[...]
    grp, n_chunks = ref.shape[:2]
    return jnp.concatenate([ref[s, c] for c in range(n_chunks) for s in range(grp)], axis=0)


def _store_part_tile(ref, x):
    if len(ref.shape) == 2:
        ref[...] = x
        return
    grp, n_chunks, chunk = ref.shape[:3]
    for c in range(n_chunks):
        for s in range(grp):
            r0 = (c * grp + s) * chunk
            ref[s, c] = x[r0:r0 + chunk]


def _read_token_tile(refs, bounds):
    i = pl.program_id(0)
    x = _load_part_tile(refs[-1])
    for k in range(len(refs) - 2, -1, -1):
        x = jnp.where(i < bounds[k + 1], _load_part_tile(refs[k]), x)
    return x


def _inproj_kernel(n_parts, bounds, *refs):
    x_refs = refs[:n_parts]
    nw_ref, w_ref, o_ref, wp_ref = refs[n_parts:]

    @pl.when(pl.program_id(0) == 0)
    def _():
        rows = 128
        ml_gates = slice(C_MLO, C_MLO + 2 * N_HEADS)
        sd_dt = slice(W_IN_COLS - N_HEADS, W_IN_COLS)
        n_gates = 3 * N_HEADS
        for r0 in range(0, D_MODEL, rows):
            w = w_ref[0, r0:r0 + rows, :]
            wp_ref[r0:r0 + rows, 0:C_MLO] = w[:, 0:ml_gates.start].astype(BF16)
            wp_ref[r0:r0 + rows, C_MLO:C_GATE] = w[:, ml_gates.stop:sd_dt.start].astype(BF16)
            gates = jnp.concatenate([w[:, ml_gates], w[:, sd_dt],
                                     jnp.zeros((rows, D_PROJ - C_GATE - n_gates), F32)], axis=1)
            wp_ref[r0:r0 + rows, C_GATE:D_PROJ] = gates.astype(BF16)

    xn = _rms(_read_token_tile(x_refs, bounds), nw_ref[...]).astype(BF16)
    o_ref[...] = jnp.dot(xn, wp_ref[...], preferred_element_type=F32)


def _inproj(x_parts, nw, w_all, layer):
    n = sum(_part_rows(p) for p in x_parts)
    tm = _pick_tile(np.gcd.reduce([_part_rows(p) for p in x_parts]), TOKEN_TILES)
    x_specs, bounds = _token_parts_specs(x_parts, tm)
    assert w_all.shape[1:] == (D_MODEL, W_IN_COLS)
    return pl.pallas_call(
        functools.partial(_inproj_kernel, len(x_parts), bounds),
        grid=(n // tm,),
        in_specs=x_specs + [pl.BlockSpec((1, D_MODEL), lambda i: (0, 0)),
                            pl.BlockSpec((1, D_MODEL, W_IN_COLS), lambda i: (layer, 0, 0),
                                         pipeline_mode=pl.Buffered(1))],
        out_specs=pl.BlockSpec((tm, D_PROJ), lambda i: (i, 0)),
        out_shape=jax.ShapeDtypeStruct((n, D_PROJ), F32),
        scratch_shapes=[pltpu.VMEM((D_MODEL, D_PROJ), BF16)],
        compiler_params=pltpu.CompilerParams(dimension_semantics=("arbitrary",),
                                             vmem_limit_bytes=VMEM_LIMIT),
        name="inproj",
    )(*x_parts, nw, w_all)


def _head_norm(x, width):
    parts = []
    for g in range(x.shape[1] // width):
        xg = x[:, g * width:(g + 1) * width]
        parts.append(xg * lax.rsqrt(jnp.mean(xg * xg, axis=-1, keepdims=True) + EPS))
    return jnp.concatenate(parts, axis=1)


def _mixer_kernel(prompt_steps,
                  sinks_ref,
                  proj_ref, rope_ref, c0_ref, m0_ref, s0_ref, k0_ref, v0_ref, h0_ref, cv0_ref,
                  gbias_ref, alog_ref, mlnw_ref, hglb_ref, hgnw_ref,
                  convw_ref, convb_ref, ssdd_ref, ssdnw_ref,
                  tri_ref, triblk_ref, blkones_ref, ind_ref,
                  mixed_ref, c_ref, m_ref, s_ref, k_ref, v_ref, h_ref, cv_ref,
                  convbuf, bloc_s, kk_s, hv_s):
    i = pl.program_id(0)
    is_prompt = i < prompt_steps
    n_valid = jnp.where(is_prompt, jnp.minimum(i * CHUNK, WINDOW), WINDOW)
    state_refs = (c_ref, m_ref, s_ref, k_ref, v_ref, h_ref, cv_ref)

    @pl.when(i == 0)
    def _():
        for ref in state_refs:
            ref[...] = jnp.zeros_like(ref)

    @pl.when(jnp.logical_not(is_prompt))
    def _():
        for ref, init_ref in zip(state_refs, (c0_ref, m0_ref, s0_ref, k0_ref, v0_ref, h0_ref, cv0_ref)):
            ref[...] = init_ref[...]

    members = [_mixer_chunk(g, n_valid, sinks_ref, proj_ref, rope_ref,
                            gbias_ref, alog_ref, mlnw_ref, hglb_ref, hgnw_ref,
                            convw_ref, convb_ref, ssdd_ref, ssdnw_ref,
                            tri_ref, triblk_ref, blkones_ref, ind_ref,
                            mixed_ref, c_ref, m_ref, s_ref, k_ref, v_ref, h_ref, cv_ref,
                            convbuf, bloc_s, kk_s, hv_s) for g in range(SEQ_GROUP)]
    for phase in zip(*[m[0] for m in members]):
        chains = [c for per_member in zip(*phase) for c in per_member]
        while chains:
            alive = []
            for chain in chains:
                if next(chain, _CHAIN_DONE) is not _CHAIN_DONE:
                    alive.append(chain)
            chains = alive
    for _, finish in members:
        finish()


_CHAIN_DONE = object()


def _mixer_chunk(g, n_valid, sinks_ref, proj_ref, rope_ref,
                 gbias_ref, alog_ref, mlnw_ref, hglb_ref, hgnw_ref,
                 convw_ref, convb_ref, ssdd_ref, ssdnw_ref,
                 tri_ref, triblk_ref, blkones_ref, ind_ref,
                 mixed_ref, c_ref, m_ref, s_ref, k_ref, v_ref, h_ref, cv_ref,
                 convbuf, bloc_s, kk_s, hv_s):
    rs = slice(g * CHUNK, (g + 1) * CHUNK)
    L = CHUNK
    row = lax.broadcasted_iota(jnp.int32, (L, L), 0)
    col = lax.broadcasted_iota(jnp.int32, (L, L), 1)
    causal = row >= col
    tri = tri_ref[...]

    gate = proj_ref[rs,C_GATE:C_GATE + 128] + gbias_ref[...]
    ig = gate
    lf = pltpu.roll(-_softplus(-gate), 124, 1)
    dtv = pltpu.roll(_softplus(gate), 120, 1)

    b = _mm_sel(tri, lf)
    a = ig - b
    cm = a
    rowg = lax.broadcasted_iota(jnp.int32, (L, 128), 0)
    for sh in (1, 2, 4, 8, 16, 32):
        cm = jnp.where(rowg >= sh, jnp.maximum(cm, pltpu.roll(cm, sh, 0)), cm)
    m_prev = m_ref[g, 0:1, :]
    m_t = b + jnp.maximum(m_prev, cm)
    inter = jnp.exp(b + m_prev - m_t)
    bm = b - m_t
    a_t = a.T
    m_last = m_t[L - 1:L, :]
    b_last = b[L - 1:L, :]
    ws = jnp.exp(b_last + a - m_last)
    decay = jnp.exp(b_last + m_prev - m_last)
    emt = jnp.exp(-m_t)
    m_ref[g, 0:1, :] = m_last

    lane2 = lax.broadcasted_iota(jnp.int32, (L, 2 * HEAD), 1)
    low = lane2 < HEAD
    causal2 = lax.broadcasted_iota(jnp.int32, (L, 2 * HEAD), 0) >= (lane2 & (HEAD - 1))
    ones_blk = ind_ref[0:2 * HEAD, 0:2 * HEAD]
    ones_rows = jnp.ones((L, 2 * HEAD), BF16)
    st_row = lax.broadcasted_iota(jnp.int32, (2 * HEAD, 4 * HEAD), 0) // HEAD
    st_lane = (lax.broadcasted_iota(jnp.int32, (2 * HEAD, 4 * HEAD), 1) // HEAD) % 2
    ml_out = [None] * (N_HEADS // 2)

    def pair_cols(x, p):
        return jnp.where(low, x[:, 2 * p:2 * p + 1], x[:, 2 * p + 1:2 * p + 2])

    def split_rows(x):
        return jnp.concatenate([jnp.where(low, x, 0.0), jnp.where(low, 0.0, x)], axis=0)

    def ml_pair(p):
        cs = slice(2 * p * HEAD, 2 * (p + 1) * HEAD)
        q = proj_ref[rs, C_MLQ + cs.start:C_MLQ + cs.stop]
        k = proj_ref[rs, C_MLK + cs.start:C_MLK + cs.stop] * (HEAD ** -0.5)
        v = proj_ref[rs, C_MLV + cs.start:C_MLV + cs.stop]
        qk = _mm_nt(q, split_rows(k))
        yield
        st = c_ref[g, p]
        from_state = _mm(q, st)
        yield
        a_row = jnp.concatenate([a_t[2 * p:2 * p + 1, :], a_t[2 * p + 1:2 * p + 2, :]], axis=1)
        w = jnp.where(causal2, jnp.exp(pair_cols(bm, p) + a_row), 0.0)
        rhs = jnp.concatenate([split_rows(v).astype(BF16), ones_blk], axis=1)
        gate_in = pair_cols(inter, p)
        nd = (jnp.dot((w * qk).astype(BF16), rhs, preferred_element_type=F32)
              + jnp.concatenate([gate_in, gate_in], axis=1) * from_state)
        yield
        kws = k * pair_cols(ws, p)
        upd = jnp.dot(kws.T.astype(BF16), jnp.concatenate([v.astype(BF16), ones_rows], axis=1),
                      preferred_element_type=F32)
        yield
        den = jnp.maximum(jnp.abs(nd[:, 2 * HEAD:]), pair_cols(emt, p))
        ml_out[p] = nd[:, :2 * HEAD] / den
        dec = jnp.where(st_row == 0, decay[:, 2 * p:2 * p + 1], decay[:, 2 * p + 1:2 * p + 2])
        c_ref[g, p] = dec * st + jnp.where(st_row == st_lane, upd, 0.0)

    def ml_finish():
        ml = _head_norm(jnp.concatenate(ml_out, axis=1), HEAD)
        mixed_ref[rs,0:MIX_W] = (_sigmoid(proj_ref[rs,C_MLO:C_MLO + MIX_W]) * ml
                                 * mlnw_ref[...]).astype(mixed_ref.dtype)

    lb = hglb_ref[...]
    fg = lb + (1.0 - lb) * _sigmoid(proj_ref[rs,C_HGF:C_HGF + MIX_W])
    hq = _silu(proj_ref[rs,C_HGQ:C_HGQ + MIX_W]) * (HEAD ** -0.5)
    kk = 1.0 - fg
    lfg = jnp.log(fg)
    hv = proj_ref[rs,C_HGI:C_HGI + MIX_W]
    bloc = _mm_sel(triblk_ref[...], lfg)
    blast = _mm_sel(blkones_ref[...], lfg)
    qe = hq * jnp.exp(bloc)
    kw = kk * jnp.exp(blast - bloc)
    bloc_s[g] = bloc
    kk_s[g] = kk
    hv_s[g] = hv
    ind = ind_ref[...]
    rows = lax.broadcasted_iota(jnp.int32, (SUB, MIX_W), 0)
    n_blk = L // SUB
    hg_state_part = [None] * n_blk
    hg_block_part = [None] * n_blk

    def hg_within_block(blk):
        r0 = blk * SUB
        q_blk = hq[r0:r0 + SUB]
        b_blk = bloc[r0:r0 + SUB]
        xs = []
        for s in range(SUB):
            r = r0 + s
            e = jnp.exp(jnp.minimum(b_blk - bloc_s[g, r:r + 1, :], 0.0))
            xs.append((q_blk * (kk_s[g, r:r + 1, :] * e)).astype(BF16))
            if s % HG_SLICE == HG_SLICE - 1:
                yield
        att = jnp.dot(jnp.concatenate(xs, axis=0), ind, preferred_element_type=F32)
        yield
        o_blk = att[0:SUB] * hv_s[g, r0:r0 + 1, :]
        for s in range(1, SUB):
            r = r0 + s
            o_blk = o_blk + jnp.where(rows >= s, att[s * SUB:(s + 1) * SUB], 0.0) * hv_s[g, r:r + 1, :]
            if s % HG_SLICE == HG_SLICE - 1:
                yield
        hg_block_part[blk] = o_blk

    def hg_state_chain():
        st = s_ref[g]
        for blk in range(n_blk):
            r0 = blk * SUB
            hg_state_part[blk] = jnp.concatenate(
                [_mm_nt(qe[r0:r0 + SUB, h * HEAD:(h + 1) * HEAD], st[:, h * HEAD:(h + 1) * HEAD])
                 for h in range(N_HEADS)], axis=1)
            upd = jnp.concatenate(
                [_mm_tn(hv[r0:r0 + SUB, h * HEAD:(h + 1) * HEAD], kw[r0:r0 + SUB, h * HEAD:(h + 1) * HEAD])
                 for h in range(N_HEADS)], axis=1)
            yield
            st = st * jnp.exp(blast[r0:r0 + 1, :]) + upd
        s_ref[g] = st

    def hg_finish():
        o = jnp.concatenate([hg_state_part[blk] + hg_block_part[blk] for blk in range(n_blk)], axis=0)
        mixed_ref[rs,MIX_W:2 * MIX_W] = (_head_norm(o, HEAD) * hgnw_ref[...]
                                         * _silu(proj_ref[rs,C_HGG:C_HGG + MIX_W])).astype(mixed_ref.dtype)

    @functools.lru_cache(maxsize=None)
    def sw_prep():
        cos = rope_ref[:, 0:128]
        sin_a = rope_ref[:, 128:256]
        sin_b = rope_ref[:, 256:384]

        def rope(x):
            return x * cos + pltpu.roll(x, 120, 1) * sin_a + pltpu.roll(x, 8, 1) * sin_b

        q_rot = [rope(proj_ref[rs,C_SWQ:C_SWQ + 128]), rope(proj_ref[rs,C_SWQ + 128:C_SWQ + 256])]
        k_rot = rope(proj_ref[rs,C_SWK:C_SWK + 128])
        v_cur = proj_ref[rs,C_SWV:C_SWV + 128]
        colw = lax.broadcasted_iota(jnp.int32, (2 * L, WINDOW), 1)
        row2 = lax.broadcasted_iota(jnp.int32, (2 * L, 1), 0)
        return q_rot, k_rot, v_cur, k_ref[g], v_ref[g], colw >= WINDOW - n_valid, row2

    sw_prep()
    sw_out = [None] * (2 * SW_KV)

    def sw_group(kv):
        q_rot, k_rot, v_cur, k_prev, v_prev, prev_ok, row2 = sw_prep()
        gs = slice(kv * HEAD, (kv + 1) * HEAD)
        q2 = jnp.concatenate([q_rot[kv][:, 0:HEAD], q_rot[kv][:, HEAD:2 * HEAD]], axis=0) * (HEAD ** -0.5)
        s_p = jnp.where(prev_ok, _mm_nt(q2, k_prev[:, gs]), NEG)
        yield
        s_c = _mm_nt(q2, k_rot[:, gs])
        yield
        sink = jnp.where(row2 < L, sinks_ref[2 * kv], sinks_ref[2 * kv + 1])
        mx = jnp.maximum(jnp.maximum(jnp.max(s_p, axis=-1, keepdims=True),
                                     jnp.max(s_c, axis=-1, keepdims=True)), sink)
        e_p = jnp.exp(s_p - mx)
        e_c = jnp.exp(s_c - mx)
        den = (jnp.sum(e_p, axis=-1, keepdims=True) + jnp.sum(e_c, axis=-1, keepdims=True)
               + jnp.exp(sink - mx))
        o_p = _mm(e_p, v_prev[:, gs])
        yield
        o = (o_p + _mm(e_c, v_cur[:, gs])) / den
        yield
        sw_out[2 * kv] = o[0:L]
        sw_out[2 * kv + 1] = o[L:2 * L]

    def sw_finish():
        _, k_rot, v_cur, k_prev, v_prev, _, _ = sw_prep()
        mixed_ref[rs,2 * MIX_W:3 * MIX_W] = jnp.concatenate(sw_out, axis=1).astype(mixed_ref.dtype)
        k_ref[g, 0:WINDOW - L, :] = k_prev[L:WINDOW]
        k_ref[g, WINDOW - L:WINDOW, :] = k_rot
        v_ref[g, 0:WINDOW - L, :] = v_prev[L:WINDOW]
        v_ref[g, WINDOW - L:WINDOW, :] = v_cur

    @functools.lru_cache(maxsize=None)
    def sd_prep():
        convbuf[g, 0:8, :] = cv_ref[g]
        convbuf[g, 8:8 + L, :] = proj_ref[rs,C_SDX:C_SDX + SSD_CONV_DIM]
        acc = convb_ref[...] + convbuf[g, 5:5 + L, :] * convw_ref[0:1, :]
        for j in range(1, SSD_CONV):
            acc = acc + convbuf[g, 5 + j:5 + j + L, :] * convw_ref[j:j + 1, :]
        xbc = _silu(acc)
        cv_ref[g] = convbuf[g, L:L + 8, :]
        a_neg = -jnp.exp(alog_ref[...])
        ca = _mm_sel(tri, dtv * a_neg)
        ca_last = ca[L - 1:L, :]
        return (xbc, xbc[:, 0:MIX_W], ca, ca.T, dtv.T, jnp.exp(ca), jnp.exp(ca_last - ca) * dtv,
                jnp.exp(ca_last))

    sd_prep()
    sd_out = [None] * (N_HEADS // 2)

    def sd_group(grp):
        xbc, xs_all, ca, ca_t, dt_t, eca, wss, sdec = sd_prep()
        bm_g = xbc[:, MIX_W + grp * SSD_N:MIX_W + (grp + 1) * SSD_N]
        cm_g = xbc[:, MIX_W + 2 * SSD_N + grp * SSD_N:MIX_W + 2 * SSD_N + (grp + 1) * SSD_N]
        e, o = 2 * grp, 2 * grp + 1
        x = xs_all[:, e * HEAD:(o + 1) * HEAD]
        cb2 = _mm_nt(cm_g, jnp.concatenate([bm_g, bm_g], axis=0))
        yield
        hst = h_ref[g, e:o + 1].reshape(2 * HEAD, SSD_N)
        from_state = _mm_nt(cm_g, hst)
        yield
        ca_row = jnp.concatenate([ca_t[e:e + 1, :], ca_t[o:o + 1, :]], axis=1)
        dt_row = jnp.concatenate([dt_t[e:e + 1, :], dt_t[o:o + 1, :]], axis=1)
        dec = jnp.where(causal2, jnp.exp(pair_cols(ca, grp) - ca_row), 0.0)
        sd_out[grp] = _mm(cb2 * dec * dt_row, split_rows(x)) + pair_cols(eca, grp) * from_state
        yield
        upd = _mm_tn(x * pair_cols(wss, grp), bm_g)
        yield
        keep = jnp.where(lax.broadcasted_iota(jnp.int32, (2 * HEAD, SSD_N), 0) < HEAD,
                         sdec[:, e:e + 1], sdec[:, o:o + 1])
        h_ref[g, e:o + 1] = (keep * hst + upd).reshape(2, HEAD, SSD_N)

    def sd_finish():
        xs_all = sd_prep()[1]
        y = jnp.concatenate(sd_out, axis=1) + ssdd_ref[...] * xs_all
        gated = y * _silu(proj_ref[rs,C_SDZ:C_SDZ + MIX_W])
        mixed_ref[rs,3 * MIX_W:4 * MIX_W] = (_head_norm(gated, 2 * HEAD)
                                             * ssdnw_ref[...]).astype(mixed_ref.dtype)

    chains = ([[hg_within_block(blk) for blk in range(n_blk)]]
              + [[ml_pair(p)] for p in range(N_HEADS // 2)] + [[hg_state_chain(), sd_group(0)]]
              + [[sd_group(1)]]
              + [[sw_group(kv)] for kv in range(SW_KV)])

    def finish():
        ml_finish()
        hg_finish()
        sw_finish()
        sd_finish()

    return chains, finish


def _mixer(proj, rope_tab, init, lp, consts, chunks_per_prompt, n_seq):
    n = proj.shape[0]
    rows = SEQ_GROUP * CHUNK
    n_steps = n // rows

    def group_of(i):
        return jnp.where(i < chunks_per_prompt, 0, i - chunks_per_prompt + 1)

    def rope_blk(i):
        return jnp.minimum(i, chunks_per_prompt)

    def per_seq(shape):
        nd = len(shape)
        return pl.BlockSpec((SEQ_GROUP,) + shape, lambda i, s: (group_of(i),) + (0,) * nd)

    def per_sample_seq(shape):
        nd = len(shape)
        return pl.BlockSpec((SEQ_GROUP,) + shape, lambda i, s: (jnp.maximum(group_of(i) - 1, 0),) + (0,) * nd)

    def const(shape):
        nd = len(shape)
        return pl.BlockSpec(shape, lambda i, s: (0,) * nd)

    state_shapes = [(N_HEADS // 2, 2 * HEAD, 4 * HEAD), (8, 128), (HEAD, MIX_W), (WINDOW, 128), (WINDOW, 128),
                    (N_HEADS, HEAD, SSD_N), (8, SSD_CONV_DIM)]
    in_specs = ([pl.BlockSpec((rows, D_PROJ), lambda i, s: (i, 0)),
                 pl.BlockSpec((CHUNK, 384), lambda i, s: (rope_blk(i), 0))]
                + [per_sample_seq(s) for s in state_shapes]
                + [const((1, 128)), const((1, 128)), const((1, MIX_W)), const((1, MIX_W)), const((1, MIX_W)),
                   const((SSD_CONV, SSD_CONV_DIM)), const((1, SSD_CONV_DIM)), const((1, MIX_W)),
                   const((1, MIX_W)),
                   const((CHUNK, CHUNK)), const((CHUNK, CHUNK)), const((CHUNK, CHUNK)), const((MIX_W, MIX_W))])
    out_specs = ([pl.BlockSpec((rows, D_MODEL), lambda i, s: (i, 0))]
                 + [per_seq(s) for s in state_shapes])
    out_shape = ([jax.ShapeDtypeStruct((n, D_MODEL), BF16)]
                 + [jax.ShapeDtypeStruct((n_seq,) + s, F32) for s in state_shapes])
    grid_spec = pltpu.PrefetchScalarGridSpec(
        num_scalar_prefetch=1, grid=(n_steps,), in_specs=in_specs, out_specs=out_specs,
        scratch_shapes=[pltpu.VMEM((SEQ_GROUP, CHUNK + 8, SSD_CONV_DIM), F32),
                        pltpu.VMEM((SEQ_GROUP, CHUNK, MIX_W), F32), pltpu.VMEM((SEQ_GROUP, CHUNK, MIX_W), F32),
                        pltpu.VMEM((SEQ_GROUP, CHUNK, MIX_W), F32)])
    return pl.pallas_call(
        functools.partial(_mixer_kernel, chunks_per_prompt),
        grid_spec=grid_spec, out_shape=out_shape,
        compiler_params=pltpu.CompilerParams(dimension_semantics=("arbitrary",),
                                             vmem_limit_bytes=VMEM_LIMIT),
        name="mixer",
    )(lp["sinks"], proj, rope_tab, *init,
      lp["gbias"], lp["alog"], lp["mlnw"], lp["hglb"], lp["hgnw"],
      lp["convw"], lp["convb"], lp["ssdd"], lp["ssdnw"], *consts)


def _dense_ffn_kernel(f_chunk, n_parts, bounds, *refs):
    x_refs = refs[:n_parts]
    mix_ref, wo_ref, nw_ref, wg_ref, wu_ref, wd_ref, o_ref = refs[n_parts:]
    x1 = _read_token_tile(x_refs, bounds) + jnp.dot(mix_ref[...], wo_ref[...], preferred_element_type=F32)
    hn = _rms(x1, nw_ref[...]).astype(BF16)
    o_ref[...] = x1
    for f0 in range(0, wg_ref.shape[1], f_chunk):
        g = jnp.dot(hn, wg_ref[:, f0:f0 + f_chunk], preferred_element_type=F32)
        u = jnp.dot(hn, wu_ref[:, f0:f0 + f_chunk], preferred_element_type=F32)
        act = (_silu(g) * u).astype(BF16)
        o_ref[...] += jnp.dot(act, wd_ref[f0:f0 + f_chunk, :], preferred_element_type=F32)


def _dense_ffn(x_parts, mixed, wo, nw, wg, wu, wd):
    n = mixed.shape[0]
    d_ff = wg.shape[1]
    tm = _pick_tile(np.gcd.reduce([_part_rows(p) for p in x_parts]), TOKEN_TILES)
    x_specs, bounds = _token_parts_specs(x_parts, tm)
    full = lambda shape: pl.BlockSpec(shape, lambda i: (0, 0))
    return pl.pallas_call(
        functools.partial(_dense_ffn_kernel, 256, len(x_parts), bounds),
        grid=(n // tm,),
        in_specs=x_specs + [
                  pl.BlockSpec((tm, D_MODEL), lambda i: (i, 0)),
                  full((D_MODEL, D_MODEL)), full((1, D_MODEL)),
                  full((D_MODEL, d_ff)), full((D_MODEL, d_ff)), full((d_ff, D_MODEL))],
        out_specs=pl.BlockSpec((tm, D_MODEL), lambda i: (i, 0)),
        out_shape=jax.ShapeDtypeStruct((n, D_MODEL), F32),
        compiler_params=pltpu.CompilerParams(dimension_semantics=("arbitrary",),
                                             vmem_limit_bytes=VMEM_LIMIT),
        name="dense_ffn",
    )(*x_parts, mixed, wo, nw, wg, wu, wd)


TOK_BLK = 1024
FFN_TILE = 1024


def _pack_bf16_pairs(x):
    w = x.shape[1] // 2
    bits = pltpu.bitcast(x.astype(BF16).astype(F32), jnp.int32)
    return lax.shift_right_logical(bits[:, :w], 16) | bits[:, w:]


def _unpack_bf16_pairs(p):
    lo = pltpu.bitcast(lax.shift_left(p, 16), F32)
    hi = pltpu.bitcast(p & jnp.int32(-65536), F32)
    return lo, hi


def _router_kernel(x_ref, mix_ref, wo_ref, nw_ref, wrt_ref, upper_ref,
                   x1_ref, hn_ref, meta_ref, gate_ref, count_ref, carry):
    @pl.when(pl.program_id(0) == 0)
    def _():
        carry[...] = jnp.zeros_like(carry)

    x1 = x_ref[...] + jnp.dot(mix_ref[...], wo_ref[...], preferred_element_type=F32)
    x1_ref[...] = x1
    hn_f = _rms(x1, nw_ref[...])
    hn = hn_f.astype(BF16)
    hn_ref[...] = _pack_bf16_pairs(hn_f)
    logits = lax.dot_general(wrt_ref[...], hn, (((1,), (1,)), ((), ())),
                             preferred_element_type=F32)
    sub = lax.broadcasted_iota(jnp.int32, logits.shape, 0)
    l1 = jnp.where(sub < N_EXPERTS, logits, NEG)
    m1 = jnp.max(l1, axis=0, keepdims=True)
    i1 = jnp.min(jnp.where(l1 == m1, sub, 99), axis=0, keepdims=True)
    l2 = jnp.where(sub == i1, NEG, l1)
    m2 = jnp.max(l2, axis=0, keepdims=True)
    i2 = jnp.min(jnp.where(l2 == m2, sub, 99), axis=0, keepdims=True)
    e2 = jnp.exp(m2 - m1)
    g1 = 1.0 / (1.0 + e2)
    g2 = e2 / (1.0 + e2)
    sel = jnp.where(sub == i1, 1.0, jnp.where(sub == i2, 1.0, 0.0))
    before = jnp.dot(sel.astype(BF16), upper_ref[...], preferred_element_type=F32) + carry[:, 0:1]
    r1 = jnp.sum(jnp.where(sub == i1, before, 0.0), axis=0, keepdims=True)
    r2 = jnp.sum(jnp.where(sub == i2, before, 0.0), axis=0, keepdims=True)
    meta_ref[...] = jnp.concatenate([i1.astype(F32), i2.astype(F32), r1, r2,
                                     jnp.zeros((4, r1.shape[1]), F32)], axis=0)
    gate_ref[...] = jnp.concatenate([g1, g2, jnp.zeros((126, r1.shape[1]), F32)], axis=0).T
    total = carry[...] + jnp.sum(sel, axis=1, keepdims=True)
    carry[...] = total
    count_ref[...] = total


def _router(x, mixed, wo, nw, wrt, upper):
    n = x.shape[0]
    tm = TOK_BLK
    full = lambda shape: pl.BlockSpec(shape, lambda i: (0,) * len(shape))
    tok = lambda w: pl.BlockSpec((tm, w), lambda i: (i, 0))
    return pl.pallas_call(
        _router_kernel,
        grid=(n // tm,),
        in_specs=[tok(D_MODEL), tok(D_MODEL), full((D_MODEL, D_MODEL)), full((1, D_MODEL)),
                  full((16, D_MODEL)), full((tm, tm))],
        out_specs=[tok(D_MODEL), tok(D_MODEL // 2),
                   pl.BlockSpec((8, tm), lambda i: (0, i)), tok(128), full((16, 128))],
        out_shape=[jax.ShapeDtypeStruct((n, D_MODEL), F32), jax.ShapeDtypeStruct((n, D_MODEL // 2), jnp.int32),
                   jax.ShapeDtypeStruct((8, n), F32), jax.ShapeDtypeStruct((n, 128), F32),
                   jax.ShapeDtypeStruct((16, 128), F32)],
        scratch_shapes=[pltpu.VMEM((16, 128), F32)],
        compiler_params=pltpu.CompilerParams(dimension_semantics=("arbitrary",),
                                             vmem_limit_bytes=VMEM_LIMIT),
        name="router",
    )(x, mixed, wo, nw, wrt, upper)


def _sc_row_gather(table, idx):
    n_workers = SC_CORES * SC_SUBCORES
    b = idx.shape[0]
    d = table.shape[1]
    assert b % (8 * n_workers) == 0
    per_worker = b // n_workers
    chunk = max(c for c in range(8, SC_MAX_GATHER_ROWS + 1, 8) if per_worker % c == 0)
    mesh = plsc.VectorSubcoreMesh(core_axis_name="c", subcore_axis_name="s")

    @functools.partial(
        pl.kernel, mesh=mesh, out_type=jax.ShapeDtypeStruct((b, d), table.dtype),
        scratch_types=[pltpu.VMEM((chunk,), jnp.int32), pltpu.VMEM((chunk, d), table.dtype),
                       pltpu.SemaphoreType.DMA])
    def gather(table_hbm, idx_hbm, out_hbm, idx_v, rows_v, sem):
        worker = lax.axis_index("s") * SC_CORES + lax.axis_index("c")
        base = worker * per_worker

        @pl.loop(0, per_worker // chunk)
        def _(c):
            off = pl.multiple_of(base + c * chunk, 8)
            pltpu.sync_copy(idx_hbm.at[pl.ds(off, chunk)], idx_v)
            pltpu.async_copy(table_hbm.at[idx_v], rows_v, sem).wait()
            pltpu.sync_copy(rows_v, out_hbm.at[pl.ds(off, chunk)])

    return gather(table, idx)


def _sc_row_scatter(rows, idx):
    n_workers = SC_CORES * SC_SUBCORES
    b = n_out = idx.shape[0]
    v, d = rows.shape
    assert b % (8 * n_workers) == 0
    per_worker = b // n_workers
    chunk = max(c for c in range(8, SC_MAX_GATHER_ROWS + 1, 8) if per_worker % c == 0 and v % c == 0)
    mesh = plsc.VectorSubcoreMesh(core_axis_name="c", subcore_axis_name="s")

    @functools.partial(
        pl.kernel, mesh=mesh, out_type=jax.ShapeDtypeStruct((n_out, d), rows.dtype),
        scratch_types=[pltpu.VMEM((chunk,), jnp.int32), pltpu.VMEM((chunk, d), rows.dtype)])
    def scatter(rows_hbm, idx_hbm, out_hbm, idx_v, rows_v):
        worker = lax.axis_index("s") * SC_CORES + lax.axis_index("c")
        base = worker * per_worker

        @pl.loop(0, per_worker // chunk)
        def _(c):
            off = pl.multiple_of(base + c * chunk, 8)
            pltpu.sync_copy(idx_hbm.at[pl.ds(off, chunk)], idx_v)
            pltpu.sync_copy(rows_hbm.at[pl.ds(pl.multiple_of(lax.rem(off, v), 8), chunk)], rows_v)
            pltpu.sync_copy(rows_v, out_hbm.at[idx_v])

    return scatter(rows, idx)


def _expert_ffn_kernel(texp_ref, tvalid_ref, xs_ref, wg_ref, wu_ref, wd_ref, o_ref, acc_ref, xb_ref):
    j = pl.program_id(0)
    f = pl.program_id(1)
    last = f == pl.num_programs(1) - 1
    valid = tvalid_ref[j] == 1
    half = D_MODEL // 2

    @pl.when(jnp.logical_and(valid, f == 0))
    def _():
        lo, hi = _unpack_bf16_pairs(xs_ref[...])
        xb_ref[:, :half] = lo.astype(BF16)
        xb_ref[:, half:] = hi.astype(BF16)

        acc_ref[...] = jnp.zeros_like(acc_ref)

    @pl.when(valid)
    def _():
        xs = xb_ref[...]
        g = jnp.dot(xs, wg_ref[0].astype(BF16), preferred_element_type=F32)
        u = jnp.dot(xs, wu_ref[0].astype(BF16), preferred_element_type=F32)
        act = (_silu(g) * u).astype(BF16)
        acc_ref[...] += jnp.dot(act, wd_ref[0].astype(BF16), preferred_element_type=F32)

        @pl.when(last)
        def _():
            o_ref[...] = _pack_bf16_pairs(acc_ref[...])

    @pl.when(jnp.logical_and(jnp.logical_not(valid), last))
    def _():
        o_ref[...] = jnp.zeros_like(o_ref)


def _expert_ffn(f_exp, f_valid, xs, wg, wu, wd):
    n_slots = xs.shape[0]
    d_ff = wg.shape[2]
    tf = _pick_tile(d_ff, (512, 256, 128))
    grid_spec = pltpu.PrefetchScalarGridSpec(
        num_scalar_prefetch=2, grid=(n_slots // FFN_TILE, d_ff // tf),
        in_specs=[pl.BlockSpec((FFN_TILE, D_MODEL // 2), lambda j, f, te, tv: (j, 0)),
                  pl.BlockSpec((1, D_MODEL, tf), lambda j, f, te, tv: (te[j], 0, f)),
                  pl.BlockSpec((1, D_MODEL, tf), lambda j, f, te, tv: (te[j], 0, f)),
                  pl.BlockSpec((1, tf, D_MODEL), lambda j, f, te, tv: (te[j], f, 0))],
        out_specs=pl.BlockSpec((FFN_TILE, D_MODEL // 2), lambda j, f, te, tv: (j, 0)),
        scratch_shapes=[pltpu.VMEM((FFN_TILE, D_MODEL), F32), pltpu.VMEM((FFN_TILE, D_MODEL), BF16)])
    return pl.pallas_call(
        _expert_ffn_kernel, grid_spec=grid_spec,
        out_shape=jax.ShapeDtypeStruct((n_slots, D_MODEL // 2), jnp.int32),
        compiler_params=pltpu.CompilerParams(dimension_semantics=("arbitrary", "arbitrary"),
                                             vmem_limit_bytes=VMEM_LIMIT),
        name="expert_ffn",
    )(f_exp, f_valid, xs, wg, wu, wd)


def _combine_kernel(part_tiles, x1_ref, o1_ref, o2_ref, gate_ref, fnw_ref, *y_refs):
    i = pl.program_id(0)
    lo1, hi1 = _unpack_bf16_pairs(o1_ref[...])
    lo2, hi2 = _unpack_bf16_pairs(o2_ref[...])
    g1 = gate_ref[:, 0:1]
    g2 = gate_ref[:, 1:2]
    half = D_MODEL // 2
    x1 = x1_ref[...]
    fnw = fnw_ref[...]
    ya = x1[:, :half] + g1 * lo1 + g2 * lo2
    yb = x1[:, half:] + g1 * hi1 + g2 * hi2
    scale = lax.rsqrt((jnp.sum(ya * ya, axis=-1, keepdims=True) + jnp.sum(yb * yb, axis=-1, keepdims=True))
                      / D_MODEL + EPS)
    y = jnp.concatenate([ya * scale * fnw[:, :half], yb * scale * fnw[:, half:]], axis=1)
    lo_tile = 0
    for y_ref, tiles in zip(y_refs, part_tiles):
        @pl.when(jnp.logical_and(i >= lo_tile, i < lo_tile + tiles))
        def _(y_ref=y_ref):
            _store_part_tile(y_ref, y)
        lo_tile += tiles


def _combine(x1, o12, gates, fnw, part_shapes):
    n = x1.shape[0]
    parts = [jax.ShapeDtypeStruct(s, F32) for s in part_shapes]
    tm = _pick_tile(np.gcd.reduce([_part_rows(p) for p in parts]), TOKEN_TILES)
    n_tiles = n // tm
    out_specs, bounds = _token_parts_specs(parts, tm)
    part_tiles = [bounds[k + 1] - bounds[k] for k in range(len(parts))]
    return pl.pallas_call(
        functools.partial(_combine_kernel, part_tiles),
        grid=(n_tiles,),
        in_specs=[pl.BlockSpec((tm, D_MODEL), lambda i: (i, 0)),
                  pl.BlockSpec((tm, D_MODEL // 2), lambda i: (i, 0)),
                  pl.BlockSpec((tm, D_MODEL // 2), lambda i: (i + n_tiles, 0)),
                  pl.BlockSpec((tm, 128), lambda i: (i, 0)),
                  pl.BlockSpec((1, D_MODEL), lambda i: (0, 0))],
        out_specs=out_specs,
        out_shape=parts,
        compiler_params=pltpu.CompilerParams(dimension_semantics=("arbitrary",),
                                             vmem_limit_bytes=VMEM_LIMIT),
        name="combine",
    )(x1, o12, o12, gates, fnw)


def _moe(x, mixed, wo, nw, router_w, wg, wu, wd, fnw, part_shapes):
    i32 = jnp.int32
    n = x.shape[0]
    assert n % TOK_BLK == 0
    wrt = jnp.pad(router_w.T, ((0, 16 - N_EXPERTS), (0, 0))).astype(BF16)
    t = np.arange(TOK_BLK)
    upper = jnp.asarray((t[:, None] < t[None, :]).astype(np.float32), dtype=BF16)
    x1, hn_p, meta, gates, count = _router(x, mixed, wo, nw, wrt, upper)

    n_slots = (-(-2 * n // FFN_TILE) + N_EXPERTS) * FFN_TILE
    counts = count[:N_EXPERTS, 0].astype(i32)
    gsize = (counts + FFN_TILE - 1) // FFN_TILE * FFN_TILE
    gend = jnp.cumsum(gsize)
    goff = gend - gsize
    fstart = jnp.arange(n_slots // FFN_TILE, dtype=i32) * FFN_TILE
    f_valid = (fstart < gend[-1]).astype(i32)
    f_exp = jnp.sum((fstart[:, None] >= gend[None, :]).astype(i32), axis=1)
    f_exp = jnp.minimum(f_exp, jnp.sum((gend[-1] - 1 >= gend).astype(i32)))
    expert_ids = jnp.arange(N_EXPERTS, dtype=i32)[:, None]
    top = meta[0:2].astype(i32)
    rank = meta[2:4].astype(i32)
    slot = jnp.stack([jnp.sum(jnp.where(top[k][None, :] == expert_ids, goff[:, None], 0), axis=0) + rank[k]
                      for k in range(2)]).reshape(-1)
    pad = gsize - counts
    cpad = jnp.cumsum(pad)
    j = jnp.arange(n_slots - 2 * n, dtype=i32)
    pad_e = jnp.sum((j[:, None] >= cpad[None, :]).astype(i32), axis=1)
    pad_base = jnp.sum(jnp.where(pad_e[:, None] == expert_ids.T, (goff + counts - (cpad - pad))[None, :], 0), axis=1)
    pad_slot = jnp.where(pad_e < N_EXPERTS, pad_base + j, gend[-1] + j - cpad[-1])

    xs_p = _sc_row_scatter(hn_p, jnp.concatenate([slot, pad_slot]))
    out_p = _expert_ffn(f_exp, f_valid, xs_p, wg, wu, wd)
    o12 = _sc_row_gather(out_p, slot)
    return _combine(x1, o12, gates, fnw, part_shapes)


def _rope_table(seq_len, dec_len):
    half = ROPE_DIMS // 2
    pos = jnp.concatenate([jnp.arange(seq_len), PAST_LEN + jnp.arange(dec_len)]).astype(F32)
    inv_freq = jnp.power(jnp.float32(ROPE_THETA), -jnp.arange(half, dtype=F32) / half)
    lane = np.arange(128) % HEAD
    inv_lane = jnp.where(lane < ROPE_DIMS, inv_freq[lane % half], 0.0)
    ang = pos[:, None] * inv_lane[None, :]
    cos, sin = jnp.cos(ang), jnp.sin(ang)
    sin_a = jnp.where(lane < half, -sin, 0.0)
    sin_b = jnp.where((lane >= half) & (lane < ROPE_DIMS), sin, 0.0)
    return jnp.concatenate([cos, sin_a, sin_b], axis=1)


def _mixer_consts():
    t = np.arange(CHUNK)
    tri = (t[:, None] >= t[None, :])
    same = (t[:, None] // SUB) == (t[None, :] // SUB)
    c = np.arange(MIX_W)
    ind = (c[:, None] // HEAD) == (c[None, :] // HEAD)
    as_bf = lambda m: jnp.asarray(m.astype(np.float32), dtype=BF16)
    return [as_bf(tri), as_bf(tri & same), as_bf(same), as_bf(ind)]


def _mlstm_state_to_pairs(c, n):
    c = c.astype(F32)
    n_b = jnp.broadcast_to(n.astype(F32)[..., None], c.shape)
    z = jnp.zeros_like(c[:, 0::2])
    top = jnp.concatenate([c[:, 0::2], z, n_b[:, 0::2], z], axis=-1)
    bottom = jnp.concatenate([z, c[:, 1::2], z, n_b[:, 1::2]], axis=-1)
    return jnp.concatenate([top, bottom], axis=-2)


def _mlstm_state_from_pairs(s):
    b = s.shape[0]
    c = jnp.stack([s[:, :, a * HEAD:(a + 1) * HEAD, a * HEAD:(a + 1) * HEAD] for a in range(2)], axis=2)
    n = jnp.stack([s[:, :, a * HEAD:(a + 1) * HEAD, (2 + a) * HEAD] for a in range(2)], axis=2)
    return (c.reshape(b, N_HEADS, HEAD, HEAD), n.reshape(b, N_HEADS, HEAD))


def _pad_lanes(v, width):
    return jnp.pad(v, ((0, 0), (0, width - v.shape[1])))


def kernel(x_prompt, x_sample, state_mlstm_C, state_mlstm_n, state_mlstm_m, state_hgrn_S, cache_swa_k, cache_swa_v, state_ssd_h, state_ssd_conv, norm1_w, w_in, ml_ig_b, ml_fg_b, ml_norm_w, hg_lb_logits, hg_norm_w, sw_sinks, ssd_conv_w, ssd_conv_b, ssd_dt_bias, ssd_A_log, ssd_D, ssd_norm_w, w_out, norm2_w, ffn_w_gate, ffn_w_up, ffn_w_down, moe_router, moe_w_gate, moe_w_up, moe_w_down, final_norm_w):
    depth = w_in.shape[0]
    bp, seq_len, _ = x_prompt.shape
    bs, dec_len, _ = x_sample.shape
    assert seq_len % CHUNK == 0 and dec_len == CHUNK and depth % 2 == 0
    n_seq = bp + bs
    chunks_per_prompt = seq_len // CHUNK

    assert bp == SEQ_GROUP and bs % SEQ_GROUP == 0
    x_parts = (x_prompt.reshape(bp, chunks_per_prompt, CHUNK, D_MODEL), x_sample.reshape(bs * dec_len, D_MODEL))
    part_shapes = tuple(p.shape for p in x_parts)
    rope_tab = _rope_table(seq_len, dec_len)
    consts = _mixer_consts()

    sm = jax.nn.softmax(hg_lb_logits.astype(F32), axis=0)
    hg_lb = jnp.cumsum(sm, axis=0) - sm[0]

    state_outs = []
    for l in range(depth):
        lp = {
            "sinks": sw_sinks[l].astype(F32),
            "gbias": _pad_lanes(jnp.concatenate([ml_ig_b[l], ml_fg_b[l], ssd_dt_bias[l]])[None, :].astype(F32), 128),
            "alog": _pad_lanes(ssd_A_log[l][None, :].astype(F32), 128),
            "mlnw": ml_norm_w[l][None, :].astype(F32),
            "hglb": hg_lb[l][None, :],
            "hgnw": hg_norm_w[l][None, :].astype(F32),
            "convw": ssd_conv_w[l].astype(F32),
            "convb": ssd_conv_b[l][None, :].astype(F32),
            "ssdd": jnp.repeat(ssd_D[l].astype(F32), HEAD)[None, :],
            "ssdnw": ssd_norm_w[l][None, :].astype(F32),
        }
        c_aug = _mlstm_state_to_pairs(state_mlstm_C[l], state_mlstm_n[l])
        m_pad = jnp.pad(state_mlstm_m[l][:, None, :], ((0, 0), (0, 7), (0, 128 - N_HEADS)))
        s_t = jnp.swapaxes(state_hgrn_S[l], -1, -2)
        s_t = jnp.moveaxis(s_t, 1, 2).reshape(bs, HEAD, MIX_W)
        cv_pad = jnp.pad(state_ssd_conv[l], ((0, 0), (8 - (SSD_CONV - 1), 0), (0, 0)))
        init = [a.astype(F32) for a in (
            c_aug, m_pad, s_t, cache_swa_k[l].reshape(bs, WINDOW, 128), cache_swa_v[l].reshape(bs, WINDOW, 128),
            state_ssd_h[l], cv_pad)]

        proj = _inproj(x_parts, norm1_w[l][None, :], w_in, l)
        mixed, c_o, m_o, s_o, k_o, v_o, h_o, cv_o = _mixer(
            proj, rope_tab, init, lp, consts, chunks_per_prompt, n_seq)
        wo = w_out[l].astype(BF16)
        j = l // 2
        if l % 2 == 0:
            x_parts = (_dense_ffn(x_parts, mixed, wo, norm2_w[l][None, :], ffn_w_gate[j].astype(BF16),
                                  ffn_w_up[j].astype(BF16), ffn_w_down[j].astype(BF16)),)
        else:
            assert l == depth - 1 and len(x_parts) == 1
            y_parts = _moe(x_parts[0], mixed, wo, norm2_w[l][None, :], moe_router[j], moe_w_gate[j],
                           moe_w_up[j], moe_w_down[j], final_norm_w[None, :], part_shapes)
        s_back = jnp.moveaxis(s_o.reshape(n_seq, HEAD, N_HEADS, HEAD), 2, 1)
        state_outs.append(_mlstm_state_from_pairs(c_o) + (m_o[:, 0, :N_HEADS],
                           jnp.swapaxes(s_back, -1, -2),
                           k_o.reshape(n_seq, WINDOW, SW_KV, HEAD), v_o.reshape(n_seq, WINDOW, SW_KV, HEAD),
                           h_o, cv_o[:, 8 - (SSD_CONV - 1):, :]))

    y_prompt = y_parts[0].reshape(bp, seq_len, D_MODEL)
    y_sample = y_parts[1].reshape(bs, dec_len, D_MODEL)
    stacked = [jnp.stack([so[k] for so in state_outs]) for k in range(8)]
    return (y_prompt, y_sample) + tuple(s[:, :bp] for s in stacked) + tuple(s[:, bp:] for s in stacked)
```

```python
import functools

import numpy as np
import jax
import jax.numpy as jnp
from jax import lax
from jax.experimental import pallas as pl
from jax.experimental.pallas import tpu as pltpu
from jax.experimental.pallas import tpu_sc as plsc

F32 = jnp.float32
BF16 = jnp.bfloat16

D_MODEL = 1024
CHUNK = 64
EPS = 1e-6
N_HEADS = 4
HEAD = 64
MIX_W = N_HEADS * HEAD
SW_KV = 2
WINDOW = 128
ROPE_DIMS = 16
ROPE_THETA = 500000.0
SSD_N = 128
SSD_CONV = 4
SSD_CONV_DIM = 768
PAST_LEN = 4096
N_EXPERTS = 8
SUB = 16
SEQ_GROUP = 2
HG_SLICE = 4
TOKEN_TILES = (512, 256, 128)

C_MLQ, C_MLK, C_MLV, C_MLO = 0, 256, 512, 768
C_HGQ, C_HGF, C_HGI, C_HGG = 1024, 1280, 1536, 1792
C_SWQ, C_SWK, C_SWV = 2048, 2304, 2432
C_SDZ, C_SDX, C_GATE = 2560, 2816, 3584
D_PROJ = 3712
W_IN_COLS = C_GATE + 3 * N_HEADS
NEG = -1e30
V7X_VMEM_BYTES = 64 * 1024 * 1024
VMEM_LIMIT = V7X_VMEM_BYTES - 8 * 1024 * 1024
SC_CORES = 2
SC_SUBCORES = 16
SC_MAX_GATHER_ROWS = 128


def _mm(a, b):
    return jnp.dot(a.astype(BF16), b.astype(BF16), preferred_element_type=F32)


def _mm_nt(a, b):
    return lax.dot_general(a.astype(BF16), b.astype(BF16), (((1,), (1,)), ((), ())),
                           preferred_element_type=F32)


def _mm_tn(a, b):
    return jnp.dot(a.T.astype(BF16), b.astype(BF16), preferred_element_type=F32)


def _mm_sel(sel, x):
    hi = x.astype(BF16)
    r1 = x - hi.astype(F32)
    mid = r1.astype(BF16)
    lo = (r1 - mid.astype(F32)).astype(BF16)
    return (jnp.dot(sel, hi, preferred_element_type=F32)
            + jnp.dot(sel, mid, preferred_element_type=F32)
            + jnp.dot(sel, lo, preferred_element_type=F32))


def _sigmoid(x):
    return 1.0 / (1.0 + jnp.exp(-x))


def _silu(x):
    return x * _sigmoid(x)


def _softplus(x):
    return jnp.maximum(x, 0.0) + jnp.log(1.0 + jnp.exp(-jnp.abs(x)))


def _rms(x, w):
    return x * lax.rsqrt(jnp.mean(x * x, axis=-1, keepdims=True) + EPS) * w


def _pick_tile(n, candidates):
    for c in candidates:
        if n % c == 0:
            return c
    raise ValueError(f"no tile for {n}")


def _token_parts_specs(parts, tm):
    bounds = np.cumsum([0] + [_part_rows(p) // tm for p in parts])
    specs = []
    for k, p in enumerate(parts):
        assert _part_rows(p) % tm == 0
        lo, hi = int(bounds[k]), int(bounds[k + 1])
        specs.append(_part_block_spec(p, tm, lo, hi - lo))
    return specs, [int(b) for b in bounds]


def _part_rows(p):
    return int(np.prod(p.shape[:-1]))


def _part_block_spec(p, tm, first_tile, n_tiles):
    idx = lambda i: jnp.clip(i - first_tile, 0, n_tiles - 1)
    if len(p.shape) == 2:
        return pl.BlockSpec((tm, p.shape[1]), lambda i: (idx(i), 0))
    grp, _, chunk, d = p.shape
    assert tm % (grp * chunk) == 0
    return pl.BlockSpec((grp, tm // (grp * chunk), chunk, d), lambda i: (0, idx(i), 0, 0))


def _load_part_tile(ref):
    if len(ref.shape) == 2:
        return ref[...]
    grp, n_chunks = ref.shape[:2]
    return jnp.concatenate([ref[s, c] for c in range(n_chunks) for s in range(grp)], axis=0)


def _store_part_tile(ref, x):
    if len(ref.shape) == 2:
        ref[...] = x
        return
    grp, n_chunks, chunk = ref.shape[:3]
    for c in range(n_chunks):
        for s in range(grp):
            r0 = (c * grp + s) * chunk
            ref[s, c] = x[r0:r0 + chunk]


def _read_token_tile(refs, bounds):
    i = pl.program_id(0)
    x = _load_part_tile(refs[-1])
    for k in range(len(refs) - 2, -1, -1):
        x = jnp.where(i < bounds[k + 1], _load_part_tile(refs[k]), x)
    return x


def _inproj_kernel(n_parts, bounds, *refs):
    x_refs = refs[:n_parts]
    nw_ref, w_ref, o_ref, wp_ref = refs[n_parts:]

    @pl.when(pl.program_id(0) == 0)
    def _():
        rows = 128
        ml_gates = slice(C_MLO, C_MLO + 2 * N_HEADS)
        sd_dt = slice(W_IN_COLS - N_HEADS, W_IN_COLS)
        n_gates = 3 * N_HEADS
        for r0 in range(0, D_MODEL, rows):
            w = w_ref[0, r0:r0 + rows, :]
            wp_ref[r0:r0 + rows, 0:C_MLO] = w[:, 0:ml_gates.start].astype(BF16)
            wp_ref[r0:r0 + rows, C_MLO:C_GATE] = w[:, ml_gates.stop:sd_dt.start].astype(BF16)
            gates = jnp.concatenate([w[:, ml_gates], w[:, sd_dt],
                                     jnp.zeros((rows, D_PROJ - C_GATE - n_gates), F32)], axis=1)
            wp_ref[r0:r0 + rows, C_GATE:D_PROJ] = gates.astype(BF16)

    xn = _rms(_read_token_tile(x_refs, bounds), nw_ref[...]).astype(BF16)
    o_ref[...] = jnp.dot(xn, wp_ref[...], preferred_element_type=F32)


def _inproj(x_parts, nw, w_all, layer):
    n = sum(_part_rows(p) for p in x_parts)
    tm = _pick_tile(np.gcd.reduce([_part_rows(p) for p in x_parts]), TOKEN_TILES)
    x_specs, bounds = _token_parts_specs(x_parts, tm)
    assert w_all.shape[1:] == (D_MODEL, W_IN_COLS)
    return pl.pallas_call(
        functools.partial(_inproj_kernel, len(x_parts), bounds),
        grid=(n // tm,),
        in_specs=x_specs + [pl.BlockSpec((1, D_MODEL), lambda i: (0, 0)),
                            pl.BlockSpec((1, D_MODEL, W_IN_COLS), lambda i: (layer, 0, 0),
                                         pipeline_mode=pl.Buffered(1))],
        out_specs=pl.BlockSpec((tm, D_PROJ), lambda i: (i, 0)),
        out_shape=jax.ShapeDtypeStruct((n, D_PROJ), F32),
        scratch_shapes=[pltpu.VMEM((D_MODEL, D_PROJ), BF16)],
        compiler_params=pltpu.CompilerParams(dimension_semantics=("arbitrary",),
                                             vmem_limit_bytes=VMEM_LIMIT),
        name="inproj",
    )(*x_parts, nw, w_all)


def _head_norm(x, width):
    parts = []
    for g in range(x.shape[1] // width):
        xg = x[:, g * width:(g + 1) * width]
        parts.append(xg * lax.rsqrt(jnp.mean(xg * xg, axis=-1, keepdims=True) + EPS))
    return jnp.concatenate(parts, axis=1)


def _mixer_kernel(prompt_steps,
                  sinks_ref,
                  proj_ref, rope_ref, c0_ref, m0_ref, s0_ref, k0_ref, v0_ref, h0_ref, cv0_ref,
                  gbias_ref, alog_ref, mlnw_ref, hglb_ref, hgnw_ref,
                  convw_ref, convb_ref, ssdd_ref, ssdnw_ref,
                  tri_ref, triblk_ref, blkones_ref, ind_ref,
                  mixed_ref, c_ref, m_ref, s_ref, k_ref, v_ref, h_ref, cv_ref,
                  convbuf, bloc_s, kk_s, hv_s):
    i = pl.program_id(0)
    is_prompt = i < prompt_steps
    n_valid = jnp.where(is_prompt, jnp.minimum(i * CHUNK, WINDOW), WINDOW)
    state_refs = (c_ref, m_ref, s_ref, k_ref, v_ref, h_ref, cv_ref)

    @pl.when(i == 0)
    def _():
        for ref in state_refs:
            ref[...] = jnp.zeros_like(ref)

    @pl.when(jnp.logical_not(is_prompt))
    def _():
        for ref, init_ref in zip(state_refs, (c0_ref, m0_ref, s0_ref, k0_ref, v0_ref, h0_ref, cv0_ref)):
            ref[...] = init_ref[...]

    members = [_mixer_chunk(g, n_valid, sinks_ref, proj_ref, rope_ref,
                            gbias_ref, alog_ref, mlnw_ref, hglb_ref, hgnw_ref,
                            convw_ref, convb_ref, ssdd_ref, ssdnw_ref,
                            tri_ref, triblk_ref, blkones_ref, ind_ref,
                            mixed_ref, c_ref, m_ref, s_ref, k_ref, v_ref, h_ref, cv_ref,
                            convbuf, bloc_s, kk_s, hv_s) for g in range(SEQ_GROUP)]
    for phase in zip(*[m[0] for m in members]):
        chains = [c for per_member in zip(*phase) for c in per_member]
        while chains:
            alive = []
            for chain in chains:
                if next(chain, _CHAIN_DONE) is not _CHAIN_DONE:
                    alive.append(chain)
            chains = alive
    for _, finish in members:
        finish()


_CHAIN_DONE = object()


def _mixer_chunk(g, n_valid, sinks_ref, proj_ref, rope_ref,
                 gbias_ref, alog_ref, mlnw_ref, hglb_ref, hgnw_ref,
                 convw_ref, convb_ref, ssdd_ref, ssdnw_ref,
                 tri_ref, triblk_ref, blkones_ref, ind_ref,
                 mixed_ref, c_ref, m_ref, s_ref, k_ref, v_ref, h_ref, cv_ref,
                 convbuf, bloc_s, kk_s, hv_s):
    rs = slice(g * CHUNK, (g + 1) * CHUNK)
    L = CHUNK
    row = lax.broadcasted_iota(jnp.int32, (L, L), 0)
    col = lax.broadcasted_iota(jnp.int32, (L, L), 1)
    causal = row >= col
    tri = tri_ref[...]

    gate = proj_ref[rs,C_GATE:C_GATE + 128] + gbias_ref[...]
    ig = gate
    lf = pltpu.roll(-_softplus(-gate), 124, 1)
    dtv = pltpu.roll(_softplus(gate), 120, 1)

    b = _mm_sel(tri, lf)
    a = ig - b
    cm = a
    rowg = lax.broadcasted_iota(jnp.int32, (L, 128), 0)
    for sh in (1, 2, 4, 8, 16, 32):
        cm = jnp.where(rowg >= sh, jnp.maximum(cm, pltpu.roll(cm, sh, 0)), cm)
    m_prev = m_ref[g, 0:1, :]
    m_t = b + jnp.maximum(m_prev, cm)
    inter = jnp.exp(b + m_prev - m_t)
    bm = b - m_t
    a_t = a.T
    m_last = m_t[L - 1:L, :]
    b_last = b[L - 1:L, :]
    ws = jnp.exp(b_last + a - m_last)
    decay = jnp.exp(b_last + m_prev - m_last)
    emt = jnp.exp(-m_t)
    m_ref[g, 0:1, :] = m_last

    lane2 = lax.broadcasted_iota(jnp.int32, (L, 2 * HEAD), 1)
    low = lane2 < HEAD
    causal2 = lax.broadcasted_iota(jnp.int32, (L, 2 * HEAD), 0) >= (lane2 & (HEAD - 1))
    ones_blk = ind_ref[0:2 * HEAD, 0:2 * HEAD]
    ones_rows = jnp.ones((L, 2 * HEAD), BF16)
    st_row = lax.broadcasted_iota(jnp.int32, (2 * HEAD, 4 * HEAD), 0) // HEAD
    st_lane = (lax.broadcasted_iota(jnp.int32, (2 * HEAD, 4 * HEAD), 1) // HEAD) % 2
    ml_out = [None] * (N_HEADS // 2)

    def pair_cols(x, p):
        return jnp.where(low, x[:, 2 * p:2 * p + 1], x[:, 2 * p + 1:2 * p + 2])

    def split_rows(x):
        return jnp.concatenate([jnp.where(low, x, 0.0), jnp.where(low, 0.0, x)], axis=0)

    def ml_pair(p):
        cs = slice(2 * p * HEAD, 2 * (p + 1) * HEAD)
        q = proj_ref[rs, C_MLQ + cs.start:C_MLQ + cs.stop]
        k = proj_ref[rs, C_MLK + cs.start:C_MLK + cs.stop] * (HEAD ** -0.5)
        v = proj_ref[rs, C_MLV + cs.start:C_MLV + cs.stop]
        qk = _mm_nt(q, split_rows(k))
        yield
        st = c_ref[g, p]
        from_state = _mm(q, st)
        yield
        a_row = jnp.concatenate([a_t[2 * p:2 * p + 1, :], a_t[2 * p + 1:2 * p + 2, :]], axis=1)
        w = jnp.where(causal2, jnp.exp(pair_cols(bm, p) + a_row), 0.0)
        rhs = jnp.concatenate([split_rows(v).astype(BF16), ones_blk], axis=1)
        gate_in = pair_cols(inter, p)
        nd = (jnp.dot((w * qk).astype(BF16), rhs, preferred_element_type=F32)
              + jnp.concatenate([gate_in, gate_in], axis=1) * from_state)
        yield
        kws = k * pair_cols(ws, p)
        upd = jnp.dot(kws.T.astype(BF16), jnp.concatenate([v.astype(BF16), ones_rows], axis=1),
                      preferred_element_type=F32)
        yield
        den = jnp.maximum(jnp.abs(nd[:, 2 * HEAD:]), pair_cols(emt, p))
        ml_out[p] = nd[:, :2 * HEAD] / den
        dec = jnp.where(st_row == 0, decay[:, 2 * p:2 * p + 1], decay[:, 2 * p + 1:2 * p + 2])
        c_ref[g, p] = dec * st + jnp.where(st_row == st_lane, upd, 0.0)

    def ml_finish():
        ml = _head_norm(jnp.concatenate(ml_out, axis=1), HEAD)
        mixed_ref[rs,0:MIX_W] = (_sigmoid(proj_ref[rs,C_MLO:C_MLO + MIX_W]) * ml
                                 * mlnw_ref[...]).astype(mixed_ref.dtype)

    lb = hglb_ref[...]
    fg = lb + (1.0 - lb) * _sigmoid(proj_ref[rs,C_HGF:C_HGF + MIX_W])
    hq = _silu(proj_ref[rs,C_HGQ:C_HGQ + MIX_W]) * (HEAD ** -0.5)
    kk = 1.0 - fg
    lfg = jnp.log(fg)
    hv = proj_ref[rs,C_HGI:C_HGI + MIX_W]
    bloc = _mm_sel(triblk_ref[...], lfg)
    blast = _mm_sel(blkones_ref[...], lfg)
    qe = hq * jnp.exp(bloc)
    kw = kk * jnp.exp(blast - bloc)
    bloc_s[g] = bloc
    kk_s[g] = kk
    hv_s[g] = hv
    ind = ind_ref[...]
    rows = lax.broadcasted_iota(jnp.int32, (SUB, MIX_W), 0)
    n_blk = L // SUB
    hg_state_part = [None] * n_blk
    hg_block_part = [None] * n_blk

    def hg_within_block(blk):
        r0 = blk * SUB
        q_blk = hq[r0:r0 + SUB]
        b_blk = bloc[r0:r0 + SUB]
        xs = []
        for s in range(SUB):
            r = r0 + s
            e = jnp.exp(jnp.minimum(b_blk - bloc_s[g, r:r + 1, :], 0.0))
            xs.append((q_blk * (kk_s[g, r:r + 1, :] * e)).astype(BF16))
            if s % HG_SLICE == HG_SLICE - 1:
                yield
        att = jnp.dot(jnp.concatenate(xs, axis=0), ind, preferred_element_type=F32)
        yield
        o_blk = att[0:SUB] * hv_s[g, r0:r0 + 1, :]
        for s in range(1, SUB):
            r = r0 + s
            o_blk = o_blk + jnp.where(rows >= s, att[s * SUB:(s + 1) * SUB], 0.0) * hv_s[g, r:r + 1, :]
            if s % HG_SLICE == HG_SLICE - 1:
                yield
        hg_block_part[blk] = o_blk

    def hg_state_chain():
        st = s_ref[g]
        for blk in range(n_blk):
            r0 = blk * SUB
            hg_state_part[blk] = jnp.concatenate(
                [_mm_nt(qe[r0:r0 + SUB, h * HEAD:(h + 1) * HEAD], st[:, h * HEAD:(h + 1) * HEAD])
                 for h in range(N_HEADS)], axis=1)
            upd = jnp.concatenate(
                [_mm_tn(hv[r0:r0 + SUB, h * HEAD:(h + 1) * HEAD], kw[r0:r0 + SUB, h * HEAD:(h + 1) * HEAD])
                 for h in range(N_HEADS)], axis=1)
            yield
            st = st * jnp.exp(blast[r0:r0 + 1, :]) + upd
        s_ref[g] = st

    def hg_finish():
        o = jnp.concatenate([hg_state_part[blk] + hg_block_part[blk] for blk in range(n_blk)], axis=0)
        mixed_ref[rs,MIX_W:2 * MIX_W] = (_head_norm(o, HEAD) * hgnw_ref[...]
                                         * _silu(proj_ref[rs,C_HGG:C_HGG + MIX_W])).astype(mixed_ref.dtype)

    @functools.lru_cache(maxsize=None)
    def sw_prep():
        cos = rope_ref[:, 0:128]
        sin_a = rope_ref[:, 128:256]
        sin_b = rope_ref[:, 256:384]

        def rope(x):
            return x * cos + pltpu.roll(x, 120, 1) * sin_a + pltpu.roll(x, 8, 1) * sin_b

        q_rot = [rope(proj_ref[rs,C_SWQ:C_SWQ + 128]), rope(proj_ref[rs,C_SWQ + 128:C_SWQ + 256])]
        k_rot = rope(proj_ref[rs,C_SWK:C_SWK + 128])
        v_cur = proj_ref[rs,C_SWV:C_SWV + 128]
        colw = lax.broadcasted_iota(jnp.int32, (2 * L, WINDOW), 1)
        row2 = lax.broadcasted_iota(jnp.int32, (2 * L, 1), 0)
        return q_rot, k_rot, v_cur, k_ref[g], v_ref[g], colw >= WINDOW - n_valid, row2

    sw_prep()
    sw_out = [None] * (2 * SW_KV)

    def sw_group(kv):
        q_rot, k_rot, v_cur, k_prev, v_prev, prev_ok, row2 = sw_prep()
        gs = slice(kv * HEAD, (kv + 1) * HEAD)
        q2 = jnp.concatenate([q_rot[kv][:, 0:HEAD], q_rot[kv][:, HEAD:2 * HEAD]], axis=0) * (HEAD ** -0.5)
        s_p = jnp.where(prev_ok, _mm_nt(q2, k_prev[:, gs]), NEG)
        yield
        s_c = _mm_nt(q2, k_rot[:, gs])
        yield
        sink = jnp.where(row2 < L, sinks_ref[2 * kv], sinks_ref[2 * kv + 1])
        mx = jnp.maximum(jnp.maximum(jnp.max(s_p, axis=-1, keepdims=True),
                                     jnp.max(s_c, axis=-1, keepdims=True)), sink)
        e_p = jnp.exp(s_p - mx)
        e_c = jnp.exp(s_c - mx)
        den = (jnp.sum(e_p, axis=-1, keepdims=True) + jnp.sum(e_c, axis=-1, keepdims=True)
               + jnp.exp(sink - mx))
        o_p = _mm(e_p, v_prev[:, gs])
        yield
        o = (o_p + _mm(e_c, v_cur[:, gs])) / den
        yield
        sw_out[2 * kv] = o[0:L]
        sw_out[2 * kv + 1] = o[L:2 * L]

    def sw_finish():
        _, k_rot, v_cur, k_prev, v_prev, _, _ = sw_prep()
        mixed_ref[rs,2 * MIX_W:3 * MIX_W] = jnp.concatenate(sw_out, axis=1).astype(mixed_ref.dtype)
        k_ref[g, 0:WINDOW - L, :] = k_prev[L:WINDOW]
        k_ref[g, WINDOW - L:WINDOW, :] = k_rot
        v_ref[g, 0:WINDOW - L, :] = v_prev[L:WINDOW]
        v_ref[g, WINDOW - L:WINDOW, :] = v_cur

    @functools.lru_cache(maxsize=None)
    def sd_prep():
        convbuf[g, 0:8, :] = cv_ref[g]
        convbuf[g, 8:8 + L, :] = proj_ref[rs,C_SDX:C_SDX + SSD_CONV_DIM]
        acc = convb_ref[...] + convbuf[g, 5:5 + L, :] * convw_ref[0:1, :]
        for j in range(1, SSD_CONV):
            acc = acc + convbuf[g, 5 + j:5 + j + L, :] * convw_ref[j:j + 1, :]
        xbc = _silu(acc)
        cv_ref[g] = convbuf[g, L:L + 8, :]
        a_neg = -jnp.exp(alog_ref[...])
        ca = _mm_sel(tri, dtv * a_neg)
        ca_last = ca[L - 1:L, :]
        return (xbc, xbc[:, 0:MIX_W], ca, ca.T, dtv.T, jnp.exp(ca), jnp.exp(ca_last - ca) * dtv,
                jnp.exp(ca_last))

    sd_prep()
    sd_out = [None] * (N_HEADS // 2)

    def sd_group(grp):
        xbc, xs_all, ca, ca_t, dt_t, eca, wss, sdec = sd_prep()
        bm_g = xbc[:, MIX_W + grp * SSD_N:MIX_W + (grp + 1) * SSD_N]
        cm_g = xbc[:, MIX_W + 2 * SSD_N + grp * SSD_N:MIX_W + 2 * SSD_N + (grp + 1) * SSD_N]
        e, o = 2 * grp, 2 * grp + 1
        x = xs_all[:, e * HEAD:(o + 1) * HEAD]
        cb2 = _mm_nt(cm_g, jnp.concatenate([bm_g, bm_g], axis=0))
        yield
        hst = h_ref[g, e:o + 1].reshape(2 * HEAD, SSD_N)
        from_state = _mm_nt(cm_g, hst)
        yield
        ca_row = jnp.concatenate([ca_t[e:e + 1, :], ca_t[o:o + 1, :]], axis=1)
        dt_row = jnp.concatenate([dt_t[e:e + 1, :], dt_t[o:o + 1, :]], axis=1)
        dec = jnp.where(causal2, jnp.exp(pair_cols(ca, grp) - ca_row), 0.0)
        sd_out[grp] = _mm(cb2 * dec * dt_row, split_rows(x)) + pair_cols(eca, grp) * from_state
        yield
        upd = _mm_tn(x * pair_cols(wss, grp), bm_g)
        yield
        keep = jnp.where(lax.broadcasted_iota(jnp.int32, (2 * HEAD, SSD_N), 0) < HEAD,
                         sdec[:, e:e + 1], sdec[:, o:o + 1])
        h_ref[g, e:o + 1] = (keep * hst + upd).reshape(2, HEAD, SSD_N)

    def sd_finish():
        xs_all = sd_prep()[1]
        y = jnp.concatenate(sd_out, axis=1) + ssdd_ref[...] * xs_all
        gated = y * _silu(proj_ref[rs,C_SDZ:C_SDZ + MIX_W])
        mixed_ref[rs,3 * MIX_W:4 * MIX_W] = (_head_norm(gated, 2 * HEAD)
                                             * ssdnw_ref[...]).astype(mixed_ref.dtype)

    chains = ([[hg_within_block(blk) for blk in range(n_blk)]]
              + [[ml_pair(p)] for p in range(N_HEADS // 2)] + [[hg_state_chain(), sd_group(0)]]
              + [[sd_group(1)]]
              + [[sw_group(kv)] for kv in range(SW_KV)])

    def finish():
        ml_finish()
        hg_finish()
        sw_finish()
        sd_finish()

    return chains, finish


def _mixer(proj, rope_tab, init, lp, consts, chunks_per_prompt, n_seq):
    n = proj.shape[0]
    rows = SEQ_GROUP * CHUNK
    n_steps = n // rows

    def group_of(i):
        return jnp.where(i < chunks_per_prompt, 0, i - chunks_per_prompt + 1)

    def rope_blk(i):
        return jnp.minimum(i, chunks_per_prompt)

    def per_seq(shape):
        nd = len(shape)
        return pl.BlockSpec((SEQ_GROUP,) + shape, lambda i, s: (group_of(i),) + (0,) * nd)

    def per_sample_seq(shape):
        nd = len(shape)
        return pl.BlockSpec((SEQ_GROUP,) + shape, lambda i, s: (jnp.maximum(group_of(i) - 1, 0),) + (0,) * nd)

    def const(shape):
        nd = len(shape)
        return pl.BlockSpec(shape, lambda i, s: (0,) * nd)

    state_shapes = [(N_HEADS // 2, 2 * HEAD, 4 * HEAD), (8, 128), (HEAD, MIX_W), (WINDOW, 128), (WINDOW, 128),
                    (N_HEADS, HEAD, SSD_N), (8, SSD_CONV_DIM)]
    in_specs = ([pl.BlockSpec((rows, D_PROJ), lambda i, s: (i, 0)),
                 pl.BlockSpec((CHUNK, 384), lambda i, s: (rope_blk(i), 0))]
                + [per_sample_seq(s) for s in state_shapes]
                + [const((1, 128)), const((1, 128)), const((1, MIX_W)), const((1, MIX_W)), const((1, MIX_W)),
                   const((SSD_CONV, SSD_CONV_DIM)), const((1, SSD_CONV_DIM)), const((1, MIX_W)),
                   const((1, MIX_W)),
                   const((CHUNK, CHUNK)), const((CHUNK, CHUNK)), const((CHUNK, CHUNK)), const((MIX_W, MIX_W))])
    out_specs = ([pl.BlockSpec((rows, D_MODEL), lambda i, s: (i, 0))]
                 + [per_seq(s) for s in state_shapes])
    out_shape = ([jax.ShapeDtypeStruct((n, D_MODEL), BF16)]
                 + [jax.ShapeDtypeStruct((n_seq,) + s, F32) for s in state_shapes])
    grid_spec = pltpu.PrefetchScalarGridSpec(
        num_scalar_prefetch=1, grid=(n_steps,), in_specs=in_specs, out_specs=out_specs,
        scratch_shapes=[pltpu.VMEM((SEQ_GROUP, CHUNK + 8, SSD_CONV_DIM), F32),
                        pltpu.VMEM((SEQ_GROUP, CHUNK, MIX_W), F32), pltpu.VMEM((SEQ_GROUP, CHUNK, MIX_W), F32),
                        pltpu.VMEM((SEQ_GROUP, CHUNK, MIX_W), F32)])
    return pl.pallas_call(
        functools.partial(_mixer_kernel, chunks_per_prompt),
        grid_spec=grid_spec, out_shape=out_shape,
        compiler_params=pltpu.CompilerParams(dimension_semantics=("arbitrary",),
                                             vmem_limit_bytes=VMEM_LIMIT),
        name="mixer",
    )(lp["sinks"], proj, rope_tab, *init,
      lp["gbias"], lp["alog"], lp["mlnw"], lp["hglb"], lp["hgnw"],
      lp["convw"], lp["convb"], lp["ssdd"], lp["ssdnw"], *consts)


def _dense_ffn_kernel(f_chunk, n_parts, bounds, *refs):
    x_refs = refs[:n_parts]
    mix_ref, wo_ref, nw_ref, wg_ref, wu_ref, wd_ref, o_ref = refs[n_parts:]
    x1 = _read_token_tile(x_refs, bounds) + jnp.dot(mix_ref[...], wo_ref[...], preferred_element_type=F32)
    hn = _rms(x1, nw_ref[...]).astype(BF16)
    o_ref[...] = x1
    for f0 in range(0, wg_ref.shape[1], f_chunk):
        g = jnp.dot(hn, wg_ref[:, f0:f0 + f_chunk], preferred_element_type=F32)
        u = jnp.dot(hn, wu_ref[:, f0:f0 + f_chunk], preferred_element_type=F32)
        act = (_silu(g) * u).astype(BF16)
        o_ref[...] += jnp.dot(act, wd_ref[f0:f0 + f_chunk, :], preferred_element_type=F32)


def _dense_ffn(x_parts, mixed, wo, nw, wg, wu, wd):
    n = mixed.shape[0]
    d_ff = wg.shape[1]
    tm = _pick_tile(np.gcd.reduce([_part_rows(p) for p in x_parts]), TOKEN_TILES)
    x_specs, bounds = _token_parts_specs(x_parts, tm)
    full = lambda shape: pl.BlockSpec(shape, lambda i: (0, 0))
    return pl.pallas_call(
        functools.partial(_dense_ffn_kernel, 256, len(x_parts), bounds),
        grid=(n // tm,),
        in_specs=x_specs + [
                  pl.BlockSpec((tm, D_MODEL), lambda i: (i, 0)),
                  full((D_MODEL, D_MODEL)), full((1, D_MODEL)),
                  full((D_MODEL, d_ff)), full((D_MODEL, d_ff)), full((d_ff, D_MODEL))],
        out_specs=pl.BlockSpec((tm, D_MODEL), lambda i: (i, 0)),
        out_shape=jax.ShapeDtypeStruct((n, D_MODEL), F32),
        compiler_params=pltpu.CompilerParams(dimension_semantics=("arbitrary",),
                                             vmem_limit_bytes=VMEM_LIMIT),
        name="dense_ffn",
    )(*x_parts, mixed, wo, nw, wg, wu, wd)


TOK_BLK = 1024
FFN_TILE = 1024


def _pack_bf16_pairs(x):
    w = x.shape[1] // 2
    bits = pltpu.bitcast(x.astype(BF16).astype(F32), jnp.int32)
    return lax.shift_right_logical(bits[:, :w], 16) | bits[:, w:]


def _unpack_bf16_pairs(p):
    lo = pltpu.bitcast(lax.shift_left(p, 16), F32)
    hi = pltpu.bitcast(p & jnp.int32(-65536), F32)
    return lo, hi


def _router_kernel(x_ref, mix_ref, wo_ref, nw_ref, wrt_ref, upper_ref,
                   x1_ref, hn_ref, meta_ref, gate_ref, count_ref, carry):
    @pl.when(pl.program_id(0) == 0)
    def _():
        carry[...] = jnp.zeros_like(carry)

    x1 = x_ref[...] + jnp.dot(mix_ref[...], wo_ref[...], preferred_element_type=F32)
    x1_ref[...] = x1
    hn_f = _rms(x1, nw_ref[...])
    hn = hn_f.astype(BF16)
    hn_ref[...] = _pack_bf16_pairs(hn_f)
    logits = lax.dot_general(wrt_ref[...], hn, (((1,), (1,)), ((), ())),
                             preferred_element_type=F32)
    sub = lax.broadcasted_iota(jnp.int32, logits.shape, 0)
    l1 = jnp.where(sub < N_EXPERTS, logits, NEG)
    m1 = jnp.max(l1, axis=0, keepdims=True)
    i1 = jnp.min(jnp.where(l1 == m1, sub, 99), axis=0, keepdims=True)
    l2 = jnp.where(sub == i1, NEG, l1)
    m2 = jnp.max(l2, axis=0, keepdims=True)
    i2 = jnp.min(jnp.where(l2 == m2, sub, 99), axis=0, keepdims=True)
    e2 = jnp.exp(m2 - m1)
    g1 = 1.0 / (1.0 + e2)
    g2 = e2 / (1.0 + e2)
    sel = jnp.where(sub == i1, 1.0, jnp.where(sub == i2, 1.0, 0.0))
    before = jnp.dot(sel.astype(BF16), upper_ref[...], preferred_element_type=F32) + carry[:, 0:1]
    r1 = jnp.sum(jnp.where(sub == i1, before, 0.0), axis=0, keepdims=True)
    r2 = jnp.sum(jnp.where(sub == i2, before, 0.0), axis=0, keepdims=True)
    meta_ref[...] = jnp.concatenate([i1.astype(F32), i2.astype(F32), r1, r2,
                                     jnp.zeros((4, r1.shape[1]), F32)], axis=0)
    gate_ref[...] = jnp.concatenate([g1, g2, jnp.zeros((126, r1.shape[1]), F32)], axis=0).T
    total = carry[...] + jnp.sum(sel, axis=1, keepdims=True)
    carry[...] = total
    count_ref[...] = total


def _router(x, mixed, wo, nw, wrt, upper):
    n = x.shape[0]
    tm = TOK_BLK
    full = lambda shape: pl.BlockSpec(shape, lambda i: (0,) * len(shape))
    tok = lambda w: pl.BlockSpec((tm, w), lambda i: (i, 0))
    return pl.pallas_call(
        _router_kernel,
        grid=(n // tm,),
        in_specs=[tok(D_MODEL), tok(D_MODEL), full((D_MODEL, D_MODEL)), full((1, D_MODEL)),
                  full((16, D_MODEL)), full((tm, tm))],
        out_specs=[tok(D_MODEL), tok(D_MODEL // 2),
                   pl.BlockSpec((8, tm), lambda i: (0, i)), tok(128), full((16, 128))],
        out_shape=[jax.ShapeDtypeStruct((n, D_MODEL), F32), jax.ShapeDtypeStruct((n, D_MODEL // 2), jnp.int32),
                   jax.ShapeDtypeStruct((8, n), F32), jax.ShapeDtypeStruct((n, 128), F32),
                   jax.ShapeDtypeStruct((16, 128), F32)],
        scratch_shapes=[pltpu.VMEM((16, 128), F32)],
        compiler_params=pltpu.CompilerParams(dimension_semantics=("arbitrary",),
                                             vmem_limit_bytes=VMEM_LIMIT),
        name="router",
    )(x, mixed, wo, nw, wrt, upper)


def _sc_row_gather(table, idx):
    n_workers = SC_CORES * SC_SUBCORES
    b = idx.shape[0]
    d = table.shape[1]
    assert b % (8 * n_workers) == 0
    per_worker = b // n_workers
    chunk = max(c for c in range(8, SC_MAX_GATHER_ROWS + 1, 8) if per_worker % c == 0)
    mesh = plsc.VectorSubcoreMesh(core_axis_name="c", subcore_axis_name="s")

    @functools.partial(
        pl.kernel, mesh=mesh, out_type=jax.ShapeDtypeStruct((b, d), table.dtype),
        scratch_types=[pltpu.VMEM((chunk,), jnp.int32), pltpu.VMEM((chunk, d), table.dtype),
                       pltpu.SemaphoreType.DMA])
    def gather(table_hbm, idx_hbm, out_hbm, idx_v, rows_v, sem):
        worker = lax.axis_index("s") * SC_CORES + lax.axis_index("c")
        base = worker * per_worker

        @pl.loop(0, per_worker // chunk)
        def _(c):
            off = pl.multiple_of(base + c * chunk, 8)
            pltpu.sync_copy(idx_hbm.at[pl.ds(off, chunk)], idx_v)
            pltpu.async_copy(table_hbm.at[idx_v], rows_v, sem).wait()
            pltpu.sync_copy(rows_v, out_hbm.at[pl.ds(off, chunk)])

    return gather(table, idx)


def _sc_row_scatter(rows, idx):
    n_workers = SC_CORES * SC_SUBCORES
    b = n_out = idx.shape[0]
    v, d = rows.shape
    assert b % (8 * n_workers) == 0
    per_worker = b // n_workers
    chunk = max(c for c in range(8, SC_MAX_GATHER_ROWS + 1, 8) if per_worker % c == 0 and v % c == 0)
    mesh = plsc.VectorSubcoreMesh(core_axis_name="c", subcore_axis_name="s")

    @functools.partial(
        pl.kernel, mesh=mesh, out_type=jax.ShapeDtypeStruct((n_out, d), rows.dtype),
        scratch_types=[pltpu.VMEM((chunk,), jnp.int32), pltpu.VMEM((chunk, d), rows.dtype)])
    def scatter(rows_hbm, idx_hbm, out_hbm, idx_v, rows_v):
        worker = lax.axis_index("s") * SC_CORES + lax.axis_index("c")
        base = worker * per_worker

        @pl.loop(0, per_worker // chunk)
        def _(c):
            off = pl.multiple_of(base + c * chunk, 8)
            pltpu.sync_copy(idx_hbm.at[pl.ds(off, chunk)], idx_v)
            pltpu.sync_copy(rows_hbm.at[pl.ds(pl.multiple_of(lax.rem(off, v), 8), chunk)], rows_v)
            pltpu.sync_copy(rows_v, out_hbm.at[idx_v])

    return scatter(rows, idx)


def _expert_ffn_kernel(texp_ref, tvalid_ref, xs_ref, wg_ref, wu_ref, wd_ref, o_ref, acc_ref, xb_ref):
    j = pl.program_id(0)
    f = pl.program_id(1)
    last = f == pl.num_programs(1) - 1
    valid = tvalid_ref[j] == 1
    half = D_MODEL // 2

    @pl.when(jnp.logical_and(valid, f == 0))
    def _():
        lo, hi = _unpack_bf16_pairs(xs_ref[...])
        xb_ref[:, :half] = lo.astype(BF16)
        xb_ref[:, half:] = hi.astype(BF16)

        acc_ref[...] = jnp.zeros_like(acc_ref)

    @pl.when(valid)
    def _():
        xs = xb_ref[...]
        g = jnp.dot(xs, wg_ref[0].astype(BF16), preferred_element_type=F32)
        u = jnp.dot(xs, wu_ref[0].astype(BF16), preferred_element_type=F32)
        act = (_silu(g) * u).astype(BF16)
        acc_ref[...] += jnp.dot(act, wd_ref[0].astype(BF16), preferred_element_type=F32)

        @pl.when(last)
        def _():
            o_ref[...] = _pack_bf16_pairs(acc_ref[...])

    @pl.when(jnp.logical_and(jnp.logical_not(valid), last))
    def _():
        o_ref[...] = jnp.zeros_like(o_ref)


def _expert_ffn(f_exp, f_valid, xs, wg, wu, wd):
    n_slots = xs.shape[0]
    d_ff = wg.shape[2]
    tf = _pick_tile(d_ff, (512, 256, 128))
    n_f = d_ff // tf

    def col(j, f, tv):
        return jnp.where(tv[j] == 1, f, n_f - 1)

    grid_spec = pltpu.PrefetchScalarGridSpec(
        num_scalar_prefetch=2, grid=(n_slots // FFN_TILE, d_ff // tf),
        in_specs=[pl.BlockSpec((FFN_TILE, D_MODEL // 2), lambda j, f, te, tv: (j, 0)),
                  pl.BlockSpec((1, D_MODEL, tf), lambda j, f, te, tv: (te[j], 0, col(j, f, tv))),
                  pl.BlockSpec((1, D_MODEL, tf), lambda j, f, te, tv: (te[j], 0, col(j, f, tv))),
                  pl.BlockSpec((1, tf, D_MODEL), lambda j, f, te, tv: (te[j], col(j, f, tv), 0))],
        out_specs=pl.BlockSpec((FFN_TILE, D_MODEL // 2), lambda j, f, te, tv: (j, 0)),
        scratch_shapes=[pltpu.VMEM((FFN_TILE, D_MODEL), F32), pltpu.VMEM((FFN_TILE, D_MODEL), BF16)])
    return pl.pallas_call(
        _expert_ffn_kernel, grid_spec=grid_spec,
        out_shape=jax.ShapeDtypeStruct((n_slots, D_MODEL // 2), jnp.int32),
        compiler_params=pltpu.CompilerParams(dimension_semantics=("arbitrary", "arbitrary"),
                                             vmem_limit_bytes=VMEM_LIMIT),
        name="expert_ffn",
    )(f_exp, f_valid, xs, wg, wu, wd)


def _combine_kernel(part_tiles, x1_ref, o1_ref, o2_ref, gate_ref, fnw_ref, *y_refs):
    i = pl.program_id(0)
    lo1, hi1 = _unpack_bf16_pairs(o1_ref[...])
    lo2, hi2 = _unpack_bf16_pairs(o2_ref[...])
    g1 = gate_ref[:, 0:1]
    g2 = gate_ref[:, 1:2]
    half = D_MODEL // 2
    x1 = x1_ref[...]
    fnw = fnw_ref[...]
    ya = x1[:, :half] + g1 * lo1 + g2 * lo2
    yb = x1[:, half:] + g1 * hi1 + g2 * hi2
    scale = lax.rsqrt((jnp.sum(ya * ya, axis=-1, keepdims=True) + jnp.sum(yb * yb, axis=-1, keepdims=True))
                      / D_MODEL + EPS)
    y = jnp.concatenate([ya * scale * fnw[:, :half], yb * scale * fnw[:, half:]], axis=1)
    lo_tile = 0
    for y_ref, tiles in zip(y_refs, part_tiles):
        @pl.when(jnp.logical_and(i >= lo_tile, i < lo_tile + tiles))
        def _(y_ref=y_ref):
            _store_part_tile(y_ref, y)
        lo_tile += tiles


def _combine(x1, o12, gates, fnw, part_shapes):
    n = x1.shape[0]
    parts = [jax.ShapeDtypeStruct(s, F32) for s in part_shapes]
    tm = _pick_tile(np.gcd.reduce([_part_rows(p) for p in parts]), TOKEN_TILES)
    n_tiles = n // tm
    out_specs, bounds = _token_parts_specs(parts, tm)
    part_tiles = [bounds[k + 1] - bounds[k] for k in range(len(parts))]
    return pl.pallas_call(
        functools.partial(_combine_kernel, part_tiles),
        grid=(n_tiles,),
        in_specs=[pl.BlockSpec((tm, D_MODEL), lambda i: (i, 0)),
                  pl.BlockSpec((tm, D_MODEL // 2), lambda i: (i, 0)),
                  pl.BlockSpec((tm, D_MODEL // 2), lambda i: (i + n_tiles, 0)),
                  pl.BlockSpec((tm, 128), lambda i: (i, 0)),
                  pl.BlockSpec((1, D_MODEL), lambda i: (0, 0))],
        out_specs=out_specs,
        out_shape=parts,
        compiler_params=pltpu.CompilerParams(dimension_semantics=("arbitrary",),
                                             vmem_limit_bytes=VMEM_LIMIT),
        name="combine",
    )(x1, o12, o12, gates, fnw)


def _moe(x, mixed, wo, nw, router_w, wg, wu, wd, fnw, part_shapes):
    i32 = jnp.int32
    n = x.shape[0]
    assert n % TOK_BLK == 0
    wrt = jnp.pad(router_w.T, ((0, 16 - N_EXPERTS), (0, 0))).astype(BF16)
    t = np.arange(TOK_BLK)
    upper = jnp.asarray((t[:, None] < t[None, :]).astype(np.float32), dtype=BF16)
    x1, hn_p, meta, gates, count = _router(x, mixed, wo, nw, wrt, upper)

    sc_quantum = int(np.lcm(FFN_TILE, SC_CORES * SC_SUBCORES * SC_MAX_GATHER_ROWS))
    n_slots = -(-(-(-2 * n // FFN_TILE) + N_EXPERTS) * FFN_TILE // sc_quantum) * sc_quantum
    counts = count[:N_EXPERTS, 0].astype(i32)
    gsize = (counts + FFN_TILE - 1) // FFN_TILE * FFN_TILE
    gend = jnp.cumsum(gsize)
    goff = gend - gsize
    fstart = jnp.arange(n_slots // FFN_TILE, dtype=i32) * FFN_TILE
    f_valid = (fstart < gend[-1]).astype(i32)
    f_exp = jnp.sum((fstart[:, None] >= gend[None, :]).astype(i32), axis=1)
    f_exp = jnp.minimum(f_exp, jnp.sum((gend[-1] - 1 >= gend).astype(i32)))
    expert_ids = jnp.arange(N_EXPERTS, dtype=i32)[:, None]
    top = meta[0:2].astype(i32)
    rank = meta[2:4].astype(i32)
    slot = jnp.stack([jnp.sum(jnp.where(top[k][None, :] == expert_ids, goff[:, None], 0), axis=0) + rank[k]
                      for k in range(2)]).reshape(-1)
    pad = gsize - counts
    cpad = jnp.cumsum(pad)
    j = jnp.arange(n_slots - 2 * n, dtype=i32)
    pad_e = jnp.sum((j[:, None] >= cpad[None, :]).astype(i32), axis=1)
    pad_base = jnp.sum(jnp.where(pad_e[:, None] == expert_ids.T, (goff + counts - (cpad - pad))[None, :], 0), axis=1)
    pad_slot = jnp.where(pad_e < N_EXPERTS, pad_base + j, gend[-1] + j - cpad[-1])

    xs_p = _sc_row_scatter(hn_p, jnp.concatenate([slot, pad_slot]))
    out_p = _expert_ffn(f_exp, f_valid, xs_p, wg, wu, wd)
    o12 = _sc_row_gather(out_p, slot)
    return _combine(x1, o12, gates, fnw, part_shapes)


def _rope_table(seq_len, dec_len):
    half = ROPE_DIMS // 2
    pos = jnp.concatenate([jnp.arange(seq_len), PAST_LEN + jnp.arange(dec_len)]).astype(F32)
    inv_freq = jnp.power(jnp.float32(ROPE_THETA), -jnp.arange(half, dtype=F32) / half)
    lane = np.arange(128) % HEAD
    inv_lane = jnp.where(lane < ROPE_DIMS, inv_freq[lane % half], 0.0)
    ang = pos[:, None] * inv_lane[None, :]
    cos, sin = jnp.cos(ang), jnp.sin(ang)
    sin_a = jnp.where(lane < half, -sin, 0.0)
    sin_b = jnp.where((lane >= half) & (lane < ROPE_DIMS), sin, 0.0)
    return jnp.concatenate([cos, sin_a, sin_b], axis=1)


def _mixer_consts():
    t = np.arange(CHUNK)
    tri = (t[:, None] >= t[None, :])
    same = (t[:, None] // SUB) == (t[None, :] // SUB)
    c = np.arange(MIX_W)
    ind = (c[:, None] // HEAD) == (c[None, :] // HEAD)
    as_bf = lambda m: jnp.asarray(m.astype(np.float32), dtype=BF16)
    return [as_bf(tri), as_bf(tri & same), as_bf(same), as_bf(ind)]


def _mlstm_state_to_pairs(c, n):
    c = c.astype(F32)
    n_b = jnp.broadcast_to(n.astype(F32)[..., None], c.shape)
    z = jnp.zeros_like(c[:, 0::2])
    top = jnp.concatenate([c[:, 0::2], z, n_b[:, 0::2], z], axis=-1)
    bottom = jnp.concatenate([z, c[:, 1::2], z, n_b[:, 1::2]], axis=-1)
    return jnp.concatenate([top, bottom], axis=-2)


def _mlstm_state_from_pairs(s):
    b = s.shape[0]
    c = jnp.stack([s[:, :, a * HEAD:(a + 1) * HEAD, a * HEAD:(a + 1) * HEAD] for a in range(2)], axis=2)
    n = jnp.stack([s[:, :, a * HEAD:(a + 1) * HEAD, (2 + a) * HEAD] for a in range(2)], axis=2)
    return (c.reshape(b, N_HEADS, HEAD, HEAD), n.reshape(b, N_HEADS, HEAD))


def _pad_lanes(v, width):
    return jnp.pad(v, ((0, 0), (0, width - v.shape[1])))


def kernel(x_prompt, x_sample, state_mlstm_C, state_mlstm_n, state_mlstm_m, state_hgrn_S, cache_swa_k, cache_swa_v, state_ssd_h, state_ssd_conv, norm1_w, w_in, ml_ig_b, ml_fg_b, ml_norm_w, hg_lb_logits, hg_norm_w, sw_sinks, ssd_conv_w, ssd_conv_b, ssd_dt_bias, ssd_A_log, ssd_D, ssd_norm_w, w_out, norm2_w, ffn_w_gate, ffn_w_up, ffn_w_down, moe_router, moe_w_gate, moe_w_up, moe_w_down, final_norm_w):
    depth = w_in.shape[0]
    bp, seq_len, _ = x_prompt.shape
    bs, dec_len, _ = x_sample.shape
    assert seq_len % CHUNK == 0 and dec_len == CHUNK and depth % 2 == 0
    n_seq = bp + bs
    chunks_per_prompt = seq_len // CHUNK

    assert bp == SEQ_GROUP and bs % SEQ_GROUP == 0
    x_parts = (x_prompt.reshape(bp, chunks_per_prompt, CHUNK, D_MODEL), x_sample.reshape(bs * dec_len, D_MODEL))
    part_shapes = tuple(p.shape for p in x_parts)
    rope_tab = _rope_table(seq_len, dec_len)
    consts = _mixer_consts()

    sm = jax.nn.softmax(hg_lb_logits.astype(F32), axis=0)
    hg_lb = jnp.cumsum(sm, axis=0) - sm[0]

    state_outs = []
    for l in range(depth):
        lp = {
            "sinks": sw_sinks[l].astype(F32),
            "gbias": _pad_lanes(jnp.concatenate([ml_ig_b[l], ml_fg_b[l], ssd_dt_bias[l]])[None, :].astype(F32), 128),
            "alog": _pad_lanes(ssd_A_log[l][None, :].astype(F32), 128),
            "mlnw": ml_norm_w[l][None, :].astype(F32),
            "hglb": hg_lb[l][None, :],
            "hgnw": hg_norm_w[l][None, :].astype(F32),
            "convw": ssd_conv_w[l].astype(F32),
            "convb": ssd_conv_b[l][None, :].astype(F32),
            "ssdd": jnp.repeat(ssd_D[l].astype(F32), HEAD)[None, :],
            "ssdnw": ssd_norm_w[l][None, :].astype(F32),
        }
        c_aug = _mlstm_state_to_pairs(state_mlstm_C[l], state_mlstm_n[l])
        m_pad = jnp.pad(state_mlstm_m[l][:, None, :], ((0, 0), (0, 7), (0, 128 - N_HEADS)))
        s_t = jnp.swapaxes(state_hgrn_S[l], -1, -2)
        s_t = jnp.moveaxis(s_t, 1, 2).reshape(bs, HEAD, MIX_W)
        cv_pad = jnp.pad(state_ssd_conv[l], ((0, 0), (8 - (SSD_CONV - 1), 0), (0, 0)))
        init = [a.astype(F32) for a in (
            c_aug, m_pad, s_t, cache_swa_k[l].reshape(bs, WINDOW, 128), cache_swa_v[l].reshape(bs, WINDOW, 128),
            state_ssd_h[l], cv_pad)]

        proj = _inproj(x_parts, norm1_w[l][None, :], w_in, l)
        mixed, c_o, m_o, s_o, k_o, v_o, h_o, cv_o = _mixer(
            proj, rope_tab, init, lp, consts, chunks_per_prompt, n_seq)
        wo = w_out[l].astype(BF16)
        j = l // 2
        if l % 2 == 0:
            x_parts = (_dense_ffn(x_parts, mixed, wo, norm2_w[l][None, :], ffn_w_gate[j].astype(BF16),
                                  ffn_w_up[j].astype(BF16), ffn_w_down[j].astype(BF16)),)
        else:
            assert l == depth - 1 and len(x_parts) == 1
            y_parts = _moe(x_parts[0], mixed, wo, norm2_w[l][None, :], moe_router[j], moe_w_gate[j],
                           moe_w_up[j], moe_w_down[j], final_norm_w[None, :], part_shapes)
        s_back = jnp.moveaxis(s_o.reshape(n_seq, HEAD, N_HEADS, HEAD), 2, 1)
        state_outs.append(_mlstm_state_from_pairs(c_o) + (m_o[:, 0, :N_HEADS],
                           jnp.swapaxes(s_back, -1, -2),
                           k_o.reshape(n_seq, WINDOW, SW_KV, HEAD), v_o.reshape(n_seq, WINDOW, SW_KV, HEAD),
                           h_o, cv_o[:, 8 - (SSD_CONV - 1):, :]))

    y_prompt = y_parts[0].reshape(bp, seq_len, D_MODEL)
    y_sample = y_parts[1].reshape(bs, dec_len, D_MODEL)
    stacked = [jnp.stack([so[k] for so in state_outs]) for k in range(8)]
    return (y_prompt, y_sample) + tuple(s[:, :bp] for s in stacked) + tuple(s[:, bp:] for s in stacked)
```

```python
import functools

import numpy as np
import jax
import jax.numpy as jnp
from jax import lax
from jax.experimental import pallas as pl
from jax.experimental.pallas import tpu as pltpu
from jax.experimental.pallas import tpu_sc as plsc

F32 = jnp.float32
BF16 = jnp.bfloat16

D_MODEL = 1024
CHUNK = 64
EPS = 1e-6
N_HEADS = 4
HEAD = 64
MIX_W = N_HEADS * HEAD
SW_KV = 2
WINDOW = 128
ROPE_DIMS = 16
ROPE_THETA = 500000.0
SSD_N = 128
SSD_CONV = 4
SSD_CONV_DIM = 768
PAST_LEN = 4096
N_EXPERTS = 8
SUB = 16
SEQ_GROUP = 2
HG_SLICE = 4
TOKEN_TILES = (512, 256, 128)

C_MLQ, C_MLK, C_MLV, C_MLO = 0, 256, 512, 768
C_HGQ, C_HGF, C_HGI, C_HGG = 1024, 1280, 1536, 1792
C_SWQ, C_SWK, C_SWV = 2048, 2304, 2432
C_SDZ, C_SDX, C_GATE = 2560, 2816, 3584
D_PROJ = 3712
W_IN_COLS = C_GATE + 3 * N_HEADS
NEG = -1e30
V7X_VMEM_BYTES = 64 * 1024 * 1024
VMEM_LIMIT = V7X_VMEM_BYTES - 8 * 1024 * 1024
SC_CORES = 2
SC_SUBCORES = 16
SC_MAX_GATHER_ROWS = 128


def _mm(a, b):
    return jnp.dot(a.astype(BF16), b.astype(BF16), preferred_element_type=F32)


def _mm_nt(a, b):
    return lax.dot_general(a.astype(BF16), b.astype(BF16), (((1,), (1,)), ((), ())),
                           preferred_element_type=F32)


def _mm_tn(a, b):
    return jnp.dot(a.T.astype(BF16), b.astype(BF16), preferred_element_type=F32)


def _mm_sel(sel, x):
    hi = x.astype(BF16)
    r1 = x - hi.astype(F32)
    mid = r1.astype(BF16)
    lo = (r1 - mid.astype(F32)).astype(BF16)
    return (jnp.dot(sel, hi, preferred_element_type=F32)
            + jnp.dot(sel, mid, preferred_element_type=F32)
            + jnp.dot(sel, lo, preferred_element_type=F32))


def _sigmoid(x):
    return 1.0 / (1.0 + jnp.exp(-x))


def _silu(x):
    return x * _sigmoid(x)


def _softplus(x):
    return jnp.maximum(x, 0.0) + jnp.log(1.0 + jnp.exp(-jnp.abs(x)))


def _rms(x, w):
    return x * lax.rsqrt(jnp.mean(x * x, axis=-1, keepdims=True) + EPS) * w


def _pick_tile(n, candidates):
    for c in candidates:
        if n % c == 0:
            return c
    raise ValueError(f"no tile for {n}")


def _token_parts_specs(parts, tm):
    bounds = np.cumsum([0] + [_part_rows(p) // tm for p in parts])
    specs = []
    for k, p in enumerate(parts):
        assert _part_rows(p) % tm == 0
        lo, hi = int(bounds[k]), int(bounds[k + 1])
        specs.append(_part_block_spec(p, tm, lo, hi - lo))
    return specs, [int(b) for b in bounds]


def _part_rows(p):
    return int(np.prod(p.shape[:-1]))


def _part_block_spec(p, tm, first_tile, n_tiles):
    idx = lambda i: jnp.clip(i - first_tile, 0, n_tiles - 1)
    if len(p.shape) == 2:
        return pl.BlockSpec((tm, p.shape[1]), lambda i: (idx(i), 0))
    grp, _, chunk, d = p.shape
    assert tm % (grp * chunk) == 0
    return pl.BlockSpec((grp, tm // (grp * chunk), chunk, d), lambda i: (0, idx(i), 0, 0))


def _load_part_tile(ref):
    if len(ref.shape) == 2:
        return ref[...]
    grp, n_chunks = ref.shape[:2]
    return jnp.concatenate([ref[s, c] for c in range(n_chunks) for s in range(grp)], axis=0)


def _store_part_tile(ref, x):
    if len(ref.shape) == 2:
        ref[...] = x
        return
    grp, n_chunks, chunk = ref.shape[:3]
    for c in range(n_chunks):
        for s in range(grp):
            r0 = (c * grp + s) * chunk
            ref[s, c] = x[r0:r0 + chunk]


def _read_token_tile(refs, bounds):
    i = pl.program_id(0)
    x = _load_part_tile(refs[-1])
    for k in range(len(refs) - 2, -1, -1):
        x = jnp.where(i < bounds[k + 1], _load_part_tile(refs[k]), x)
    return x


def _inproj_kernel(n_parts, bounds, *refs):
    x_refs = refs[:n_parts]
    nw_ref, w_ref, o_ref, wp_ref = refs[n_parts:]

    @pl.when(pl.program_id(0) == 0)
    def _():
        rows = 128
        ml_gates = slice(C_MLO, C_MLO + 2 * N_HEADS)
        sd_dt = slice(W_IN_COLS - N_HEADS, W_IN_COLS)
        n_gates = 3 * N_HEADS
        for r0 in range(0, D_MODEL, rows):
            w = w_ref[0, r0:r0 + rows, :]
            wp_ref[r0:r0 + rows, 0:C_MLO] = w[:, 0:ml_gates.start].astype(BF16)
            wp_ref[r0:r0 + rows, C_MLO:C_GATE] = w[:, ml_gates.stop:sd_dt.start].astype(BF16)
            gates = jnp.concatenate([w[:, ml_gates], w[:, sd_dt],
                                     jnp.zeros((rows, D_PROJ - C_GATE - n_gates), F32)], axis=1)
            wp_ref[r0:r0 + rows, C_GATE:D_PROJ] = gates.astype(BF16)

    xn = _rms(_read_token_tile(x_refs, bounds), nw_ref[...]).astype(BF16)
    o_ref[...] = jnp.dot(xn, wp_ref[...], preferred_element_type=F32)


def _inproj(x_parts, nw, w_all, layer):
    n = sum(_part_rows(p) for p in x_parts)
    tm = _pick_tile(np.gcd.reduce([_part_rows(p) for p in x_parts]), TOKEN_TILES)
    x_specs, bounds = _token_parts_specs(x_parts, tm)
    assert w_all.shape[1:] == (D_MODEL, W_IN_COLS)
    return pl.pallas_call(
        functools.partial(_inproj_kernel, len(x_parts), bounds),
        grid=(n // tm,),
        in_specs=x_specs + [pl.BlockSpec((1, D_MODEL), lambda i: (0, 0)),
                            pl.BlockSpec((1, D_MODEL, W_IN_COLS), lambda i: (layer, 0, 0),
                                         pipeline_mode=pl.Buffered(1))],
        out_specs=pl.BlockSpec((tm, D_PROJ), lambda i: (i, 0)),
        out_shape=jax.ShapeDtypeStruct((n, D_PROJ), F32),
        scratch_shapes=[pltpu.VMEM((D_MODEL, D_PROJ), BF16)],
        compiler_params=pltpu.CompilerParams(dimension_semantics=("arbitrary",),
                                             vmem_limit_bytes=VMEM_LIMIT),
        name="inproj",
    )(*x_parts, nw, w_all)


def _head_norm(x, width):
    parts = []
    for g in range(x.shape[1] // width):
        xg = x[:, g * width:(g + 1) * width]
        parts.append(xg * lax.rsqrt(jnp.mean(xg * xg, axis=-1, keepdims=True) + EPS))
    return jnp.concatenate(parts, axis=1)


def _mixer_kernel(prompt_steps,
                  sinks_ref,
                  proj_ref, rope_ref, c0_ref, m0_ref, s0_ref, k0_ref, v0_ref, h0_ref, cv0_ref,
                  gbias_ref, alog_ref, mlnw_ref, hglb_ref, hgnw_ref,
                  convw_ref, convb_ref, ssdd_ref, ssdnw_ref,
                  tri_ref, triblk_ref, blkones_ref, ind_ref,
                  mixed_ref, c_ref, m_ref, s_ref, k_ref, v_ref, h_ref, cv_ref,
                  convbuf, bloc_s, kk_s, hv_s):
    i = pl.program_id(0)
    is_prompt = i < prompt_steps
    n_valid = jnp.where(is_prompt, jnp.minimum(i * CHUNK, WINDOW), WINDOW)
    state_refs = (c_ref, m_ref, s_ref, k_ref, v_ref, h_ref, cv_ref)

    @pl.when(i == 0)
    def _():
        for ref in state_refs:
            ref[...] = jnp.zeros_like(ref)

    @pl.when(jnp.logical_not(is_prompt))
    def _():
        for ref, init_ref in zip(state_refs, (c0_ref, m0_ref, s0_ref, k0_ref, v0_ref, h0_ref, cv0_ref)):
            ref[...] = init_ref[...]

    members = [_mixer_chunk(g, n_valid, sinks_ref, proj_ref, rope_ref,
                            gbias_ref, alog_ref, mlnw_ref, hglb_ref, hgnw_ref,
                            convw_ref, convb_ref, ssdd_ref, ssdnw_ref,
                            tri_ref, triblk_ref, blkones_ref, ind_ref,
                            mixed_ref, c_ref, m_ref, s_ref, k_ref, v_ref, h_ref, cv_ref,
                            convbuf, bloc_s, kk_s, hv_s) for g in range(SEQ_GROUP)]
    for phase in zip(*[m[0] for m in members]):
        chains = [c for per_member in zip(*phase) for c in per_member]
        while chains:
            alive = []
            for chain in chains:
                if next(chain, _CHAIN_DONE) is not _CHAIN_DONE:
                    alive.append(chain)
            chains = alive
    for _, finish in members:
        finish()


_CHAIN_DONE = object()


def _mixer_chunk(g, n_valid, sinks_ref, proj_ref, rope_ref,
                 gbias_ref, alog_ref, mlnw_ref, hglb_ref, hgnw_ref,
                 convw_ref, convb_ref, ssdd_ref, ssdnw_ref,
                 tri_ref, triblk_ref, blkones_ref, ind_ref,
                 mixed_ref, c_ref, m_ref, s_ref, k_ref, v_ref, h_ref, cv_ref,
                 convbuf, bloc_s, kk_s, hv_s):
    rs = slice(g * CHUNK, (g + 1) * CHUNK)
    L = CHUNK
    row = lax.broadcasted_iota(jnp.int32, (L, L), 0)
    col = lax.broadcasted_iota(jnp.int32, (L, L), 1)
    causal = row >= col
    tri = tri_ref[...]

    gate = proj_ref[rs,C_GATE:C_GATE + 128] + gbias_ref[...]
    ig = gate
    lf = pltpu.roll(-_softplus(-gate), 124, 1)
    dtv = pltpu.roll(_softplus(gate), 120, 1)

    b = _mm_sel(tri, lf)
    a = ig - b
    cm = a
    rowg = lax.broadcasted_iota(jnp.int32, (L, 128), 0)
    for sh in (1, 2, 4, 8, 16, 32):
        cm = jnp.where(rowg >= sh, jnp.maximum(cm, pltpu.roll(cm, sh, 0)), cm)
    m_prev = m_ref[g, 0:1, :]
    m_t = b + jnp.maximum(m_prev, cm)
    inter = jnp.exp(b + m_prev - m_t)
    bm = b - m_t
    a_t = a.T
    m_last = m_t[L - 1:L, :]
    b_last = b[L - 1:L, :]
    ws = jnp.exp(b_last + a - m_last)
    decay = jnp.exp(b_last + m_prev - m_last)
    emt = jnp.exp(-m_t)
    m_ref[g, 0:1, :] = m_last

    lane2 = lax.broadcasted_iota(jnp.int32, (L, 2 * HEAD), 1)
    low = lane2 < HEAD
    causal2 = lax.broadcasted_iota(jnp.int32, (L, 2 * HEAD), 0) >= (lane2 & (HEAD - 1))
    ones_blk = ind_ref[0:2 * HEAD, 0:2 * HEAD]
    ones_rows = jnp.ones((L, 2 * HEAD), BF16)
    st_row = lax.broadcasted_iota(jnp.int32, (2 * HEAD, 4 * HEAD), 0) // HEAD
    st_lane = (lax.broadcasted_iota(jnp.int32, (2 * HEAD, 4 * HEAD), 1) // HEAD) % 2
    ml_out = [None] * (N_HEADS // 2)

    def pair_cols(x, p):
        return jnp.where(low, x[:, 2 * p:2 * p + 1], x[:, 2 * p + 1:2 * p + 2])

    def split_rows(x):
        return jnp.concatenate([jnp.where(low, x, 0.0), jnp.where(low, 0.0, x)], axis=0)

    def ml_pair(p):
        cs = slice(2 * p * HEAD, 2 * (p + 1) * HEAD)
        q = proj_ref[rs, C_MLQ + cs.start:C_MLQ + cs.stop]
        k = proj_ref[rs, C_MLK + cs.start:C_MLK + cs.stop] * (HEAD ** -0.5)
        v = proj_ref[rs, C_MLV + cs.start:C_MLV + cs.stop]
        qk = _mm_nt(q, split_rows(k))
        yield
        st = c_ref[g, p]
        from_state = _mm(q, st)
        yield
        a_row = jnp.concatenate([a_t[2 * p:2 * p + 1, :], a_t[2 * p + 1:2 * p + 2, :]], axis=1)
        w = jnp.where(causal2, jnp.exp(pair_cols(bm, p) + a_row), 0.0)
        rhs = jnp.concatenate([split_rows(v).astype(BF16), ones_blk], axis=1)
        gate_in = pair_cols(inter, p)
        nd = (jnp.dot((w * qk).astype(BF16), rhs, preferred_element_type=F32)
              + jnp.concatenate([gate_in, gate_in], axis=1) * from_state)
        yield
        kws = k * pair_cols(ws, p)
        upd = jnp.dot(kws.T.astype(BF16), jnp.concatenate([v.astype(BF16), ones_rows], axis=1),
                      preferred_element_type=F32)
        yield
        den = jnp.maximum(jnp.abs(nd[:, 2 * HEAD:]), pair_cols(emt, p))
        ml_out[p] = nd[:, :2 * HEAD] / den
        dec = jnp.where(st_row == 0, decay[:, 2 * p:2 * p + 1], decay[:, 2 * p + 1:2 * p + 2])
        c_ref[g, p] = dec * st + jnp.where(st_row == st_lane, upd, 0.0)

    def ml_finish():
        ml = _head_norm(jnp.concatenate(ml_out, axis=1), HEAD)
        mixed_ref[rs,0:MIX_W] = (_sigmoid(proj_ref[rs,C_MLO:C_MLO + MIX_W]) * ml
                                 * mlnw_ref[...]).astype(mixed_ref.dtype)

    lb = hglb_ref[...]
    fg = lb + (1.0 - lb) * _sigmoid(proj_ref[rs,C_HGF:C_HGF + MIX_W])
    hq = _silu(proj_ref[rs,C_HGQ:C_HGQ + MIX_W]) * (HEAD ** -0.5)
    kk = 1.0 - fg
    lfg = jnp.log(fg)
    hv = proj_ref[rs,C_HGI:C_HGI + MIX_W]
    bloc = _mm_sel(triblk_ref[...], lfg)
    blast = _mm_sel(blkones_ref[...], lfg)
    qe = hq * jnp.exp(bloc)
    kw = kk * jnp.exp(blast - bloc)
    bloc_s[g] = bloc
    kk_s[g] = kk
    hv_s[g] = hv
    ind = ind_ref[...]
    rows = lax.broadcasted_iota(jnp.int32, (SUB, MIX_W), 0)
    n_blk = L // SUB
    hg_state_part = [None] * n_blk
    hg_block_part = [None] * n_blk

    def hg_within_block(blk):
        r0 = blk * SUB
        q_blk = hq[r0:r0 + SUB]
        b_blk = bloc[r0:r0 + SUB]
        xs = []
        for s in range(SUB):
            r = r0 + s
            e = jnp.exp(jnp.minimum(b_blk - bloc_s[g, r:r + 1, :], 0.0))
            xs.append((q_blk * (kk_s[g, r:r + 1, :] * e)).astype(BF16))
            if s % HG_SLICE == HG_SLICE - 1:
                yield
        att = jnp.dot(jnp.concatenate(xs, axis=0), ind, preferred_element_type=F32)
        yield
        o_blk = att[0:SUB] * hv_s[g, r0:r0 + 1, :]
        for s in range(1, SUB):
            r = r0 + s
            o_blk = o_blk + jnp.where(rows >= s, att[s * SUB:(s + 1) * SUB], 0.0) * hv_s[g, r:r + 1, :]
            if s % HG_SLICE == HG_SLICE - 1:
                yield
        hg_block_part[blk] = o_blk

    def hg_state_chain():
        st = s_ref[g]
        for blk in range(n_blk):
            r0 = blk * SUB
            hg_state_part[blk] = jnp.concatenate(
                [_mm_nt(qe[r0:r0 + SUB, h * HEAD:(h + 1) * HEAD], st[:, h * HEAD:(h + 1) * HEAD])
                 for h in range(N_HEADS)], axis=1)
            upd = jnp.concatenate(
                [_mm_tn(hv[r0:r0 + SUB, h * HEAD:(h + 1) * HEAD], kw[r0:r0 + SUB, h * HEAD:(h + 1) * HEAD])
                 for h in range(N_HEADS)], axis=1)
            yield
            st = st * jnp.exp(blast[r0:r0 + 1, :]) + upd
        s_ref[g] = st

    def hg_finish():
        o = jnp.concatenate([hg_state_part[blk] + hg_block_part[blk] for blk in range(n_blk)], axis=0)
        mixed_ref[rs,MIX_W:2 * MIX_W] = (_head_norm(o, HEAD) * hgnw_ref[...]
                                         * _silu(proj_ref[rs,C_HGG:C_HGG + MIX_W])).astype(mixed_ref.dtype)

    @functools.lru_cache(maxsize=None)
    def sw_prep():
        cos = rope_ref[:, 0:128]
        sin_a = rope_ref[:, 128:256]
        sin_b = rope_ref[:, 256:384]

        def rope(x):
            return x * cos + pltpu.roll(x, 120, 1) * sin_a + pltpu.roll(x, 8, 1) * sin_b

        q_rot = [rope(proj_ref[rs,C_SWQ:C_SWQ + 128]), rope(proj_ref[rs,C_SWQ + 128:C_SWQ + 256])]
        k_rot = rope(proj_ref[rs,C_SWK:C_SWK + 128])
        v_cur = proj_ref[rs,C_SWV:C_SWV + 128]
        colw = lax.broadcasted_iota(jnp.int32, (2 * L, WINDOW), 1)
        row2 = lax.broadcasted_iota(jnp.int32, (2 * L, 1), 0)
        return q_rot, k_rot, v_cur, k_ref[g], v_ref[g], colw >= WINDOW - n_valid, row2

    sw_prep()
    sw_out = [None] * (2 * SW_KV)

    def sw_group(kv):
        q_rot, k_rot, v_cur, k_prev, v_prev, prev_ok, row2 = sw_prep()
        gs = slice(kv * HEAD, (kv + 1) * HEAD)
        q2 = jnp.concatenate([q_rot[kv][:, 0:HEAD], q_rot[kv][:, HEAD:2 * HEAD]], axis=0) * (HEAD ** -0.5)
        s_p = jnp.where(prev_ok, _mm_nt(q2, k_prev[:, gs]), NEG)
        yield
        s_c = _mm_nt(q2, k_rot[:, gs])
        yield
        sink = jnp.where(row2 < L, sinks_ref[2 * kv], sinks_ref[2 * kv + 1])
        mx = jnp.maximum(jnp.maximum(jnp.max(s_p, axis=-1, keepdims=True),
                                     jnp.max(s_c, axis=-1, keepdims=True)), sink)
        e_p = jnp.exp(s_p - mx)
        e_c = jnp.exp(s_c - mx)
        den = (jnp.sum(e_p, axis=-1, keepdims=True) + jnp.sum(e_c, axis=-1, keepdims=True)
               + jnp.exp(sink - mx))
        o_p = _mm(e_p, v_prev[:, gs])
        yield
        o = (o_p + _mm(e_c, v_cur[:, gs])) / den
        yield
        sw_out[2 * kv] = o[0:L]
        sw_out[2 * kv + 1] = o[L:2 * L]

    def sw_finish():
        _, k_rot, v_cur, k_prev, v_prev, _, _ = sw_prep()
        mixed_ref[rs,2 * MIX_W:3 * MIX_W] = jnp.concatenate(sw_out, axis=1).astype(mixed_ref.dtype)
        k_ref[g, 0:WINDOW - L, :] = k_prev[L:WINDOW]
        k_ref[g, WINDOW - L:WINDOW, :] = k_rot
        v_ref[g, 0:WINDOW - L, :] = v_prev[L:WINDOW]
        v_ref[g, WINDOW - L:WINDOW, :] = v_cur

    @functools.lru_cache(maxsize=None)
    def sd_prep():
        convbuf[g, 0:8, :] = cv_ref[g]
        convbuf[g, 8:8 + L, :] = proj_ref[rs,C_SDX:C_SDX + SSD_CONV_DIM]
        acc = convb_ref[...] + convbuf[g, 5:5 + L, :] * convw_ref[0:1, :]
        for j in range(1, SSD_CONV):
            acc = acc + convbuf[g, 5 + j:5 + j + L, :] * convw_ref[j:j + 1, :]
        xbc = _silu(acc)
        cv_ref[g] = convbuf[g, L:L + 8, :]
        a_neg = -jnp.exp(alog_ref[...])
        ca = _mm_sel(tri, dtv * a_neg)
        ca_last = ca[L - 1:L, :]
        return (xbc, xbc[:, 0:MIX_W], ca, ca.T, dtv.T, jnp.exp(ca), jnp.exp(ca_last - ca) * dtv,
                jnp.exp(ca_last))

    sd_prep()
    sd_out = [None] * (N_HEADS // 2)

    def sd_group(grp):
        xbc, xs_all, ca, ca_t, dt_t, eca, wss, sdec = sd_prep()
        bm_g = xbc[:, MIX_W + grp * SSD_N:MIX_W + (grp + 1) * SSD_N]
        cm_g = xbc[:, MIX_W + 2 * SSD_N + grp * SSD_N:MIX_W + 2 * SSD_N + (grp + 1) * SSD_N]
        e, o = 2 * grp, 2 * grp + 1
        x = xs_all[:, e * HEAD:(o + 1) * HEAD]
        cb2 = _mm_nt(cm_g, jnp.concatenate([bm_g, bm_g], axis=0))
        yield
        hst = h_ref[g, e:o + 1].reshape(2 * HEAD, SSD_N)
        from_state = _mm_nt(cm_g, hst)
        yield
        ca_row = jnp.concatenate([ca_t[e:e + 1, :], ca_t[o:o + 1, :]], axis=1)
        dt_row = jnp.concatenate([dt_t[e:e + 1, :], dt_t[o:o + 1, :]], axis=1)
        dec = jnp.where(causal2, jnp.exp(pair_cols(ca, grp) - ca_row), 0.0)
        sd_out[grp] = _mm(cb2 * dec * dt_row, split_rows(x)) + pair_cols(eca, grp) * from_state
        yield
        upd = _mm_tn(x * pair_cols(wss, grp), bm_g)
        yield
        keep = jnp.where(lax.broadcasted_iota(jnp.int32, (2 * HEAD, SSD_N), 0) < HEAD,
                         sdec[:, e:e + 1], sdec[:, o:o + 1])
        h_ref[g, e:o + 1] = (keep * hst + upd).reshape(2, HEAD, SSD_N)

    def sd_finish():
        xs_all = sd_prep()[1]
        y = jnp.concatenate(sd_out, axis=1) + ssdd_ref[...] * xs_all
        gated = y * _silu(proj_ref[rs,C_SDZ:C_SDZ + MIX_W])
        mixed_ref[rs,3 * MIX_W:4 * MIX_W] = (_head_norm(gated, 2 * HEAD)
                                             * ssdnw_ref[...]).astype(mixed_ref.dtype)

    chains = ([[hg_within_block(blk) for blk in range(n_blk)]]
              + [[ml_pair(p)] for p in range(N_HEADS // 2)] + [[hg_state_chain(), sd_group(0)]]
              + [[sd_group(1)]]
              + [[sw_group(kv)] for kv in range(SW_KV)])

    def finish():
        ml_finish()
        hg_finish()
        sw_finish()
        sd_finish()

    return chains, finish


def _mixer(proj, rope_tab, init, lp, consts, chunks_per_prompt, n_seq):
    n = proj.shape[0]
    rows = SEQ_GROUP * CHUNK
    n_steps = n // rows

    def group_of(i):
        return jnp.where(i < chunks_per_prompt, 0, i - chunks_per_prompt + 1)

    def rope_blk(i):
        return jnp.minimum(i, chunks_per_prompt)

    def per_seq(shape):
        nd = len(shape)
        return pl.BlockSpec((SEQ_GROUP,) + shape, lambda i, s: (group_of(i),) + (0,) * nd)

    def per_sample_seq(shape):
        nd = len(shape)
        return pl.BlockSpec((SEQ_GROUP,) + shape, lambda i, s: (jnp.maximum(group_of(i) - 1, 0),) + (0,) * nd)

    def const(shape):
        nd = len(shape)
        return pl.BlockSpec(shape, lambda i, s: (0,) * nd)

    state_shapes = [(N_HEADS // 2, 2 * HEAD, 4 * HEAD), (8, 128), (HEAD, MIX_W), (WINDOW, 128), (WINDOW, 128),
                    (N_HEADS, HEAD, SSD_N), (8, SSD_CONV_DIM)]
    in_specs = ([pl.BlockSpec((rows, D_PROJ), lambda i, s: (i, 0)),
                 pl.BlockSpec((CHUNK, 384), lambda i, s: (rope_blk(i), 0))]
                + [per_sample_seq(s) for s in state_shapes]
                + [const((1, 128)), const((1, 128)), const((1, MIX_W)), const((1, MIX_W)), const((1, MIX_W)),
                   const((SSD_CONV, SSD_CONV_DIM)), const((1, SSD_CONV_DIM)), const((1, MIX_W)),
                   const((1, MIX_W)),
                   const((CHUNK, CHUNK)), const((CHUNK, CHUNK)), const((CHUNK, CHUNK)), const((MIX_W, MIX_W))])
    out_specs = ([pl.BlockSpec((rows, D_MODEL), lambda i, s: (i, 0))]
                 + [per_seq(s) for s in state_shapes])
    out_shape = ([jax.ShapeDtypeStruct((n, D_MODEL), BF16)]
                 + [jax.ShapeDtypeStruct((n_seq,) + s, F32) for s in state_shapes])
    grid_spec = pltpu.PrefetchScalarGridSpec(
        num_scalar_prefetch=1, grid=(n_steps,), in_specs=in_specs, out_specs=out_specs,
        scratch_shapes=[pltpu.VMEM((SEQ_GROUP, CHUNK + 8, SSD_CONV_DIM), F32),
                        pltpu.VMEM((SEQ_GROUP, CHUNK, MIX_W), F32), pltpu.VMEM((SEQ_GROUP, CHUNK, MIX_W), F32),
                        pltpu.VMEM((SEQ_GROUP, CHUNK, MIX_W), F32)])
    return pl.pallas_call(
        functools.partial(_mixer_kernel, chunks_per_prompt),
        grid_spec=grid_spec, out_shape=out_shape,
        compiler_params=pltpu.CompilerParams(dimension_semantics=("arbitrary",),
                                             vmem_limit_bytes=VMEM_LIMIT),
        name="mixer",
    )(lp["sinks"], proj, rope_tab, *init,
      lp["gbias"], lp["alog"], lp["mlnw"], lp["hglb"], lp["hgnw"],
      lp["convw"], lp["convb"], lp["ssdd"], lp["ssdnw"], *consts)


def _dense_ffn_kernel(f_chunk, n_parts, bounds, *refs):
    x_refs = refs[:n_parts]
    mix_ref, wo_ref, nw_ref, wg_ref, wu_ref, wd_ref, o_ref = refs[n_parts:]
    x1 = _read_token_tile(x_refs, bounds) + jnp.dot(mix_ref[...], wo_ref[...], preferred_element_type=F32)
    hn = _rms(x1, nw_ref[...]).astype(BF16)
    o_ref[...] = x1
    for f0 in range(0, wg_ref.shape[1], f_chunk):
        g = jnp.dot(hn, wg_ref[:, f0:f0 + f_chunk], preferred_element_type=F32)
        u = jnp.dot(hn, wu_ref[:, f0:f0 + f_chunk], preferred_element_type=F32)
        act = (_silu(g) * u).astype(BF16)
        o_ref[...] += jnp.dot(act, wd_ref[f0:f0 + f_chunk, :], preferred_element_type=F32)


def _dense_ffn(x_parts, mixed, wo, nw, wg, wu, wd):
    n = mixed.shape[0]
    d_ff = wg.shape[1]
    tm = _pick_tile(np.gcd.reduce([_part_rows(p) for p in x_parts]), TOKEN_TILES)
    x_specs, bounds = _token_parts_specs(x_parts, tm)
    full = lambda shape: pl.BlockSpec(shape, lambda i: (0, 0))
    return pl.pallas_call(
        functools.partial(_dense_ffn_kernel, 256, len(x_parts), bounds),
        grid=(n // tm,),
        in_specs=x_specs + [
                  pl.BlockSpec((tm, D_MODEL), lambda i: (i, 0)),
                  full((D_MODEL, D_MODEL)), full((1, D_MODEL)),
                  full((D_MODEL, d_ff)), full((D_MODEL, d_ff)), full((d_ff, D_MODEL))],
        out_specs=pl.BlockSpec((tm, D_MODEL), lambda i: (i, 0)),
        out_shape=jax.ShapeDtypeStruct((n, D_MODEL), F32),
        compiler_params=pltpu.CompilerParams(dimension_semantics=("arbitrary",),
                                             vmem_limit_bytes=VMEM_LIMIT),
        name="dense_ffn",
    )(*x_parts, mixed, wo, nw, wg, wu, wd)


TOK_BLK = 1024
FFN_TILE = 1024


def _pack_bf16_pairs(x):
    w = x.shape[1] // 2
    bits = pltpu.bitcast(x.astype(BF16).astype(F32), jnp.int32)
    return lax.shift_right_logical(bits[:, :w], 16) | bits[:, w:]


def _unpack_bf16_pairs(p):
    lo = pltpu.bitcast(lax.shift_left(p, 16), F32)
    hi = pltpu.bitcast(p & jnp.int32(-65536), F32)
    return lo, hi


def _router_kernel(x_ref, mix_ref, wo_ref, nw_ref, wrt_ref, upper_ref,
                   x1_ref, hn_ref, meta_ref, gate_ref, count_ref, carry):
    @pl.when(pl.program_id(0) == 0)
    def _():
        carry[...] = jnp.zeros_like(carry)

    x1 = x_ref[...] + jnp.dot(mix_ref[...], wo_ref[...], preferred_element_type=F32)
    x1_ref[...] = x1
    hn_f = _rms(x1, nw_ref[...])
    hn = hn_f.astype(BF16)
    hn_ref[...] = _pack_bf16_pairs(hn_f)
    logits = lax.dot_general(wrt_ref[...], hn, (((1,), (1,)), ((), ())),
                             preferred_element_type=F32)
    sub = lax.broadcasted_iota(jnp.int32, logits.shape, 0)
    l1 = jnp.where(sub < N_EXPERTS, logits, NEG)
    m1 = jnp.max(l1, axis=0, keepdims=True)
    i1 = jnp.min(jnp.where(l1 == m1, sub, 99), axis=0, keepdims=True)
    l2 = jnp.where(sub == i1, NEG, l1)
    m2 = jnp.max(l2, axis=0, keepdims=True)
    i2 = jnp.min(jnp.where(l2 == m2, sub, 99), axis=0, keepdims=True)
    e2 = jnp.exp(m2 - m1)
    g1 = 1.0 / (1.0 + e2)
    g2 = e2 / (1.0 + e2)
    sel = jnp.where(sub == i1, 1.0, jnp.where(sub == i2, 1.0, 0.0))
    before = jnp.dot(sel.astype(BF16), upper_ref[...], preferred_element_type=F32) + carry[:, 0:1]
    r1 = jnp.sum(jnp.where(sub == i1, before, 0.0), axis=0, keepdims=True)
    r2 = jnp.sum(jnp.where(sub == i2, before, 0.0), axis=0, keepdims=True)
    meta_ref[...] = jnp.concatenate([i1.astype(F32), i2.astype(F32), r1, r2,
                                     jnp.zeros((4, r1.shape[1]), F32)], axis=0)
    gate_ref[...] = jnp.concatenate([g1, g2, jnp.zeros((126, r1.shape[1]), F32)], axis=0).T
    total = carry[...] + jnp.sum(sel, axis=1, keepdims=True)
    carry[...] = total
    count_ref[...] = total


def _router(x, mixed, wo, nw, wrt, upper):
    n = x.shape[0]
    tm = TOK_BLK
    full = lambda shape: pl.BlockSpec(shape, lambda i: (0,) * len(shape))
    tok = lambda w: pl.BlockSpec((tm, w), lambda i: (i, 0))
    return pl.pallas_call(
        _router_kernel,
        grid=(n // tm,),
        in_specs=[tok(D_MODEL), tok(D_MODEL), full((D_MODEL, D_MODEL)), full((1, D_MODEL)),
                  full((16, D_MODEL)), full((tm, tm))],
        out_specs=[tok(D_MODEL), tok(D_MODEL // 2),
                   pl.BlockSpec((8, tm), lambda i: (0, i)), tok(128), full((16, 128))],
        out_shape=[jax.ShapeDtypeStruct((n, D_MODEL), F32), jax.ShapeDtypeStruct((n, D_MODEL // 2), jnp.int32),
                   jax.ShapeDtypeStruct((8, n), F32), jax.ShapeDtypeStruct((n, 128), F32),
                   jax.ShapeDtypeStruct((16, 128), F32)],
        scratch_shapes=[pltpu.VMEM((16, 128), F32)],
        compiler_params=pltpu.CompilerParams(dimension_semantics=("arbitrary",),
                                             vmem_limit_bytes=VMEM_LIMIT),
        name="router",
    )(x, mixed, wo, nw, wrt, upper)


def _sc_row_gather(table, idx):
    n_workers = SC_CORES * SC_SUBCORES
    b = idx.shape[0]
    d = table.shape[1]
    assert b % (8 * n_workers) == 0
    per_worker = b // n_workers
    chunk = max(c for c in range(8, SC_MAX_GATHER_ROWS + 1, 8) if per_worker % c == 0)
    mesh = plsc.VectorSubcoreMesh(core_axis_name="c", subcore_axis_name="s")

    @functools.partial(
        pl.kernel, mesh=mesh, out_type=jax.ShapeDtypeStruct((b, d), table.dtype),
        scratch_types=[pltpu.VMEM((chunk,), jnp.int32), pltpu.VMEM((chunk, d), table.dtype),
                       pltpu.SemaphoreType.DMA])
    def gather(table_hbm, idx_hbm, out_hbm, idx_v, rows_v, sem):
        worker = lax.axis_index("s") * SC_CORES + lax.axis_index("c")
        base = worker * per_worker

        @pl.loop(0, per_worker // chunk)
        def _(c):
            off = pl.multiple_of(base + c * chunk, 8)
            pltpu.sync_copy(idx_hbm.at[pl.ds(off, chunk)], idx_v)
            pltpu.async_copy(table_hbm.at[idx_v], rows_v, sem).wait()
            pltpu.sync_copy(rows_v, out_hbm.at[pl.ds(off, chunk)])

    return gather(table, idx)


def _sc_row_scatter(rows, idx):
    n_workers = SC_CORES * SC_SUBCORES
    b = n_out = idx.shape[0]
    v, d = rows.shape
    assert b % (8 * n_workers) == 0
    per_worker = b // n_workers
    chunk = max(c for c in range(8, SC_MAX_GATHER_ROWS + 1, 8) if per_worker % c == 0 and v % c == 0)
    mesh = plsc.VectorSubcoreMesh(core_axis_name="c", subcore_axis_name="s")

    @functools.partial(
        pl.kernel, mesh=mesh, out_type=jax.ShapeDtypeStruct((n_out, d), rows.dtype),
        scratch_types=[pltpu.VMEM((chunk,), jnp.int32), pltpu.VMEM((chunk, d), rows.dtype)])
    def scatter(rows_hbm, idx_hbm, out_hbm, idx_v, rows_v):
        worker = lax.axis_index("s") * SC_CORES + lax.axis_index("c")
        base = worker * per_worker

        @pl.loop(0, per_worker // chunk)
        def _(c):
            off = pl.multiple_of(base + c * chunk, 8)
            pltpu.sync_copy(idx_hbm.at[pl.ds(off, chunk)], idx_v)
            pltpu.sync_copy(rows_hbm.at[pl.ds(pl.multiple_of(lax.rem(off, v), 8), chunk)], rows_v)
            pltpu.sync_copy(rows_v, out_hbm.at[idx_v])

    return scatter(rows, idx)


def _expert_ffn_kernel(texp_ref, tvalid_ref, xs_ref, wg_ref, wu_ref, wd_ref, o_ref, acc_ref, xb_ref):
    j = pl.program_id(0)
    f = pl.program_id(1)
    last = f == pl.num_programs(1) - 1
    valid = tvalid_ref[j] == 1
    half = D_MODEL // 2

    @pl.when(jnp.logical_and(valid, f == 0))
    def _():
        lo, hi = _unpack_bf16_pairs(xs_ref[...])
        xb_ref[:, :half] = lo.astype(BF16)
        xb_ref[:, half:] = hi.astype(BF16)

        acc_ref[...] = jnp.zeros_like(acc_ref)

    @pl.when(valid)
    def _():
        xs = xb_ref[...]
        cols = wg_ref.shape[2] // 2
        for c0 in range(0, wg_ref.shape[2], cols):
            g = jnp.dot(xs, wg_ref[0, :, c0:c0 + cols].astype(BF16), preferred_element_type=F32)
            u = jnp.dot(xs, wu_ref[0, :, c0:c0 + cols].astype(BF16), preferred_element_type=F32)
            act = (_silu(g) * u).astype(BF16)
            acc_ref[...] += jnp.dot(act, wd_ref[0, c0:c0 + cols, :].astype(BF16), preferred_element_type=F32)

        @pl.when(last)
        def _():
            o_ref[...] = _pack_bf16_pairs(acc_ref[...])

    @pl.when(jnp.logical_and(jnp.logical_not(valid), last))
    def _():
        o_ref[...] = jnp.zeros_like(o_ref)


def _expert_ffn(f_exp, f_valid, xs, wg, wu, wd):
    n_slots = xs.shape[0]
    d_ff = wg.shape[2]
    tf = _pick_tile(d_ff, (512, 256, 128))
    n_f = d_ff // tf

    def col(j, f, tv):
        return jnp.where(tv[j] == 1, f, n_f - 1)

    grid_spec = pltpu.PrefetchScalarGridSpec(
        num_scalar_prefetch=2, grid=(n_slots // FFN_TILE, d_ff // tf),
        in_specs=[pl.BlockSpec((FFN_TILE, D_MODEL // 2), lambda j, f, te, tv: (j, 0)),
                  pl.BlockSpec((1, D_MODEL, tf), lambda j, f, te, tv: (te[j], 0, col(j, f, tv))),
                  pl.BlockSpec((1, D_MODEL, tf), lambda j, f, te, tv: (te[j], 0, col(j, f, tv))),
                  pl.BlockSpec((1, tf, D_MODEL), lambda j, f, te, tv: (te[j], col(j, f, tv), 0))],
        out_specs=pl.BlockSpec((FFN_TILE, D_MODEL // 2), lambda j, f, te, tv: (j, 0)),
        scratch_shapes=[pltpu.VMEM((FFN_TILE, D_MODEL), F32), pltpu.VMEM((FFN_TILE, D_MODEL), BF16)])
    return pl.pallas_call(
        _expert_ffn_kernel, grid_spec=grid_spec,
        out_shape=jax.ShapeDtypeStruct((n_slots, D_MODEL // 2), jnp.int32),
        compiler_params=pltpu.CompilerParams(dimension_semantics=("arbitrary", "arbitrary"),
                                             vmem_limit_bytes=VMEM_LIMIT),
        name="expert_ffn",
    )(f_exp, f_valid, xs, wg, wu, wd)


def _combine_kernel(part_tiles, x1_ref, o1_ref, o2_ref, gate_ref, fnw_ref, *y_refs):
    i = pl.program_id(0)
    lo1, hi1 = _unpack_bf16_pairs(o1_ref[...])
    lo2, hi2 = _unpack_bf16_pairs(o2_ref[...])
    g1 = gate_ref[:, 0:1]
    g2 = gate_ref[:, 1:2]
    half = D_MODEL // 2
    x1 = x1_ref[...]
    fnw = fnw_ref[...]
    ya = x1[:, :half] + g1 * lo1 + g2 * lo2
    yb = x1[:, half:] + g1 * hi1 + g2 * hi2
    scale = lax.rsqrt((jnp.sum(ya * ya, axis=-1, keepdims=True) + jnp.sum(yb * yb, axis=-1, keepdims=True))
                      / D_MODEL + EPS)
    y = jnp.concatenate([ya * scale * fnw[:, :half], yb * scale * fnw[:, half:]], axis=1)
    lo_tile = 0
    for y_ref, tiles in zip(y_refs, part_tiles):
        @pl.when(jnp.logical_and(i >= lo_tile, i < lo_tile + tiles))
        def _(y_ref=y_ref):
            _store_part_tile(y_ref, y)
        lo_tile += tiles


def _combine(x1, o12, gates, fnw, part_shapes):
    n = x1.shape[0]
    parts = [jax.ShapeDtypeStruct(s, F32) for s in part_shapes]
    tm = _pick_tile(np.gcd.reduce([_part_rows(p) for p in parts]), TOKEN_TILES)
    n_tiles = n // tm
    out_specs, bounds = _token_parts_specs(parts, tm)
    part_tiles = [bounds[k + 1] - bounds[k] for k in range(len(parts))]
    return pl.pallas_call(
        functools.partial(_combine_kernel, part_tiles),
        grid=(n_tiles,),
        in_specs=[pl.BlockSpec((tm, D_MODEL), lambda i: (i, 0)),
                  pl.BlockSpec((tm, D_MODEL // 2), lambda i: (i, 0)),
                  pl.BlockSpec((tm, D_MODEL // 2), lambda i: (i + n_tiles, 0)),
                  pl.BlockSpec((tm, 128), lambda i: (i, 0)),
                  pl.BlockSpec((1, D_MODEL), lambda i: (0, 0))],
        out_specs=out_specs,
        out_shape=parts,
        compiler_params=pltpu.CompilerParams(dimension_semantics=("arbitrary",),
                                             vmem_limit_bytes=VMEM_LIMIT),
        name="combine",
    )(x1, o12, o12, gates, fnw)


def _moe(x, mixed, wo, nw, router_w, wg, wu, wd, fnw, part_shapes):
    i32 = jnp.int32
    n = x.shape[0]
    assert n % TOK_BLK == 0
    wrt = jnp.pad(router_w.T, ((0, 16 - N_EXPERTS), (0, 0))).astype(BF16)
    t = np.arange(TOK_BLK)
    upper = jnp.asarray((t[:, None] < t[None, :]).astype(np.float32), dtype=BF16)
    x1, hn_p, meta, gates, count = _router(x, mixed, wo, nw, wrt, upper)

    sc_quantum = int(np.lcm(FFN_TILE, SC_CORES * SC_SUBCORES * SC_MAX_GATHER_ROWS))
    n_slots = -(-(-(-2 * n // FFN_TILE) + N_EXPERTS) * FFN_TILE // sc_quantum) * sc_quantum
    counts = count[:N_EXPERTS, 0].astype(i32)
    gsize = (counts + FFN_TILE - 1) // FFN_TILE * FFN_TILE
    gend = jnp.cumsum(gsize)
    goff = gend - gsize
    fstart = jnp.arange(n_slots // FFN_TILE, dtype=i32) * FFN_TILE
    f_valid = (fstart < gend[-1]).astype(i32)
    f_exp = jnp.sum((fstart[:, None] >= gend[None, :]).astype(i32), axis=1)
    f_exp = jnp.minimum(f_exp, jnp.sum((gend[-1] - 1 >= gend).astype(i32)))
    expert_ids = jnp.arange(N_EXPERTS, dtype=i32)[:, None]
    top = meta[0:2].astype(i32)
    rank = meta[2:4].astype(i32)
    slot = jnp.stack([jnp.sum(jnp.where(top[k][None, :] == expert_ids, goff[:, None], 0), axis=0) + rank[k]
                      for k in range(2)]).reshape(-1)
    pad = gsize - counts
    cpad = jnp.cumsum(pad)
    j = jnp.arange(n_slots - 2 * n, dtype=i32)
    pad_e = jnp.sum((j[:, None] >= cpad[None, :]).astype(i32), axis=1)
    pad_base = jnp.sum(jnp.where(pad_e[:, None] == expert_ids.T, (goff + counts - (cpad - pad))[None, :], 0), axis=1)
    pad_slot = jnp.where(pad_e < N_EXPERTS, pad_base + j, gend[-1] + j - cpad[-1])

    xs_p = _sc_row_scatter(hn_p, jnp.concatenate([slot, pad_slot]))
    out_p = _expert_ffn(f_exp, f_valid, xs_p, wg, wu, wd)
    o12 = _sc_row_gather(out_p, slot)
    return _combine(x1, o12, gates, fnw, part_shapes)


def _rope_table(seq_len, dec_len):
    half = ROPE_DIMS // 2
    pos = jnp.concatenate([jnp.arange(seq_len), PAST_LEN + jnp.arange(dec_len)]).astype(F32)
    inv_freq = jnp.power(jnp.float32(ROPE_THETA), -jnp.arange(half, dtype=F32) / half)
    lane = np.arange(128) % HEAD
    inv_lane = jnp.where(lane < ROPE_DIMS, inv_freq[lane % half], 0.0)
    ang = pos[:, None] * inv_lane[None, :]
    cos, sin = jnp.cos(ang), jnp.sin(ang)
    sin_a = jnp.where(lane < half, -sin, 0.0)
    sin_b = jnp.where((lane >= half) & (lane < ROPE_DIMS), sin, 0.0)
    return jnp.concatenate([cos, sin_a, sin_b], axis=1)


def _mixer_consts():
    t = np.arange(CHUNK)
    tri = (t[:, None] >= t[None, :])
    same = (t[:, None] // SUB) == (t[None, :] // SUB)
    c = np.arange(MIX_W)
    ind = (c[:, None] // HEAD) == (c[None, :] // HEAD)
    as_bf = lambda m: jnp.asarray(m.astype(np.float32), dtype=BF16)
    return [as_bf(tri), as_bf(tri & same), as_bf(same), as_bf(ind)]


def _mlstm_state_to_pairs(c, n):
    c = c.astype(F32)
    n_b = jnp.broadcast_to(n.astype(F32)[..., None], c.shape)
    z = jnp.zeros_like(c[:, 0::2])
    top = jnp.concatenate([c[:, 0::2], z, n_b[:, 0::2], z], axis=-1)
    bottom = jnp.concatenate([z, c[:, 1::2], z, n_b[:, 1::2]], axis=-1)
    return jnp.concatenate([top, bottom], axis=-2)


def _mlstm_state_from_pairs(s):
    b = s.shape[0]
    c = jnp.stack([s[:, :, a * HEAD:(a + 1) * HEAD, a * HEAD:(a + 1) * HEAD] for a in range(2)], axis=2)
    n = jnp.stack([s[:, :, a * HEAD:(a + 1) * HEAD, (2 + a) * HEAD] for a in range(2)], axis=2)
    return (c.reshape(b, N_HEADS, HEAD, HEAD), n.reshape(b, N_HEADS, HEAD))


def _pad_lanes(v, width):
    return jnp.pad(v, ((0, 0), (0, width - v.shape[1])))


def kernel(x_prompt, x_sample, state_mlstm_C, state_mlstm_n, state_mlstm_m, state_hgrn_S, cache_swa_k, cache_swa_v, state_ssd_h, state_ssd_conv, norm1_w, w_in, ml_ig_b, ml_fg_b, ml_norm_w, hg_lb_logits, hg_norm_w, sw_sinks, ssd_conv_w, ssd_conv_b, ssd_dt_bias, ssd_A_log, ssd_D, ssd_norm_w, w_out, norm2_w, ffn_w_gate, ffn_w_up, ffn_w_down, moe_router, moe_w_gate, moe_w_up, moe_w_down, final_norm_w):
    depth = w_in.shape[0]
    bp, seq_len, _ = x_prompt.shape
    bs, dec_len, _ = x_sample.shape
    assert seq_len % CHUNK == 0 and dec_len == CHUNK and depth % 2 == 0
    n_seq = bp + bs
    chunks_per_prompt = seq_len // CHUNK

    assert bp == SEQ_GROUP and bs % SEQ_GROUP == 0
    x_parts = (x_prompt.reshape(bp, chunks_per_prompt, CHUNK, D_MODEL), x_sample.reshape(bs * dec_len, D_MODEL))
    part_shapes = tuple(p.shape for p in x_parts)
    rope_tab = _rope_table(seq_len, dec_len)
    consts = _mixer_consts()

    sm = jax.nn.softmax(hg_lb_logits.astype(F32), axis=0)
    hg_lb = jnp.cumsum(sm, axis=0) - sm[0]

    state_outs = []
    for l in range(depth):
        lp = {
            "sinks": sw_sinks[l].astype(F32),
            "gbias": _pad_lanes(jnp.concatenate([ml_ig_b[l], ml_fg_b[l], ssd_dt_bias[l]])[None, :].astype(F32), 128),
            "alog": _pad_lanes(ssd_A_log[l][None, :].astype(F32), 128),
            "mlnw": ml_norm_w[l][None, :].astype(F32),
            "hglb": hg_lb[l][None, :],
            "hgnw": hg_norm_w[l][None, :].astype(F32),
            "convw": ssd_conv_w[l].astype(F32),
            "convb": ssd_conv_b[l][None, :].astype(F32),
            "ssdd": jnp.repeat(ssd_D[l].astype(F32), HEAD)[None, :],
            "ssdnw": ssd_norm_w[l][None, :].astype(F32),
        }
        c_aug = _mlstm_state_to_pairs(state_mlstm_C[l], state_mlstm_n[l])
        m_pad = jnp.pad(state_mlstm_m[l][:, None, :], ((0, 0), (0, 7), (0, 128 - N_HEADS)))
        s_t = jnp.swapaxes(state_hgrn_S[l], -1, -2)
        s_t = jnp.moveaxis(s_t, 1, 2).reshape(bs, HEAD, MIX_W)
        cv_pad = jnp.pad(state_ssd_conv[l], ((0, 0), (8 - (SSD_CONV - 1), 0), (0, 0)))
        init = [a.astype(F32) for a in (
            c_aug, m_pad, s_t, cache_swa_k[l].reshape(bs, WINDOW, 128), cache_swa_v[l].reshape(bs, WINDOW, 128),
            state_ssd_h[l], cv_pad)]

        proj = _inproj(x_parts, norm1_w[l][None, :], w_in, l)
        mixed, c_o, m_o, s_o, k_o, v_o, h_o, cv_o = _mixer(
            proj, rope_tab, init, lp, consts, chunks_per_prompt, n_seq)
        wo = w_out[l].astype(BF16)
        j = l // 2
        if l % 2 == 0:
            x_parts = (_dense_ffn(x_parts, mixed, wo, norm2_w[l][None, :], ffn_w_gate[j].astype(BF16),
                                  ffn_w_up[j].astype(BF16), ffn_w_down[j].astype(BF16)),)
        else:
            assert l == depth - 1 and len(x_parts) == 1
            y_parts = _moe(x_parts[0], mixed, wo, norm2_w[l][None, :], moe_router[j], moe_w_gate[j],
                           moe_w_up[j], moe_w_down[j], final_norm_w[None, :], part_shapes)
        s_back = jnp.moveaxis(s_o.reshape(n_seq, HEAD, N_HEADS, HEAD), 2, 1)
        state_outs.append(_mlstm_state_from_pairs(c_o) + (m_o[:, 0, :N_HEADS],
                           jnp.swapaxes(s_back, -1, -2),
                           k_o.reshape(n_seq, WINDOW, SW_KV, HEAD), v_o.reshape(n_seq, WINDOW, SW_KV, HEAD),
                           h_o, cv_o[:, 8 - (SSD_CONV - 1):, :]))

    y_prompt = y_parts[0].reshape(bp, seq_len, D_MODEL)
    y_sample = y_parts[1].reshape(bs, dec_len, D_MODEL)
    stacked = [jnp.stack([so[k] for so in state_outs]) for k in range(8)]
    return (y_prompt, y_sample) + tuple(s[:, :bp] for s in stacked) + tuple(s[:, bp:] for s in stacked)
```
